```python
import math
import jax
import jax.numpy as jnp
from jax import lax
import numpy as np

D_MODEL = 1024
BATCH = 8
SEQ = 4096
DEPTH = 1

GRID_W = 64
CTX_LEN = 256
EPS = 1e-6
N_MOD = 6

RG_WIDTH = 512
RG_BLOCKS = 8
RG_BLOCK_DIM = RG_WIDTH // RG_BLOCKS
RG_CONV = 4
RG_C = 8.0

GDN_HEADS = 4
GDN_DK = 128
GDN_DV = 128
GDN_WIDTH = GDN_HEADS * GDN_DV
GDN_CONV = 4
GDN_CHUNK = 64

MIX_WIDTH = RG_WIDTH + GDN_WIDTH
SPLIT_POINTS = (RG_WIDTH, 2 * RG_WIDTH, 2 * RG_WIDTH + 3 * GDN_WIDTH, 2 * RG_WIDTH + 4 * GDN_WIDTH)
IN_COLS = 2 * RG_WIDTH + 4 * GDN_WIDTH + 2 * 2 * GDN_HEADS

PEER_HEADS = 8
PEER_NKEYS = 128
PEER_EXPERTS = PEER_NKEYS * PEER_NKEYS
PEER_QDIM = 256
PEER_HALF = PEER_QDIM // 2
PEER_TOPK = 16
PEER_BLOCK = 128

kernel_name = 'hybrid_rglru_gdn_peer_dit_block'


def _rmsnorm(x, g):
    xf = x.astype(jnp.float32)
    y = xf * lax.rsqrt(jnp.mean(xf * xf, axis=-1, keepdims=True) + EPS)
    return (y * g.astype(jnp.float32)).astype(x.dtype)


def _modulate(h, shift, scale):
    return h * (1 + scale) + shift


def _l2norm(t):
    return t * lax.rsqrt(jnp.sum(t * t, axis=-1, keepdims=True) + EPS)


def _dwconv_centred(x, w, b=None):
    K, C = w.shape
    left = K // 2
    y = lax.conv_general_dilated(x, w[:, None, :].astype(x.dtype), window_strides=(1,),
                                 padding=[(left, K - 1 - left)],
                                 dimension_numbers=('NWC', 'WIO', 'NWC'),
                                 feature_group_count=C)
    if b is not None:
        y = y + b.astype(x.dtype)
    return y


def _raster_to_colmajor(t, rows):
    B, L, C = t.shape
    return t.reshape(B, rows, GRID_W, C).transpose(0, 2, 1, 3).reshape(B, L, C)


def _colmajor_to_raster(t, rows):
    B, L, C = t.shape
    return t.reshape(B, GRID_W, rows, C).transpose(0, 2, 1, 3).reshape(B, L, C)


def _combine(left, right):
    a_l, b_l = left
    a_r, b_r = right
    return a_l * a_r, a_r * b_l + b_r


def _rglru_direction(xc, gate_w, gate_b, lam, h0):
    B, T, W = xc.shape
    xf = xc.astype(jnp.float32)
    xb = xf.reshape(B, T, RG_BLOCKS, RG_BLOCK_DIM)
    gates = jnp.einsum('btnd,gnde->gbtne', xb, gate_w.astype(jnp.float32)).reshape(2, B, T, W)
    gates = gates + gate_b.astype(jnp.float32)[:, None, None, :]
    r = jax.nn.sigmoid(gates[0])
    i = jax.nn.sigmoid(gates[1])
    log_a = -RG_C * r * jax.nn.softplus(-lam.astype(jnp.float32))
    a = jnp.exp(log_a)
    b = jnp.sqrt(-jnp.expm1(2.0 * log_a)) * (i * xf)
    b = b.at[:, 0].add(a[:, 0] * h0)
    _, h = lax.associative_scan(_combine, (a, b), axis=1)
    return h, h[:, -1]


def _rglru_mixer(u, gate, conv_w, conv_b, gate_w, gate_b, lam, h0_f, h0_b):
    xc = _dwconv_centred(u, conv_w, conv_b)
    h_f, s_f = _rglru_direction(xc, gate_w[0], gate_b[0], lam[0], h0_f)
    h_b, s_b = _rglru_direction(jnp.flip(xc, 1), gate_w[1], gate_b[1], lam[1], h0_b)
    y = (h_f + jnp.flip(h_b, 1)).astype(u.dtype) * jax.nn.gelu(gate)
    return y, s_f, s_b


def _gdn_chunked(q, k, v, g, beta, s0):
    B, H, T, _ = q.shape
    n = T // GDN_CHUNK
    C = GDN_CHUNK
    q, k, v = (t.reshape(B, H, n, C, t.shape[-1]) for t in (q, k, v))
    g = g.reshape(B, H, n, C)
    beta = beta.reshape(B, H, n, C)
    gc = jnp.cumsum(g, axis=-1)
    pos = jnp.arange(C)
    incl = pos[:, None] >= pos[None, :]
    strict = pos[:, None] > pos[None, :]
    diff = gc[..., :, None] - gc[..., None, :]
    decay = jnp.where(incl, jnp.exp(jnp.where(incl, diff, 0.0)), 0.0)
    kb = k * beta[..., None]
    a_strict = jnp.where(strict, jnp.einsum('bhncd,bhnsd->bhncs', kb, k) * decay, 0.0)
    rhs = jnp.concatenate([v * beta[..., None], kb * jnp.exp(gc)[..., None]], axis=-1)
    sol = lax.linalg.triangular_solve(a_strict, rhs, left_side=True, lower=True, unit_diagonal=True)
    w_val = sol[..., :GDN_DV]
    k_cum = sol[..., GDN_DV:]
    attn = jnp.where(incl, jnp.einsum('bhncd,bhnsd->bhncs', q, k) * decay, 0.0)
    xs = tuple(jnp.moveaxis(t, 2, 0) for t in (q, k, w_val, k_cum, attn, gc))

    def step(S, inp):
        qi, ki, wi, kci, ai, gi = inp
        v_new = wi - jnp.einsum('bhcd,bhde->bhce', kci, S)
        o = jnp.einsum('bhcd,bhde->bhce', qi * jnp.exp(gi)[..., None], S) + jnp.einsum('bhcs,bhse->bhce', ai, v_new)
        g_last = gi[..., -1]
        S = S * jnp.exp(g_last)[..., None, None] + jnp.einsum('bhcd,bhce->bhde', ki * jnp.exp(g_last[..., None] - gi)[..., None], v_new)
        return S, o

    s_fin, o = lax.scan(step, s0, xs)
    o = jnp.moveaxis(o, 0, 2).reshape(B, H, T, GDN_DV)
    return o, s_fin


def _gdn_mixer(qkv, z, ab, conv_w, a_log, dt_bias, norm_g, s0_f, s0_b):
    B, T, _ = qkv.shape
    f32 = jnp.float32
    qkv = jax.nn.silu(_dwconv_centred(qkv, conv_w)).astype(f32)
    q, k, v = jnp.split(qkv, 3, axis=-1)
    heads = lambda t: t.reshape(B, T, GDN_HEADS, -1).transpose(0, 2, 1, 3)
    q = _l2norm(heads(q)) * (GDN_DK ** -0.5)
    k = _l2norm(heads(k))
    v = heads(v)
    ab = ab.astype(f32).reshape(B, T, 2, 2, GDN_HEADS).transpose(2, 3, 0, 4, 1)
    g = -jnp.exp(a_log.astype(f32))[:, None, :, None] * jax.nn.softplus(ab[:, 0] + dt_bias.astype(f32)[:, None, :, None])
    beta = jax.nn.sigmoid(ab[:, 1])
    o_f, s_f = _gdn_chunked(q, k, v, g[0], beta[0], s0_f)
    fl = lambda t: jnp.flip(t, 2)
    o_b, s_b = _gdn_chunked(fl(q), fl(k), fl(v), fl(g[1]), fl(beta[1]), s0_b)
    o = (o_f + fl(o_b)).transpose(0, 2, 1, 3)
    o = _rmsnorm(o, norm_g) * jax.nn.silu(z.astype(f32).reshape(B, T, GDN_HEADS, GDN_DV))
    return o.reshape(B, T, GDN_WIDTH).astype(z.dtype), s_f, s_b


def _mix(p, rows, mix_params, states):
    rg_conv_w, rg_conv_b, rg_gate_w, rg_gate_b, rg_lambda, gdn_conv_w, gdn_a_log, gdn_dt_bias, gdn_norm_g = mix_params
    rg_h0_f, rg_h0_b, gdn_s0_f, gdn_s0_b = states
    rg_u, rg_gate, qkv, z, ab = jnp.split(p, list(SPLIT_POINTS), axis=-1)
    if rows is not None:
        qkv = _raster_to_colmajor(qkv, rows)
        z = _raster_to_colmajor(z, rows)
        ab = _raster_to_colmajor(ab, rows)
    y_rg, rg_f, rg_b = _rglru_mixer(rg_u, rg_gate, rg_conv_w, rg_conv_b, rg_gate_w, rg_gate_b, rg_lambda, rg_h0_f, rg_h0_b)
    y_gdn, s_f, s_b = _gdn_mixer(qkv, z, ab, gdn_conv_w, gdn_a_log, gdn_dt_bias, gdn_norm_g, gdn_s0_f, gdn_s0_b)
    if rows is not None:
        y_gdn = _colmajor_to_raster(y_gdn, rows)
    return jnp.concatenate([y_rg, y_gdn], axis=-1), (rg_f, rg_b, s_f, s_b)


def _peer(h, wq, keys, u, v):
    B, T, D = h.shape
    blocks = h.reshape(B * T // PEER_BLOCK, PEER_BLOCK, D)

    def one_block(hb):
        q = (hb @ wq).astype(jnp.float32).reshape(PEER_BLOCK, PEER_HEADS, 2, PEER_HALF)
        s = jnp.einsum('phxd,xkd->phxk', q, keys.astype(jnp.float32))
        s1, i1 = lax.top_k(s[:, :, 0], PEER_TOPK)
        s2, i2 = lax.top_k(s[:, :, 1], PEER_TOPK)
        cand_s = (s1[..., :, None] + s2[..., None, :]).reshape(PEER_BLOCK, PEER_HEADS, PEER_TOPK * PEER_TOPK)
        cand_i = (i1[..., :, None] * PEER_NKEYS + i2[..., None, :]).reshape(PEER_BLOCK, PEER_HEADS, PEER_TOPK * PEER_TOPK)
        top_s, pos = lax.top_k(cand_s, PEER_TOPK)
        idx = jnp.take_along_axis(cand_i, pos, axis=-1)
        gate = jax.nn.softmax(top_s, axis=-1).astype(hb.dtype)
        u_e = jnp.take(u, idx, axis=0)
        v_e = jnp.take(v, idx, axis=0)
        act = jax.nn.gelu(jnp.einsum('pd,phkd->phk', hb, u_e))
        return jnp.einsum('phk,phkd->pd', gate * act, v_e)

    return lax.map(one_block, blocks).reshape(B, T, D)


def setup_inputs(seed: int = 0) -> dict:
    key = jax.random.key(seed)
    ks = jax.random.split(key, 24)
    f32 = jnp.float32
    D = D_MODEL
    nrm = lambda k, shape, s: jax.random.normal(k, shape, f32) * s
    x = nrm(ks[0], (BATCH, SEQ, D), 1.0)
    c = nrm(ks[1], (BATCH, D), 1.0)
    ctx = nrm(ks[2], (BATCH, CTX_LEN, D), 1.0)
    c_ctx = nrm(ks[3], (D,), 1.0)
    w_mod = nrm(ks[4], (DEPTH, D, N_MOD * D), 0.5 * D ** -0.5)
    b_mod = nrm(ks[5], (DEPTH, N_MOD * D), 0.01)
    norm1_g = 1.0 + nrm(ks[6], (DEPTH, D), 0.01)
    norm2_g = 1.0 + nrm(ks[7], (DEPTH, D), 0.01)
    w_in = nrm(ks[8], (DEPTH, D, IN_COLS), D ** -0.5)
    rg_conv_w = nrm(ks[9], (DEPTH, RG_CONV, RG_WIDTH), RG_CONV ** -0.5)
    rg_conv_b = nrm(ks[10], (DEPTH, RG_WIDTH), 0.01)
    rg_gate_w = nrm(ks[11], (DEPTH, 2, 2, RG_BLOCKS, RG_BLOCK_DIM, RG_BLOCK_DIM), RG_BLOCK_DIM ** -0.5)
    rg_gate_b = nrm(ks[12], (DEPTH, 2, 2, RG_WIDTH), 0.01)
    a_pow = jax.random.uniform(ks[13], (DEPTH, 2, RG_WIDTH), f32, 0.9, 0.999)
    a0 = a_pow ** (1.0 / RG_C)
    rg_lambda = jnp.log(a0) - jnp.log1p(-a0)
    gdn_conv_w = nrm(ks[14], (DEPTH, GDN_CONV, 3 * GDN_WIDTH), GDN_CONV ** -0.5)
    gdn_a_log = jnp.log(jax.random.uniform(ks[15], (DEPTH, 2, GDN_HEADS), f32, 1.0, 16.0))
    dt = jnp.exp(jax.random.uniform(ks[16], (DEPTH, 2, GDN_HEADS), f32, math.log(1e-3), math.log(1e-1)))
    gdn_dt_bias = dt + jnp.log(-jnp.expm1(-dt))
    gdn_norm_g = 1.0 + nrm(ks[17], (DEPTH, GDN_DV), 0.01)
    w_out = nrm(ks[18], (DEPTH, MIX_WIDTH, D), MIX_WIDTH ** -0.5)
    peer_wq = nrm(ks[19], (DEPTH, D, PEER_HEADS * PEER_QDIM), D ** -0.5)
    peer_keys = nrm(ks[20], (DEPTH, 2, PEER_NKEYS, PEER_HALF), PEER_HALF ** -0.5)
    peer_u = nrm(ks[21], (DEPTH, PEER_EXPERTS, D), D ** -0.5)
    peer_v = nrm(ks[22], (DEPTH, PEER_EXPERTS, D), PEER_HEADS ** -0.5)
    final_g = 1.0 + nrm(ks[23], (D,), 0.01)
    return {'x': x, 'c': c, 'ctx': ctx, 'c_ctx': c_ctx, 'w_mod': w_mod, 'b_mod': b_mod,
            'norm1_g': norm1_g, 'norm2_g': norm2_g, 'w_in': w_in,
            'rg_conv_w': rg_conv_w, 'rg_conv_b': rg_conv_b, 'rg_gate_w': rg_gate_w,
            'rg_gate_b': rg_gate_b, 'rg_lambda': rg_lambda, 'gdn_conv_w': gdn_conv_w,
            'gdn_a_log': gdn_a_log, 'gdn_dt_bias': gdn_dt_bias, 'gdn_norm_g': gdn_norm_g,
            'w_out': w_out, 'peer_wq': peer_wq, 'peer_keys': peer_keys, 'peer_u': peer_u,
            'peer_v': peer_v, 'final_g': final_g}


def reference(x, c, ctx, c_ctx, w_mod, b_mod, norm1_g, norm2_g, w_in, rg_conv_w, rg_conv_b,
              rg_gate_w, rg_gate_b, rg_lambda, gdn_conv_w, gdn_a_log, gdn_dt_bias, gdn_norm_g,
              w_out, peer_wq, peer_keys, peer_u, peer_v, final_g):
    B, L, _ = x.shape
    rows = L // GRID_W
    zero_states = (jnp.zeros((B, RG_WIDTH), jnp.float32), jnp.zeros((B, RG_WIDTH), jnp.float32),
                   jnp.zeros((B, GDN_HEADS, GDN_DK, GDN_DV), jnp.float32),
                   jnp.zeros((B, GDN_HEADS, GDN_DK, GDN_DV), jnp.float32))
    for l in range(DEPTH):
        mod_lat = (jax.nn.silu(c) @ w_mod[l] + b_mod[l])[:, None, :]
        mod_ctx = (jax.nn.silu(c_ctx) @ w_mod[l] + b_mod[l])[None, None, :]
        sh1, sc1, gt1, sh2, sc2, gt2 = jnp.split(mod_lat, N_MOD, axis=-1)
        csh1, csc1, cgt1, csh2, csc2, cgt2 = jnp.split(mod_ctx, N_MOD, axis=-1)
        mix_params = (rg_conv_w[l], rg_conv_b[l], rg_gate_w[l], rg_gate_b[l], rg_lambda[l],
                      gdn_conv_w[l], gdn_a_log[l], gdn_dt_bias[l], gdn_norm_g[l])
        p_ctx = _modulate(_rmsnorm(ctx, norm1_g[l]), csh1, csc1) @ w_in[l]
        y_ctx, ctx_states = _mix(p_ctx, None, mix_params, zero_states)
        p_lat = _modulate(_rmsnorm(x, norm1_g[l]), sh1, sc1) @ w_in[l]
        y_lat, _ = _mix(p_lat, rows, mix_params, ctx_states)
        x = x + gt1 * (y_lat @ w_out[l])
        x = x + gt2 * _peer(_modulate(_rmsnorm(x, norm2_g[l]), sh2, sc2), peer_wq[l], peer_keys[l], peer_u[l], peer_v[l])
        if l < DEPTH - 1:
            ctx = ctx + cgt1 * (y_ctx @ w_out[l])
            ctx = ctx + cgt2 * _peer(_modulate(_rmsnorm(ctx, norm2_g[l]), csh2, csc2), peer_wq[l], peer_keys[l], peer_u[l], peer_v[l])
    return _rmsnorm(x, final_g)
```

```python
import functools
import math

import jax
import jax.numpy as jnp
from jax import lax
from jax.experimental import pallas as pl
from jax.experimental.pallas import tpu as pltpu

F32 = jnp.float32
BF16 = jnp.bfloat16

GRID_W = 64
EPS = 1e-6
RG_WIDTH = 512
RG_BLOCKS = 8
RG_C = 8.0
GDN_HEADS = 4
GDN_DK = 128
GDN_DV = 128
GDN_WIDTH = GDN_HEADS * GDN_DV
GDN_CHUNK = 64
GDN_SUB = 16
AB_PAD = 128
PEER_HEADS = 8
PEER_NKEYS = 128
PEER_HALF = 128
PEER_TOPK = 16
NEG_INF = float("-inf")

VMEM_LIMIT = 56 * 1024 * 1024


def _cparams(sem):
    return pltpu.CompilerParams(dimension_semantics=sem, vmem_limit_bytes=VMEM_LIMIT)


def _dot(a, b):
    return jnp.dot(a, b, preferred_element_type=F32)


def _dot_nt(a, b):
    return lax.dot_general(a, b, (((1,), (1,)), ((), ())), preferred_element_type=F32)


def _dot_tn(a, b):
    return lax.dot_general(a, b, (((0,), (0,)), ((), ())), preferred_element_type=F32)


def _split3(x):
    hi = x.astype(BF16)
    r = x - hi.astype(F32)
    mid = r.astype(BF16)
    lo = (r - mid.astype(F32)).astype(BF16)
    return hi, mid, lo


def _dot_sel(m, x):
    hi, mid, lo = _split3(x)
    return _dot(m, lo) + _dot(m, mid) + _dot(m, hi)


def _silu(x):
    return x * jax.nn.sigmoid(x)


def _softplus(x):
    return jnp.maximum(x, 0.0) + jnp.log1p(jnp.exp(-jnp.abs(x)))


def _rms(x, g):
    return x * lax.rsqrt(jnp.mean(x * x, axis=-1, keepdims=True) + EPS) * g


def _mod_kernel(c_ref, w_ref, b_ref, o_ref):
    s = _silu(c_ref[...])
    o_ref[...] = jnp.dot(s, w_ref[...], preferred_element_type=F32,
                         precision=lax.Precision.HIGHEST) + b_ref[...]


def _modulation(cc, w_mod, b_mod):
    m, d = cc.shape
    n = w_mod.shape[1]
    tn = 1536
    return pl.pallas_call(
        _mod_kernel,
        grid=(n // tn,),
        in_specs=[pl.BlockSpec((m, d), lambda j: (0, 0)),
                  pl.BlockSpec((d, tn), lambda j: (0, j)),
                  pl.BlockSpec((1, tn), lambda j: (0, j))],
        out_specs=pl.BlockSpec((m, tn), lambda j: (0, j)),
        out_shape=jax.ShapeDtypeStruct((m, n), F32),
        compiler_params=_cparams(("arbitrary",)),
    )(cc, w_mod, b_mod.reshape(1, n))


def _proj_kernel(x_ref, g_ref, sh_ref, sc_ref, w_ref, *o_refs, ncol, d, widths):
    x = x_ref[...]
    if ncol > 1:
        x = jnp.concatenate([x[:, j * d:(j + 1) * d] for j in range(ncol)], axis=0)
    h = _rms(x, g_ref[...]) * (1.0 + sc_ref[...]) + sh_ref[...]
    o = _dot(h.astype(BF16), w_ref[...])
    off = 0
    for o_ref, wd in zip(o_refs, widths):
        o_ref[...] = o[:, off:off + wd]
        off += wd


def _project(x3, g, sh, sc, mod_row, w, widths, *, colmajor, tile):
    b, t, d = x3.shape
    n = w.shape[1]
    if colmajor:
        rows = t // GRID_W
        ncol = tile // rows
        xv = x3.reshape(b, rows, GRID_W * d)
        x_spec = pl.BlockSpec((None, rows, ncol * d), lambda i, j: (i, 0, j))
        nt = GRID_W // ncol
    else:
        ncol = 1
        xv = x3
        x_spec = pl.BlockSpec((None, tile, d), lambda i, j: (i, j, 0))
        nt = t // tile
    vec = lambda: pl.BlockSpec((None, 1, d), lambda i, j: (mod_row(i), 0, 0))
    outs = pl.pallas_call(
        functools.partial(_proj_kernel, ncol=ncol, d=d, widths=tuple(widths)),
        grid=(b, nt),
        in_specs=[x_spec, pl.BlockSpec((1, d), lambda i, j: (0, 0)), vec(), vec(),
                  pl.BlockSpec((d, n), lambda i, j: (0, 0))],
        out_specs=[pl.BlockSpec((None, tile, wd), lambda i, j: (i, j, 0)) for wd in widths],
        out_shape=[jax.ShapeDtypeStruct((b, t, wd), F32) for wd in widths],
        compiler_params=_cparams(("parallel", "parallel")),
    )(xv, g.reshape(1, d), sh, sc, w)
    return outs


def _conv4(cur, prev8, nxt8, w_ref, first, last):
    tb = cur.shape[0]
    prev8 = jnp.where(first, 0.0, prev8)
    nxt8 = jnp.where(last, 0.0, nxt8)
    ext = jnp.concatenate([prev8, cur, nxt8], axis=0)
    n = tb + 16
    xm2 = pltpu.roll(ext, 2, 0)[8:8 + tb]
    xm1 = pltpu.roll(ext, 1, 0)[8:8 + tb]
    xp1 = pltpu.roll(ext, n - 1, 0)[8:8 + tb]
    return (w_ref[0:1, :] * xm2 + w_ref[1:2, :] * xm1 + w_ref[2:3, :] * cur + w_ref[3:4, :] * xp1)


def _halo_specs(tb, t, width, lane_blk, tmap):
    r = tb // 8
    nb8 = t // 8
    return [pl.BlockSpec((None, tb, width), lambda i, j: (i, tmap(j), lane_blk)),
            pl.BlockSpec((None, 8, width), lambda i, j: (i, jnp.maximum(tmap(j) * r - 1, 0), lane_blk)),
            pl.BlockSpec((None, 8, width), lambda i, j: (i, jnp.minimum((tmap(j) + 1) * r, nb8 - 1), lane_blk))]


def _rglru_kernel(*refs, tb, nt, reverse):
    if reverse:
        (u_ref, up_ref, un_ref, cw_ref, cb_ref, wg_ref, gb_ref, cl_ref, h0_ref, gate_ref, hf_ref,
         y_ref, st_ref, carry) = refs
    else:
        (u_ref, up_ref, un_ref, cw_ref, cb_ref, wg_ref, gb_ref, cl_ref, h0_ref,
         y_ref, st_ref, carry) = refs
    j = pl.program_id(1)
    tblk = (nt - 1 - j) if reverse else j
    w = u_ref.shape[-1]

    @pl.when(j == 0)
    def _():
        carry[...] = h0_ref[...]

    u = u_ref[...]
    xc = _conv4(u, up_ref[...], un_ref[...], cw_ref, tblk == 0, tblk == nt - 1) + cb_ref[...]
    gates = _dot(xc.astype(BF16), wg_ref[...]) + gb_ref[...]
    r = jax.nn.sigmoid(gates[:, :w])
    ig = jax.nn.sigmoid(gates[:, w:])
    log_a = r * cl_ref[...]
    a = jnp.exp(log_a)
    th = jnp.tanh(log_a)
    bb = jnp.sqrt(-2.0 * th / (1.0 - th)) * (ig * xc)

    row = lax.broadcasted_iota(jnp.int32, (tb, w), 0)
    s = 1
    while s < tb:
        if reverse:
            a_sh = pltpu.roll(a, tb - s, 0)
            b_sh = pltpu.roll(bb, tb - s, 0)
            ok = row < tb - s
        else:
            a_sh = pltpu.roll(a, s, 0)
            b_sh = pltpu.roll(bb, s, 0)
            ok = row >= s
        a_sh = jnp.where(ok, a_sh, 1.0)
        b_sh = jnp.where(ok, b_sh, 0.0)
        bb = a * b_sh + bb
        a = a * a_sh
        s *= 2
    h = a * carry[...] + bb
    carry[...] = h[0:1, :] if reverse else h[tb - 1:tb, :]
    st_ref[...] = carry[...]
    if reverse:
        y_ref[...] = ((hf_ref[...] + h) * jax.nn.gelu(gate_ref[...])).astype(y_ref.dtype)
    else:
        y_ref[...] = h


def _rglru_pass(p_rg, conv_w, conv_b, wg, gate_b, c_lam, h0, hf, *, reverse, tb):
    b, t, w2 = p_rg.shape
    w = w2 // 2
    nt = t // tb
    tmap = (lambda j: nt - 1 - j) if reverse else (lambda j: j)
    const = lambda shape: pl.BlockSpec(shape, lambda i, j: (0,) * len(shape))
    in_specs = _halo_specs(tb, t, w, 0, tmap) + [
        const((4, w)), const((1, w)), const((w, 2 * w)), const((1, 2 * w)), const((1, w)),
        pl.BlockSpec((None, 1, w), lambda i, j: (i, 0, 0))]
    args = [p_rg, p_rg, p_rg, conv_w, conv_b, wg, gate_b, c_lam, h0]
    if reverse:
        in_specs += [pl.BlockSpec((None, tb, w), lambda i, j: (i, tmap(j), 1)),
                     pl.BlockSpec((None, tb, w), lambda i, j: (i, tmap(j), 0))]
        args += [p_rg, hf]
    y, st = pl.pallas_call(
        functools.partial(_rglru_kernel, tb=tb, nt=nt, reverse=reverse),
        grid=(b, nt),
        in_specs=in_specs,
        out_specs=[pl.BlockSpec((None, tb, w), lambda i, j: (i, tmap(j), 0)),
                   pl.BlockSpec((None, 1, w), lambda i, j: (i, 0, 0))],
        out_shape=[jax.ShapeDtypeStruct((b, t, w), BF16 if reverse else F32),
                   jax.ShapeDtypeStruct((b, 1, w), F32)],
        scratch_shapes=[pltpu.VMEM((1, w), F32)],
        compiler_params=_cparams(("parallel", "arbitrary")),
    )(*args)
    return y, st


def _unit_tri_inverse(a, diag_mask):
    c = a.shape[0]
    eye = (lax.broadcasted_iota(jnp.int32, (c, c), 0) == lax.broadcasted_iota(jnp.int32, (c, c), 1)).astype(F32)
    mm = lambda p, q: _dot(p.astype(BF16), q.astype(BF16))
    ad = jnp.where(diag_mask, a, 0.0)
    x = eye - ad
    pw = ad
    k = 2
    while k < GDN_SUB:
        pw = mm(pw, pw)
        x = x + mm(x, pw)
        k *= 2
    n = mm(x, a - ad)
    nblk = c // GDN_SUB
    y = eye - n
    pw = n
    k = 2
    while k < nblk:
        pw = mm(pw, pw)
        y = y + mm(y, pw)
        k *= 2
    return mm(y, x)


def _gdn_prep_kernel(qkv_ref, qp_ref, qn_ref, ab_ref, cw_ref, ea_ref, dtb_ref,
                     qg_ref, kd_ref, kc_ref, wv_ref, at_ref, eg_ref, *, tb, nt):
    c = GDN_CHUNK
    j = pl.program_id(1)
    hw = GDN_WIDTH
    x = _silu(_conv4(qkv_ref[...], qp_ref[...], qn_ref[...], cw_ref, j == 0, j == nt - 1))

    ab = ab_ref[...]
    col = lax.broadcasted_iota(jnp.int32, (tb, AB_PAD), 1)
    is_a = (col & 4) == 0
    gbv = jnp.where(is_a, -ea_ref[...] * _softplus(ab + dtb_ref[...]), jax.nn.sigmoid(ab))

    ri = lax.broadcasted_iota(jnp.int32, (tb, tb), 0)
    ci = lax.broadcasted_iota(jnp.int32, (tb, tb), 1)
    same = (ri // c) == (ci // c)
    l_f = (same & (ci <= ri)).astype(BF16)
    l_b = (same & (ci >= ri)).astype(BF16)
    l_t = same.astype(BF16)
    gcum = jnp.where(col < 8, _dot_sel(l_f, gbv), _dot_sel(l_b, gbv))
    gtot = _dot_sel(l_t, gbv)

    n_combo = 2 * GDN_HEADS
    er = lax.broadcasted_iota(jnp.int32, (AB_PAD, n_combo * 128), 0)
    ec = lax.broadcasted_iota(jnp.int32, (AB_PAD, n_combo * 128), 1) // 128
    src_a = (ec // GDN_HEADS) * 8 + (ec % GDN_HEADS)
    e_a = (er == src_a).astype(BF16)
    e_b = (er == src_a + 4).astype(BF16)
    gc_all = _bcast_cols(gcum, e_a)
    gt_all = _bcast_cols(gtot, e_a)
    be_all = _bcast_cols(gbv, e_b)

    ii = lax.broadcasted_iota(jnp.int32, (c, 128), 0)
    jj = lax.broadcasted_iota(jnp.int32, (c, 128), 1)
    eye_w = ii == jj
    ones_c = jnp.ones((c, c), BF16)
    i64 = lax.broadcasted_iota(jnp.int32, (c, c), 0)
    j64 = lax.broadcasted_iota(jnp.int32, (c, c), 1)
    diag_blk = (i64 // GDN_SUB) == (j64 // GDN_SUB)
    zrows = jnp.zeros((c, GDN_DK), BF16)

    for hd in range(GDN_HEADS):
        q = x[:, hd * GDN_DK:(hd + 1) * GDN_DK]
        k = x[:, hw + hd * GDN_DK: hw + (hd + 1) * GDN_DK]
        v = x[:, 2 * hw + hd * GDN_DV: 2 * hw + (hd + 1) * GDN_DV]
        q = q * lax.rsqrt(jnp.sum(q * q, axis=-1, keepdims=True) + EPS) * (GDN_DK ** -0.5)
        k = k * lax.rsqrt(jnp.sum(k * k, axis=-1, keepdims=True) + EPS)
        for dr in range(2):
            cb = (dr * GDN_HEADS + hd) * 128
            incl = (jj <= ii) if dr == 0 else ((jj >= ii) & (jj < c))
            strict = (j64 < i64) if dr == 0 else (j64 > i64)
            for ch in range(tb // c):
                rs = slice(ch * c, (ch + 1) * c)
                gcb = gc_all[rs, cb:cb + 128]
                gtb = gt_all[rs, cb:cb + 128]
                beb = be_all[rs, cb:cb + 128]
                qc, kc_, vc = q[rs], k[rs], v[rs]
                eg = jnp.exp(gcb)
                kb = kc_ * beb
                gcr = _dot_sel(ones_c, jnp.where(eye_w, gcb, 0.0))
                decay = jnp.where(incl, jnp.exp(jnp.where(incl, gcb - gcr, 0.0)), 0.0)
                kpad = jnp.concatenate([kc_.astype(BF16), zrows], axis=0)
                qk = _dot_nt(jnp.concatenate([qc, kb], axis=0).astype(BF16), kpad)
                attn = qk[:c] * decay
                a_mat = jnp.where(strict, (qk[c:] * decay)[:, :c], 0.0)
                tinv = _unit_tri_inverse(a_mat, diag_blk)
                rhs = jnp.concatenate([vc * beb, kb * eg], axis=1)
                sol = _dot(tinv.astype(BF16), rhs.astype(BF16))
                ls = slice(hd * 128, (hd + 1) * 128)
                qg_ref[dr, rs, ls] = (qc * eg).astype(BF16)
                kd_ref[dr, rs, ls] = (kc_ * jnp.exp(gtb - gcb)).astype(BF16)
                kc_ref[dr, rs, ls] = sol[:, GDN_DV:].astype(BF16)
                wv_ref[dr, rs, ls] = sol[:, :GDN_DV]
                at_ref[dr, rs, ls] = attn.astype(BF16)
                eg_ref[dr, ch, :, ls] = jnp.exp(gtb[0:8, :])


def _bcast_cols(x, sel):
    hi, mid, lo = _split3(x)
    return _dot(lo, sel) + _dot(mid, sel) + _dot(hi, sel)


def _gdn_prep(qkv, ab, conv_w, ea_row, dtb_row, *, tb):
    b, t, cw = qkv.shape
    nt = t // tb
    nch = t // GDN_CHUNK
    cpb = tb // GDN_CHUNK
    const = lambda shape: pl.BlockSpec(shape, lambda i, j: (0,) * len(shape))
    hw = GDN_WIDTH
    big = lambda: pl.BlockSpec((2, None, tb, hw), lambda i, j: (0, i, j, 0))
    outs = pl.pallas_call(
        functools.partial(_gdn_prep_kernel, tb=tb, nt=nt),
        grid=(b, nt),
        in_specs=_halo_specs(tb, t, cw, 0, lambda j: j) + [
            pl.BlockSpec((None, tb, AB_PAD), lambda i, j: (i, j, 0)),
            const((4, cw)), const((1, AB_PAD)), const((1, AB_PAD))],
        out_specs=[big(), big(), big(), big(), big(),
                   pl.BlockSpec((2, None, cpb, 8, hw), lambda i, j: (0, i, j, 0, 0))],
        out_shape=[jax.ShapeDtypeStruct((2, b, t, hw), BF16),
                   jax.ShapeDtypeStruct((2, b, t, hw), BF16),
                   jax.ShapeDtypeStruct((2, b, t, hw), BF16),
                   jax.ShapeDtypeStruct((2, b, t, hw), F32),
                   jax.ShapeDtypeStruct((2, b, t, hw), BF16),
                   jax.ShapeDtypeStruct((2, b, nch, 8, hw), F32)],
        compiler_params=_cparams(("parallel", "parallel")),
    )(qkv, qkv, qkv, ab, conv_w, ea_row, dtb_row)
    return outs


def _gdn_scan_kernel(qg_f, kd_f, kc_f, wv_f, at_f, eg_f, qg_b, kd_b, kc_b, wv_b, at_b, eg_b, s0_ref,
                     of_ref, ob_ref, sfin_ref, s_scr):
    c = GDN_CHUNK
    j = pl.program_id(1)

    @pl.when(j == 0)
    def _():
        s_scr[...] = s0_ref[...]

    for dr, (qg, kd, kc, wv, at, eg, o_ref) in enumerate(
            ((qg_f, kd_f, kc_f, wv_f, at_f, eg_f, of_ref), (qg_b, kd_b, kc_b, wv_b, at_b, eg_b, ob_ref))):
        for hd in range(GDN_HEADS):
            ls = slice(hd * 128, (hd + 1) * 128)
            s = s_scr[dr, hd]
            kq = jnp.concatenate([kc[:, ls], qg[:, ls]], axis=0)
            r = _dot(kq, s.astype(BF16))
            v_new = wv[:, ls] - r[:c]
            vb = v_new.astype(BF16)
            o_ref[:, ls] = r[c:] + _dot(at[:, ls][:, :c], vb)
            s_scr[dr, hd] = s * eg[0:1, ls] + _dot_tn(kd[:, ls], vb)
    sfin_ref[...] = s_scr[...]


def _gdn_scan(prep, s0):
    qg, kd, kc, wv, at, eg = prep
    _, b, t, hw = qg.shape
    c = GDN_CHUNK
    nch = t // c
    fw = lambda: pl.BlockSpec((None, None, c, hw), lambda i, j: (0, i, j, 0))
    bw = lambda: pl.BlockSpec((None, None, c, hw), lambda i, j: (1, i, nch - 1 - j, 0))
    egf = pl.BlockSpec((None, None, None, 8, hw), lambda i, j: (0, i, j, 0, 0))
    egb = pl.BlockSpec((None, None, None, 8, hw), lambda i, j: (1, i, nch - 1 - j, 0, 0))
    st = pl.BlockSpec((None, 2, GDN_HEADS, GDN_DK, GDN_DV), lambda i, j: (i, 0, 0, 0, 0))
    o_f, o_b, s_fin = pl.pallas_call(
        _gdn_scan_kernel,
        grid=(b, nch),
        in_specs=[fw(), fw(), fw(), fw(), fw(), egf, bw(), bw(), bw(), bw(), bw(), egb, st],
        out_specs=[pl.BlockSpec((None, c, hw), lambda i, j: (i, j, 0)),
                   pl.BlockSpec((None, c, hw), lambda i, j: (i, nch - 1 - j, 0)),
                   st],
        out_shape=[jax.ShapeDtypeStruct((b, t, hw), F32), jax.ShapeDtypeStruct((b, t, hw), F32),
                   jax.ShapeDtypeStruct((b, 2, GDN_HEADS, GDN_DK, GDN_DV), F32)],
        scratch_shapes=[pltpu.VMEM((2, GDN_HEADS, GDN_DK, GDN_DV), F32)],
        compiler_params=_cparams(("parallel", "arbitrary")),
    )(qg, kd, kc, wv, at, eg, qg, kd, kc, wv, at, eg, s0)
    return o_f, o_b, s_fin


def _out_rg_kernel(x_ref, y_ref, w_ref, gt_ref, o_ref):
    o_ref[...] = x_ref[...] + gt_ref[...] * _dot(y_ref[...], w_ref[...])


def _out_gdn_kernel(x_ref, of_ref, ob_ref, z_ref, ng_ref, w_ref, gt_ref, o_ref, *, ncol, d):
    o = of_ref[...] + ob_ref[...]
    z = z_ref[...]
    parts = []
    for hd in range(GDN_HEADS):
        ls = slice(hd * GDN_DV, (hd + 1) * GDN_DV)
        parts.append(_rms(o[:, ls], ng_ref[...]) * _silu(z[:, ls]))
    y = jnp.concatenate(parts, axis=1).astype(BF16)
    t2 = gt_ref[...] * _dot(y, w_ref[...])
    rows = t2.shape[0] // ncol
    t2 = jnp.concatenate([t2[j * rows:(j + 1) * rows] for j in range(ncol)], axis=1)
    o_ref[...] = x_ref[...] + t2


def _out_project(x3, y_rg, o_f, o_b, z, norm_g, w_out, gt, *, tile):
    b, t, d = x3.shape
    w = RG_WIDTH
    vec = lambda: pl.BlockSpec((None, 1, d), lambda i, j: (i, 0, 0))
    x1 = pl.pallas_call(
        _out_rg_kernel,
        grid=(b, t // tile),
        in_specs=[pl.BlockSpec((None, tile, d), lambda i, j: (i, j, 0)),
                  pl.BlockSpec((None, tile, w), lambda i, j: (i, j, 0)),
                  pl.BlockSpec((w, d), lambda i, j: (0, 0)), vec()],
        out_specs=pl.BlockSpec((None, tile, d), lambda i, j: (i, j, 0)),
        out_shape=jax.ShapeDtypeStruct((b, t, d), F32),
        compiler_params=_cparams(("parallel", "parallel")),
    )(x3, y_rg, w_out[:w], gt)
    rows = t // GRID_W
    ncol = tile // rows
    hw = GDN_WIDTH
    x1v = x1.reshape(b, rows, GRID_W * d)
    cm = lambda wd: pl.BlockSpec((None, tile, wd), lambda i, j: (i, j, 0))
    x1v = pl.pallas_call(
        functools.partial(_out_gdn_kernel, ncol=ncol, d=d),
        grid=(b, GRID_W // ncol),
        in_specs=[pl.BlockSpec((None, rows, ncol * d), lambda i, j: (i, 0, j)),
                  cm(hw), cm(hw), cm(hw),
                  pl.BlockSpec((1, GDN_DV), lambda i, j: (0, 0)),
                  pl.BlockSpec((hw, d), lambda i, j: (0, 0)), vec()],
        out_specs=pl.BlockSpec((None, rows, ncol * d), lambda i, j: (i, 0, j)),
        out_shape=jax.ShapeDtypeStruct((b, rows, GRID_W * d), F32),
        input_output_aliases={0: 0},
        compiler_params=_cparams(("parallel", "parallel")),
    )(x1v, o_f, o_b, z, norm_g, w_out[w:], gt)
    return x1v.reshape(b, t, d)


def _fold_kernel(wq_ref, k_ref, o_ref):
    o_ref[...] = lax.dot_general(wq_ref[...], k_ref[...], (((1,), (1,)), ((), ())),
                                 preferred_element_type=F32, precision=lax.Precision.HIGHEST)


def _fold_keys(wq, keys):
    d, n = wq.shape
    nblk = n // PEER_HALF
    return pl.pallas_call(
        _fold_kernel,
        grid=(nblk,),
        in_specs=[pl.BlockSpec((d, PEER_HALF), lambda j: (0, j)),
                  pl.BlockSpec((None, PEER_NKEYS, PEER_HALF), lambda j: (j % 2, 0, 0))],
        out_specs=pl.BlockSpec((d, PEER_NKEYS), lambda j: (0, j)),
        out_shape=jax.ShapeDtypeStruct((d, nblk * PEER_NKEYS), F32),
        compiler_params=_cparams(("parallel",)),
    )(wq, keys)


def _top16_rows(vals, s_scr, i_scr):
    n, p = vals.shape
    rid = lax.broadcasted_iota(jnp.int32, (n, p), 0)
    for r in range(PEER_TOPK):
        m = jnp.max(vals, axis=0, keepdims=True)
        idx = jnp.min(jnp.where(vals == m, rid, n), axis=0, keepdims=True)
        s_scr[r:r + 1, :] = m
        i_scr[r:r + 1, :] = idx
        vals = jnp.where(rid == idx, NEG_INF, vals)


def _peer_sel_kernel(x_ref, g_ref, sh_ref, sc_ref, wt_ref, h_ref, a_ref, b_ref, gate_ref,
                     sc_scr, s1, i1, s2, i2, ts, tp):
    k = PEER_TOPK
    h = _rms(x_ref[...], g_ref[...]) * (1.0 + sc_ref[...]) + sh_ref[...]
    hb = h.astype(BF16)
    h_ref[...] = hb
    sc_scr[...] = _dot_nt(wt_ref[...], hb)
    p = hb.shape[0]

    sub = lax.broadcasted_iota(jnp.int32, (8, p), 0)
    sub16 = lax.broadcasted_iota(jnp.int32, (16, p), 0)

    def head(hd, carry):
        base = pl.multiple_of(hd * 2 * PEER_NKEYS, 2 * PEER_NKEYS)
        _top16_rows(sc_scr[pl.ds(base, PEER_NKEYS), :], s1, i1)
        _top16_rows(sc_scr[pl.ds(base + PEER_NKEYS, PEER_NKEYS), :], s2, i2)
        s1v, s2v = s1[...], s2[...]
        cands = [s1v[0:1, :] + s2v]
        poss = [sub16]
        for r in range(1, 8):
            cmax = k // (r + 1)
            cands.append(jnp.where(sub < cmax, s1v[r:r + 1, :] + s2v[0:8, :], NEG_INF))
            poss.append(r * k + sub)
        cands.append(s1v[8:16, :] + s2v[0:1, :])
        poss.append((sub + 8) * k)
        cand = jnp.concatenate(cands, axis=0)
        pos = jnp.concatenate(poss, axis=0)
        big = k * k
        for r in range(k):
            m = jnp.max(cand, axis=0, keepdims=True)
            sel = jnp.min(jnp.where(cand == m, pos, big), axis=0, keepdims=True)
            ts[r:r + 1, :] = m
            tp[r:r + 1, :] = sel
            cand = jnp.where(pos == sel, NEG_INF, cand)
        top_s, top_p = ts[...], tp[...]
        rr = top_p >> 4
        cc = top_p & (k - 1)
        i1v, i2v = i1[...], i2[...]
        av = jnp.zeros_like(top_p)
        bv = jnp.zeros_like(top_p)
        for q in range(k):
            av = jnp.where(rr == q, i1v[q:q + 1, :], av)
            bv = jnp.where(cc == q, i2v[q:q + 1, :], bv)
        e = jnp.exp(top_s - top_s[0:1, :])
        gate = e / jnp.sum(e, axis=0, keepdims=True)
        o = pl.multiple_of(hd * k, k)
        a_ref[pl.ds(o, k), :] = av
        b_ref[pl.ds(o, k), :] = bv
        gate_ref[pl.ds(o, k), :] = gate
        return carry

    lax.fori_loop(0, PEER_HEADS, head, 0)


def _peer_select(x1, g, sh, sc, wt, *, tile):
    b, t, d = x1.shape
    nt = t // tile
    nk = PEER_HEADS * PEER_TOPK
    nrow = wt.shape[0]
    vec = lambda: pl.BlockSpec((None, 1, d), lambda i, j: (i, 0, 0))
    tr = lambda: pl.BlockSpec((None, nk, tile), lambda i, j: (i, 0, j))
    k = PEER_TOPK
    return pl.pallas_call(
        _peer_sel_kernel,
        grid=(b, nt),
        in_specs=[pl.BlockSpec((None, tile, d), lambda i, j: (i, j, 0)),
                  pl.BlockSpec((1, d), lambda i, j: (0, 0)), vec(), vec(),
                  pl.BlockSpec((nrow, d), lambda i, j: (0, 0))],
        out_specs=[pl.BlockSpec((None, tile, d), lambda i, j: (i, j, 0)), tr(), tr(), tr()],
        out_shape=[jax.ShapeDtypeStruct((b, t, d), BF16),
                   jax.ShapeDtypeStruct((b, nk, t), jnp.int32),
                   jax.ShapeDtypeStruct((b, nk, t), jnp.int32),
                   jax.ShapeDtypeStruct((b, nk, t), F32)],
        scratch_shapes=[pltpu.VMEM((nrow, tile), F32),
                        pltpu.VMEM((k, tile), F32), pltpu.VMEM((k, tile), jnp.int32),
                        pltpu.VMEM((k, tile), F32), pltpu.VMEM((k, tile), jnp.int32),
                        pltpu.VMEM((k, tile), F32), pltpu.VMEM((k, tile), jnp.int32)],
        compiler_params=_cparams(("parallel", "parallel")),
    )(x1, g.reshape(1, d), sh, sc, wt)


def _peer_mix_kernel(h_ref, a_ref, b_ref, gate_ref, u_ref, v_ref, x_ref, gt_ref, fg_ref, o_ref,
                     m_scr, acc, *, tile, n_steps):
    nk = PEER_NKEYS
    step = pl.program_id(2)

    @pl.when(step == 0)
    def _():
        sub = lax.broadcasted_iota(jnp.int32, (nk, a_ref.shape[1]), 0)

        def build(p, carry):
            arow = a_ref[pl.ds(p, 1), :]
            brow = b_ref[pl.ds(p, 1), :]
            grow = gate_ref[pl.ds(p, 1), :]
            xa = jnp.where(sub == arow, 1.0, 0.0).astype(BF16)
            yb = jnp.where(sub == brow, grow, 0.0).astype(BF16)
            m_scr[pl.ds(pl.multiple_of(p * nk, nk), nk), :] = _dot_nt(xa, yb)
            return carry

        lax.fori_loop(0, tile, build, 0)
        acc[...] = jnp.zeros(acc.shape, F32)

    s = _dot_nt(h_ref[...], u_ref[...])
    act = jax.nn.gelu(s)
    m0 = m_scr[pl.ds(2 * step, tile, stride=nk), :]
    m1 = m_scr[pl.ds(2 * step + 1, tile, stride=nk), :]
    wgt = (act * jnp.concatenate([m0, m1], axis=1)).astype(BF16)
    acc[...] += _dot(wgt, v_ref[...])

    @pl.when(step == n_steps - 1)
    def _():
        x2 = x_ref[...] + gt_ref[...] * acc[...]
        o_ref[...] = _rms(x2, fg_ref[...])


def _peer_mix(hb, aidx, bidx, gate, u, v, x1, gt, final_g, *, tile):
    b, t, d = x1.shape
    nt = t // tile
    nk = PEER_NKEYS
    npk = aidx.shape[-1]
    n_steps = nk // 2
    tok = lambda wd: pl.BlockSpec((None, tile, wd), lambda i, j, s: (i, j, 0))
    tab = lambda: pl.BlockSpec((2 * nk, d), lambda i, j, s: (s, 0))
    return pl.pallas_call(
        functools.partial(_peer_mix_kernel, tile=tile, n_steps=n_steps),
        grid=(b, nt, n_steps),
        in_specs=[tok(d), tok(npk), tok(npk), tok(npk), tab(), tab(), tok(d),
                  pl.BlockSpec((None, 1, d), lambda i, j, s: (i, 0, 0)),
                  pl.BlockSpec((1, d), lambda i, j, s: (0, 0))],
        out_specs=tok(d),
        out_shape=jax.ShapeDtypeStruct((b, t, d), F32),
        scratch_shapes=[pltpu.VMEM((tile * nk, nk), F32), pltpu.VMEM((tile, d), F32)],
        compiler_params=_cparams(("parallel", "parallel", "arbitrary")),
    )(hb, aidx, bidx, gate, u, v, x1, gt, final_g.reshape(1, d))


def _block_diag(w):
    n, e, _ = w.shape
    eye = jnp.eye(n, dtype=w.dtype)
    return (eye[:, None, :, None] * w[:, :, None, :]).reshape(n * e, n * e)


def _mix_sequence(x3, ctx_mode, params, states, mods, *, rg_tb, gdn_tb, proj_tile):
    (norm1_g, w_rg, w_gdn, rg_conv_w, rg_conv_b, wg, gate_b, c_lam, gdn_conv_w, ea_row, dtb_row) = params
    sh1, sc1, mod_row = mods
    rg_h0_f, rg_h0_b, gdn_s0 = states
    (p_rg,) = _project(x3, norm1_g, sh1, sc1, mod_row, w_rg, (2 * RG_WIDTH,), colmajor=False, tile=proj_tile)
    qkv, z, ab = _project(x3, norm1_g, sh1, sc1, mod_row, w_gdn, (3 * GDN_WIDTH, GDN_WIDTH, AB_PAD),
                          colmajor=not ctx_mode, tile=proj_tile)
    h_f, st_f = _rglru_pass(p_rg, rg_conv_w, rg_conv_b, wg[0], gate_b[0], c_lam[0], rg_h0_f, None,
                            reverse=False, tb=rg_tb)
    y_rg, st_b = _rglru_pass(p_rg, rg_conv_w, rg_conv_b, wg[1], gate_b[1], c_lam[1], rg_h0_b, h_f,
                             reverse=True, tb=rg_tb)
    prep = _gdn_prep(qkv, ab, gdn_conv_w, ea_row, dtb_row, tb=gdn_tb)
    o_f, o_b, s_fin = _gdn_scan(prep, gdn_s0)
    return y_rg, o_f, o_b, z, (st_f, st_b, s_fin)


def kernel(x, c, ctx, c_ctx, w_mod, b_mod, norm1_g, norm2_g, w_in, rg_conv_w, rg_conv_b, rg_gate_w, rg_gate_b,
           rg_lambda, gdn_conv_w, gdn_a_log, gdn_dt_bias, gdn_norm_g, w_out, peer_wq, peer_keys, peer_u, peer_v,
           final_g):
    b, t, d = x.shape
    depth = w_mod.shape[0]
    assert depth == 1, "context residual stream update is only needed for depth > 1"
    l = 0
    w = RG_WIDTH

    cc = jnp.zeros((16, d), F32).at[:b].set(c).at[b].set(c_ctx)
    w_rg = w_in[l][:, :2 * w].astype(BF16)
    n_ab = w_in.shape[2] - 2 * w - 4 * GDN_WIDTH
    w_gdn = jnp.concatenate([w_in[l][:, 2 * w:2 * w + 4 * GDN_WIDTH],
                             jnp.pad(w_in[l][:, 2 * w + 4 * GDN_WIDTH:], ((0, 0), (0, AB_PAD - n_ab)))],
                            axis=1).astype(BF16)
    wg = jnp.stack([jnp.concatenate([_block_diag(rg_gate_w[l, dr, 0]), _block_diag(rg_gate_w[l, dr, 1])], axis=1)
                    for dr in range(2)]).astype(BF16)
    gate_b = rg_gate_b[l].reshape(2, 1, 2 * w)
    c_lam = (-RG_C * jax.nn.softplus(-rg_lambda[l])).reshape(2, 1, w)
    ea = jnp.exp(gdn_a_log[l])
    pad4 = jnp.zeros((2, GDN_HEADS), F32)
    ea_row = jnp.pad(jnp.concatenate([ea, pad4], axis=1).reshape(1, -1), ((0, 0), (0, AB_PAD - n_ab)))
    dtb_row = jnp.pad(jnp.concatenate([gdn_dt_bias[l], pad4], axis=1).reshape(1, -1), ((0, 0), (0, AB_PAD - n_ab)))
    params = (norm1_g[l], w_rg, w_gdn, rg_conv_w[l], rg_conv_b[l].reshape(1, w), wg, gate_b, c_lam,
              gdn_conv_w[l], ea_row, dtb_row)

    mod = _modulation(cc, w_mod[l], b_mod[l])
    sh1, sc1, gt1, sh2, sc2, gt2 = [mod[:, i * d:(i + 1) * d].reshape(16, 1, d) for i in range(6)]

    tc = ctx.shape[1]
    zero_states = (jnp.zeros((b, 1, w), F32), jnp.zeros((b, 1, w), F32),
                   jnp.zeros((b, 2, GDN_HEADS, GDN_DK, GDN_DV), F32))
    ctx_tb = min(tc, 256)
    _, _, _, _, ctx_states = _mix_sequence(ctx, True, params, zero_states, (sh1, sc1, lambda i: b),
                                           rg_tb=ctx_tb, gdn_tb=min(tc, 128), proj_tile=ctx_tb)

    tile = min(t, 512)
    y_rg, o_f, o_b, z, _ = _mix_sequence(x, False, params, ctx_states, (sh1, sc1, lambda i: i),
                                         rg_tb=min(t, 256), gdn_tb=128, proj_tile=tile)
    x1 = _out_project(x, y_rg, o_f, o_b, z, gdn_norm_g[l].reshape(1, GDN_DV), w_out[l].astype(BF16), gt1, tile=tile)

    wfold_t = _fold_keys(peer_wq[l], peer_keys[l]).T.astype(BF16)
    sel_tile = min(t, 256)
    hb, aidx, bidx, gate = _peer_select(x1, norm2_g[l], sh2, sc2, wfold_t, tile=sel_tile)
    tr = lambda a: jnp.swapaxes(a, 1, 2)
    out = _peer_mix(hb, tr(aidx), tr(bidx), tr(gate), peer_u[l].astype(BF16), peer_v[l].astype(BF16),
                    x1, gt2, final_g, tile=tile)
    return out
```

```python
import functools
import math

import jax
import jax.numpy as jnp
from jax import lax
from jax.experimental import pallas as pl
from jax.experimental.pallas import tpu as pltpu

F32 = jnp.float32
BF16 = jnp.bfloat16

GRID_W = 64
EPS = 1e-6
RG_WIDTH = 512
RG_BLOCKS = 8
RG_C = 8.0
GDN_HEADS = 4
GDN_DK = 128
GDN_DV = 128
GDN_WIDTH = GDN_HEADS * GDN_DV
GDN_CHUNK = 64
GDN_SUB = 16
AB_PAD = 128
PEER_HEADS = 8
PEER_NKEYS = 128
PEER_HALF = 128
PEER_TOPK = 16
NEG_INF = float("-inf")

VMEM_LIMIT = 56 * 1024 * 1024


def _cparams(sem):
    return pltpu.CompilerParams(dimension_semantics=sem, vmem_limit_bytes=VMEM_LIMIT)


def _dot(a, b):
    return jnp.dot(a, b, preferred_element_type=F32)


def _dot_nt(a, b):
    return lax.dot_general(a, b, (((1,), (1,)), ((), ())), preferred_element_type=F32)


def _dot_tn(a, b):
    return lax.dot_general(a, b, (((0,), (0,)), ((), ())), preferred_element_type=F32)


def _split3(x):
    hi = x.astype(BF16)
    r = x - hi.astype(F32)
    mid = r.astype(BF16)
    lo = (r - mid.astype(F32)).astype(BF16)
    return hi, mid, lo


def _dot_sel(m, x):
    hi, mid, lo = _split3(x)
    return _dot(m, lo) + _dot(m, mid) + _dot(m, hi)


def _silu(x):
    return x * jax.nn.sigmoid(x)


def _softplus(x):
    return jnp.maximum(x, 0.0) + jnp.log1p(jnp.exp(-jnp.abs(x)))


def _rms(x, g):
    return x * lax.rsqrt(jnp.mean(x * x, axis=-1, keepdims=True) + EPS) * g


def _mod_kernel(c_ref, w_ref, b_ref, o_ref):
    s = _silu(c_ref[...])
    o_ref[...] = jnp.dot(s, w_ref[...], preferred_element_type=F32,
                         precision=lax.Precision.HIGHEST) + b_ref[...]


def _modulation(cc, w_mod, b_mod):
    m, d = cc.shape
    n = w_mod.shape[1]
    tn = 1536
    return pl.pallas_call(
        _mod_kernel,
        grid=(n // tn,),
        in_specs=[pl.BlockSpec((m, d), lambda j: (0, 0)),
                  pl.BlockSpec((d, tn), lambda j: (0, j)),
                  pl.BlockSpec((1, tn), lambda j: (0, j))],
        out_specs=pl.BlockSpec((m, tn), lambda j: (0, j)),
        out_shape=jax.ShapeDtypeStruct((m, n), F32),
        compiler_params=_cparams(("arbitrary",)),
    )(cc, w_mod, b_mod.reshape(1, n))


def _proj_kernel(x_ref, g_ref, sh_ref, sc_ref, w_ref, *o_refs, ncol, d, widths):
    x = x_ref[...]
    if ncol > 1:
        x = jnp.concatenate([x[:, j * d:(j + 1) * d] for j in range(ncol)], axis=0)
    h = _rms(x, g_ref[...]) * (1.0 + sc_ref[...]) + sh_ref[...]
    o = _dot(h.astype(BF16), w_ref[...])
    off = 0
    for o_ref, wd in zip(o_refs, widths):
        o_ref[...] = o[:, off:off + wd]
        off += wd


def _project(x3, g, sh, sc, mod_row, w, widths, *, colmajor, tile):
    b, t, d = x3.shape
    n = w.shape[1]
    if colmajor:
        rows = t // GRID_W
        ncol = tile // rows
        xv = x3.reshape(b, rows, GRID_W * d)
        x_spec = pl.BlockSpec((None, rows, ncol * d), lambda i, j: (i, 0, j))
        nt = GRID_W // ncol
    else:
        ncol = 1
        xv = x3
        x_spec = pl.BlockSpec((None, tile, d), lambda i, j: (i, j, 0))
        nt = t // tile
    vec = lambda: pl.BlockSpec((None, 1, d), lambda i, j: (mod_row(i), 0, 0))
    outs = pl.pallas_call(
        functools.partial(_proj_kernel, ncol=ncol, d=d, widths=tuple(widths)),
        grid=(b, nt),
        in_specs=[x_spec, pl.BlockSpec((1, d), lambda i, j: (0, 0)), vec(), vec(),
                  pl.BlockSpec((d, n), lambda i, j: (0, 0))],
        out_specs=[pl.BlockSpec((None, tile, wd), lambda i, j: (i, j, 0)) for wd in widths],
        out_shape=[jax.ShapeDtypeStruct((b, t, wd), F32) for wd in widths],
        compiler_params=_cparams(("parallel", "parallel")),
    )(xv, g.reshape(1, d), sh, sc, w)
    return outs


def _conv4(cur, prev8, nxt8, w_ref, first, last):
    tb = cur.shape[0]
    prev8 = jnp.where(first, 0.0, prev8)
    nxt8 = jnp.where(last, 0.0, nxt8)
    ext = jnp.concatenate([prev8, cur, nxt8], axis=0)
    n = tb + 16
    xm2 = pltpu.roll(ext, 2, 0)[8:8 + tb]
    xm1 = pltpu.roll(ext, 1, 0)[8:8 + tb]
    xp1 = pltpu.roll(ext, n - 1, 0)[8:8 + tb]
    return (w_ref[0:1, :] * xm2 + w_ref[1:2, :] * xm1 + w_ref[2:3, :] * cur + w_ref[3:4, :] * xp1)


def _halo_specs(tb, t, width, lane_blk, tmap):
    r = tb // 8
    nb8 = t // 8
    return [pl.BlockSpec((None, tb, width), lambda i, j: (i, tmap(j), lane_blk)),
            pl.BlockSpec((None, 8, width), lambda i, j: (i, jnp.maximum(tmap(j) * r - 1, 0), lane_blk)),
            pl.BlockSpec((None, 8, width), lambda i, j: (i, jnp.minimum((tmap(j) + 1) * r, nb8 - 1), lane_blk))]


def _rglru_kernel(*refs, tb, nt, reverse):
    if reverse:
        (u_ref, up_ref, un_ref, cw_ref, cb_ref, wg_ref, gb_ref, cl_ref, h0_ref, gate_ref, hf_ref,
         y_ref, st_ref, carry) = refs
    else:
        (u_ref, up_ref, un_ref, cw_ref, cb_ref, wg_ref, gb_ref, cl_ref, h0_ref,
         y_ref, st_ref, carry) = refs
    j = pl.program_id(1)
    tblk = (nt - 1 - j) if reverse else j
    w = u_ref.shape[-1]

    @pl.when(j == 0)
    def _():
        carry[...] = h0_ref[...]

    u = u_ref[...]
    xc = _conv4(u, up_ref[...], un_ref[...], cw_ref, tblk == 0, tblk == nt - 1) + cb_ref[...]
    gates = _dot(xc.astype(BF16), wg_ref[...]) + gb_ref[...]
    r = jax.nn.sigmoid(gates[:, :w])
    ig = jax.nn.sigmoid(gates[:, w:])
    log_a = r * cl_ref[...]
    a = jnp.exp(log_a)
    th = jnp.tanh(log_a)
    bb = jnp.sqrt(-2.0 * th / (1.0 - th)) * (ig * xc)

    row = lax.broadcasted_iota(jnp.int32, (tb, w), 0)
    s = 1
    while s < tb:
        if reverse:
            a_sh = pltpu.roll(a, tb - s, 0)
            b_sh = pltpu.roll(bb, tb - s, 0)
            ok = row < tb - s
        else:
            a_sh = pltpu.roll(a, s, 0)
            b_sh = pltpu.roll(bb, s, 0)
            ok = row >= s
        a_sh = jnp.where(ok, a_sh, 1.0)
        b_sh = jnp.where(ok, b_sh, 0.0)
        bb = a * b_sh + bb
        a = a * a_sh
        s *= 2
    h = a * carry[...] + bb
    carry[...] = h[0:1, :] if reverse else h[tb - 1:tb, :]
    st_ref[...] = carry[...]
    if reverse:
        y_ref[...] = ((hf_ref[...] + h) * jax.nn.gelu(gate_ref[...])).astype(y_ref.dtype)
    else:
        y_ref[...] = h


def _rglru_pass(p_rg, conv_w, conv_b, wg, gate_b, c_lam, h0, hf, *, reverse, tb):
    b, t, w2 = p_rg.shape
    w = w2 // 2
    nt = t // tb
    tmap = (lambda j: nt - 1 - j) if reverse else (lambda j: j)
    const = lambda shape: pl.BlockSpec(shape, lambda i, j: (0,) * len(shape))
    in_specs = _halo_specs(tb, t, w, 0, tmap) + [
        const((4, w)), const((1, w)), const((w, 2 * w)), const((1, 2 * w)), const((1, w)),
        pl.BlockSpec((None, 1, w), lambda i, j: (i, 0, 0))]
    args = [p_rg, p_rg, p_rg, conv_w, conv_b, wg, gate_b, c_lam, h0]
    if reverse:
        in_specs += [pl.BlockSpec((None, tb, w), lambda i, j: (i, tmap(j), 1)),
                     pl.BlockSpec((None, tb, w), lambda i, j: (i, tmap(j), 0))]
        args += [p_rg, hf]
    y, st = pl.pallas_call(
        functools.partial(_rglru_kernel, tb=tb, nt=nt, reverse=reverse),
        grid=(b, nt),
        in_specs=in_specs,
        out_specs=[pl.BlockSpec((None, tb, w), lambda i, j: (i, tmap(j), 0)),
                   pl.BlockSpec((None, 1, w), lambda i, j: (i, 0, 0))],
        out_shape=[jax.ShapeDtypeStruct((b, t, w), BF16 if reverse else F32),
                   jax.ShapeDtypeStruct((b, 1, w), F32)],
        scratch_shapes=[pltpu.VMEM((1, w), F32)],
        compiler_params=_cparams(("parallel", "arbitrary")),
    )(*args)
    return y, st


def _bmm(a, b):
    return lax.dot_general(a, b, (((2,), (1,)), ((0,), (0,))), preferred_element_type=F32)


def _bmm_nt(a, b):
    return lax.dot_general(a, b, (((2,), (2,)), ((0,), (0,))), preferred_element_type=F32)


def _unit_tri_inverse(a, diag_mask):
    c = a.shape[-1]
    eye = (lax.broadcasted_iota(jnp.int32, (1, c, c), 1) == lax.broadcasted_iota(jnp.int32, (1, c, c), 2)).astype(F32)
    mm = lambda p, q: _bmm(p.astype(BF16), q.astype(BF16))
    ad = jnp.where(diag_mask, a, 0.0)
    x = eye - ad
    pw = ad
    k = 2
    while k < GDN_SUB:
        pw = mm(pw, pw)
        x = x + mm(x, pw)
        k *= 2
    n = mm(x, a - ad)
    nblk = c // GDN_SUB
    y = eye - n
    pw = n
    k = 2
    while k < nblk:
        pw = mm(pw, pw)
        y = y + mm(y, pw)
        k *= 2
    return mm(y, x)


def _gdn_prep_kernel(qkv_ref, qp_ref, qn_ref, ab_ref, cw_ref, ea_ref, dtb_ref,
                     qg_ref, kd_ref, kc_ref, wv_ref, at_ref, eg_ref, *, tb, nt):
    c = GDN_CHUNK
    j = pl.program_id(1)
    hw = GDN_WIDTH
    x = _silu(_conv4(qkv_ref[...], qp_ref[...], qn_ref[...], cw_ref, j == 0, j == nt - 1))

    ab = ab_ref[...]
    col = lax.broadcasted_iota(jnp.int32, (tb, AB_PAD), 1)
    is_a = (col & 4) == 0
    gbv = jnp.where(is_a, -ea_ref[...] * _softplus(ab + dtb_ref[...]), jax.nn.sigmoid(ab))

    ri = lax.broadcasted_iota(jnp.int32, (tb, tb), 0)
    ci = lax.broadcasted_iota(jnp.int32, (tb, tb), 1)
    same = (ri // c) == (ci // c)
    l_f = (same & (ci <= ri)).astype(BF16)
    l_b = (same & (ci >= ri)).astype(BF16)
    l_t = same.astype(BF16)
    gcum = jnp.where(col < 8, _dot_sel(l_f, gbv), _dot_sel(l_b, gbv))
    gtot = _dot_sel(l_t, gbv)

    n_combo = 2 * GDN_HEADS
    er = lax.broadcasted_iota(jnp.int32, (AB_PAD, n_combo * 128), 0)
    ec = lax.broadcasted_iota(jnp.int32, (AB_PAD, n_combo * 128), 1) // 128
    src_a = (ec // GDN_HEADS) * 8 + (ec % GDN_HEADS)
    e_a = (er == src_a).astype(BF16)
    e_b = (er == src_a + 4).astype(BF16)
    gc_all = _bcast_cols(gcum, e_a)
    gt_all = _bcast_cols(gtot, e_a)
    be_all = _bcast_cols(gbv, e_b)

    qs, ks, vs = [], [], []
    for hd in range(GDN_HEADS):
        q = x[:, hd * GDN_DK:(hd + 1) * GDN_DK]
        k = x[:, hw + hd * GDN_DK: hw + (hd + 1) * GDN_DK]
        qs.append(q * lax.rsqrt(jnp.sum(q * q, axis=-1, keepdims=True) + EPS) * (GDN_DK ** -0.5))
        ks.append(k * lax.rsqrt(jnp.sum(k * k, axis=-1, keepdims=True) + EPS))
        vs.append(x[:, 2 * hw + hd * GDN_DV: 2 * hw + (hd + 1) * GDN_DV])
    inst = [(ch, dr, hd) for ch in range(tb // c) for dr in range(2) for hd in range(GDN_HEADS)]
    n_inst = len(inst)
    rows = lambda ch: slice(ch * c, (ch + 1) * c)
    lanes = lambda dr, hd: slice((dr * GDN_HEADS + hd) * 128, (dr * GDN_HEADS + hd + 1) * 128)
    stack = lambda f: jnp.stack([f(ch, dr, hd) for ch, dr, hd in inst], axis=0)
    qq = stack(lambda ch, dr, hd: qs[hd][rows(ch)])
    kk = stack(lambda ch, dr, hd: ks[hd][rows(ch)])
    vv = stack(lambda ch, dr, hd: vs[hd][rows(ch)])
    gcb = stack(lambda ch, dr, hd: gc_all[rows(ch), lanes(dr, hd)])
    gtb = stack(lambda ch, dr, hd: gt_all[rows(ch), lanes(dr, hd)])
    beb = stack(lambda ch, dr, hd: be_all[rows(ch), lanes(dr, hd)])

    fwd = ((lax.broadcasted_iota(jnp.int32, (n_inst, c, 128), 0) // GDN_HEADS) % 2) == 0
    ii = lax.broadcasted_iota(jnp.int32, (n_inst, c, 128), 1)
    jj = lax.broadcasted_iota(jnp.int32, (n_inst, c, 128), 2)
    incl = (fwd & (jj <= ii)) | (jnp.logical_not(fwd) & (jj >= ii) & (jj < c))
    fwd_c = ((lax.broadcasted_iota(jnp.int32, (n_inst, c, c), 0) // GDN_HEADS) % 2) == 0
    i64 = lax.broadcasted_iota(jnp.int32, (n_inst, c, c), 1)
    j64 = lax.broadcasted_iota(jnp.int32, (n_inst, c, c), 2)
    strict = (fwd_c & (j64 < i64)) | (jnp.logical_not(fwd_c) & (j64 > i64))
    diag_blk = (i64 // GDN_SUB) == (j64 // GDN_SUB)

    eg = jnp.exp(gcb)
    kb = kk * beb
    diag = jnp.where(ii == jj, gcb, 0.0)
    gcr = _dot_sel(jnp.ones((c, c), BF16), jnp.concatenate([diag[g] for g in range(n_inst)], axis=1))
    gcr = jnp.stack([gcr[:, g * 128:(g + 1) * 128] for g in range(n_inst)], axis=0)
    decay = jnp.where(incl, jnp.exp(jnp.where(incl, gcb - gcr, 0.0)), 0.0)
    kpad = jnp.concatenate([kk.astype(BF16), jnp.zeros((n_inst, c, GDN_DK), BF16)], axis=1)
    qk = _bmm_nt(jnp.concatenate([qq, kb], axis=1).astype(BF16), kpad)
    attn = (qk[:, :c] * decay).astype(BF16)
    a_mat = jnp.where(strict, (qk[:, c:] * decay)[:, :, :c], 0.0)
    tinv = _unit_tri_inverse(a_mat, diag_blk)
    rhs = jnp.concatenate([vv * beb, kb * eg], axis=2)
    sol = _bmm(tinv.astype(BF16), rhs.astype(BF16))
    qg = (qq * eg).astype(BF16)
    kd = (kk * jnp.exp(gtb - gcb)).astype(BF16)
    egt = jnp.exp(gtb[:, 0:8, :])
    for g, (ch, dr, hd) in enumerate(inst):
        rs = rows(ch)
        ls = slice(hd * 128, (hd + 1) * 128)
        qg_ref[dr, rs, ls] = qg[g]
        kd_ref[dr, rs, ls] = kd[g]
        kc_ref[dr, rs, ls] = sol[g, :, GDN_DV:].astype(BF16)
        wv_ref[dr, rs, ls] = sol[g, :, :GDN_DV]
        at_ref[dr, rs, ls] = attn[g]
        eg_ref[dr, ch, :, ls] = egt[g]


def _bcast_cols(x, sel):
    hi, mid, lo = _split3(x)
    return _dot(lo, sel) + _dot(mid, sel) + _dot(hi, sel)


def _gdn_prep(qkv, ab, conv_w, ea_row, dtb_row, *, tb):
    b, t, cw = qkv.shape
    nt = t // tb
    nch = t // GDN_CHUNK
    cpb = tb // GDN_CHUNK
    const = lambda shape: pl.BlockSpec(shape, lambda i, j: (0,) * len(shape))
    hw = GDN_WIDTH
    big = lambda: pl.BlockSpec((2, None, tb, hw), lambda i, j: (0, i, j, 0))
    outs = pl.pallas_call(
        functools.partial(_gdn_prep_kernel, tb=tb, nt=nt),
        grid=(b, nt),
        in_specs=_halo_specs(tb, t, cw, 0, lambda j: j) + [
            pl.BlockSpec((None, tb, AB_PAD), lambda i, j: (i, j, 0)),
            const((4, cw)), const((1, AB_PAD)), const((1, AB_PAD))],
        out_specs=[big(), big(), big(), big(), big(),
                   pl.BlockSpec((2, None, cpb, 8, hw), lambda i, j: (0, i, j, 0, 0))],
        out_shape=[jax.ShapeDtypeStruct((2, b, t, hw), BF16),
                   jax.ShapeDtypeStruct((2, b, t, hw), BF16),
                   jax.ShapeDtypeStruct((2, b, t, hw), BF16),
                   jax.ShapeDtypeStruct((2, b, t, hw), F32),
                   jax.ShapeDtypeStruct((2, b, t, hw), BF16),
                   jax.ShapeDtypeStruct((2, b, nch, 8, hw), F32)],
        compiler_params=_cparams(("parallel", "parallel")),
    )(qkv, qkv, qkv, ab, conv_w, ea_row, dtb_row)
    return outs


def _gdn_scan_kernel(qg_f, kd_f, kc_f, wv_f, at_f, eg_f, qg_b, kd_b, kc_b, wv_b, at_b, eg_b, s0_ref,
                     of_ref, ob_ref, sfin_ref, s_scr):
    c = GDN_CHUNK
    j = pl.program_id(1)

    @pl.when(j == 0)
    def _():
        s_scr[...] = s0_ref[...]

    dirs = ((qg_f, kd_f, kc_f, wv_f, at_f, eg_f), (qg_b, kd_b, kc_b, wv_b, at_b, eg_b))
    chains = [(dr, hd) for dr in range(2) for hd in range(GDN_HEADS)]
    ls = lambda hd: slice(hd * 128, (hd + 1) * 128)
    stack = lambda f: jnp.stack([f(dirs[dr], ls(hd)) for dr, hd in chains], axis=0)
    kq = stack(lambda r, l: jnp.concatenate([r[2][:, l], r[0][:, l]], axis=0))
    wv = stack(lambda r, l: r[3][:, l])
    at = stack(lambda r, l: r[4][:, l][:, :c])
    eg = stack(lambda r, l: r[5][0:1, l])
    s = s_scr[...].reshape(2 * GDN_HEADS, GDN_DK, GDN_DV)
    r = _bmm(kq, s.astype(BF16))
    vb = (wv - r[:, :c]).astype(BF16)
    o = r[:, c:] + _bmm(at, vb)
    for g, (dr, hd) in enumerate(chains):
        (of_ref, ob_ref)[dr][:, ls(hd)] = o[g]
        s_scr[dr, hd] = s[g] * eg[g] + _dot_tn(dirs[dr][1][:, ls(hd)], vb[g])
    sfin_ref[...] = s_scr[...]


def _gdn_scan(prep, s0):
    qg, kd, kc, wv, at, eg = prep
    _, b, t, hw = qg.shape
    c = GDN_CHUNK
    nch = t // c
    fw = lambda: pl.BlockSpec((None, None, c, hw), lambda i, j: (0, i, j, 0))
    bw = lambda: pl.BlockSpec((None, None, c, hw), lambda i, j: (1, i, nch - 1 - j, 0))
    egf = pl.BlockSpec((None, None, None, 8, hw), lambda i, j: (0, i, j, 0, 0))
    egb = pl.BlockSpec((None, None, None, 8, hw), lambda i, j: (1, i, nch - 1 - j, 0, 0))
    st = pl.BlockSpec((None, 2, GDN_HEADS, GDN_DK, GDN_DV), lambda i, j: (i, 0, 0, 0, 0))
    o_f, o_b, s_fin = pl.pallas_call(
        _gdn_scan_kernel,
        grid=(b, nch),
        in_specs=[fw(), fw(), fw(), fw(), fw(), egf, bw(), bw(), bw(), bw(), bw(), egb, st],
        out_specs=[pl.BlockSpec((None, c, hw), lambda i, j: (i, j, 0)),
                   pl.BlockSpec((None, c, hw), lambda i, j: (i, nch - 1 - j, 0)),
                   st],
        out_shape=[jax.ShapeDtypeStruct((b, t, hw), F32), jax.ShapeDtypeStruct((b, t, hw), F32),
                   jax.ShapeDtypeStruct((b, 2, GDN_HEADS, GDN_DK, GDN_DV), F32)],
        scratch_shapes=[pltpu.VMEM((2, GDN_HEADS, GDN_DK, GDN_DV), F32)],
        compiler_params=_cparams(("parallel", "arbitrary")),
    )(qg, kd, kc, wv, at, eg, qg, kd, kc, wv, at, eg, s0)
    return o_f, o_b, s_fin


def _out_rg_kernel(x_ref, y_ref, w_ref, gt_ref, o_ref):
    o_ref[...] = x_ref[...] + gt_ref[...] * _dot(y_ref[...], w_ref[...])


def _out_gdn_kernel(x_ref, of_ref, ob_ref, z_ref, ng_ref, w_ref, gt_ref, o_ref, *, ncol, d):
    o = of_ref[...] + ob_ref[...]
    z = z_ref[...]
    parts = []
    for hd in range(GDN_HEADS):
        ls = slice(hd * GDN_DV, (hd + 1) * GDN_DV)
        parts.append(_rms(o[:, ls], ng_ref[...]) * _silu(z[:, ls]))
    y = jnp.concatenate(parts, axis=1).astype(BF16)
    t2 = gt_ref[...] * _dot(y, w_ref[...])
    rows = t2.shape[0] // ncol
    t2 = jnp.concatenate([t2[j * rows:(j + 1) * rows] for j in range(ncol)], axis=1)
    o_ref[...] = x_ref[...] + t2


def _out_project(x3, y_rg, o_f, o_b, z, norm_g, w_out, gt, *, tile):
    b, t, d = x3.shape
    w = RG_WIDTH
    vec = lambda: pl.BlockSpec((None, 1, d), lambda i, j: (i, 0, 0))
    x1 = pl.pallas_call(
        _out_rg_kernel,
        grid=(b, t // tile),
        in_specs=[pl.BlockSpec((None, tile, d), lambda i, j: (i, j, 0)),
                  pl.BlockSpec((None, tile, w), lambda i, j: (i, j, 0)),
                  pl.BlockSpec((w, d), lambda i, j: (0, 0)), vec()],
        out_specs=pl.BlockSpec((None, tile, d), lambda i, j: (i, j, 0)),
        out_shape=jax.ShapeDtypeStruct((b, t, d), F32),
        compiler_params=_cparams(("parallel", "parallel")),
    )(x3, y_rg, w_out[:w], gt)
    rows = t // GRID_W
    ncol = tile // rows
    hw = GDN_WIDTH
    x1v = x1.reshape(b, rows, GRID_W * d)
    cm = lambda wd: pl.BlockSpec((None, tile, wd), lambda i, j: (i, j, 0))
    x1v = pl.pallas_call(
        functools.partial(_out_gdn_kernel, ncol=ncol, d=d),
        grid=(b, GRID_W // ncol),
        in_specs=[pl.BlockSpec((None, rows, ncol * d), lambda i, j: (i, 0, j)),
                  cm(hw), cm(hw), cm(hw),
                  pl.BlockSpec((1, GDN_DV), lambda i, j: (0, 0)),
                  pl.BlockSpec((hw, d), lambda i, j: (0, 0)), vec()],
        out_specs=pl.BlockSpec((None, rows, ncol * d), lambda i, j: (i, 0, j)),
        out_shape=jax.ShapeDtypeStruct((b, rows, GRID_W * d), F32),
        input_output_aliases={0: 0},
        compiler_params=_cparams(("parallel", "parallel")),
    )(x1v, o_f, o_b, z, norm_g, w_out[w:], gt)
    return x1v.reshape(b, t, d)


def _fold_kernel(wq_ref, k_ref, o_ref):
    o_ref[...] = lax.dot_general(wq_ref[...], k_ref[...], (((1,), (1,)), ((), ())),
                                 preferred_element_type=F32, precision=lax.Precision.HIGHEST)


def _fold_keys(wq, keys):
    d, n = wq.shape
    nblk = n // PEER_HALF
    return pl.pallas_call(
        _fold_kernel,
        grid=(nblk,),
        in_specs=[pl.BlockSpec((d, PEER_HALF), lambda j: (0, j)),
                  pl.BlockSpec((None, PEER_NKEYS, PEER_HALF), lambda j: (j % 2, 0, 0))],
        out_specs=pl.BlockSpec((d, PEER_NKEYS), lambda j: (0, j)),
        out_shape=jax.ShapeDtypeStruct((d, nblk * PEER_NKEYS), F32),
        compiler_params=_cparams(("parallel",)),
    )(wq, keys)


def _top16_rows(vals, s_scr, i_scr):
    n, p = vals.shape
    rid = lax.broadcasted_iota(jnp.int32, (n, p), 0)
    for r in range(PEER_TOPK):
        m = jnp.max(vals, axis=0, keepdims=True)
        idx = jnp.min(jnp.where(vals == m, rid, n), axis=0, keepdims=True)
        s_scr[r:r + 1, :] = m
        i_scr[r:r + 1, :] = idx
        vals = jnp.where(rid == idx, NEG_INF, vals)


def _peer_sel_kernel(x_ref, g_ref, sh_ref, sc_ref, wt_ref, h_ref, a_ref, b_ref, gate_ref,
                     sc_scr, s1, i1, s2, i2, ts, tp):
    k = PEER_TOPK
    h = _rms(x_ref[...], g_ref[...]) * (1.0 + sc_ref[...]) + sh_ref[...]
    hb = h.astype(BF16)
    h_ref[...] = hb
    sc_scr[...] = _dot_nt(wt_ref[...], hb)
    p = hb.shape[0]

    sub = lax.broadcasted_iota(jnp.int32, (8, p), 0)
    sub16 = lax.broadcasted_iota(jnp.int32, (16, p), 0)

    def head(hd, carry):
        base = pl.multiple_of(hd * 2 * PEER_NKEYS, 2 * PEER_NKEYS)
        _top16_rows(sc_scr[pl.ds(base, PEER_NKEYS), :], s1, i1)
        _top16_rows(sc_scr[pl.ds(base + PEER_NKEYS, PEER_NKEYS), :], s2, i2)
        s1v, s2v = s1[...], s2[...]
        cands = [s1v[0:1, :] + s2v]
        poss = [sub16]
        for r in range(1, 8):
            cmax = k // (r + 1)
            cands.append(jnp.where(sub < cmax, s1v[r:r + 1, :] + s2v[0:8, :], NEG_INF))
            poss.append(r * k + sub)
        cands.append(s1v[8:16, :] + s2v[0:1, :])
        poss.append((sub + 8) * k)
        cand = jnp.concatenate(cands, axis=0)
        pos = jnp.concatenate(poss, axis=0)
        big = k * k
        for r in range(k):
            m = jnp.max(cand, axis=0, keepdims=True)
            sel = jnp.min(jnp.where(cand == m, pos, big), axis=0, keepdims=True)
            ts[r:r + 1, :] = m
            tp[r:r + 1, :] = sel
            cand = jnp.where(pos == sel, NEG_INF, cand)
        top_s, top_p = ts[...], tp[...]
        rr = top_p >> 4
        cc = top_p & (k - 1)
        i1v, i2v = i1[...], i2[...]
        av = jnp.zeros_like(top_p)
        bv = jnp.zeros_like(top_p)
        for q in range(k):
            av = jnp.where(rr == q, i1v[q:q + 1, :], av)
            bv = jnp.where(cc == q, i2v[q:q + 1, :], bv)
        e = jnp.exp(top_s - top_s[0:1, :])
        gate = e / jnp.sum(e, axis=0, keepdims=True)
        o = pl.multiple_of(hd * k, k)
        a_ref[pl.ds(o, k), :] = av
        b_ref[pl.ds(o, k), :] = bv
        gate_ref[pl.ds(o, k), :] = gate
        return carry

    lax.fori_loop(0, PEER_HEADS, head, 0)


def _peer_select(x1, g, sh, sc, wt, *, tile):
    b, t, d = x1.shape
    nt = t // tile
    nk = PEER_HEADS * PEER_TOPK
    nrow = wt.shape[0]
    vec = lambda: pl.BlockSpec((None, 1, d), lambda i, j: (i, 0, 0))
    tr = lambda: pl.BlockSpec((None, nk, tile), lambda i, j: (i, 0, j))
    k = PEER_TOPK
    return pl.pallas_call(
        _peer_sel_kernel,
        grid=(b, nt),
        in_specs=[pl.BlockSpec((None, tile, d), lambda i, j: (i, j, 0)),
                  pl.BlockSpec((1, d), lambda i, j: (0, 0)), vec(), vec(),
                  pl.BlockSpec((nrow, d), lambda i, j: (0, 0))],
        out_specs=[pl.BlockSpec((None, tile, d), lambda i, j: (i, j, 0)), tr(), tr(), tr()],
        out_shape=[jax.ShapeDtypeStruct((b, t, d), BF16),
                   jax.ShapeDtypeStruct((b, nk, t), jnp.int32),
                   jax.ShapeDtypeStruct((b, nk, t), jnp.int32),
                   jax.ShapeDtypeStruct((b, nk, t), F32)],
        scratch_shapes=[pltpu.VMEM((nrow, tile), F32),
                        pltpu.VMEM((k, tile), F32), pltpu.VMEM((k, tile), jnp.int32),
                        pltpu.VMEM((k, tile), F32), pltpu.VMEM((k, tile), jnp.int32),
                        pltpu.VMEM((k, tile), F32), pltpu.VMEM((k, tile), jnp.int32)],
        compiler_params=_cparams(("parallel", "parallel")),
    )(x1, g.reshape(1, d), sh, sc, wt)


def _peer_mix_kernel(h_ref, a_ref, b_ref, gate_ref, u_ref, v_ref, x_ref, gt_ref, fg_ref, o_ref,
                     m_scr, acc, *, tile, n_steps, pairs):
    nk = PEER_NKEYS
    half = nk // 2
    step = pl.program_id(2)
    hi_mask = jnp.uint32(0xFFFF0000)

    @pl.when(step == 0)
    def _():
        sub = lax.broadcasted_iota(jnp.int32, (nk, a_ref.shape[1]), 0)
        a_of_row = jnp.where(sub < half, 2 * sub, 2 * (sub - half) + 1)

        def build(p, carry):
            arow = a_ref[pl.ds(p, 1), :]
            brow = b_ref[pl.ds(p, 1), :]
            grow = gate_ref[pl.ds(p, 1), :]
            xa = jnp.where(a_of_row == arow, 1.0, 0.0).astype(BF16)
            yb = jnp.where(sub == brow, grow, 0.0).astype(BF16)
            m = _dot_nt(xa, yb).astype(BF16).astype(F32)
            bits = lax.bitcast_convert_type(m, jnp.uint32)
            m_scr[pl.ds(pl.multiple_of(p * half, half), half), :] = (bits[half:] & hi_mask) | (bits[:half] >> 16)
            return carry

        lax.fori_loop(0, tile, build, 0, unroll=16)

    h = h_ref[...]
    parts = []
    for q in range(pairs):
        pr = step * pairs + q
        act = jax.nn.gelu(_dot_nt(h, u_ref[q * 2 * nk:(q + 1) * 2 * nk, :]))
        w = m_scr[pl.ds(pr, tile, stride=half), :]
        m_even = lax.bitcast_convert_type(w << 16, F32)
        m_odd = lax.bitcast_convert_type(w & hi_mask, F32)
        parts.append((act * jnp.concatenate([m_even, m_odd], axis=1)).astype(BF16))
    contrib = _dot(jnp.concatenate(parts, axis=1), v_ref[...])

    @pl.when(step == 0)
    def _():
        acc[...] = contrib

    @pl.when(step > 0)
    def _():
        acc[...] += contrib

    @pl.when(step == n_steps - 1)
    def _():
        x2 = x_ref[...] + gt_ref[...] * acc[...]
        o_ref[...] = _rms(x2, fg_ref[...])


def _peer_mix(hb, aidx, bidx, gate, u, v, x1, gt, final_g, *, tile, pairs):
    b, t, d = x1.shape
    nt = t // tile
    nk = PEER_NKEYS
    npk = aidx.shape[-1]
    n_steps = nk // (2 * pairs)
    eb = 2 * nk * pairs
    tok = lambda wd: pl.BlockSpec((None, tile, wd), lambda i, j, s: (i, j, 0))
    tab = lambda: pl.BlockSpec((eb, d), lambda i, j, s: (s, 0))
    return pl.pallas_call(
        functools.partial(_peer_mix_kernel, tile=tile, n_steps=n_steps, pairs=pairs),
        grid=(b, nt, n_steps),
        in_specs=[tok(d), tok(npk), tok(npk), tok(npk), tab(), tab(), tok(d),
                  pl.BlockSpec((None, 1, d), lambda i, j, s: (i, 0, 0)),
                  pl.BlockSpec((1, d), lambda i, j, s: (0, 0))],
        out_specs=tok(d),
        out_shape=jax.ShapeDtypeStruct((b, t, d), F32),
        scratch_shapes=[pltpu.VMEM((tile * nk // 2, nk), jnp.uint32), pltpu.VMEM((tile, d), F32)],
        compiler_params=_cparams(("parallel", "parallel", "arbitrary")),
    )(hb, aidx, bidx, gate, u, v, x1, gt, final_g.reshape(1, d))


def _block_diag(w):
    n, e, _ = w.shape
    eye = jnp.eye(n, dtype=w.dtype)
    return (eye[:, None, :, None] * w[:, :, None, :]).reshape(n * e, n * e)


def _mix_sequence(x3, ctx_mode, params, states, mods, *, rg_tb, gdn_tb, proj_tile):
    (norm1_g, w_rg, w_gdn, rg_conv_w, rg_conv_b, wg, gate_b, c_lam, gdn_conv_w, ea_row, dtb_row) = params
    sh1, sc1, mod_row = mods
    rg_h0_f, rg_h0_b, gdn_s0 = states
    (p_rg,) = _project(x3, norm1_g, sh1, sc1, mod_row, w_rg, (2 * RG_WIDTH,), colmajor=False, tile=proj_tile)
    qkv, z, ab = _project(x3, norm1_g, sh1, sc1, mod_row, w_gdn, (3 * GDN_WIDTH, GDN_WIDTH, AB_PAD),
                          colmajor=not ctx_mode, tile=proj_tile)
    h_f, st_f = _rglru_pass(p_rg, rg_conv_w, rg_conv_b, wg[0], gate_b[0], c_lam[0], rg_h0_f, None,
                            reverse=False, tb=rg_tb)
    y_rg, st_b = _rglru_pass(p_rg, rg_conv_w, rg_conv_b, wg[1], gate_b[1], c_lam[1], rg_h0_b, h_f,
                             reverse=True, tb=rg_tb)
    prep = _gdn_prep(qkv, ab, gdn_conv_w, ea_row, dtb_row, tb=gdn_tb)
    o_f, o_b, s_fin = _gdn_scan(prep, gdn_s0)
    return y_rg, o_f, o_b, z, (st_f, st_b, s_fin)


def kernel(x, c, ctx, c_ctx, w_mod, b_mod, norm1_g, norm2_g, w_in, rg_conv_w, rg_conv_b, rg_gate_w, rg_gate_b,
           rg_lambda, gdn_conv_w, gdn_a_log, gdn_dt_bias, gdn_norm_g, w_out, peer_wq, peer_keys, peer_u, peer_v,
           final_g):
    b, t, d = x.shape
    depth = w_mod.shape[0]
    assert depth == 1, "context residual stream update is only needed for depth > 1"
    l = 0
    w = RG_WIDTH

    cc = jnp.zeros((16, d), F32).at[:b].set(c).at[b].set(c_ctx)
    w_rg = w_in[l][:, :2 * w].astype(BF16)
    n_ab = w_in.shape[2] - 2 * w - 4 * GDN_WIDTH
    w_gdn = jnp.concatenate([w_in[l][:, 2 * w:2 * w + 4 * GDN_WIDTH],
                             jnp.pad(w_in[l][:, 2 * w + 4 * GDN_WIDTH:], ((0, 0), (0, AB_PAD - n_ab)))],
                            axis=1).astype(BF16)
    wg = jnp.stack([jnp.concatenate([_block_diag(rg_gate_w[l, dr, 0]), _block_diag(rg_gate_w[l, dr, 1])], axis=1)
                    for dr in range(2)]).astype(BF16)
    gate_b = rg_gate_b[l].reshape(2, 1, 2 * w)
    c_lam = (-RG_C * jax.nn.softplus(-rg_lambda[l])).reshape(2, 1, w)
    ea = jnp.exp(gdn_a_log[l])
    pad4 = jnp.zeros((2, GDN_HEADS), F32)
    ea_row = jnp.pad(jnp.concatenate([ea, pad4], axis=1).reshape(1, -1), ((0, 0), (0, AB_PAD - n_ab)))
    dtb_row = jnp.pad(jnp.concatenate([gdn_dt_bias[l], pad4], axis=1).reshape(1, -1), ((0, 0), (0, AB_PAD - n_ab)))
    params = (norm1_g[l], w_rg, w_gdn, rg_conv_w[l], rg_conv_b[l].reshape(1, w), wg, gate_b, c_lam,
              gdn_conv_w[l], ea_row, dtb_row)

    mod = _modulation(cc, w_mod[l], b_mod[l])
    sh1, sc1, gt1, sh2, sc2, gt2 = [mod[:, i * d:(i + 1) * d].reshape(16, 1, d) for i in range(6)]

    tc = ctx.shape[1]
    zero_states = (jnp.zeros((b, 1, w), F32), jnp.zeros((b, 1, w), F32),
                   jnp.zeros((b, 2, GDN_HEADS, GDN_DK, GDN_DV), F32))
    ctx_tb = min(tc, 256)
    _, _, _, _, ctx_states = _mix_sequence(ctx, True, params, zero_states, (sh1, sc1, lambda i: b),
                                           rg_tb=ctx_tb, gdn_tb=min(tc, 128), proj_tile=ctx_tb)

    tile = min(t, 512)
    y_rg, o_f, o_b, z, _ = _mix_sequence(x, False, params, ctx_states, (sh1, sc1, lambda i: i),
                                         rg_tb=min(t, 256), gdn_tb=128, proj_tile=tile)
    x1 = _out_project(x, y_rg, o_f, o_b, z, gdn_norm_g[l].reshape(1, GDN_DV), w_out[l].astype(BF16), gt1, tile=tile)

    wfold_t = _fold_keys(peer_wq[l], peer_keys[l]).T.astype(BF16)
    sel_tile = min(t, 256)
    hb, aidx, bidx, gate = _peer_select(x1, norm2_g[l], sh2, sc2, wfold_t, tile=sel_tile)
    tr = lambda a: jnp.swapaxes(a, 1, 2)
    out = _peer_mix(hb, tr(aidx), tr(bidx), tr(gate), peer_u[l].astype(BF16), peer_v[l].astype(BF16),
                    x1, gt2, final_g, tile=tile, pairs=4)
    return out
```

```python
import functools
import math

import jax
import jax.numpy as jnp
from jax import lax
from jax.experimental import pallas as pl
from jax.experimental.pallas import tpu as pltpu

F32 = jnp.float32
BF16 = jnp.bfloat16

GRID_W = 64
EPS = 1e-6
RG_WIDTH = 512
RG_BLOCKS = 8
RG_C = 8.0
GDN_HEADS = 4
GDN_DK = 128
GDN_DV = 128
GDN_WIDTH = GDN_HEADS * GDN_DV
GDN_CHUNK = 64
GDN_SUB = 16
AB_PAD = 128
PEER_HEADS = 8
PEER_NKEYS = 128
PEER_HALF = 128
PEER_TOPK = 16
NEG_INF = float("-inf")

VMEM_LIMIT = 56 * 1024 * 1024


def _cparams(sem):
    return pltpu.CompilerParams(dimension_semantics=sem, vmem_limit_bytes=VMEM_LIMIT)


def _dot(a, b):
    return jnp.dot(a, b, preferred_element_type=F32)


def _dot_nt(a, b):
    return lax.dot_general(a, b, (((1,), (1,)), ((), ())), preferred_element_type=F32)


def _dot_tn(a, b):
    return lax.dot_general(a, b, (((0,), (0,)), ((), ())), preferred_element_type=F32)


def _split3(x):
    hi = x.astype(BF16)
    r = x - hi.astype(F32)
    mid = r.astype(BF16)
    lo = (r - mid.astype(F32)).astype(BF16)
    return hi, mid, lo


def _dot_sel(m, x):
    hi, mid, lo = _split3(x)
    return _dot(m, lo) + _dot(m, mid) + _dot(m, hi)


def _silu(x):
    return x * jax.nn.sigmoid(x)


def _softplus(x):
    return jnp.maximum(x, 0.0) + jnp.log1p(jnp.exp(-jnp.abs(x)))


def _rms(x, g):
    return x * lax.rsqrt(jnp.mean(x * x, axis=-1, keepdims=True) + EPS) * g


def _mod_kernel(c_ref, w_ref, b_ref, o_ref):
    s = _silu(c_ref[...])
    o_ref[...] = jnp.dot(s, w_ref[...], preferred_element_type=F32,
                         precision=lax.Precision.HIGHEST) + b_ref[...]


def _modulation(cc, w_mod, b_mod):
    m, d = cc.shape
    n = w_mod.shape[1]
    tn = 1536
    return pl.pallas_call(
        _mod_kernel,
        grid=(n // tn,),
        in_specs=[pl.BlockSpec((m, d), lambda j: (0, 0)),
                  pl.BlockSpec((d, tn), lambda j: (0, j)),
                  pl.BlockSpec((1, tn), lambda j: (0, j))],
        out_specs=pl.BlockSpec((m, tn), lambda j: (0, j)),
        out_shape=jax.ShapeDtypeStruct((m, n), F32),
        compiler_params=_cparams(("arbitrary",)),
    )(cc, w_mod, b_mod.reshape(1, n))


def _proj_kernel(x_ref, g_ref, sh_ref, sc_ref, w_ref, *o_refs, ncol, d, widths):
    x = x_ref[...]
    if ncol > 1:
        x = jnp.concatenate([x[:, j * d:(j + 1) * d] for j in range(ncol)], axis=0)
    h = _rms(x, g_ref[...]) * (1.0 + sc_ref[...]) + sh_ref[...]
    o = _dot(h.astype(BF16), w_ref[...])
    off = 0
    for o_ref, wd in zip(o_refs, widths):
        o_ref[...] = o[:, off:off + wd]
        off += wd


def _project(x3, g, sh, sc, mod_row, w, widths, *, colmajor, tile):
    b, t, d = x3.shape
    n = w.shape[1]
    if colmajor:
        rows = t // GRID_W
        ncol = tile // rows
        xv = x3.reshape(b, rows, GRID_W * d)
        x_spec = pl.BlockSpec((None, rows, ncol * d), lambda i, j: (i, 0, j))
        nt = GRID_W // ncol
    else:
        ncol = 1
        xv = x3
        x_spec = pl.BlockSpec((None, tile, d), lambda i, j: (i, j, 0))
        nt = t // tile
    vec = lambda: pl.BlockSpec((None, 1, d), lambda i, j: (mod_row(i), 0, 0))
    outs = pl.pallas_call(
        functools.partial(_proj_kernel, ncol=ncol, d=d, widths=tuple(widths)),
        grid=(b, nt),
        in_specs=[x_spec, pl.BlockSpec((1, d), lambda i, j: (0, 0)), vec(), vec(),
                  pl.BlockSpec((d, n), lambda i, j: (0, 0))],
        out_specs=[pl.BlockSpec((None, tile, wd), lambda i, j: (i, j, 0)) for wd in widths],
        out_shape=[jax.ShapeDtypeStruct((b, t, wd), F32) for wd in widths],
        compiler_params=_cparams(("parallel", "parallel")),
    )(xv, g.reshape(1, d), sh, sc, w)
    return outs


def _conv4(cur, prev8, nxt8, w_ref, first, last):
    tb = cur.shape[0]
    prev8 = jnp.where(first, 0.0, prev8)
    nxt8 = jnp.where(last, 0.0, nxt8)
    ext = jnp.concatenate([prev8, cur, nxt8], axis=0)
    n = tb + 16
    xm2 = pltpu.roll(ext, 2, 0)[8:8 + tb]
    xm1 = pltpu.roll(ext, 1, 0)[8:8 + tb]
    xp1 = pltpu.roll(ext, n - 1, 0)[8:8 + tb]
    return (w_ref[0:1, :] * xm2 + w_ref[1:2, :] * xm1 + w_ref[2:3, :] * cur + w_ref[3:4, :] * xp1)


def _halo_specs(tb, t, width, lane_blk, tmap):
    r = tb // 8
    nb8 = t // 8
    return [pl.BlockSpec((None, tb, width), lambda i, j: (i, tmap(j), lane_blk)),
            pl.BlockSpec((None, 8, width), lambda i, j: (i, jnp.maximum(tmap(j) * r - 1, 0), lane_blk)),
            pl.BlockSpec((None, 8, width), lambda i, j: (i, jnp.minimum((tmap(j) + 1) * r, nb8 - 1), lane_blk))]


def _rglru_kernel(*refs, tb, nt, reverse):
    if reverse:
        (u_ref, up_ref, un_ref, cw_ref, cb_ref, wg_ref, gb_ref, cl_ref, h0_ref, gate_ref, hf_ref,
         y_ref, st_ref, carry) = refs
    else:
        (u_ref, up_ref, un_ref, cw_ref, cb_ref, wg_ref, gb_ref, cl_ref, h0_ref,
         y_ref, st_ref, carry) = refs
    j = pl.program_id(1)
    tblk = (nt - 1 - j) if reverse else j
    w = u_ref.shape[-1]

    @pl.when(j == 0)
    def _():
        carry[...] = h0_ref[...]

    u = u_ref[...]
    xc = _conv4(u, up_ref[...], un_ref[...], cw_ref, tblk == 0, tblk == nt - 1) + cb_ref[...]
    gates = _dot(xc.astype(BF16), wg_ref[...]) + gb_ref[...]
    r = jax.nn.sigmoid(gates[:, :w])
    ig = jax.nn.sigmoid(gates[:, w:])
    log_a = r * cl_ref[...]
    a = jnp.exp(log_a)
    th = jnp.tanh(log_a)
    bb = jnp.sqrt(-2.0 * th / (1.0 - th)) * (ig * xc)

    row = lax.broadcasted_iota(jnp.int32, (tb, w), 0)
    s = 1
    while s < tb:
        if reverse:
            a_sh = pltpu.roll(a, tb - s, 0)
            b_sh = pltpu.roll(bb, tb - s, 0)
            ok = row < tb - s
        else:
            a_sh = pltpu.roll(a, s, 0)
            b_sh = pltpu.roll(bb, s, 0)
            ok = row >= s
        a_sh = jnp.where(ok, a_sh, 1.0)
        b_sh = jnp.where(ok, b_sh, 0.0)
        bb = a * b_sh + bb
        a = a * a_sh
        s *= 2
    h = a * carry[...] + bb
    carry[...] = h[0:1, :] if reverse else h[tb - 1:tb, :]
    st_ref[...] = carry[...]
    if reverse:
        y_ref[...] = ((hf_ref[...] + h) * jax.nn.gelu(gate_ref[...])).astype(y_ref.dtype)
    else:
        y_ref[...] = h


def _rglru_pass(p_rg, conv_w, conv_b, wg, gate_b, c_lam, h0, hf, *, reverse, tb):
    b, t, w2 = p_rg.shape
    w = w2 // 2
    nt = t // tb
    tmap = (lambda j: nt - 1 - j) if reverse else (lambda j: j)
    const = lambda shape: pl.BlockSpec(shape, lambda i, j: (0,) * len(shape))
    in_specs = _halo_specs(tb, t, w, 0, tmap) + [
        const((4, w)), const((1, w)), const((w, 2 * w)), const((1, 2 * w)), const((1, w)),
        pl.BlockSpec((None, 1, w), lambda i, j: (i, 0, 0))]
    args = [p_rg, p_rg, p_rg, conv_w, conv_b, wg, gate_b, c_lam, h0]
    if reverse:
        in_specs += [pl.BlockSpec((None, tb, w), lambda i, j: (i, tmap(j), 1)),
                     pl.BlockSpec((None, tb, w), lambda i, j: (i, tmap(j), 0))]
        args += [p_rg, hf]
    y, st = pl.pallas_call(
        functools.partial(_rglru_kernel, tb=tb, nt=nt, reverse=reverse),
        grid=(b, nt),
        in_specs=in_specs,
        out_specs=[pl.BlockSpec((None, tb, w), lambda i, j: (i, tmap(j), 0)),
                   pl.BlockSpec((None, 1, w), lambda i, j: (i, 0, 0))],
        out_shape=[jax.ShapeDtypeStruct((b, t, w), BF16 if reverse else F32),
                   jax.ShapeDtypeStruct((b, 1, w), F32)],
        scratch_shapes=[pltpu.VMEM((1, w), F32)],
        compiler_params=_cparams(("parallel", "arbitrary")),
    )(*args)
    return y, st


def _bmm(a, b):
    return lax.dot_general(a, b, (((2,), (1,)), ((0,), (0,))), preferred_element_type=F32)


def _bmm_nt(a, b):
    return lax.dot_general(a, b, (((2,), (2,)), ((0,), (0,))), preferred_element_type=F32)


def _unit_tri_inverse(a, diag_mask):
    c = a.shape[-1]
    eye = (lax.broadcasted_iota(jnp.int32, (1, c, c), 1) == lax.broadcasted_iota(jnp.int32, (1, c, c), 2)).astype(F32)
    mm = lambda p, q: _bmm(p.astype(BF16), q.astype(BF16))
    ad = jnp.where(diag_mask, a, 0.0)
    x = eye - ad
    pw = ad
    k = 2
    while k < GDN_SUB:
        pw = mm(pw, pw)
        x = x + mm(x, pw)
        k *= 2
    n = mm(x, a - ad)
    nblk = c // GDN_SUB
    y = eye - n
    pw = n
    k = 2
    while k < nblk:
        pw = mm(pw, pw)
        y = y + mm(y, pw)
        k *= 2
    return mm(y, x)


def _gdn_prep_kernel(qkv_ref, qp_ref, qn_ref, ab_ref, cw_ref, ea_ref, dtb_ref,
                     qg_ref, kd_ref, kc_ref, wv_ref, at_ref, eg_ref, *, tb, nt):
    c = GDN_CHUNK
    j = pl.program_id(1)
    hw = GDN_WIDTH
    x = _silu(_conv4(qkv_ref[...], qp_ref[...], qn_ref[...], cw_ref, j == 0, j == nt - 1))

    ab = ab_ref[...]
    col = lax.broadcasted_iota(jnp.int32, (tb, AB_PAD), 1)
    is_a = (col & 4) == 0
    gbv = jnp.where(is_a, -ea_ref[...] * _softplus(ab + dtb_ref[...]), jax.nn.sigmoid(ab))

    ri = lax.broadcasted_iota(jnp.int32, (tb, tb), 0)
    ci = lax.broadcasted_iota(jnp.int32, (tb, tb), 1)
    same = (ri // c) == (ci // c)
    l_f = (same & (ci <= ri)).astype(BF16)
    l_b = (same & (ci >= ri)).astype(BF16)
    l_t = same.astype(BF16)
    gcum = jnp.where(col < 8, _dot_sel(l_f, gbv), _dot_sel(l_b, gbv))
    gtot = _dot_sel(l_t, gbv)

    n_combo = 2 * GDN_HEADS
    er = lax.broadcasted_iota(jnp.int32, (AB_PAD, n_combo * 128), 0)
    ec = lax.broadcasted_iota(jnp.int32, (AB_PAD, n_combo * 128), 1) // 128
    src_a = (ec // GDN_HEADS) * 8 + (ec % GDN_HEADS)
    e_a = (er == src_a).astype(BF16)
    e_b = (er == src_a + 4).astype(BF16)
    gc_all = _bcast_cols(gcum, e_a)
    gt_all = _bcast_cols(gtot, e_a)
    be_all = _bcast_cols(gbv, e_b)

    qs, ks, vs = [], [], []
    for hd in range(GDN_HEADS):
        q = x[:, hd * GDN_DK:(hd + 1) * GDN_DK]
        k = x[:, hw + hd * GDN_DK: hw + (hd + 1) * GDN_DK]
        qs.append(q * lax.rsqrt(jnp.sum(q * q, axis=-1, keepdims=True) + EPS) * (GDN_DK ** -0.5))
        ks.append(k * lax.rsqrt(jnp.sum(k * k, axis=-1, keepdims=True) + EPS))
        vs.append(x[:, 2 * hw + hd * GDN_DV: 2 * hw + (hd + 1) * GDN_DV])
    inst = [(ch, dr, hd) for ch in range(tb // c) for dr in range(2) for hd in range(GDN_HEADS)]
    n_inst = len(inst)
    rows = lambda ch: slice(ch * c, (ch + 1) * c)
    lanes = lambda dr, hd: slice((dr * GDN_HEADS + hd) * 128, (dr * GDN_HEADS + hd + 1) * 128)
    stack = lambda f: jnp.stack([f(ch, dr, hd) for ch, dr, hd in inst], axis=0)
    qq = stack(lambda ch, dr, hd: qs[hd][rows(ch)])
    kk = stack(lambda ch, dr, hd: ks[hd][rows(ch)])
    vv = stack(lambda ch, dr, hd: vs[hd][rows(ch)])
    gcb = stack(lambda ch, dr, hd: gc_all[rows(ch), lanes(dr, hd)])
    gtb = stack(lambda ch, dr, hd: gt_all[rows(ch), lanes(dr, hd)])
    beb = stack(lambda ch, dr, hd: be_all[rows(ch), lanes(dr, hd)])

    fwd = ((lax.broadcasted_iota(jnp.int32, (n_inst, c, 128), 0) // GDN_HEADS) % 2) == 0
    ii = lax.broadcasted_iota(jnp.int32, (n_inst, c, 128), 1)
    jj = lax.broadcasted_iota(jnp.int32, (n_inst, c, 128), 2)
    incl = (fwd & (jj <= ii)) | (jnp.logical_not(fwd) & (jj >= ii) & (jj < c))
    fwd_c = ((lax.broadcasted_iota(jnp.int32, (n_inst, c, c), 0) // GDN_HEADS) % 2) == 0
    i64 = lax.broadcasted_iota(jnp.int32, (n_inst, c, c), 1)
    j64 = lax.broadcasted_iota(jnp.int32, (n_inst, c, c), 2)
    strict = (fwd_c & (j64 < i64)) | (jnp.logical_not(fwd_c) & (j64 > i64))
    diag_blk = (i64 // GDN_SUB) == (j64 // GDN_SUB)

    eg = jnp.exp(gcb)
    kb = kk * beb
    diag = jnp.where(ii == jj, gcb, 0.0)
    gcr = _dot_sel(jnp.ones((c, c), BF16), jnp.concatenate([diag[g] for g in range(n_inst)], axis=1))
    gcr = jnp.stack([gcr[:, g * 128:(g + 1) * 128] for g in range(n_inst)], axis=0)
    decay = jnp.where(incl, jnp.exp(jnp.where(incl, gcb - gcr, 0.0)), 0.0)
    kpad = jnp.concatenate([kk.astype(BF16), jnp.zeros((n_inst, c, GDN_DK), BF16)], axis=1)
    qk = _bmm_nt(jnp.concatenate([qq, kb], axis=1).astype(BF16), kpad)
    attn = (qk[:, :c] * decay).astype(BF16)
    a_mat = jnp.where(strict, (qk[:, c:] * decay)[:, :, :c], 0.0)
    tinv = _unit_tri_inverse(a_mat, diag_blk)
    rhs = jnp.concatenate([vv * beb, kb * eg], axis=2)
    sol = _bmm(tinv.astype(BF16), rhs.astype(BF16))
    qg = (qq * eg).astype(BF16)
    kd = (kk * jnp.exp(gtb - gcb)).astype(BF16)
    egt = jnp.exp(gtb[:, 0:8, :])
    for g, (ch, dr, hd) in enumerate(inst):
        rs = rows(ch)
        ls = slice(hd * 128, (hd + 1) * 128)
        qg_ref[dr, rs, ls] = qg[g]
        kd_ref[dr, rs, ls] = kd[g]
        kc_ref[dr, rs, ls] = sol[g, :, GDN_DV:].astype(BF16)
        wv_ref[dr, rs, ls] = sol[g, :, :GDN_DV]
        at_ref[dr, rs, ls] = attn[g]
        eg_ref[dr, ch, :, ls] = egt[g]


def _bcast_cols(x, sel):
    hi, mid, lo = _split3(x)
    return _dot(lo, sel) + _dot(mid, sel) + _dot(hi, sel)


def _gdn_prep(qkv, ab, conv_w, ea_row, dtb_row, *, tb):
    b, t, cw = qkv.shape
    nt = t // tb
    nch = t // GDN_CHUNK
    cpb = tb // GDN_CHUNK
    const = lambda shape: pl.BlockSpec(shape, lambda i, j: (0,) * len(shape))
    hw = GDN_WIDTH
    big = lambda: pl.BlockSpec((2, None, tb, hw), lambda i, j: (0, i, j, 0))
    outs = pl.pallas_call(
        functools.partial(_gdn_prep_kernel, tb=tb, nt=nt),
        grid=(b, nt),
        in_specs=_halo_specs(tb, t, cw, 0, lambda j: j) + [
            pl.BlockSpec((None, tb, AB_PAD), lambda i, j: (i, j, 0)),
            const((4, cw)), const((1, AB_PAD)), const((1, AB_PAD))],
        out_specs=[big(), big(), big(), big(), big(),
                   pl.BlockSpec((2, None, cpb, 8, hw), lambda i, j: (0, i, j, 0, 0))],
        out_shape=[jax.ShapeDtypeStruct((2, b, t, hw), BF16),
                   jax.ShapeDtypeStruct((2, b, t, hw), BF16),
                   jax.ShapeDtypeStruct((2, b, t, hw), BF16),
                   jax.ShapeDtypeStruct((2, b, t, hw), F32),
                   jax.ShapeDtypeStruct((2, b, t, hw), BF16),
                   jax.ShapeDtypeStruct((2, b, nch, 8, hw), F32)],
        compiler_params=_cparams(("parallel", "parallel")),
    )(qkv, qkv, qkv, ab, conv_w, ea_row, dtb_row)
    return outs


def _gdn_scan_kernel(qg_f, kd_f, kc_f, wv_f, at_f, eg_f, qg_b, kd_b, kc_b, wv_b, at_b, eg_b, s0_ref,
                     of_ref, ob_ref, sfin_ref, s_scr):
    c = GDN_CHUNK
    j = pl.program_id(1)

    @pl.when(j == 0)
    def _():
        s_scr[...] = s0_ref[...]

    dirs = ((qg_f, kd_f, kc_f, wv_f, at_f, eg_f), (qg_b, kd_b, kc_b, wv_b, at_b, eg_b))
    chains = [(dr, hd) for dr in range(2) for hd in range(GDN_HEADS)]
    ls = lambda hd: slice(hd * 128, (hd + 1) * 128)
    stack = lambda f: jnp.stack([f(dirs[dr], ls(hd)) for dr, hd in chains], axis=0)
    kq = stack(lambda r, l: jnp.concatenate([r[2][:, l], r[0][:, l]], axis=0))
    wv = stack(lambda r, l: r[3][:, l])
    at = stack(lambda r, l: r[4][:, l][:, :c])
    eg = stack(lambda r, l: r[5][0:1, l])
    s = s_scr[...].reshape(2 * GDN_HEADS, GDN_DK, GDN_DV)
    r = _bmm(kq, s.astype(BF16))
    vb = (wv - r[:, :c]).astype(BF16)
    o = r[:, c:] + _bmm(at, vb)
    for g, (dr, hd) in enumerate(chains):
        (of_ref, ob_ref)[dr][:, ls(hd)] = o[g]
        s_scr[dr, hd] = s[g] * eg[g] + _dot_tn(dirs[dr][1][:, ls(hd)], vb[g])
    sfin_ref[...] = s_scr[...]


def _gdn_scan(prep, s0):
    qg, kd, kc, wv, at, eg = prep
    _, b, t, hw = qg.shape
    c = GDN_CHUNK
    nch = t // c
    fw = lambda: pl.BlockSpec((None, None, c, hw), lambda i, j: (0, i, j, 0))
    bw = lambda: pl.BlockSpec((None, None, c, hw), lambda i, j: (1, i, nch - 1 - j, 0))
    egf = pl.BlockSpec((None, None, None, 8, hw), lambda i, j: (0, i, j, 0, 0))
    egb = pl.BlockSpec((None, None, None, 8, hw), lambda i, j: (1, i, nch - 1 - j, 0, 0))
    st = pl.BlockSpec((None, 2, GDN_HEADS, GDN_DK, GDN_DV), lambda i, j: (i, 0, 0, 0, 0))
    o_f, o_b, s_fin = pl.pallas_call(
        _gdn_scan_kernel,
        grid=(b, nch),
        in_specs=[fw(), fw(), fw(), fw(), fw(), egf, bw(), bw(), bw(), bw(), bw(), egb, st],
        out_specs=[pl.BlockSpec((None, c, hw), lambda i, j: (i, j, 0)),
                   pl.BlockSpec((None, c, hw), lambda i, j: (i, nch - 1 - j, 0)),
                   st],
        out_shape=[jax.ShapeDtypeStruct((b, t, hw), F32), jax.ShapeDtypeStruct((b, t, hw), F32),
                   jax.ShapeDtypeStruct((b, 2, GDN_HEADS, GDN_DK, GDN_DV), F32)],
        scratch_shapes=[pltpu.VMEM((2, GDN_HEADS, GDN_DK, GDN_DV), F32)],
        compiler_params=_cparams(("parallel", "arbitrary")),
    )(qg, kd, kc, wv, at, eg, qg, kd, kc, wv, at, eg, s0)
    return o_f, o_b, s_fin


def _out_rg_kernel(x_ref, y_ref, w_ref, gt_ref, o_ref):
    o_ref[...] = x_ref[...] + gt_ref[...] * _dot(y_ref[...], w_ref[...])


def _out_gdn_kernel(x_ref, of_ref, ob_ref, z_ref, ng_ref, w_ref, gt_ref, o_ref, *, ncol, d):
    o = of_ref[...] + ob_ref[...]
    z = z_ref[...]
    parts = []
    for hd in range(GDN_HEADS):
        ls = slice(hd * GDN_DV, (hd + 1) * GDN_DV)
        parts.append(_rms(o[:, ls], ng_ref[...]) * _silu(z[:, ls]))
    y = jnp.concatenate(parts, axis=1).astype(BF16)
    t2 = gt_ref[...] * _dot(y, w_ref[...])
    rows = t2.shape[0] // ncol
    t2 = jnp.concatenate([t2[j * rows:(j + 1) * rows] for j in range(ncol)], axis=1)
    o_ref[...] = x_ref[...] + t2


def _out_project(x3, y_rg, o_f, o_b, z, norm_g, w_out, gt, *, tile):
    b, t, d = x3.shape
    w = RG_WIDTH
    vec = lambda: pl.BlockSpec((None, 1, d), lambda i, j: (i, 0, 0))
    x1 = pl.pallas_call(
        _out_rg_kernel,
        grid=(b, t // tile),
        in_specs=[pl.BlockSpec((None, tile, d), lambda i, j: (i, j, 0)),
                  pl.BlockSpec((None, tile, w), lambda i, j: (i, j, 0)),
                  pl.BlockSpec((w, d), lambda i, j: (0, 0)), vec()],
        out_specs=pl.BlockSpec((None, tile, d), lambda i, j: (i, j, 0)),
        out_shape=jax.ShapeDtypeStruct((b, t, d), F32),
        compiler_params=_cparams(("parallel", "parallel")),
    )(x3, y_rg, w_out[:w], gt)
    rows = t // GRID_W
    ncol = tile // rows
    hw = GDN_WIDTH
    x1v = x1.reshape(b, rows, GRID_W * d)
    cm = lambda wd: pl.BlockSpec((None, tile, wd), lambda i, j: (i, j, 0))
    x1v = pl.pallas_call(
        functools.partial(_out_gdn_kernel, ncol=ncol, d=d),
        grid=(b, GRID_W // ncol),
        in_specs=[pl.BlockSpec((None, rows, ncol * d), lambda i, j: (i, 0, j)),
                  cm(hw), cm(hw), cm(hw),
                  pl.BlockSpec((1, GDN_DV), lambda i, j: (0, 0)),
                  pl.BlockSpec((hw, d), lambda i, j: (0, 0)), vec()],
        out_specs=pl.BlockSpec((None, rows, ncol * d), lambda i, j: (i, 0, j)),
        out_shape=jax.ShapeDtypeStruct((b, rows, GRID_W * d), F32),
        input_output_aliases={0: 0},
        compiler_params=_cparams(("parallel", "parallel")),
    )(x1v, o_f, o_b, z, norm_g, w_out[w:], gt)
    return x1v.reshape(b, t, d)


def _fold_kernel(wq_ref, k_ref, o_ref):
    o_ref[...] = lax.dot_general(wq_ref[...], k_ref[...], (((1,), (1,)), ((), ())),
                                 preferred_element_type=F32, precision=lax.Precision.HIGHEST)


def _fold_keys(wq, keys):
    d, n = wq.shape
    nblk = n // PEER_HALF
    return pl.pallas_call(
        _fold_kernel,
        grid=(nblk,),
        in_specs=[pl.BlockSpec((d, PEER_HALF), lambda j: (0, j)),
                  pl.BlockSpec((None, PEER_NKEYS, PEER_HALF), lambda j: (j % 2, 0, 0))],
        out_specs=pl.BlockSpec((d, PEER_NKEYS), lambda j: (0, j)),
        out_shape=jax.ShapeDtypeStruct((d, nblk * PEER_NKEYS), F32),
        compiler_params=_cparams(("parallel",)),
    )(wq, keys)


def _top16_rows(vals, s_scr, i_scr):
    n, p = vals.shape
    rid = lax.broadcasted_iota(jnp.int32, (n, p), 0)
    for r in range(PEER_TOPK):
        m = jnp.max(vals, axis=0, keepdims=True)
        idx = jnp.min(jnp.where(vals == m, rid, n), axis=0, keepdims=True)
        s_scr[r:r + 1, :] = m
        i_scr[r:r + 1, :] = idx
        vals = jnp.where(rid == idx, NEG_INF, vals)


def _peer_sel_kernel(x_ref, g_ref, sh_ref, sc_ref, wt_ref, h_ref, a_ref, b_ref, gate_ref,
                     sc_scr, s1, i1, s2, i2, ts, tp, a_t, b_t, g_t):
    k = PEER_TOPK
    h = _rms(x_ref[...], g_ref[...]) * (1.0 + sc_ref[...]) + sh_ref[...]
    hb = h.astype(BF16)
    h_ref[...] = hb
    sc_scr[...] = _dot_nt(wt_ref[...], hb)
    p = hb.shape[0]

    sub = lax.broadcasted_iota(jnp.int32, (8, p), 0)
    sub16 = lax.broadcasted_iota(jnp.int32, (16, p), 0)

    def head(hd, carry):
        base = pl.multiple_of(hd * 2 * PEER_NKEYS, 2 * PEER_NKEYS)
        _top16_rows(sc_scr[pl.ds(base, PEER_NKEYS), :], s1, i1)
        _top16_rows(sc_scr[pl.ds(base + PEER_NKEYS, PEER_NKEYS), :], s2, i2)
        s1v, s2v = s1[...], s2[...]
        cands = [s1v[0:1, :] + s2v]
        poss = [sub16]
        for r in range(1, 8):
            cmax = k // (r + 1)
            cands.append(jnp.where(sub < cmax, s1v[r:r + 1, :] + s2v[0:8, :], NEG_INF))
            poss.append(r * k + sub)
        cands.append(s1v[8:16, :] + s2v[0:1, :])
        poss.append((sub + 8) * k)
        cand = jnp.concatenate(cands, axis=0)
        pos = jnp.concatenate(poss, axis=0)
        big = k * k
        for r in range(k):
            m = jnp.max(cand, axis=0, keepdims=True)
            sel = jnp.min(jnp.where(cand == m, pos, big), axis=0, keepdims=True)
            ts[r:r + 1, :] = m
            tp[r:r + 1, :] = sel
            cand = jnp.where(pos == sel, NEG_INF, cand)
        top_s, top_p = ts[...], tp[...]
        rr = top_p >> 4
        cc = top_p & (k - 1)
        i1v, i2v = i1[...], i2[...]
        av = jnp.zeros_like(top_p)
        bv = jnp.zeros_like(top_p)
        for q in range(k):
            av = jnp.where(rr == q, i1v[q:q + 1, :], av)
            bv = jnp.where(cc == q, i2v[q:q + 1, :], bv)
        e = jnp.exp(top_s - top_s[0:1, :])
        gate = e / jnp.sum(e, axis=0, keepdims=True)
        o = pl.multiple_of(hd * k, k)
        a_t[pl.ds(o, k), :] = av
        b_t[pl.ds(o, k), :] = bv
        g_t[pl.ds(o, k), :] = gate
        return carry

    lax.fori_loop(0, PEER_HEADS, head, 0)
    a_ref[...] = a_t[...].T
    b_ref[...] = b_t[...].T
    gate_ref[...] = g_t[...].T


def _peer_select(x1, g, sh, sc, wt, *, tile):
    b, t, d = x1.shape
    nt = t // tile
    nk = PEER_HEADS * PEER_TOPK
    nrow = wt.shape[0]
    vec = lambda: pl.BlockSpec((None, 1, d), lambda i, j: (i, 0, 0))
    tr = lambda: pl.BlockSpec((None, tile, nk), lambda i, j: (i, j, 0))
    k = PEER_TOPK
    return pl.pallas_call(
        _peer_sel_kernel,
        grid=(b, nt),
        in_specs=[pl.BlockSpec((None, tile, d), lambda i, j: (i, j, 0)),
                  pl.BlockSpec((1, d), lambda i, j: (0, 0)), vec(), vec(),
                  pl.BlockSpec((nrow, d), lambda i, j: (0, 0))],
        out_specs=[pl.BlockSpec((None, tile, d), lambda i, j: (i, j, 0)), tr(), tr(), tr()],
        out_shape=[jax.ShapeDtypeStruct((b, t, d), BF16),
                   jax.ShapeDtypeStruct((b, t, nk), jnp.int32),
                   jax.ShapeDtypeStruct((b, t, nk), jnp.int32),
                   jax.ShapeDtypeStruct((b, t, nk), F32)],
        scratch_shapes=[pltpu.VMEM((nrow, tile), F32),
                        pltpu.VMEM((k, tile), F32), pltpu.VMEM((k, tile), jnp.int32),
                        pltpu.VMEM((k, tile), F32), pltpu.VMEM((k, tile), jnp.int32),
                        pltpu.VMEM((k, tile), F32), pltpu.VMEM((k, tile), jnp.int32),
                        pltpu.VMEM((nk, tile), jnp.int32), pltpu.VMEM((nk, tile), jnp.int32),
                        pltpu.VMEM((nk, tile), F32)],
        compiler_params=_cparams(("parallel", "parallel")),
    )(x1, g.reshape(1, d), sh, sc, wt)


def _peer_mix_kernel(h_ref, a_ref, b_ref, gate_ref, u_ref, v_ref, x_ref, gt_ref, fg_ref, o_ref,
                     m_scr, acc, *, tile, n_steps, pairs):
    nk = PEER_NKEYS
    half = nk // 2
    step = pl.program_id(2)
    hi_mask = jnp.uint32(0xFFFF0000)

    @pl.when(step == 0)
    def _():
        sub = lax.broadcasted_iota(jnp.int32, (nk, a_ref.shape[1]), 0)
        a_of_row = jnp.where(sub < half, 2 * sub, 2 * (sub - half) + 1)

        def build(p, carry):
            arow = a_ref[pl.ds(p, 1), :]
            brow = b_ref[pl.ds(p, 1), :]
            grow = gate_ref[pl.ds(p, 1), :]
            xa = jnp.where(a_of_row == arow, 1.0, 0.0).astype(BF16)
            yb = jnp.where(sub == brow, grow, 0.0).astype(BF16)
            m = _dot_nt(xa, yb).astype(BF16).astype(F32)
            bits = lax.bitcast_convert_type(m, jnp.uint32)
            m_scr[pl.ds(pl.multiple_of(p * half, half), half), :] = (bits[half:] & hi_mask) | (bits[:half] >> 16)
            return carry

        lax.fori_loop(0, tile, build, 0, unroll=16)

    h = h_ref[...]
    parts = []
    for q in range(pairs):
        pr = step * pairs + q
        act = jax.nn.gelu(_dot_nt(h, u_ref[q * 2 * nk:(q + 1) * 2 * nk, :]))
        w = m_scr[pl.ds(pr, tile, stride=half), :]
        m_even = lax.bitcast_convert_type(w << 16, F32)
        m_odd = lax.bitcast_convert_type(w & hi_mask, F32)
        parts.append((act * jnp.concatenate([m_even, m_odd], axis=1)).astype(BF16))
    contrib = _dot(jnp.concatenate(parts, axis=1), v_ref[...])

    @pl.when(step == 0)
    def _():
        acc[...] = contrib

    @pl.when(step > 0)
    def _():
        acc[...] += contrib

    @pl.when(step == n_steps - 1)
    def _():
        x2 = x_ref[...] + gt_ref[...] * acc[...]
        o_ref[...] = _rms(x2, fg_ref[...])


def _peer_mix(hb, aidx, bidx, gate, u, v, x1, gt, final_g, *, tile, pairs):
    b, t, d = x1.shape
    nt = t // tile
    nk = PEER_NKEYS
    npk = aidx.shape[-1]
    n_steps = nk // (2 * pairs)
    eb = 2 * nk * pairs
    tok = lambda wd: pl.BlockSpec((None, tile, wd), lambda i, j, s: (i, j, 0))
    tab = lambda: pl.BlockSpec((eb, d), lambda i, j, s: (s, 0))
    return pl.pallas_call(
        functools.partial(_peer_mix_kernel, tile=tile, n_steps=n_steps, pairs=pairs),
        grid=(b, nt, n_steps),
        in_specs=[tok(d), tok(npk), tok(npk), tok(npk), tab(), tab(), tok(d),
                  pl.BlockSpec((None, 1, d), lambda i, j, s: (i, 0, 0)),
                  pl.BlockSpec((1, d), lambda i, j, s: (0, 0))],
        out_specs=tok(d),
        out_shape=jax.ShapeDtypeStruct((b, t, d), F32),
        scratch_shapes=[pltpu.VMEM((tile * nk // 2, nk), jnp.uint32), pltpu.VMEM((tile, d), F32)],
        compiler_params=_cparams(("parallel", "parallel", "arbitrary")),
    )(hb, aidx, bidx, gate, u, v, x1, gt, final_g.reshape(1, d))


def _block_diag(w):
    n, e, _ = w.shape
    eye = jnp.eye(n, dtype=w.dtype)
    return (eye[:, None, :, None] * w[:, :, None, :]).reshape(n * e, n * e)


def _mix_sequence(x3, ctx_mode, params, states, mods, *, rg_tb, gdn_tb, proj_tile):
    (norm1_g, w_rg, w_gdn, rg_conv_w, rg_conv_b, wg, gate_b, c_lam, gdn_conv_w, ea_row, dtb_row) = params
    sh1, sc1, mod_row = mods
    rg_h0_f, rg_h0_b, gdn_s0 = states
    (p_rg,) = _project(x3, norm1_g, sh1, sc1, mod_row, w_rg, (2 * RG_WIDTH,), colmajor=False, tile=proj_tile)
    qkv, z, ab = _project(x3, norm1_g, sh1, sc1, mod_row, w_gdn, (3 * GDN_WIDTH, GDN_WIDTH, AB_PAD),
                          colmajor=not ctx_mode, tile=proj_tile)
    h_f, st_f = _rglru_pass(p_rg, rg_conv_w, rg_conv_b, wg[0], gate_b[0], c_lam[0], rg_h0_f, None,
                            reverse=False, tb=rg_tb)
    y_rg, st_b = _rglru_pass(p_rg, rg_conv_w, rg_conv_b, wg[1], gate_b[1], c_lam[1], rg_h0_b, h_f,
                             reverse=True, tb=rg_tb)
    prep = _gdn_prep(qkv, ab, gdn_conv_w, ea_row, dtb_row, tb=gdn_tb)
    o_f, o_b, s_fin = _gdn_scan(prep, gdn_s0)
    return y_rg, o_f, o_b, z, (st_f, st_b, s_fin)


def kernel(x, c, ctx, c_ctx, w_mod, b_mod, norm1_g, norm2_g, w_in, rg_conv_w, rg_conv_b, rg_gate_w, rg_gate_b,
           rg_lambda, gdn_conv_w, gdn_a_log, gdn_dt_bias, gdn_norm_g, w_out, peer_wq, peer_keys, peer_u, peer_v,
           final_g):
    b, t, d = x.shape
    depth = w_mod.shape[0]
    assert depth == 1, "context residual stream update is only needed for depth > 1"
    l = 0
    w = RG_WIDTH

    cc = jnp.zeros((16, d), F32).at[:b].set(c).at[b].set(c_ctx)
    w_rg = w_in[l][:, :2 * w].astype(BF16)
    n_ab = w_in.shape[2] - 2 * w - 4 * GDN_WIDTH
    w_gdn = jnp.concatenate([w_in[l][:, 2 * w:2 * w + 4 * GDN_WIDTH],
                             jnp.pad(w_in[l][:, 2 * w + 4 * GDN_WIDTH:], ((0, 0), (0, AB_PAD - n_ab)))],
                            axis=1).astype(BF16)
    wg = jnp.stack([jnp.concatenate([_block_diag(rg_gate_w[l, dr, 0]), _block_diag(rg_gate_w[l, dr, 1])], axis=1)
                    for dr in range(2)]).astype(BF16)
    gate_b = rg_gate_b[l].reshape(2, 1, 2 * w)
    c_lam = (-RG_C * jax.nn.softplus(-rg_lambda[l])).reshape(2, 1, w)
    ea = jnp.exp(gdn_a_log[l])
    pad4 = jnp.zeros((2, GDN_HEADS), F32)
    ea_row = jnp.pad(jnp.concatenate([ea, pad4], axis=1).reshape(1, -1), ((0, 0), (0, AB_PAD - n_ab)))
    dtb_row = jnp.pad(jnp.concatenate([gdn_dt_bias[l], pad4], axis=1).reshape(1, -1), ((0, 0), (0, AB_PAD - n_ab)))
    params = (norm1_g[l], w_rg, w_gdn, rg_conv_w[l], rg_conv_b[l].reshape(1, w), wg, gate_b, c_lam,
              gdn_conv_w[l], ea_row, dtb_row)

    mod = _modulation(cc, w_mod[l], b_mod[l])
    sh1, sc1, gt1, sh2, sc2, gt2 = [mod[:, i * d:(i + 1) * d].reshape(16, 1, d) for i in range(6)]

    tc = ctx.shape[1]
    zero_states = (jnp.zeros((b, 1, w), F32), jnp.zeros((b, 1, w), F32),
                   jnp.zeros((b, 2, GDN_HEADS, GDN_DK, GDN_DV), F32))
    ctx_tb = min(tc, 256)
    _, _, _, _, ctx_states = _mix_sequence(ctx, True, params, zero_states, (sh1, sc1, lambda i: b),
                                           rg_tb=ctx_tb, gdn_tb=min(tc, 128), proj_tile=ctx_tb)

    tile = min(t, 512)
    y_rg, o_f, o_b, z, _ = _mix_sequence(x, False, params, ctx_states, (sh1, sc1, lambda i: i),
                                         rg_tb=min(t, 256), gdn_tb=128, proj_tile=tile)
    x1 = _out_project(x, y_rg, o_f, o_b, z, gdn_norm_g[l].reshape(1, GDN_DV), w_out[l].astype(BF16), gt1, tile=tile)

    wfold_t = _fold_keys(peer_wq[l], peer_keys[l]).T.astype(BF16)
    sel_tile = min(t, 256)
    hb, aidx, bidx, gate = _peer_select(x1, norm2_g[l], sh2, sc2, wfold_t, tile=sel_tile)
    out = _peer_mix(hb, aidx, bidx, gate, peer_u[l].astype(BF16), peer_v[l].astype(BF16),
                    x1, gt2, final_g, tile=tile, pairs=8)
    return out
```

```python
import functools
import math

import jax
import jax.numpy as jnp
from jax import lax
from jax.experimental import pallas as pl
from jax.experimental.pallas import tpu as pltpu

F32 = jnp.float32
BF16 = jnp.bfloat16

GRID_W = 64
EPS = 1e-6
RG_WIDTH = 512
RG_BLOCKS = 8
RG_C = 8.0
GDN_HEADS = 4
GDN_DK = 128
GDN_DV = 128
GDN_WIDTH = GDN_HEADS * GDN_DV
GDN_CHUNK = 64
GDN_SUB = 16
AB_PAD = 128
PEER_HEADS = 8
PEER_NKEYS = 128
PEER_HALF = 128
PEER_TOPK = 16
NEG_INF = float("-inf")

VMEM_LIMIT = 56 * 1024 * 1024


def _cparams(sem):
    return pltpu.CompilerParams(dimension_semantics=sem, vmem_limit_bytes=VMEM_LIMIT)


def _dot(a, b):
    return jnp.dot(a, b, preferred_element_type=F32)


def _dot_nt(a, b):
    return lax.dot_general(a, b, (((1,), (1,)), ((), ())), preferred_element_type=F32)


def _dot_tn(a, b):
    return lax.dot_general(a, b, (((0,), (0,)), ((), ())), preferred_element_type=F32)


def _split3(x):
    hi = x.astype(BF16)
    r = x - hi.astype(F32)
    mid = r.astype(BF16)
    lo = (r - mid.astype(F32)).astype(BF16)
    return hi, mid, lo


def _dot_sel(m, x):
    hi, mid, lo = _split3(x)
    return _dot(m, lo) + _dot(m, mid) + _dot(m, hi)


def _silu(x):
    return x * jax.nn.sigmoid(x)


def _softplus(x):
    return jnp.maximum(x, 0.0) + jnp.log1p(jnp.exp(-jnp.abs(x)))


def _rms(x, g):
    return x * lax.rsqrt(jnp.mean(x * x, axis=-1, keepdims=True) + EPS) * g


def _mod_kernel(c_ref, w_ref, b_ref, o_ref):
    s = _silu(c_ref[...])
    o_ref[...] = jnp.dot(s, w_ref[...], preferred_element_type=F32,
                         precision=lax.Precision.HIGHEST) + b_ref[...]


def _modulation(cc, w_mod, b_mod):
    m, d = cc.shape
    n = w_mod.shape[1]
    tn = 1536
    return pl.pallas_call(
        _mod_kernel,
        grid=(n // tn,),
        in_specs=[pl.BlockSpec((m, d), lambda j: (0, 0)),
                  pl.BlockSpec((d, tn), lambda j: (0, j)),
                  pl.BlockSpec((1, tn), lambda j: (0, j))],
        out_specs=pl.BlockSpec((m, tn), lambda j: (0, j)),
        out_shape=jax.ShapeDtypeStruct((m, n), F32),
        compiler_params=_cparams(("arbitrary",)),
    )(cc, w_mod, b_mod.reshape(1, n))


def _proj_kernel(x_ref, g_ref, sh_ref, sc_ref, w_ref, *o_refs, ncol, d, widths):
    if ncol > 1:
        x = jnp.concatenate([x_ref[:, j, :] for j in range(ncol)], axis=0)
    else:
        x = x_ref[...]
    h = _rms(x, g_ref[...]) * (1.0 + sc_ref[...]) + sh_ref[...]
    o = _dot(h.astype(BF16), w_ref[...])
    off = 0
    for o_ref, wd in zip(o_refs, widths):
        o_ref[...] = o[:, off:off + wd]
        off += wd


def _project(x3, g, sh, sc, mod_row, w, widths, *, colmajor, tile):
    b, t, d = x3.shape
    n = w.shape[1]
    if colmajor:
        rows = t // GRID_W
        ncol = tile // rows
        xv = x3.reshape(b, rows, GRID_W, d)
        x_spec = pl.BlockSpec((None, rows, ncol, d), lambda i, j: (i, 0, j, 0))
        nt = GRID_W // ncol
    else:
        ncol = 1
        xv = x3
        x_spec = pl.BlockSpec((None, tile, d), lambda i, j: (i, j, 0))
        nt = t // tile
    vec = lambda: pl.BlockSpec((None, 1, d), lambda i, j: (mod_row(i), 0, 0))
    outs = pl.pallas_call(
        functools.partial(_proj_kernel, ncol=ncol, d=d, widths=tuple(widths)),
        grid=(b, nt),
        in_specs=[x_spec, pl.BlockSpec((1, d), lambda i, j: (0, 0)), vec(), vec(),
                  pl.BlockSpec((d, n), lambda i, j: (0, 0))],
        out_specs=[pl.BlockSpec((None, tile, wd), lambda i, j: (i, j, 0)) for wd in widths],
        out_shape=[jax.ShapeDtypeStruct((b, t, wd), F32) for wd in widths],
        compiler_params=_cparams(("parallel", "parallel")),
    )(xv, g.reshape(1, d), sh, sc, w)
    return outs


def _conv4(cur, prev8, nxt8, w_ref, first, last):
    tb, c = cur.shape
    g = tb // 8
    prev8 = jnp.where(first, 0.0, prev8)
    nxt8 = jnp.where(last, 0.0, nxt8)
    ext = jnp.concatenate([prev8, cur, nxt8], axis=0).reshape(g + 2, 8, c)
    sub = lax.broadcasted_iota(jnp.int32, (g, 8, c), 1)

    def back(k):
        r = pltpu.roll(ext, k, 1)
        return jnp.where(sub >= k, r[1:g + 1], r[0:g])

    r = pltpu.roll(ext, 7, 1)
    xp1 = jnp.where(sub < 7, r[1:g + 1], r[2:g + 2])
    y = (w_ref[0:1, :] * back(2) + w_ref[1:2, :] * back(1) + w_ref[2:3, :] * ext[1:g + 1] + w_ref[3:4, :] * xp1)
    return y.reshape(tb, c)


def _halo_specs(tb, t, width, lane_blk, tmap):
    r = tb // 8
    nb8 = t // 8
    return [pl.BlockSpec((None, tb, width), lambda i, j: (i, tmap(j), lane_blk)),
            pl.BlockSpec((None, 8, width), lambda i, j: (i, jnp.maximum(tmap(j) * r - 1, 0), lane_blk)),
            pl.BlockSpec((None, 8, width), lambda i, j: (i, jnp.minimum((tmap(j) + 1) * r, nb8 - 1), lane_blk))]


def _rglru_kernel(*refs, tb, nt, reverse):
    if reverse:
        (u_ref, up_ref, un_ref, cw_ref, cb_ref, wg_ref, gb_ref, cl_ref, h0_ref, gate_ref, hf_ref,
         y_ref, st_ref, carry) = refs
    else:
        (u_ref, up_ref, un_ref, cw_ref, cb_ref, wg_ref, gb_ref, cl_ref, h0_ref,
         y_ref, st_ref, carry) = refs
    j = pl.program_id(1)
    tblk = (nt - 1 - j) if reverse else j
    w = u_ref.shape[-1]

    @pl.when(j == 0)
    def _():
        carry[...] = h0_ref[...]

    u = u_ref[...]
    xc = _conv4(u, up_ref[...], un_ref[...], cw_ref, tblk == 0, tblk == nt - 1) + cb_ref[...]
    gates = _dot(xc.astype(BF16), wg_ref[...]) + gb_ref[...]
    r = jax.nn.sigmoid(gates[:, :w])
    ig = jax.nn.sigmoid(gates[:, w:])
    log_a = r * cl_ref[...]
    a = jnp.exp(log_a)
    th = jnp.tanh(log_a)
    bb = jnp.sqrt(-2.0 * th / (1.0 - th)) * (ig * xc)

    n_grp = tb // 8
    a = a.reshape(n_grp, 8, w)
    bb = bb.reshape(n_grp, 8, w)
    sub = lax.broadcasted_iota(jnp.int32, (n_grp, 8, w), 1)
    s = 1
    while s < 8:
        shift, ok = (8 - s, sub < 8 - s) if reverse else (s, sub >= s)
        a_sh = jnp.where(ok, pltpu.roll(a, shift, 1), 1.0)
        b_sh = jnp.where(ok, pltpu.roll(bb, shift, 1), 0.0)
        bb = a * b_sh + bb
        a = a * a_sh
        s *= 2
    h_prev = carry[...]
    hs = [None] * n_grp
    for g in (range(n_grp - 1, -1, -1) if reverse else range(n_grp)):
        hg = a[g] * h_prev + bb[g]
        h_prev = hg[0:1, :] if reverse else hg[7:8, :]
        hs[g] = hg
    h = jnp.concatenate(hs, axis=0)
    carry[...] = h_prev
    st_ref[...] = carry[...]
    if reverse:
        y_ref[...] = ((hf_ref[...] + h) * jax.nn.gelu(gate_ref[...])).astype(y_ref.dtype)
    else:
        y_ref[...] = h


def _rglru_pass(p_rg, conv_w, conv_b, wg, gate_b, c_lam, h0, hf, *, reverse, tb):
    b, t, w2 = p_rg.shape
    w = w2 // 2
    nt = t // tb
    tmap = (lambda j: nt - 1 - j) if reverse else (lambda j: j)
    const = lambda shape: pl.BlockSpec(shape, lambda i, j: (0,) * len(shape))
    in_specs = _halo_specs(tb, t, w, 0, tmap) + [
        const((4, w)), const((1, w)), const((w, 2 * w)), const((1, 2 * w)), const((1, w)),
        pl.BlockSpec((None, 1, w), lambda i, j: (i, 0, 0))]
    args = [p_rg, p_rg, p_rg, conv_w, conv_b, wg, gate_b, c_lam, h0]
    if reverse:
        in_specs += [pl.BlockSpec((None, tb, w), lambda i, j: (i, tmap(j), 1)),
                     pl.BlockSpec((None, tb, w), lambda i, j: (i, tmap(j), 0))]
        args += [p_rg, hf]
    y, st = pl.pallas_call(
        functools.partial(_rglru_kernel, tb=tb, nt=nt, reverse=reverse),
        grid=(b, nt),
        in_specs=in_specs,
        out_specs=[pl.BlockSpec((None, tb, w), lambda i, j: (i, tmap(j), 0)),
                   pl.BlockSpec((None, 1, w), lambda i, j: (i, 0, 0))],
        out_shape=[jax.ShapeDtypeStruct((b, t, w), BF16 if reverse else F32),
                   jax.ShapeDtypeStruct((b, 1, w), F32)],
        scratch_shapes=[pltpu.VMEM((1, w), F32)],
        compiler_params=_cparams(("parallel", "arbitrary")),
    )(*args)
    return y, st


def _bmm(a, b):
    return lax.dot_general(a, b, (((2,), (1,)), ((0,), (0,))), preferred_element_type=F32)


def _bmm_nt(a, b):
    return lax.dot_general(a, b, (((2,), (2,)), ((0,), (0,))), preferred_element_type=F32)


def _unit_tri_inverse(a, diag_mask):
    c = a.shape[-1]
    eye = (lax.broadcasted_iota(jnp.int32, (1, c, c), 1) == lax.broadcasted_iota(jnp.int32, (1, c, c), 2)).astype(F32)
    mm = lambda p, q: _bmm(p.astype(BF16), q.astype(BF16))
    ad = jnp.where(diag_mask, a, 0.0)
    x = eye - ad
    pw = ad
    k = 2
    while k < GDN_SUB:
        pw = mm(pw, pw)
        x = x + mm(x, pw)
        k *= 2
    n = mm(x, a - ad)
    nblk = c // GDN_SUB
    y = eye - n
    pw = n
    k = 2
    while k < nblk:
        pw = mm(pw, pw)
        y = y + mm(y, pw)
        k *= 2
    return mm(y, x)


def _gdn_prep_kernel(qkv_ref, qp_ref, qn_ref, ab_ref, cw_ref, ea_ref, dtb_ref,
                     qg_ref, kd_ref, kc_ref, wv_ref, at_ref, eg_ref, *, tb, nt):
    c = GDN_CHUNK
    j = pl.program_id(1)
    hw = GDN_WIDTH
    x = _silu(_conv4(qkv_ref[...], qp_ref[...], qn_ref[...], cw_ref, j == 0, j == nt - 1))

    ab = ab_ref[...]
    col = lax.broadcasted_iota(jnp.int32, (tb, AB_PAD), 1)
    is_a = (col & 4) == 0
    gbv = jnp.where(is_a, -ea_ref[...] * _softplus(ab + dtb_ref[...]), jax.nn.sigmoid(ab))

    ri = lax.broadcasted_iota(jnp.int32, (tb, tb), 0)
    ci = lax.broadcasted_iota(jnp.int32, (tb, tb), 1)
    same = (ri // c) == (ci // c)
    l_f = (same & (ci <= ri)).astype(BF16)
    l_b = (same & (ci >= ri)).astype(BF16)
    l_t = same.astype(BF16)
    gcum = jnp.where(col < 8, _dot_sel(l_f, gbv), _dot_sel(l_b, gbv))
    gtot = _dot_sel(l_t, gbv)

    n_combo = 2 * GDN_HEADS
    er = lax.broadcasted_iota(jnp.int32, (AB_PAD, n_combo * 128), 0)
    ec = lax.broadcasted_iota(jnp.int32, (AB_PAD, n_combo * 128), 1) // 128
    src_a = (ec // GDN_HEADS) * 8 + (ec % GDN_HEADS)
    e_a = (er == src_a).astype(BF16)
    e_b = (er == src_a + 4).astype(BF16)
    gc_all = _bcast_cols(gcum, e_a)
    gt_all = _bcast_cols(gtot, e_a)
    be_all = _bcast_cols(gbv, e_b)

    qs, ks, vs = [], [], []
    for hd in range(GDN_HEADS):
        q = x[:, hd * GDN_DK:(hd + 1) * GDN_DK]
        k = x[:, hw + hd * GDN_DK: hw + (hd + 1) * GDN_DK]
        qs.append(q * lax.rsqrt(jnp.sum(q * q, axis=-1, keepdims=True) + EPS) * (GDN_DK ** -0.5))
        ks.append(k * lax.rsqrt(jnp.sum(k * k, axis=-1, keepdims=True) + EPS))
        vs.append(x[:, 2 * hw + hd * GDN_DV: 2 * hw + (hd + 1) * GDN_DV])
    inst = [(ch, dr, hd) for ch in range(tb // c) for dr in range(2) for hd in range(GDN_HEADS)]
    n_inst = len(inst)
    rows = lambda ch: slice(ch * c, (ch + 1) * c)
    lanes = lambda dr, hd: slice((dr * GDN_HEADS + hd) * 128, (dr * GDN_HEADS + hd + 1) * 128)
    stack = lambda f: jnp.stack([f(ch, dr, hd) for ch, dr, hd in inst], axis=0)
    qq = stack(lambda ch, dr, hd: qs[hd][rows(ch)])
    kk = stack(lambda ch, dr, hd: ks[hd][rows(ch)])
    vv = stack(lambda ch, dr, hd: vs[hd][rows(ch)])
    gcb = stack(lambda ch, dr, hd: gc_all[rows(ch), lanes(dr, hd)])
    gtb = stack(lambda ch, dr, hd: gt_all[rows(ch), lanes(dr, hd)])
    beb = stack(lambda ch, dr, hd: be_all[rows(ch), lanes(dr, hd)])

    fwd = ((lax.broadcasted_iota(jnp.int32, (n_inst, c, 128), 0) // GDN_HEADS) % 2) == 0
    ii = lax.broadcasted_iota(jnp.int32, (n_inst, c, 128), 1)
    jj = lax.broadcasted_iota(jnp.int32, (n_inst, c, 128), 2)
    incl = (fwd & (jj <= ii)) | (jnp.logical_not(fwd) & (jj >= ii) & (jj < c))
    fwd_c = ((lax.broadcasted_iota(jnp.int32, (n_inst, c, c), 0) // GDN_HEADS) % 2) == 0
    i64 = lax.broadcasted_iota(jnp.int32, (n_inst, c, c), 1)
    j64 = lax.broadcasted_iota(jnp.int32, (n_inst, c, c), 2)
    strict = (fwd_c & (j64 < i64)) | (jnp.logical_not(fwd_c) & (j64 > i64))
    diag_blk = (i64 // GDN_SUB) == (j64 // GDN_SUB)

    eg = jnp.exp(gcb)
    kb = kk * beb
    diag = jnp.where(ii == jj, gcb, 0.0)
    gcr = _dot_sel(jnp.ones((c, c), BF16), jnp.concatenate([diag[g] for g in range(n_inst)], axis=1))
    gcr = jnp.stack([gcr[:, g * 128:(g + 1) * 128] for g in range(n_inst)], axis=0)
    decay = jnp.where(incl, jnp.exp(jnp.where(incl, gcb - gcr, 0.0)), 0.0)
    kpad = jnp.concatenate([kk.astype(BF16), jnp.zeros((n_inst, c, GDN_DK), BF16)], axis=1)
    qk = _bmm_nt(jnp.concatenate([qq, kb], axis=1).astype(BF16), kpad)
    attn = (qk[:, :c] * decay).astype(BF16)
    a_mat = jnp.where(strict, (qk[:, c:] * decay)[:, :, :c], 0.0)
    tinv = _unit_tri_inverse(a_mat, diag_blk)
    rhs = jnp.concatenate([vv * beb, kb * eg], axis=2)
    sol = _bmm(tinv.astype(BF16), rhs.astype(BF16))
    qg = (qq * eg).astype(BF16)
    kd = (kk * jnp.exp(gtb - gcb)).astype(BF16)
    egt = jnp.exp(gtb[:, 0:8, :])
    for g, (ch, dr, hd) in enumerate(inst):
        rs = rows(ch)
        ls = slice(hd * 128, (hd + 1) * 128)
        qg_ref[dr, rs, ls] = qg[g]
        kd_ref[dr, rs, ls] = kd[g]
        kc_ref[dr, rs, ls] = sol[g, :, GDN_DV:].astype(BF16)
        wv_ref[dr, rs, ls] = sol[g, :, :GDN_DV]
        at_ref[dr, rs, ls] = attn[g]
        eg_ref[dr, ch, :, ls] = egt[g]


def _bcast_cols(x, sel):
    hi, mid, lo = _split3(x)
    return _dot(lo, sel) + _dot(mid, sel) + _dot(hi, sel)


def _gdn_prep(qkv, ab, conv_w, ea_row, dtb_row, *, tb):
    b, t, cw = qkv.shape
    nt = t // tb
    nch = t // GDN_CHUNK
    cpb = tb // GDN_CHUNK
    const = lambda shape: pl.BlockSpec(shape, lambda i, j: (0,) * len(shape))
    hw = GDN_WIDTH
    big = lambda: pl.BlockSpec((2, None, tb, hw), lambda i, j: (0, i, j, 0))
    outs = pl.pallas_call(
        functools.partial(_gdn_prep_kernel, tb=tb, nt=nt),
        grid=(b, nt),
        in_specs=_halo_specs(tb, t, cw, 0, lambda j: j) + [
            pl.BlockSpec((None, tb, AB_PAD), lambda i, j: (i, j, 0)),
            const((4, cw)), const((1, AB_PAD)), const((1, AB_PAD))],
        out_specs=[big(), big(), big(), big(), big(),
                   pl.BlockSpec((2, None, cpb, 8, hw), lambda i, j: (0, i, j, 0, 0))],
        out_shape=[jax.ShapeDtypeStruct((2, b, t, hw), BF16),
                   jax.ShapeDtypeStruct((2, b, t, hw), BF16),
                   jax.ShapeDtypeStruct((2, b, t, hw), BF16),
                   jax.ShapeDtypeStruct((2, b, t, hw), F32),
                   jax.ShapeDtypeStruct((2, b, t, hw), BF16),
                   jax.ShapeDtypeStruct((2, b, nch, 8, hw), F32)],
        compiler_params=_cparams(("parallel", "parallel")),
    )(qkv, qkv, qkv, ab, conv_w, ea_row, dtb_row)
    return outs


def _gdn_scan_kernel(qg_f, kd_f, kc_f, wv_f, at_f, eg_f, qg_b, kd_b, kc_b, wv_b, at_b, eg_b, s0_ref,
                     of_ref, ob_ref, sfin_ref, s_scr):
    c = GDN_CHUNK
    j = pl.program_id(1)

    @pl.when(j == 0)
    def _():
        s_scr[...] = s0_ref[...]

    dirs = ((qg_f, kd_f, kc_f, wv_f, at_f, eg_f), (qg_b, kd_b, kc_b, wv_b, at_b, eg_b))
    chains = [(dr, hd) for dr in range(2) for hd in range(GDN_HEADS)]
    ls = lambda hd: slice(hd * 128, (hd + 1) * 128)
    stack = lambda f: jnp.stack([f(dirs[dr], ls(hd)) for dr, hd in chains], axis=0)
    kq = stack(lambda r, l: jnp.concatenate([r[2][:, l], r[0][:, l]], axis=0))
    wv = stack(lambda r, l: r[3][:, l])
    at = stack(lambda r, l: r[4][:, l][:, :c])
    eg = stack(lambda r, l: r[5][0:1, l])
    s = s_scr[...].reshape(2 * GDN_HEADS, GDN_DK, GDN_DV)
    r = _bmm(kq, s.astype(BF16))
    vb = (wv - r[:, :c]).astype(BF16)
    o = r[:, c:] + _bmm(at, vb)
    for g, (dr, hd) in enumerate(chains):
        (of_ref, ob_ref)[dr][:, ls(hd)] = o[g]
        s_scr[dr, hd] = s[g] * eg[g] + _dot_tn(dirs[dr][1][:, ls(hd)], vb[g])
    sfin_ref[...] = s_scr[...]


def _gdn_scan(prep, s0):
    qg, kd, kc, wv, at, eg = prep
    _, b, t, hw = qg.shape
    c = GDN_CHUNK
    nch = t // c
    fw = lambda: pl.BlockSpec((None, None, c, hw), lambda i, j: (0, i, j, 0))
    bw = lambda: pl.BlockSpec((None, None, c, hw), lambda i, j: (1, i, nch - 1 - j, 0))
    egf = pl.BlockSpec((None, None, None, 8, hw), lambda i, j: (0, i, j, 0, 0))
    egb = pl.BlockSpec((None, None, None, 8, hw), lambda i, j: (1, i, nch - 1 - j, 0, 0))
    st = pl.BlockSpec((None, 2, GDN_HEADS, GDN_DK, GDN_DV), lambda i, j: (i, 0, 0, 0, 0))
    o_f, o_b, s_fin = pl.pallas_call(
        _gdn_scan_kernel,
        grid=(b, nch),
        in_specs=[fw(), fw(), fw(), fw(), fw(), egf, bw(), bw(), bw(), bw(), bw(), egb, st],
        out_specs=[pl.BlockSpec((None, c, hw), lambda i, j: (i, j, 0)),
                   pl.BlockSpec((None, c, hw), lambda i, j: (i, nch - 1 - j, 0)),
                   st],
        out_shape=[jax.ShapeDtypeStruct((b, t, hw), F32), jax.ShapeDtypeStruct((b, t, hw), F32),
                   jax.ShapeDtypeStruct((b, 2, GDN_HEADS, GDN_DK, GDN_DV), F32)],
        scratch_shapes=[pltpu.VMEM((2, GDN_HEADS, GDN_DK, GDN_DV), F32)],
        compiler_params=_cparams(("parallel", "arbitrary")),
    )(qg, kd, kc, wv, at, eg, qg, kd, kc, wv, at, eg, s0)
    return o_f, o_b, s_fin


def _out_rg_kernel(x_ref, y_ref, w_ref, gt_ref, o_ref):
    o_ref[...] = x_ref[...] + gt_ref[...] * _dot(y_ref[...], w_ref[...])


def _out_gdn_kernel(x_ref, of_ref, ob_ref, z_ref, ng_ref, w_ref, gt_ref, o_ref, *, ncol, d):
    o = of_ref[...] + ob_ref[...]
    z = z_ref[...]
    parts = []
    for hd in range(GDN_HEADS):
        ls = slice(hd * GDN_DV, (hd + 1) * GDN_DV)
        parts.append(_rms(o[:, ls], ng_ref[...]) * _silu(z[:, ls]))
    y = jnp.concatenate(parts, axis=1).astype(BF16)
    t2 = gt_ref[...] * _dot(y, w_ref[...])
    rows = t2.shape[0] // ncol
    for j in range(ncol):
        o_ref[:, j, :] = x_ref[:, j, :] + t2[j * rows:(j + 1) * rows]


def _out_project(x3, y_rg, o_f, o_b, z, norm_g, w_out, gt, *, tile):
    b, t, d = x3.shape
    w = RG_WIDTH
    vec = lambda: pl.BlockSpec((None, 1, d), lambda i, j: (i, 0, 0))
    x1 = pl.pallas_call(
        _out_rg_kernel,
        grid=(b, t // tile),
        in_specs=[pl.BlockSpec((None, tile, d), lambda i, j: (i, j, 0)),
                  pl.BlockSpec((None, tile, w), lambda i, j: (i, j, 0)),
                  pl.BlockSpec((w, d), lambda i, j: (0, 0)), vec()],
        out_specs=pl.BlockSpec((None, tile, d), lambda i, j: (i, j, 0)),
        out_shape=jax.ShapeDtypeStruct((b, t, d), F32),
        compiler_params=_cparams(("parallel", "parallel")),
    )(x3, y_rg, w_out[:w], gt)
    rows = t // GRID_W
    ncol = tile // rows
    hw = GDN_WIDTH
    x1v = x1.reshape(b, rows, GRID_W, d)
    cm = lambda wd: pl.BlockSpec((None, tile, wd), lambda i, j: (i, j, 0))
    x1v = pl.pallas_call(
        functools.partial(_out_gdn_kernel, ncol=ncol, d=d),
        grid=(b, GRID_W // ncol),
        in_specs=[pl.BlockSpec((None, rows, ncol, d), lambda i, j: (i, 0, j, 0)),
                  cm(hw), cm(hw), cm(hw),
                  pl.BlockSpec((1, GDN_DV), lambda i, j: (0, 0)),
                  pl.BlockSpec((hw, d), lambda i, j: (0, 0)), vec()],
        out_specs=pl.BlockSpec((None, rows, ncol, d), lambda i, j: (i, 0, j, 0)),
        out_shape=jax.ShapeDtypeStruct((b, rows, GRID_W, d), F32),
        input_output_aliases={0: 0},
        compiler_params=_cparams(("parallel", "parallel")),
    )(x1v, o_f, o_b, z, norm_g, w_out[w:], gt)
    return x1v.reshape(b, t, d)


def _fold_kernel(wq_ref, k_ref, o_ref):
    o_ref[...] = lax.dot_general(wq_ref[...], k_ref[...], (((1,), (1,)), ((), ())),
                                 preferred_element_type=F32, precision=lax.Precision.HIGHEST)


def _fold_keys(wq, keys):
    d, n = wq.shape
    nblk = n // PEER_HALF
    return pl.pallas_call(
        _fold_kernel,
        grid=(nblk,),
        in_specs=[pl.BlockSpec((d, PEER_HALF), lambda j: (0, j)),
                  pl.BlockSpec((None, PEER_NKEYS, PEER_HALF), lambda j: (j % 2, 0, 0))],
        out_specs=pl.BlockSpec((d, PEER_NKEYS), lambda j: (0, j)),
        out_shape=jax.ShapeDtypeStruct((d, nblk * PEER_NKEYS), F32),
        compiler_params=_cparams(("parallel",)),
    )(wq, keys)


def _top16_rows(vals, s_scr, i_scr):
    n, p = vals.shape
    rid = lax.broadcasted_iota(jnp.int32, (n, p), 0)
    for r in range(PEER_TOPK):
        m = jnp.max(vals, axis=0, keepdims=True)
        idx = jnp.min(jnp.where(vals == m, rid, n), axis=0, keepdims=True)
        s_scr[r:r + 1, :] = m
        i_scr[r:r + 1, :] = idx
        vals = jnp.where(rid == idx, NEG_INF, vals)


def _peer_sel_kernel(x_ref, g_ref, sh_ref, sc_ref, wt_ref, h_ref, a_ref, b_ref, gate_ref,
                     sc_scr, s1, i1, s2, i2, ts, tp, a_t, b_t, g_t):
    k = PEER_TOPK
    h = _rms(x_ref[...], g_ref[...]) * (1.0 + sc_ref[...]) + sh_ref[...]
    hb = h.astype(BF16)
    h_ref[...] = hb
    sc_scr[...] = _dot_nt(wt_ref[...], hb)
    p = hb.shape[0]

    sub = lax.broadcasted_iota(jnp.int32, (8, p), 0)
    sub16 = lax.broadcasted_iota(jnp.int32, (16, p), 0)

    def head(hd, carry):
        base = pl.multiple_of(hd * 2 * PEER_NKEYS, 2 * PEER_NKEYS)
        _top16_rows(sc_scr[pl.ds(base, PEER_NKEYS), :], s1, i1)
        _top16_rows(sc_scr[pl.ds(base + PEER_NKEYS, PEER_NKEYS), :], s2, i2)
        s1v, s2v = s1[...], s2[...]
        cands = [s1v[0:1, :] + s2v]
        poss = [sub16]
        for r in range(1, 8):
            cmax = k // (r + 1)
            cands.append(jnp.where(sub < cmax, s1v[r:r + 1, :] + s2v[0:8, :], NEG_INF))
            poss.append(r * k + sub)
        cands.append(s1v[8:16, :] + s2v[0:1, :])
        poss.append((sub + 8) * k)
        cand = jnp.concatenate(cands, axis=0)
        pos = jnp.concatenate(poss, axis=0)
        big = k * k
        for r in range(k):
            m = jnp.max(cand, axis=0, keepdims=True)
            sel = jnp.min(jnp.where(cand == m, pos, big), axis=0, keepdims=True)
            ts[r:r + 1, :] = m
            tp[r:r + 1, :] = sel
            cand = jnp.where(pos == sel, NEG_INF, cand)
        top_s, top_p = ts[...], tp[...]
        rr = top_p >> 4
        cc = top_p & (k - 1)
        i1v, i2v = i1[...], i2[...]
        av = jnp.zeros_like(top_p)
        bv = jnp.zeros_like(top_p)
        for q in range(k):
            av = jnp.where(rr == q, i1v[q:q + 1, :], av)
            bv = jnp.where(cc == q, i2v[q:q + 1, :], bv)
        e = jnp.exp(top_s - top_s[0:1, :])
        gate = e / jnp.sum(e, axis=0, keepdims=True)
        o = pl.multiple_of(hd * k, k)
        a_t[pl.ds(o, k), :] = av
        b_t[pl.ds(o, k), :] = bv
        g_t[pl.ds(o, k), :] = gate
        return carry

    lax.fori_loop(0, PEER_HEADS, head, 0)
    a_ref[...] = a_t[...].T
    b_ref[...] = b_t[...].T
    gate_ref[...] = g_t[...].T


def _peer_select(x1, g, sh, sc, wt, *, tile):
    b, t, d = x1.shape
    nt = t // tile
    nk = PEER_HEADS * PEER_TOPK
    nrow = wt.shape[0]
    vec = lambda: pl.BlockSpec((None, 1, d), lambda i, j: (i, 0, 0))
    tr = lambda: pl.BlockSpec((None, tile, nk), lambda i, j: (i, j, 0))
    k = PEER_TOPK
    return pl.pallas_call(
        _peer_sel_kernel,
        grid=(b, nt),
        in_specs=[pl.BlockSpec((None, tile, d), lambda i, j: (i, j, 0)),
                  pl.BlockSpec((1, d), lambda i, j: (0, 0)), vec(), vec(),
                  pl.BlockSpec((nrow, d), lambda i, j: (0, 0))],
        out_specs=[pl.BlockSpec((None, tile, d), lambda i, j: (i, j, 0)), tr(), tr(), tr()],
        out_shape=[jax.ShapeDtypeStruct((b, t, d), BF16),
                   jax.ShapeDtypeStruct((b, t, nk), jnp.int32),
                   jax.ShapeDtypeStruct((b, t, nk), jnp.int32),
                   jax.ShapeDtypeStruct((b, t, nk), F32)],
        scratch_shapes=[pltpu.VMEM((nrow, tile), F32),
                        pltpu.VMEM((k, tile), F32), pltpu.VMEM((k, tile), jnp.int32),
                        pltpu.VMEM((k, tile), F32), pltpu.VMEM((k, tile), jnp.int32),
                        pltpu.VMEM((k, tile), F32), pltpu.VMEM((k, tile), jnp.int32),
                        pltpu.VMEM((nk, tile), jnp.int32), pltpu.VMEM((nk, tile), jnp.int32),
                        pltpu.VMEM((nk, tile), F32)],
        compiler_params=_cparams(("parallel", "parallel")),
    )(x1, g.reshape(1, d), sh, sc, wt)


def _peer_mix_kernel(h_ref, a_ref, b_ref, gate_ref, u_ref, v_ref, x_ref, gt_ref, fg_ref, o_ref,
                     m_scr, acc, *, tile, n_steps, pairs):
    nk = PEER_NKEYS
    half = nk // 2
    step = pl.program_id(2)
    hi_mask = jnp.uint32(0xFFFF0000)

    @pl.when(step == 0)
    def _():
        sub = lax.broadcasted_iota(jnp.int32, (nk, a_ref.shape[1]), 0)
        a_of_row = jnp.where(sub < half, 2 * sub, 2 * (sub - half) + 1)

        def build(p, carry):
            arow = a_ref[pl.ds(p, 1), :]
            brow = b_ref[pl.ds(p, 1), :]
            grow = gate_ref[pl.ds(p, 1), :]
            xa = jnp.where(a_of_row == arow, 1.0, 0.0).astype(BF16)
            yb = jnp.where(sub == brow, grow, 0.0).astype(BF16)
            m = _dot_nt(xa, yb).astype(BF16).astype(F32)
            bits = lax.bitcast_convert_type(m, jnp.uint32)
            m_scr[pl.ds(pl.multiple_of(p * half, half), half), :] = (bits[half:] & hi_mask) | (bits[:half] >> 16)
            return carry

        lax.fori_loop(0, tile, build, 0, unroll=16)

    h = h_ref[...]
    parts = []
    for q in range(pairs):
        pr = step * pairs + q
        act = jax.nn.gelu(_dot_nt(h, u_ref[q * 2 * nk:(q + 1) * 2 * nk, :]))
        w = m_scr[pl.ds(pr, tile, stride=half), :]
        m_even = lax.bitcast_convert_type(w << 16, F32)
        m_odd = lax.bitcast_convert_type(w & hi_mask, F32)
        parts.append((act * jnp.concatenate([m_even, m_odd], axis=1)).astype(BF16))
    contrib = _dot(jnp.concatenate(parts, axis=1), v_ref[...])

    @pl.when(step == 0)
    def _():
        acc[...] = contrib

    @pl.when(step > 0)
    def _():
        acc[...] += contrib

    @pl.when(step == n_steps - 1)
    def _():
        x2 = x_ref[...] + gt_ref[...] * acc[...]
        o_ref[...] = _rms(x2, fg_ref[...])


def _peer_mix(hb, aidx, bidx, gate, u, v, x1, gt, final_g, *, tile, pairs):
    b, t, d = x1.shape
    nt = t // tile
    nk = PEER_NKEYS
    npk = aidx.shape[-1]
    n_steps = nk // (2 * pairs)
    eb = 2 * nk * pairs
    tok = lambda wd: pl.BlockSpec((None, tile, wd), lambda i, j, s: (i, j, 0))
    tab = lambda: pl.BlockSpec((eb, d), lambda i, j, s: (s, 0))
    return pl.pallas_call(
        functools.partial(_peer_mix_kernel, tile=tile, n_steps=n_steps, pairs=pairs),
        grid=(b, nt, n_steps),
        in_specs=[tok(d), tok(npk), tok(npk), tok(npk), tab(), tab(), tok(d),
                  pl.BlockSpec((None, 1, d), lambda i, j, s: (i, 0, 0)),
                  pl.BlockSpec((1, d), lambda i, j, s: (0, 0))],
        out_specs=tok(d),
        out_shape=jax.ShapeDtypeStruct((b, t, d), F32),
        scratch_shapes=[pltpu.VMEM((tile * nk // 2, nk), jnp.uint32), pltpu.VMEM((tile, d), F32)],
        compiler_params=_cparams(("parallel", "parallel", "arbitrary")),
    )(hb, aidx, bidx, gate, u, v, x1, gt, final_g.reshape(1, d))


def _block_diag(w):
    n, e, _ = w.shape
    eye = jnp.eye(n, dtype=w.dtype)
    return (eye[:, None, :, None] * w[:, :, None, :]).reshape(n * e, n * e)


def _mix_sequence(x3, ctx_mode, params, states, mods, *, rg_tb, gdn_tb, proj_tile):
    (norm1_g, w_rg, w_gdn, rg_conv_w, rg_conv_b, wg, gate_b, c_lam, gdn_conv_w, ea_row, dtb_row) = params
    sh1, sc1, mod_row = mods
    rg_h0_f, rg_h0_b, gdn_s0 = states
    (p_rg,) = _project(x3, norm1_g, sh1, sc1, mod_row, w_rg, (2 * RG_WIDTH,), colmajor=False, tile=proj_tile)
    qkv, z, ab = _project(x3, norm1_g, sh1, sc1, mod_row, w_gdn, (3 * GDN_WIDTH, GDN_WIDTH, AB_PAD),
                          colmajor=not ctx_mode, tile=proj_tile)
    h_f, st_f = _rglru_pass(p_rg, rg_conv_w, rg_conv_b, wg[0], gate_b[0], c_lam[0], rg_h0_f, None,
                            reverse=False, tb=rg_tb)
    y_rg, st_b = _rglru_pass(p_rg, rg_conv_w, rg_conv_b, wg[1], gate_b[1], c_lam[1], rg_h0_b, h_f,
                             reverse=True, tb=rg_tb)
    prep = _gdn_prep(qkv, ab, gdn_conv_w, ea_row, dtb_row, tb=gdn_tb)
    o_f, o_b, s_fin = _gdn_scan(prep, gdn_s0)
    return y_rg, o_f, o_b, z, (st_f, st_b, s_fin)


def kernel(x, c, ctx, c_ctx, w_mod, b_mod, norm1_g, norm2_g, w_in, rg_conv_w, rg_conv_b, rg_gate_w, rg_gate_b,
           rg_lambda, gdn_conv_w, gdn_a_log, gdn_dt_bias, gdn_norm_g, w_out, peer_wq, peer_keys, peer_u, peer_v,
           final_g):
    b, t, d = x.shape
    depth = w_mod.shape[0]
    assert depth == 1, "context residual stream update is only needed for depth > 1"
    l = 0
    w = RG_WIDTH

    cc = jnp.zeros((16, d), F32).at[:b].set(c).at[b].set(c_ctx)
    w_rg = w_in[l][:, :2 * w].astype(BF16)
    n_ab = w_in.shape[2] - 2 * w - 4 * GDN_WIDTH
    w_gdn = jnp.concatenate([w_in[l][:, 2 * w:2 * w + 4 * GDN_WIDTH],
                             jnp.pad(w_in[l][:, 2 * w + 4 * GDN_WIDTH:], ((0, 0), (0, AB_PAD - n_ab)))],
                            axis=1).astype(BF16)
    wg = jnp.stack([jnp.concatenate([_block_diag(rg_gate_w[l, dr, 0]), _block_diag(rg_gate_w[l, dr, 1])], axis=1)
                    for dr in range(2)]).astype(BF16)
    gate_b = rg_gate_b[l].reshape(2, 1, 2 * w)
    c_lam = (-RG_C * jax.nn.softplus(-rg_lambda[l])).reshape(2, 1, w)
    ea = jnp.exp(gdn_a_log[l])
    pad4 = jnp.zeros((2, GDN_HEADS), F32)
    ea_row = jnp.pad(jnp.concatenate([ea, pad4], axis=1).reshape(1, -1), ((0, 0), (0, AB_PAD - n_ab)))
    dtb_row = jnp.pad(jnp.concatenate([gdn_dt_bias[l], pad4], axis=1).reshape(1, -1), ((0, 0), (0, AB_PAD - n_ab)))
    params = (norm1_g[l], w_rg, w_gdn, rg_conv_w[l], rg_conv_b[l].reshape(1, w), wg, gate_b, c_lam,
              gdn_conv_w[l], ea_row, dtb_row)

    mod = _modulation(cc, w_mod[l], b_mod[l])
    sh1, sc1, gt1, sh2, sc2, gt2 = [mod[:, i * d:(i + 1) * d].reshape(16, 1, d) for i in range(6)]

    tc = ctx.shape[1]
    zero_states = (jnp.zeros((b, 1, w), F32), jnp.zeros((b, 1, w), F32),
                   jnp.zeros((b, 2, GDN_HEADS, GDN_DK, GDN_DV), F32))
    ctx_tb = min(tc, 256)
    _, _, _, _, ctx_states = _mix_sequence(ctx, True, params, zero_states, (sh1, sc1, lambda i: b),
                                           rg_tb=ctx_tb, gdn_tb=min(tc, 128), proj_tile=ctx_tb)

    tile = min(t, 512)
    y_rg, o_f, o_b, z, _ = _mix_sequence(x, False, params, ctx_states, (sh1, sc1, lambda i: i),
                                         rg_tb=min(t, 256), gdn_tb=128, proj_tile=tile)
    x1 = _out_project(x, y_rg, o_f, o_b, z, gdn_norm_g[l].reshape(1, GDN_DV), w_out[l].astype(BF16), gt1, tile=tile)

    wfold_t = _fold_keys(peer_wq[l], peer_keys[l]).T.astype(BF16)
    sel_tile = min(t, 256)
    hb, aidx, bidx, gate = _peer_select(x1, norm2_g[l], sh2, sc2, wfold_t, tile=sel_tile)
    out = _peer_mix(hb, aidx, bidx, gate, peer_u[l].astype(BF16), peer_v[l].astype(BF16),
                    x1, gt2, final_g, tile=tile, pairs=8)
    return out
```

```python
import functools
import math

import jax
import jax.numpy as jnp
from jax import lax
from jax.experimental import pallas as pl
from jax.experimental.pallas import tpu as pltpu

F32 = jnp.float32
BF16 = jnp.bfloat16

GRID_W = 64
EPS = 1e-6
RG_WIDTH = 512
RG_BLOCKS = 8
RG_C = 8.0
GDN_HEADS = 4
GDN_DK = 128
GDN_DV = 128
GDN_WIDTH = GDN_HEADS * GDN_DV
GDN_CHUNK = 64
GDN_SUB = 16
AB_PAD = 128
PEER_HEADS = 8
PEER_NKEYS = 128
PEER_HALF = 128
PEER_TOPK = 16
NEG_INF = float("-inf")

VMEM_LIMIT = 56 * 1024 * 1024


def _cparams(sem):
    return pltpu.CompilerParams(dimension_semantics=sem, vmem_limit_bytes=VMEM_LIMIT)


def _dot(a, b):
    return jnp.dot(a, b, preferred_element_type=F32)


def _dot_nt(a, b):
    return lax.dot_general(a, b, (((1,), (1,)), ((), ())), preferred_element_type=F32)


def _dot_tn(a, b):
    return lax.dot_general(a, b, (((0,), (0,)), ((), ())), preferred_element_type=F32)


def _split3(x):
    hi = x.astype(BF16)
    r = x - hi.astype(F32)
    mid = r.astype(BF16)
    lo = (r - mid.astype(F32)).astype(BF16)
    return hi, mid, lo


def _dot_sel(m, x):
    hi, mid, lo = _split3(x)
    return _dot(m, lo) + _dot(m, mid) + _dot(m, hi)


def _silu(x):
    return x * jax.nn.sigmoid(x)


def _softplus(x):
    return jnp.maximum(x, 0.0) + jnp.log1p(jnp.exp(-jnp.abs(x)))


def _rms(x, g):
    return x * lax.rsqrt(jnp.mean(x * x, axis=-1, keepdims=True) + EPS) * g


def _mod_kernel(c_ref, w_ref, b_ref, o_ref):
    s = _silu(c_ref[...])
    o_ref[...] = jnp.dot(s, w_ref[...], preferred_element_type=F32,
                         precision=lax.Precision.HIGHEST) + b_ref[...]


def _modulation(cc, w_mod, b_mod):
    m, d = cc.shape
    n = w_mod.shape[1]
    tn = 1536
    return pl.pallas_call(
        _mod_kernel,
        grid=(n // tn,),
        in_specs=[pl.BlockSpec((m, d), lambda j: (0, 0)),
                  pl.BlockSpec((d, tn), lambda j: (0, j)),
                  pl.BlockSpec((1, tn), lambda j: (0, j))],
        out_specs=pl.BlockSpec((m, tn), lambda j: (0, j)),
        out_shape=jax.ShapeDtypeStruct((m, n), F32),
        compiler_params=_cparams(("arbitrary",)),
    )(cc, w_mod, b_mod.reshape(1, n))


def _proj_kernel(x_ref, g_ref, sh_ref, sc_ref, w_ref, *o_refs, ncol, d, widths):
    if ncol > 1:
        x = jnp.concatenate([x_ref[:, j, :] for j in range(ncol)], axis=0)
    else:
        x = x_ref[...]
    h = _rms(x, g_ref[...]) * (1.0 + sc_ref[...]) + sh_ref[...]
    o = _dot(h.astype(BF16), w_ref[...])
    off = 0
    for o_ref, wd in zip(o_refs, widths):
        o_ref[...] = o[:, off:off + wd]
        off += wd


def _project(x3, g, sh, sc, mod_row, w, widths, *, colmajor, tile):
    b, t, d = x3.shape
    n = w.shape[1]
    if colmajor:
        rows = t // GRID_W
        ncol = tile // rows
        xv = x3.reshape(b, rows, GRID_W, d)
        x_spec = pl.BlockSpec((None, rows, ncol, d), lambda i, j: (i, 0, j, 0))
        nt = GRID_W // ncol
    else:
        ncol = 1
        xv = x3
        x_spec = pl.BlockSpec((None, tile, d), lambda i, j: (i, j, 0))
        nt = t // tile
    vec = lambda: pl.BlockSpec((None, 1, d), lambda i, j: (mod_row(i), 0, 0))
    outs = pl.pallas_call(
        functools.partial(_proj_kernel, ncol=ncol, d=d, widths=tuple(widths)),
        grid=(b, nt),
        in_specs=[x_spec, pl.BlockSpec((1, d), lambda i, j: (0, 0)), vec(), vec(),
                  pl.BlockSpec((d, n), lambda i, j: (0, 0))],
        out_specs=[pl.BlockSpec((None, tile, wd), lambda i, j: (i, j, 0)) for wd in widths],
        out_shape=[jax.ShapeDtypeStruct((b, t, wd), F32) for wd in widths],
        compiler_params=_cparams(("parallel", "parallel")),
    )(xv, g.reshape(1, d), sh, sc, w)
    return outs


def _conv4(cur, prev8, nxt8, w_ref, first, last):
    tb, c = cur.shape
    g = tb // 8
    prev8 = jnp.where(first, 0.0, prev8)
    nxt8 = jnp.where(last, 0.0, nxt8)
    ext = jnp.concatenate([prev8, cur, nxt8], axis=0).reshape(g + 2, 8, c)
    sub = lax.broadcasted_iota(jnp.int32, (g, 8, c), 1)

    def back(k):
        r = pltpu.roll(ext, k, 1)
        return jnp.where(sub >= k, r[1:g + 1], r[0:g])

    r = pltpu.roll(ext, 7, 1)
    xp1 = jnp.where(sub < 7, r[1:g + 1], r[2:g + 2])
    y = (w_ref[0:1, :] * back(2) + w_ref[1:2, :] * back(1) + w_ref[2:3, :] * ext[1:g + 1] + w_ref[3:4, :] * xp1)
    return y.reshape(tb, c)


def _halo_specs(tb, t, width, lane_blk, tmap):
    r = tb // 8
    nb8 = t // 8
    return [pl.BlockSpec((None, tb, width), lambda i, j: (i, tmap(j), lane_blk)),
            pl.BlockSpec((None, 8, width), lambda i, j: (i, jnp.maximum(tmap(j) * r - 1, 0), lane_blk)),
            pl.BlockSpec((None, 8, width), lambda i, j: (i, jnp.minimum((tmap(j) + 1) * r, nb8 - 1), lane_blk))]


def _rglru_kernel(*refs, tb, nt, reverse):
    if reverse:
        (u_ref, up_ref, un_ref, cw_ref, cb_ref, wg_ref, gb_ref, cl_ref, h0_ref, gate_ref, hf_ref,
         y_ref, st_ref, carry) = refs
    else:
        (u_ref, up_ref, un_ref, cw_ref, cb_ref, wg_ref, gb_ref, cl_ref, h0_ref,
         y_ref, st_ref, carry) = refs
    j = pl.program_id(1)
    tblk = (nt - 1 - j) if reverse else j
    w = u_ref.shape[-1]

    @pl.when(j == 0)
    def _():
        carry[...] = h0_ref[...]

    u = u_ref[...]
    xc = _conv4(u, up_ref[...], un_ref[...], cw_ref, tblk == 0, tblk == nt - 1) + cb_ref[...]
    gates = _dot(xc.astype(BF16), wg_ref[...]) + gb_ref[...]
    r = jax.nn.sigmoid(gates[:, :w])
    ig = jax.nn.sigmoid(gates[:, w:])
    log_a = r * cl_ref[...]
    a = jnp.exp(log_a)
    th = jnp.tanh(log_a)
    bb = jnp.sqrt(-2.0 * th / (1.0 - th)) * (ig * xc)

    n_grp = tb // 8
    a = a.reshape(n_grp, 8, w)
    bb = bb.reshape(n_grp, 8, w)
    sub = lax.broadcasted_iota(jnp.int32, (n_grp, 8, w), 1)
    s = 1
    while s < 8:
        shift, ok = (8 - s, sub < 8 - s) if reverse else (s, sub >= s)
        a_sh = jnp.where(ok, pltpu.roll(a, shift, 1), 1.0)
        b_sh = jnp.where(ok, pltpu.roll(bb, shift, 1), 0.0)
        bb = a * b_sh + bb
        a = a * a_sh
        s *= 2
    h_prev = carry[...]
    hs = [None] * n_grp
    for g in (range(n_grp - 1, -1, -1) if reverse else range(n_grp)):
        hg = a[g] * h_prev + bb[g]
        h_prev = hg[0:1, :] if reverse else hg[7:8, :]
        hs[g] = hg
    h = jnp.concatenate(hs, axis=0)
    carry[...] = h_prev
    st_ref[...] = carry[...]
    if reverse:
        y_ref[...] = ((hf_ref[...] + h) * jax.nn.gelu(gate_ref[...])).astype(y_ref.dtype)
    else:
        y_ref[...] = h


def _rglru_pass(p_rg, conv_w, conv_b, wg, gate_b, c_lam, h0, hf, *, reverse, tb):
    b, t, w2 = p_rg.shape
    w = w2 // 2
    nt = t // tb
    tmap = (lambda j: nt - 1 - j) if reverse else (lambda j: j)
    const = lambda shape: pl.BlockSpec(shape, lambda i, j: (0,) * len(shape))
    in_specs = _halo_specs(tb, t, w, 0, tmap) + [
        const((4, w)), const((1, w)), const((w, 2 * w)), const((1, 2 * w)), const((1, w)),
        pl.BlockSpec((None, 1, w), lambda i, j: (i, 0, 0))]
    args = [p_rg, p_rg, p_rg, conv_w, conv_b, wg, gate_b, c_lam, h0]
    if reverse:
        in_specs += [pl.BlockSpec((None, tb, w), lambda i, j: (i, tmap(j), 1)),
                     pl.BlockSpec((None, tb, w), lambda i, j: (i, tmap(j), 0))]
        args += [p_rg, hf]
    y, st = pl.pallas_call(
        functools.partial(_rglru_kernel, tb=tb, nt=nt, reverse=reverse),
        grid=(b, nt),
        in_specs=in_specs,
        out_specs=[pl.BlockSpec((None, tb, w), lambda i, j: (i, tmap(j), 0)),
                   pl.BlockSpec((None, 1, w), lambda i, j: (i, 0, 0))],
        out_shape=[jax.ShapeDtypeStruct((b, t, w), BF16 if reverse else F32),
                   jax.ShapeDtypeStruct((b, 1, w), F32)],
        scratch_shapes=[pltpu.VMEM((1, w), F32)],
        compiler_params=_cparams(("parallel", "arbitrary")),
    )(*args)
    return y, st


def _bmm(a, b):
    return lax.dot_general(a, b, (((2,), (1,)), ((0,), (0,))), preferred_element_type=F32)


def _bmm_nt(a, b):
    return lax.dot_general(a, b, (((2,), (2,)), ((0,), (0,))), preferred_element_type=F32)


def _unit_tri_inverse(a, diag_mask):
    c = a.shape[-1]
    eye = (lax.broadcasted_iota(jnp.int32, (1, c, c), 1) == lax.broadcasted_iota(jnp.int32, (1, c, c), 2)).astype(F32)
    mm = lambda p, q: _bmm(p.astype(BF16), q.astype(BF16))
    ad = jnp.where(diag_mask, a, 0.0)
    x = eye - ad
    pw = ad
    k = 2
    while k < GDN_SUB:
        pw = mm(pw, pw)
        x = x + mm(x, pw)
        k *= 2
    n = mm(x, a - ad)
    nblk = c // GDN_SUB
    y = eye - n
    pw = n
    k = 2
    while k < nblk:
        pw = mm(pw, pw)
        y = y + mm(y, pw)
        k *= 2
    return mm(y, x)


def _gdn_prep_kernel(qkv_ref, qp_ref, qn_ref, ab_ref, cw_ref, ea_ref, dtb_ref,
                     qg_ref, kd_ref, kc_ref, wv_ref, at_ref, eg_ref, *, tb, nt):
    c = GDN_CHUNK
    j = pl.program_id(1)
    hw = GDN_WIDTH
    x = _silu(_conv4(qkv_ref[...], qp_ref[...], qn_ref[...], cw_ref, j == 0, j == nt - 1))

    ab = ab_ref[...]
    col = lax.broadcasted_iota(jnp.int32, (tb, AB_PAD), 1)
    is_a = (col & 4) == 0
    gbv = jnp.where(is_a, -ea_ref[...] * _softplus(ab + dtb_ref[...]), jax.nn.sigmoid(ab))

    ri = lax.broadcasted_iota(jnp.int32, (tb, tb), 0)
    ci = lax.broadcasted_iota(jnp.int32, (tb, tb), 1)
    same = (ri // c) == (ci // c)
    l_f = (same & (ci <= ri)).astype(BF16)
    l_b = (same & (ci >= ri)).astype(BF16)
    l_t = same.astype(BF16)
    gcum = jnp.where(col < 8, _dot_sel(l_f, gbv), _dot_sel(l_b, gbv))
    gtot = _dot_sel(l_t, gbv)

    n_combo = 2 * GDN_HEADS
    er = lax.broadcasted_iota(jnp.int32, (AB_PAD, n_combo * 128), 0)
    ec = lax.broadcasted_iota(jnp.int32, (AB_PAD, n_combo * 128), 1) // 128
    src_a = (ec // GDN_HEADS) * 8 + (ec % GDN_HEADS)
    e_a = (er == src_a).astype(BF16)
    e_b = (er == src_a + 4).astype(BF16)
    gc_all = _bcast_cols(gcum, e_a)
    gt_all = _bcast_cols(gtot, e_a)
    be_all = _bcast_cols(gbv, e_b)

    qs, ks, vs = [], [], []
    for hd in range(GDN_HEADS):
        q = x[:, hd * GDN_DK:(hd + 1) * GDN_DK]
        k = x[:, hw + hd * GDN_DK: hw + (hd + 1) * GDN_DK]
        qs.append(q * lax.rsqrt(jnp.sum(q * q, axis=-1, keepdims=True) + EPS) * (GDN_DK ** -0.5))
        ks.append(k * lax.rsqrt(jnp.sum(k * k, axis=-1, keepdims=True) + EPS))
        vs.append(x[:, 2 * hw + hd * GDN_DV: 2 * hw + (hd + 1) * GDN_DV])
    inst = [(ch, dr, hd) for ch in range(tb // c) for dr in range(2) for hd in range(GDN_HEADS)]
    n_inst = len(inst)
    rows = lambda ch: slice(ch * c, (ch + 1) * c)
    lanes = lambda dr, hd: slice((dr * GDN_HEADS + hd) * 128, (dr * GDN_HEADS + hd + 1) * 128)
    stack = lambda f: jnp.stack([f(ch, dr, hd) for ch, dr, hd in inst], axis=0)
    qq = stack(lambda ch, dr, hd: qs[hd][rows(ch)])
    kk = stack(lambda ch, dr, hd: ks[hd][rows(ch)])
    vv = stack(lambda ch, dr, hd: vs[hd][rows(ch)])
    gcb = stack(lambda ch, dr, hd: gc_all[rows(ch), lanes(dr, hd)])
    gtb = stack(lambda ch, dr, hd: gt_all[rows(ch), lanes(dr, hd)])
    beb = stack(lambda ch, dr, hd: be_all[rows(ch), lanes(dr, hd)])

    fwd = ((lax.broadcasted_iota(jnp.int32, (n_inst, c, 128), 0) // GDN_HEADS) % 2) == 0
    ii = lax.broadcasted_iota(jnp.int32, (n_inst, c, 128), 1)
    jj = lax.broadcasted_iota(jnp.int32, (n_inst, c, 128), 2)
    incl = (fwd & (jj <= ii)) | (jnp.logical_not(fwd) & (jj >= ii) & (jj < c))
    fwd_c = ((lax.broadcasted_iota(jnp.int32, (n_inst, c, c), 0) // GDN_HEADS) % 2) == 0
    i64 = lax.broadcasted_iota(jnp.int32, (n_inst, c, c), 1)
    j64 = lax.broadcasted_iota(jnp.int32, (n_inst, c, c), 2)
    strict = (fwd_c & (j64 < i64)) | (jnp.logical_not(fwd_c) & (j64 > i64))
    diag_blk = (i64 // GDN_SUB) == (j64 // GDN_SUB)

    eg = jnp.exp(gcb)
    kb = kk * beb
    diag = jnp.where(ii == jj, gcb, 0.0)
    gcr = _dot_sel(jnp.ones((c, c), BF16), jnp.concatenate([diag[g] for g in range(n_inst)], axis=1))
    gcr = jnp.stack([gcr[:, g * 128:(g + 1) * 128] for g in range(n_inst)], axis=0)
    decay = jnp.where(incl, jnp.exp(jnp.where(incl, gcb - gcr, 0.0)), 0.0)
    kpad = jnp.concatenate([kk.astype(BF16), jnp.zeros((n_inst, c, GDN_DK), BF16)], axis=1)
    qk = _bmm_nt(jnp.concatenate([qq, kb], axis=1).astype(BF16), kpad)
    attn = (qk[:, :c] * decay).astype(BF16)
    a_mat = jnp.where(strict, (qk[:, c:] * decay)[:, :, :c], 0.0)
    tinv = _unit_tri_inverse(a_mat, diag_blk)
    rhs = jnp.concatenate([vv * beb, kb * eg], axis=2)
    sol = _bmm(tinv.astype(BF16), rhs.astype(BF16))
    qg = (qq * eg).astype(BF16)
    kd = (kk * jnp.exp(gtb - gcb)).astype(BF16)
    egt = jnp.exp(gtb[:, 0:8, :])
    for g, (ch, dr, hd) in enumerate(inst):
        rs = rows(ch)
        ls = slice(hd * 128, (hd + 1) * 128)
        qg_ref[dr, rs, ls] = qg[g]
        kd_ref[dr, rs, ls] = kd[g]
        kc_ref[dr, rs, ls] = sol[g, :, GDN_DV:].astype(BF16)
        wv_ref[dr, rs, ls] = sol[g, :, :GDN_DV]
        at_ref[dr, rs, ls] = attn[g]
        eg_ref[dr, ch, :, ls] = egt[g]


def _bcast_cols(x, sel):
    hi, mid, lo = _split3(x)
    return _dot(lo, sel) + _dot(mid, sel) + _dot(hi, sel)


def _gdn_prep(qkv, ab, conv_w, ea_row, dtb_row, *, tb):
    b, t, cw = qkv.shape
    nt = t // tb
    nch = t // GDN_CHUNK
    cpb = tb // GDN_CHUNK
    const = lambda shape: pl.BlockSpec(shape, lambda i, j: (0,) * len(shape))
    hw = GDN_WIDTH
    big = lambda: pl.BlockSpec((2, None, tb, hw), lambda i, j: (0, i, j, 0))
    outs = pl.pallas_call(
        functools.partial(_gdn_prep_kernel, tb=tb, nt=nt),
        grid=(b, nt),
        in_specs=_halo_specs(tb, t, cw, 0, lambda j: j) + [
            pl.BlockSpec((None, tb, AB_PAD), lambda i, j: (i, j, 0)),
            const((4, cw)), const((1, AB_PAD)), const((1, AB_PAD))],
        out_specs=[big(), big(), big(), big(), big(),
                   pl.BlockSpec((2, None, cpb, 8, hw), lambda i, j: (0, i, j, 0, 0))],
        out_shape=[jax.ShapeDtypeStruct((2, b, t, hw), BF16),
                   jax.ShapeDtypeStruct((2, b, t, hw), BF16),
                   jax.ShapeDtypeStruct((2, b, t, hw), BF16),
                   jax.ShapeDtypeStruct((2, b, t, hw), F32),
                   jax.ShapeDtypeStruct((2, b, t, hw), BF16),
                   jax.ShapeDtypeStruct((2, b, nch, 8, hw), F32)],
        compiler_params=_cparams(("parallel", "parallel")),
    )(qkv, qkv, qkv, ab, conv_w, ea_row, dtb_row)
    return outs


def _gdn_scan_kernel(qg_f, kd_f, kc_f, wv_f, at_f, eg_f, qg_b, kd_b, kc_b, wv_b, at_b, eg_b, s0_ref,
                     of_ref, ob_ref, sfin_ref, s_scr, *, cps):
    c = GDN_CHUNK
    j = pl.program_id(1)

    @pl.when(j == 0)
    def _():
        s_scr[...] = s0_ref[...]

    dirs = ((qg_f, kd_f, kc_f, wv_f, at_f, eg_f), (qg_b, kd_b, kc_b, wv_b, at_b, eg_b))
    chains = [(dr, hd) for dr in range(2) for hd in range(GDN_HEADS)]
    ls = lambda hd: slice(hd * 128, (hd + 1) * 128)
    s = s_scr[...].reshape(2 * GDN_HEADS, GDN_DK, GDN_DV)
    for q in range(cps):
        chunk = (q, cps - 1 - q)
        rs = lambda dr: slice(chunk[dr] * c, (chunk[dr] + 1) * c)
        stack = lambda f: jnp.stack([f(dirs[dr], rs(dr), ls(hd), chunk[dr]) for dr, hd in chains], axis=0)
        kq = stack(lambda r, t, l, n: jnp.concatenate([r[2][t, l], r[0][t, l]], axis=0))
        wv = stack(lambda r, t, l, n: r[3][t, l])
        at = stack(lambda r, t, l, n: r[4][t, l][:, :c])
        eg = stack(lambda r, t, l, n: r[5][n, 0:1, l])
        r = _bmm(kq, s.astype(BF16))
        vb = (wv - r[:, :c]).astype(BF16)
        o = r[:, c:] + _bmm(at, vb)
        upd = []
        for g, (dr, hd) in enumerate(chains):
            (of_ref, ob_ref)[dr][rs(dr), ls(hd)] = o[g]
            upd.append(_dot_tn(dirs[dr][1][rs(dr), ls(hd)], vb[g]))
        s = s * eg + jnp.stack(upd, axis=0)
    s_scr[...] = s.reshape(2, GDN_HEADS, GDN_DK, GDN_DV)
    sfin_ref[...] = s_scr[...]


def _gdn_scan(prep, s0, *, cps):
    qg, kd, kc, wv, at, eg = prep
    _, b, t, hw = qg.shape
    c = GDN_CHUNK
    nblk = t // (c * cps)
    fw = lambda: pl.BlockSpec((None, None, cps * c, hw), lambda i, j: (0, i, j, 0))
    bw = lambda: pl.BlockSpec((None, None, cps * c, hw), lambda i, j: (1, i, nblk - 1 - j, 0))
    egf = pl.BlockSpec((None, None, cps, 8, hw), lambda i, j: (0, i, j, 0, 0))
    egb = pl.BlockSpec((None, None, cps, 8, hw), lambda i, j: (1, i, nblk - 1 - j, 0, 0))
    st = pl.BlockSpec((None, 2, GDN_HEADS, GDN_DK, GDN_DV), lambda i, j: (i, 0, 0, 0, 0))
    o_f, o_b, s_fin = pl.pallas_call(
        functools.partial(_gdn_scan_kernel, cps=cps),
        grid=(b, nblk),
        in_specs=[fw(), fw(), fw(), fw(), fw(), egf, bw(), bw(), bw(), bw(), bw(), egb, st],
        out_specs=[pl.BlockSpec((None, cps * c, hw), lambda i, j: (i, j, 0)),
                   pl.BlockSpec((None, cps * c, hw), lambda i, j: (i, nblk - 1 - j, 0)),
                   st],
        out_shape=[jax.ShapeDtypeStruct((b, t, hw), F32), jax.ShapeDtypeStruct((b, t, hw), F32),
                   jax.ShapeDtypeStruct((b, 2, GDN_HEADS, GDN_DK, GDN_DV), F32)],
        scratch_shapes=[pltpu.VMEM((2, GDN_HEADS, GDN_DK, GDN_DV), F32)],
        compiler_params=_cparams(("parallel", "arbitrary")),
    )(qg, kd, kc, wv, at, eg, qg, kd, kc, wv, at, eg, s0)
    return o_f, o_b, s_fin


def _out_rg_kernel(x_ref, y_ref, w_ref, gt_ref, o_ref):
    o_ref[...] = x_ref[...] + gt_ref[...] * _dot(y_ref[...], w_ref[...])


def _out_gdn_kernel(x_ref, of_ref, ob_ref, z_ref, ng_ref, w_ref, gt_ref, o_ref, *, ncol, d):
    o = of_ref[...] + ob_ref[...]
    z = z_ref[...]
    parts = []
    for hd in range(GDN_HEADS):
        ls = slice(hd * GDN_DV, (hd + 1) * GDN_DV)
        parts.append(_rms(o[:, ls], ng_ref[...]) * _silu(z[:, ls]))
    y = jnp.concatenate(parts, axis=1).astype(BF16)
    t2 = gt_ref[...] * _dot(y, w_ref[...])
    rows = t2.shape[0] // ncol
    for j in range(ncol):
        o_ref[:, j, :] = x_ref[:, j, :] + t2[j * rows:(j + 1) * rows]


def _out_project(x3, y_rg, o_f, o_b, z, norm_g, w_out, gt, *, tile):
    b, t, d = x3.shape
    w = RG_WIDTH
    vec = lambda: pl.BlockSpec((None, 1, d), lambda i, j: (i, 0, 0))
    x1 = pl.pallas_call(
        _out_rg_kernel,
        grid=(b, t // tile),
        in_specs=[pl.BlockSpec((None, tile, d), lambda i, j: (i, j, 0)),
                  pl.BlockSpec((None, tile, w), lambda i, j: (i, j, 0)),
                  pl.BlockSpec((w, d), lambda i, j: (0, 0)), vec()],
        out_specs=pl.BlockSpec((None, tile, d), lambda i, j: (i, j, 0)),
        out_shape=jax.ShapeDtypeStruct((b, t, d), F32),
        compiler_params=_cparams(("parallel", "parallel")),
    )(x3, y_rg, w_out[:w], gt)
    rows = t // GRID_W
    ncol = tile // rows
    hw = GDN_WIDTH
    x1v = x1.reshape(b, rows, GRID_W, d)
    cm = lambda wd: pl.BlockSpec((None, tile, wd), lambda i, j: (i, j, 0))
    x1v = pl.pallas_call(
        functools.partial(_out_gdn_kernel, ncol=ncol, d=d),
        grid=(b, GRID_W // ncol),
        in_specs=[pl.BlockSpec((None, rows, ncol, d), lambda i, j: (i, 0, j, 0)),
                  cm(hw), cm(hw), cm(hw),
                  pl.BlockSpec((1, GDN_DV), lambda i, j: (0, 0)),
                  pl.BlockSpec((hw, d), lambda i, j: (0, 0)), vec()],
        out_specs=pl.BlockSpec((None, rows, ncol, d), lambda i, j: (i, 0, j, 0)),
        out_shape=jax.ShapeDtypeStruct((b, rows, GRID_W, d), F32),
        input_output_aliases={0: 0},
        compiler_params=_cparams(("parallel", "parallel")),
    )(x1v, o_f, o_b, z, norm_g, w_out[w:], gt)
    return x1v.reshape(b, t, d)


def _fold_kernel(wq_ref, k_ref, o_ref):
    o_ref[...] = lax.dot_general(wq_ref[...], k_ref[...], (((1,), (1,)), ((), ())),
                                 preferred_element_type=F32, precision=lax.Precision.HIGHEST)


def _fold_keys(wq, keys):
    d, n = wq.shape
    nblk = n // PEER_HALF
    return pl.pallas_call(
        _fold_kernel,
        grid=(nblk,),
        in_specs=[pl.BlockSpec((d, PEER_HALF), lambda j: (0, j)),
                  pl.BlockSpec((None, PEER_NKEYS, PEER_HALF), lambda j: (j % 2, 0, 0))],
        out_specs=pl.BlockSpec((d, PEER_NKEYS), lambda j: (0, j)),
        out_shape=jax.ShapeDtypeStruct((d, nblk * PEER_NKEYS), F32),
        compiler_params=_cparams(("parallel",)),
    )(wq, keys)


def _extract16(vals, ids, s_scr, i_scr, *, fast, n_masked=0):
    n, p = vals.shape
    for r in range(PEER_TOPK):
        m = jnp.max(vals, axis=0, keepdims=True)
        eq = vals == m
        if fast:
            idx = _dot(ids, jnp.where(eq, 1.0, 0.0).astype(BF16))[0:1, :].astype(jnp.int32)
            vals = jnp.where(eq, NEG_INF, vals)
        else:
            idx = jnp.min(jnp.where(eq, ids, 1 << 20), axis=0, keepdims=True)
            vals = jnp.where(ids == idx, NEG_INF, vals)
        s_scr[r:r + 1, :] = m
        i_scr[r:r + 1, :] = idx
    if not fast:
        return None
    dropped = jnp.sum(jnp.where(vals == NEG_INF, 1.0, 0.0), axis=0, keepdims=True)
    return jnp.where(dropped == float(PEER_TOPK + n_masked), 0.0, 1.0)


def _peer_sel_kernel(x_ref, g_ref, sh_ref, sc_ref, wt_ref, h_ref, a_ref, b_ref, gate_ref,
                     sc_scr, s1, i1, s2, i2, ts, tp, a_t, b_t, g_t):
    k = PEER_TOPK
    h = _rms(x_ref[...], g_ref[...]) * (1.0 + sc_ref[...]) + sh_ref[...]
    hb = h.astype(BF16)
    h_ref[...] = hb
    sc_scr[...] = _dot_nt(wt_ref[...], hb)
    p = hb.shape[0]

    n_cand = 80
    n_masked = sum(8 - k // (r + 1) for r in range(1, 8))

    def cand_pos(row):
        return jnp.where(row < 16, row, jnp.where(row < 72, ((row - 16) // 8 + 1) * k + (row - 16) % 8,
                                                  (row - 72 + 8) * k))

    def head(hd, bad, fast):
        if fast:
            key_ids = lax.broadcasted_iota(jnp.int32, (8, PEER_NKEYS), 1).astype(F32).astype(BF16)
            pos_ids = cand_pos(lax.broadcasted_iota(jnp.int32, (8, n_cand), 1)).astype(F32).astype(BF16)
        else:
            key_ids = lax.broadcasted_iota(jnp.int32, (PEER_NKEYS, p), 0)
            pos_ids = cand_pos(lax.broadcasted_iota(jnp.int32, (n_cand, p), 0))
        sub = lax.broadcasted_iota(jnp.int32, (8, p), 0)
        base = pl.multiple_of(hd * 2 * PEER_NKEYS, 2 * PEER_NKEYS)
        bad1 = _extract16(sc_scr[pl.ds(base, PEER_NKEYS), :], key_ids, s1, i1, fast=fast)
        bad2 = _extract16(sc_scr[pl.ds(base + PEER_NKEYS, PEER_NKEYS), :], key_ids, s2, i2, fast=fast)
        s1v, s2v = s1[...], s2[...]
        cands = [s1v[0:1, :] + s2v]
        for r in range(1, 8):
            cands.append(jnp.where(sub < k // (r + 1), s1v[r:r + 1, :] + s2v[0:8, :], NEG_INF))
        cands.append(s1v[8:16, :] + s2v[0:1, :])
        bad3 = _extract16(jnp.concatenate(cands, axis=0), pos_ids, ts, tp, fast=fast, n_masked=n_masked)
        if fast:
            bad = bad + bad1 + bad2 + bad3
        top_s, top_p = ts[...], tp[...]
        rr = top_p >> 4
        cc = top_p & (k - 1)
        i1v, i2v = i1[...], i2[...]
        av = jnp.zeros_like(top_p)
        bv = jnp.zeros_like(top_p)
        for q in range(k):
            av = jnp.where(rr == q, i1v[q:q + 1, :], av)
            bv = jnp.where(cc == q, i2v[q:q + 1, :], bv)
        e = jnp.exp(top_s - top_s[0:1, :])
        gate = e / jnp.sum(e, axis=0, keepdims=True)
        o = pl.multiple_of(hd * k, k)
        a_t[pl.ds(o, k), :] = av
        b_t[pl.ds(o, k), :] = bv
        g_t[pl.ds(o, k), :] = gate
        return bad

    bad = lax.fori_loop(0, PEER_HEADS, functools.partial(head, fast=True), jnp.zeros((1, p), F32))

    @pl.when(jnp.max(bad) > 0.0)
    def _():
        lax.fori_loop(0, PEER_HEADS, functools.partial(head, fast=False), jnp.zeros((1, p), F32))

    a_ref[...] = a_t[...].T
    b_ref[...] = b_t[...].T
    gate_ref[...] = g_t[...].T


def _peer_select(x1, g, sh, sc, wt, *, tile):
    b, t, d = x1.shape
    nt = t // tile
    nk = PEER_HEADS * PEER_TOPK
    nrow = wt.shape[0]
    vec = lambda: pl.BlockSpec((None, 1, d), lambda i, j: (i, 0, 0))
    tr = lambda: pl.BlockSpec((None, tile, nk), lambda i, j: (i, j, 0))
    k = PEER_TOPK
    return pl.pallas_call(
        _peer_sel_kernel,
        grid=(b, nt),
        in_specs=[pl.BlockSpec((None, tile, d), lambda i, j: (i, j, 0)),
                  pl.BlockSpec((1, d), lambda i, j: (0, 0)), vec(), vec(),
                  pl.BlockSpec((nrow, d), lambda i, j: (0, 0))],
        out_specs=[pl.BlockSpec((None, tile, d), lambda i, j: (i, j, 0)), tr(), tr(), tr()],
        out_shape=[jax.ShapeDtypeStruct((b, t, d), BF16),
                   jax.ShapeDtypeStruct((b, t, nk), jnp.int32),
                   jax.ShapeDtypeStruct((b, t, nk), jnp.int32),
                   jax.ShapeDtypeStruct((b, t, nk), F32)],
        scratch_shapes=[pltpu.VMEM((nrow, tile), F32),
                        pltpu.VMEM((k, tile), F32), pltpu.VMEM((k, tile), jnp.int32),
                        pltpu.VMEM((k, tile), F32), pltpu.VMEM((k, tile), jnp.int32),
                        pltpu.VMEM((k, tile), F32), pltpu.VMEM((k, tile), jnp.int32),
                        pltpu.VMEM((nk, tile), jnp.int32), pltpu.VMEM((nk, tile), jnp.int32),
                        pltpu.VMEM((nk, tile), F32)],
        compiler_params=_cparams(("parallel", "parallel")),
    )(x1, g.reshape(1, d), sh, sc, wt)


def _peer_mix_kernel(h_ref, a_ref, b_ref, gate_ref, u_ref, v_ref, x_ref, gt_ref, fg_ref, o_ref,
                     m_scr, acc, *, tile, n_steps, pairs):
    nk = PEER_NKEYS
    half = nk // 2
    step = pl.program_id(2)
    hi_mask = jnp.uint32(0xFFFF0000)

    @pl.when(step == 0)
    def _():
        sub = lax.broadcasted_iota(jnp.int32, (nk, a_ref.shape[1]), 0)
        a_of_row = jnp.where(sub < half, 2 * sub, 2 * (sub - half) + 1)

        def build(p, carry):
            arow = a_ref[pl.ds(p, 1), :]
            brow = b_ref[pl.ds(p, 1), :]
            grow = gate_ref[pl.ds(p, 1), :]
            xa = jnp.where(a_of_row == arow, 1.0, 0.0).astype(BF16)
            yb = jnp.where(sub == brow, grow, 0.0).astype(BF16)
            m = _dot_nt(xa, yb).astype(BF16).astype(F32)
            bits = lax.bitcast_convert_type(m, jnp.uint32)
            m_scr[pl.ds(pl.multiple_of(p * half, half), half), :] = (bits[half:] & hi_mask) | (bits[:half] >> 16)
            return carry

        lax.fori_loop(0, tile, build, 0, unroll=16)

    h = h_ref[...]
    parts = []
    for q in range(pairs):
        pr = step * pairs + q
        act = jax.nn.gelu(_dot_nt(h, u_ref[q * 2 * nk:(q + 1) * 2 * nk, :]))
        w = m_scr[pl.ds(pr, tile, stride=half), :]
        m_even = lax.bitcast_convert_type(w << 16, F32)
        m_odd = lax.bitcast_convert_type(w & hi_mask, F32)
        parts.append((act * jnp.concatenate([m_even, m_odd], axis=1)).astype(BF16))
    contrib = _dot(jnp.concatenate(parts, axis=1), v_ref[...])

    @pl.when(step == 0)
    def _():
        acc[...] = contrib

    @pl.when(step > 0)
    def _():
        acc[...] += contrib

    @pl.when(step == n_steps - 1)
    def _():
        x2 = x_ref[...] + gt_ref[...] * acc[...]
        o_ref[...] = _rms(x2, fg_ref[...])


def _peer_mix(hb, aidx, bidx, gate, u, v, x1, gt, final_g, *, tile, pairs):
    b, t, d = x1.shape
    nt = t // tile
    nk = PEER_NKEYS
    npk = aidx.shape[-1]
    n_steps = nk // (2 * pairs)
    eb = 2 * nk * pairs
    tok = lambda wd: pl.BlockSpec((None, tile, wd), lambda i, j, s: (i, j, 0))
    tab = lambda: pl.BlockSpec((eb, d), lambda i, j, s: (s, 0))
    return pl.pallas_call(
        functools.partial(_peer_mix_kernel, tile=tile, n_steps=n_steps, pairs=pairs),
        grid=(b, nt, n_steps),
        in_specs=[tok(d), tok(npk), tok(npk), tok(npk), tab(), tab(), tok(d),
                  pl.BlockSpec((None, 1, d), lambda i, j, s: (i, 0, 0)),
                  pl.BlockSpec((1, d), lambda i, j, s: (0, 0))],
        out_specs=tok(d),
        out_shape=jax.ShapeDtypeStruct((b, t, d), F32),
        scratch_shapes=[pltpu.VMEM((tile * nk // 2, nk), jnp.uint32), pltpu.VMEM((tile, d), F32)],
        compiler_params=_cparams(("parallel", "parallel", "arbitrary")),
    )(hb, aidx, bidx, gate, u, v, x1, gt, final_g.reshape(1, d))


def _block_diag(w):
    n, e, _ = w.shape
    eye = jnp.eye(n, dtype=w.dtype)
    return (eye[:, None, :, None] * w[:, :, None, :]).reshape(n * e, n * e)


def _mix_sequence(x3, ctx_mode, params, states, mods, *, rg_tb, gdn_tb, proj_tile):
    (norm1_g, w_rg, w_gdn, rg_conv_w, rg_conv_b, wg, gate_b, c_lam, gdn_conv_w, ea_row, dtb_row) = params
    sh1, sc1, mod_row = mods
    rg_h0_f, rg_h0_b, gdn_s0 = states
    (p_rg,) = _project(x3, norm1_g, sh1, sc1, mod_row, w_rg, (2 * RG_WIDTH,), colmajor=False, tile=proj_tile)
    qkv, z, ab = _project(x3, norm1_g, sh1, sc1, mod_row, w_gdn, (3 * GDN_WIDTH, GDN_WIDTH, AB_PAD),
                          colmajor=not ctx_mode, tile=proj_tile)
    h_f, st_f = _rglru_pass(p_rg, rg_conv_w, rg_conv_b, wg[0], gate_b[0], c_lam[0], rg_h0_f, None,
                            reverse=False, tb=rg_tb)
    y_rg, st_b = _rglru_pass(p_rg, rg_conv_w, rg_conv_b, wg[1], gate_b[1], c_lam[1], rg_h0_b, h_f,
                             reverse=True, tb=rg_tb)
    prep = _gdn_prep(qkv, ab, gdn_conv_w, ea_row, dtb_row, tb=gdn_tb)
    o_f, o_b, s_fin = _gdn_scan(prep, gdn_s0, cps=min(4, x3.shape[1] // GDN_CHUNK))
    return y_rg, o_f, o_b, z, (st_f, st_b, s_fin)


def kernel(x, c, ctx, c_ctx, w_mod, b_mod, norm1_g, norm2_g, w_in, rg_conv_w, rg_conv_b, rg_gate_w, rg_gate_b,
           rg_lambda, gdn_conv_w, gdn_a_log, gdn_dt_bias, gdn_norm_g, w_out, peer_wq, peer_keys, peer_u, peer_v,
           final_g):
    b, t, d = x.shape
    depth = w_mod.shape[0]
    assert depth == 1, "context residual stream update is only needed for depth > 1"
    l = 0
    w = RG_WIDTH

    cc = jnp.zeros((16, d), F32).at[:b].set(c).at[b].set(c_ctx)
    w_rg = w_in[l][:, :2 * w].astype(BF16)
    n_ab = w_in.shape[2] - 2 * w - 4 * GDN_WIDTH
    w_gdn = jnp.concatenate([w_in[l][:, 2 * w:2 * w + 4 * GDN_WIDTH],
                             jnp.pad(w_in[l][:, 2 * w + 4 * GDN_WIDTH:], ((0, 0), (0, AB_PAD - n_ab)))],
                            axis=1).astype(BF16)
    wg = jnp.stack([jnp.concatenate([_block_diag(rg_gate_w[l, dr, 0]), _block_diag(rg_gate_w[l, dr, 1])], axis=1)
                    for dr in range(2)]).astype(BF16)
    gate_b = rg_gate_b[l].reshape(2, 1, 2 * w)
    c_lam = (-RG_C * jax.nn.softplus(-rg_lambda[l])).reshape(2, 1, w)
    ea = jnp.exp(gdn_a_log[l])
    pad4 = jnp.zeros((2, GDN_HEADS), F32)
    ea_row = jnp.pad(jnp.concatenate([ea, pad4], axis=1).reshape(1, -1), ((0, 0), (0, AB_PAD - n_ab)))
    dtb_row = jnp.pad(jnp.concatenate([gdn_dt_bias[l], pad4], axis=1).reshape(1, -1), ((0, 0), (0, AB_PAD - n_ab)))
    params = (norm1_g[l], w_rg, w_gdn, rg_conv_w[l], rg_conv_b[l].reshape(1, w), wg, gate_b, c_lam,
              gdn_conv_w[l], ea_row, dtb_row)

    mod = _modulation(cc, w_mod[l], b_mod[l])
    sh1, sc1, gt1, sh2, sc2, gt2 = [mod[:, i * d:(i + 1) * d].reshape(16, 1, d) for i in range(6)]

    tc = ctx.shape[1]
    zero_states = (jnp.zeros((b, 1, w), F32), jnp.zeros((b, 1, w), F32),
                   jnp.zeros((b, 2, GDN_HEADS, GDN_DK, GDN_DV), F32))
    ctx_tb = min(tc, 256)
    _, _, _, _, ctx_states = _mix_sequence(ctx, True, params, zero_states, (sh1, sc1, lambda i: b),
                                           rg_tb=ctx_tb, gdn_tb=min(tc, 128), proj_tile=ctx_tb)

    tile = min(t, 512)
    y_rg, o_f, o_b, z, _ = _mix_sequence(x, False, params, ctx_states, (sh1, sc1, lambda i: i),
                                         rg_tb=min(t, 256), gdn_tb=128, proj_tile=tile)
    x1 = _out_project(x, y_rg, o_f, o_b, z, gdn_norm_g[l].reshape(1, GDN_DV), w_out[l].astype(BF16), gt1, tile=tile)

    wfold_t = _fold_keys(peer_wq[l], peer_keys[l]).T.astype(BF16)
    sel_tile = min(t, 256)
    hb, aidx, bidx, gate = _peer_select(x1, norm2_g[l], sh2, sc2, wfold_t, tile=sel_tile)
    out = _peer_mix(hb, aidx, bidx, gate, peer_u[l].astype(BF16), peer_v[l].astype(BF16),
                    x1, gt2, final_g, tile=tile, pairs=8)
    return out
```

```python
import functools
import math

import jax
import jax.numpy as jnp
from jax import lax
from jax.experimental import pallas as pl
from jax.experimental.pallas import tpu as pltpu

F32 = jnp.float32
BF16 = jnp.bfloat16

GRID_W = 64
EPS = 1e-6
RG_WIDTH = 512
RG_BLOCKS = 8
RG_C = 8.0
GDN_HEADS = 4
GDN_DK = 128
GDN_DV = 128
GDN_WIDTH = GDN_HEADS * GDN_DV
GDN_CHUNK = 64
GDN_SUB = 16
AB_PAD = 128
PEER_HEADS = 8
PEER_NKEYS = 128
PEER_HALF = 128
PEER_TOPK = 16
NEG_INF = float("-inf")

VMEM_LIMIT = 56 * 1024 * 1024


def _cparams(sem):
    return pltpu.CompilerParams(dimension_semantics=sem, vmem_limit_bytes=VMEM_LIMIT)


def _dot(a, b):
    return jnp.dot(a, b, preferred_element_type=F32)


def _dot_nt(a, b):
    return lax.dot_general(a, b, (((1,), (1,)), ((), ())), preferred_element_type=F32)


def _dot_tn(a, b):
    return lax.dot_general(a, b, (((0,), (0,)), ((), ())), preferred_element_type=F32)


def _split3(x):
    hi = x.astype(BF16)
    r = x - hi.astype(F32)
    mid = r.astype(BF16)
    lo = (r - mid.astype(F32)).astype(BF16)
    return hi, mid, lo


def _dot_sel(m, x):
    hi, mid, lo = _split3(x)
    return _dot(m, lo) + _dot(m, mid) + _dot(m, hi)


def _silu(x):
    return x * jax.nn.sigmoid(x)


def _softplus(x):
    return jnp.maximum(x, 0.0) + jnp.log1p(jnp.exp(-jnp.abs(x)))


def _gelu_times_2(x):
    c = math.sqrt(2.0 / math.pi)
    return x * (1.0 + jnp.tanh(x * (c + (c * 0.044715) * (x * x))))


def _rms(x, g):
    return x * lax.rsqrt(jnp.mean(x * x, axis=-1, keepdims=True) + EPS) * g


def _mod_kernel(c_ref, w_ref, b_ref, o_ref):
    s = _silu(c_ref[...])
    o_ref[...] = jnp.dot(s, w_ref[...], preferred_element_type=F32,
                         precision=lax.Precision.HIGHEST) + b_ref[...]


def _modulation(cc, w_mod, b_mod):
    m, d = cc.shape
    n = w_mod.shape[1]
    tn = 1536
    return pl.pallas_call(
        _mod_kernel,
        grid=(n // tn,),
        in_specs=[pl.BlockSpec((m, d), lambda j: (0, 0)),
                  pl.BlockSpec((d, tn), lambda j: (0, j)),
                  pl.BlockSpec((1, tn), lambda j: (0, j))],
        out_specs=pl.BlockSpec((m, tn), lambda j: (0, j)),
        out_shape=jax.ShapeDtypeStruct((m, n), F32),
        compiler_params=_cparams(("arbitrary",)),
    )(cc, w_mod, b_mod.reshape(1, n))


def _proj_kernel(x_ref, g_ref, sh_ref, sc_ref, w_ref, *o_refs, ncol, d, widths):
    if ncol > 1:
        x = jnp.concatenate([x_ref[:, j, :] for j in range(ncol)], axis=0)
    else:
        x = x_ref[...]
    h = _rms(x, g_ref[...]) * (1.0 + sc_ref[...]) + sh_ref[...]
    o = _dot(h.astype(BF16), w_ref[...])
    off = 0
    for o_ref, wd in zip(o_refs, widths):
        o_ref[...] = o[:, off:off + wd]
        off += wd


def _project(x3, g, sh, sc, mod_row, w, widths, *, colmajor, tile):
    b, t, d = x3.shape
    n = w.shape[1]
    if colmajor:
        rows = t // GRID_W
        ncol = tile // rows
        xv = x3.reshape(b, rows, GRID_W, d)
        x_spec = pl.BlockSpec((None, rows, ncol, d), lambda i, j: (i, 0, j, 0))
        nt = GRID_W // ncol
    else:
        ncol = 1
        xv = x3
        x_spec = pl.BlockSpec((None, tile, d), lambda i, j: (i, j, 0))
        nt = t // tile
    vec = lambda: pl.BlockSpec((None, 1, d), lambda i, j: (mod_row(i), 0, 0))
    outs = pl.pallas_call(
        functools.partial(_proj_kernel, ncol=ncol, d=d, widths=tuple(widths)),
        grid=(b, nt),
        in_specs=[x_spec, pl.BlockSpec((1, d), lambda i, j: (0, 0)), vec(), vec(),
                  pl.BlockSpec((d, n), lambda i, j: (0, 0))],
        out_specs=[pl.BlockSpec((None, tile, wd), lambda i, j: (i, j, 0)) for wd in widths],
        out_shape=[jax.ShapeDtypeStruct((b, t, wd), F32) for wd in widths],
        compiler_params=_cparams(("parallel", "parallel")),
    )(xv, g.reshape(1, d), sh, sc, w)
    return outs


def _conv4(cur, prev8, nxt8, w_ref, first, last):
    tb, c = cur.shape
    g = tb // 8
    prev8 = jnp.where(first, 0.0, prev8)
    nxt8 = jnp.where(last, 0.0, nxt8)
    ext = jnp.concatenate([prev8, cur, nxt8], axis=0).reshape(g + 2, 8, c)
    sub = lax.broadcasted_iota(jnp.int32, (g, 8, c), 1)

    def back(k):
        r = pltpu.roll(ext, k, 1)
        return jnp.where(sub >= k, r[1:g + 1], r[0:g])

    r = pltpu.roll(ext, 7, 1)
    xp1 = jnp.where(sub < 7, r[1:g + 1], r[2:g + 2])
    y = (w_ref[0:1, :] * back(2) + w_ref[1:2, :] * back(1) + w_ref[2:3, :] * ext[1:g + 1] + w_ref[3:4, :] * xp1)
    return y.reshape(tb, c)


def _halo_specs(tb, t, width, lane_blk, tmap):
    r = tb // 8
    nb8 = t // 8
    return [pl.BlockSpec((None, tb, width), lambda i, j: (i, tmap(j), lane_blk)),
            pl.BlockSpec((None, 8, width), lambda i, j: (i, jnp.maximum(tmap(j) * r - 1, 0), lane_blk)),
            pl.BlockSpec((None, 8, width), lambda i, j: (i, jnp.minimum((tmap(j) + 1) * r, nb8 - 1), lane_blk))]


def _rglru_kernel(*refs, tb, nt, reverse):
    if reverse:
        (u_ref, up_ref, un_ref, cw_ref, cb_ref, wg_ref, gb_ref, cl_ref, h0_ref, gate_ref, hf_ref,
         y_ref, st_ref, carry) = refs
    else:
        (u_ref, up_ref, un_ref, cw_ref, cb_ref, wg_ref, gb_ref, cl_ref, h0_ref,
         y_ref, st_ref, carry) = refs
    j = pl.program_id(1)
    tblk = (nt - 1 - j) if reverse else j
    w = u_ref.shape[-1]

    @pl.when(j == 0)
    def _():
        carry[...] = h0_ref[...]

    u = u_ref[...]
    xc = _conv4(u, up_ref[...], un_ref[...], cw_ref, tblk == 0, tblk == nt - 1) + cb_ref[...]
    gates = _dot(xc.astype(BF16), wg_ref[...]) + gb_ref[...]
    r = jax.nn.sigmoid(gates[:, :w])
    ig = jax.nn.sigmoid(gates[:, w:])
    log_a = r * cl_ref[...]
    a = jnp.exp(log_a)
    th = jnp.tanh(log_a)
    bb = jnp.sqrt(-2.0 * th / (1.0 - th)) * (ig * xc)

    n_grp = tb // 8
    a = a.reshape(n_grp, 8, w)
    bb = bb.reshape(n_grp, 8, w)
    sub = lax.broadcasted_iota(jnp.int32, (n_grp, 8, w), 1)
    s = 1
    while s < 8:
        shift, ok = (8 - s, sub < 8 - s) if reverse else (s, sub >= s)
        a_sh = jnp.where(ok, pltpu.roll(a, shift, 1), 1.0)
        b_sh = jnp.where(ok, pltpu.roll(bb, shift, 1), 0.0)
        bb = a * b_sh + bb
        a = a * a_sh
        s *= 2
    h_prev = carry[...]
    hs = [None] * n_grp
    for g in (range(n_grp - 1, -1, -1) if reverse else range(n_grp)):
        hg = a[g] * h_prev + bb[g]
        h_prev = hg[0:1, :] if reverse else hg[7:8, :]
        hs[g] = hg
    h = jnp.concatenate(hs, axis=0)
    carry[...] = h_prev
    st_ref[...] = carry[...]
    if reverse:
        y_ref[...] = ((hf_ref[...] + h) * jax.nn.gelu(gate_ref[...])).astype(y_ref.dtype)
    else:
        y_ref[...] = h


def _rglru_pass(p_rg, conv_w, conv_b, wg, gate_b, c_lam, h0, hf, *, reverse, tb):
    b, t, w2 = p_rg.shape
    w = w2 // 2
    nt = t // tb
    tmap = (lambda j: nt - 1 - j) if reverse else (lambda j: j)
    const = lambda shape: pl.BlockSpec(shape, lambda i, j: (0,) * len(shape))
    in_specs = _halo_specs(tb, t, w, 0, tmap) + [
        const((4, w)), const((1, w)), const((w, 2 * w)), const((1, 2 * w)), const((1, w)),
        pl.BlockSpec((None, 1, w), lambda i, j: (i, 0, 0))]
    args = [p_rg, p_rg, p_rg, conv_w, conv_b, wg, gate_b, c_lam, h0]
    if reverse:
        in_specs += [pl.BlockSpec((None, tb, w), lambda i, j: (i, tmap(j), 1)),
                     pl.BlockSpec((None, tb, w), lambda i, j: (i, tmap(j), 0))]
        args += [p_rg, hf]
    y, st = pl.pallas_call(
        functools.partial(_rglru_kernel, tb=tb, nt=nt, reverse=reverse),
        grid=(b, nt),
        in_specs=in_specs,
        out_specs=[pl.BlockSpec((None, tb, w), lambda i, j: (i, tmap(j), 0)),
                   pl.BlockSpec((None, 1, w), lambda i, j: (i, 0, 0))],
        out_shape=[jax.ShapeDtypeStruct((b, t, w), BF16 if reverse else F32),
                   jax.ShapeDtypeStruct((b, 1, w), F32)],
        scratch_shapes=[pltpu.VMEM((1, w), F32)],
        compiler_params=_cparams(("parallel", "arbitrary")),
    )(*args)
    return y, st


def _bmm(a, b):
    return lax.dot_general(a, b, (((2,), (1,)), ((0,), (0,))), preferred_element_type=F32)


def _bmm_nt(a, b):
    return lax.dot_general(a, b, (((2,), (2,)), ((0,), (0,))), preferred_element_type=F32)


def _unit_tri_inverse(a, diag_mask):
    c = a.shape[-1]
    eye = (lax.broadcasted_iota(jnp.int32, (1, c, c), 1) == lax.broadcasted_iota(jnp.int32, (1, c, c), 2)).astype(F32)
    mm = lambda p, q: _bmm(p.astype(BF16), q.astype(BF16))
    ad = jnp.where(diag_mask, a, 0.0)
    x = eye - ad
    pw = ad
    k = 2
    while k < GDN_SUB:
        pw = mm(pw, pw)
        x = x + mm(x, pw)
        k *= 2
    n = mm(x, a - ad)
    nblk = c // GDN_SUB
    y = eye - n
    pw = n
    k = 2
    while k < nblk:
        pw = mm(pw, pw)
        y = y + mm(y, pw)
        k *= 2
    return mm(y, x)


def _gdn_prep_kernel(qkv_ref, qp_ref, qn_ref, ab_ref, cw_ref, ea_ref, dtb_ref,
                     qg_ref, kd_ref, kc_ref, wv_ref, at_ref, eg_ref, *, tb, nt):
    c = GDN_CHUNK
    j = pl.program_id(1)
    hw = GDN_WIDTH
    x = _silu(_conv4(qkv_ref[...], qp_ref[...], qn_ref[...], cw_ref, j == 0, j == nt - 1))

    ab = ab_ref[...]
    col = lax.broadcasted_iota(jnp.int32, (tb, AB_PAD), 1)
    is_a = (col & 4) == 0
    gbv = jnp.where(is_a, -ea_ref[...] * _softplus(ab + dtb_ref[...]), jax.nn.sigmoid(ab))

    ri = lax.broadcasted_iota(jnp.int32, (tb, tb), 0)
    ci = lax.broadcasted_iota(jnp.int32, (tb, tb), 1)
    same = (ri // c) == (ci // c)
    l_f = (same & (ci <= ri)).astype(BF16)
    l_b = (same & (ci >= ri)).astype(BF16)
    l_t = same.astype(BF16)
    gcum = jnp.where(col < 8, _dot_sel(l_f, gbv), _dot_sel(l_b, gbv))
    gtot = _dot_sel(l_t, gbv)

    cols_a = [dr * 8 + hd for dr in range(2) for hd in range(GDN_HEADS)]
    bcast = lambda v, cols: jnp.concatenate([jnp.broadcast_to(v[:, cc:cc + 1], (tb, 128)) for cc in cols], axis=1)
    gc_all = bcast(gcum, cols_a)
    gt_all = bcast(gtot, cols_a)
    be_all = bcast(gbv, [cc + 4 for cc in cols_a])

    qs, ks, vs = [], [], []
    for hd in range(GDN_HEADS):
        q = x[:, hd * GDN_DK:(hd + 1) * GDN_DK]
        k = x[:, hw + hd * GDN_DK: hw + (hd + 1) * GDN_DK]
        qs.append(q * lax.rsqrt(jnp.sum(q * q, axis=-1, keepdims=True) + EPS) * (GDN_DK ** -0.5))
        ks.append(k * lax.rsqrt(jnp.sum(k * k, axis=-1, keepdims=True) + EPS))
        vs.append(x[:, 2 * hw + hd * GDN_DV: 2 * hw + (hd + 1) * GDN_DV])
    inst = [(ch, dr, hd) for ch in range(tb // c) for dr in range(2) for hd in range(GDN_HEADS)]
    n_inst = len(inst)
    rows = lambda ch: slice(ch * c, (ch + 1) * c)
    lanes = lambda dr, hd: slice((dr * GDN_HEADS + hd) * 128, (dr * GDN_HEADS + hd + 1) * 128)
    stack = lambda f: jnp.stack([f(ch, dr, hd) for ch, dr, hd in inst], axis=0)
    qq = stack(lambda ch, dr, hd: qs[hd][rows(ch)])
    kk = stack(lambda ch, dr, hd: ks[hd][rows(ch)])
    vv = stack(lambda ch, dr, hd: vs[hd][rows(ch)])
    gcb = stack(lambda ch, dr, hd: gc_all[rows(ch), lanes(dr, hd)])
    gtb = stack(lambda ch, dr, hd: gt_all[rows(ch), lanes(dr, hd)])
    beb = stack(lambda ch, dr, hd: be_all[rows(ch), lanes(dr, hd)])

    fwd = ((lax.broadcasted_iota(jnp.int32, (n_inst, c, 128), 0) // GDN_HEADS) % 2) == 0
    ii = lax.broadcasted_iota(jnp.int32, (n_inst, c, 128), 1)
    jj = lax.broadcasted_iota(jnp.int32, (n_inst, c, 128), 2)
    incl = (fwd & (jj <= ii)) | (jnp.logical_not(fwd) & (jj >= ii) & (jj < c))
    fwd_c = ((lax.broadcasted_iota(jnp.int32, (n_inst, c, c), 0) // GDN_HEADS) % 2) == 0
    i64 = lax.broadcasted_iota(jnp.int32, (n_inst, c, c), 1)
    j64 = lax.broadcasted_iota(jnp.int32, (n_inst, c, c), 2)
    strict = (fwd_c & (j64 < i64)) | (jnp.logical_not(fwd_c) & (j64 > i64))
    diag_blk = (i64 // GDN_SUB) == (j64 // GDN_SUB)

    eg = jnp.exp(gcb)
    kb = kk * beb
    diag = jnp.where(ii == jj, gcb, 0.0)
    gcr = _dot_sel(jnp.ones((c, c), BF16), jnp.concatenate([diag[g] for g in range(n_inst)], axis=1))
    gcr = jnp.stack([gcr[:, g * 128:(g + 1) * 128] for g in range(n_inst)], axis=0)
    decay = jnp.where(incl, jnp.exp(jnp.where(incl, gcb - gcr, 0.0)), 0.0)
    kpad = jnp.concatenate([kk.astype(BF16), jnp.zeros((n_inst, c, GDN_DK), BF16)], axis=1)
    qk = _bmm_nt(jnp.concatenate([qq, kb], axis=1).astype(BF16), kpad)
    attn = (qk[:, :c] * decay).astype(BF16)
    a_mat = jnp.where(strict, (qk[:, c:] * decay)[:, :, :c], 0.0)
    tinv = _unit_tri_inverse(a_mat, diag_blk)
    rhs = jnp.concatenate([vv * beb, kb * eg], axis=2)
    sol = _bmm(tinv.astype(BF16), rhs.astype(BF16))
    qg = (qq * eg).astype(BF16)
    kd = (kk * jnp.exp(gtb - gcb)).astype(BF16)
    egt = jnp.exp(gtb[:, 0:8, :])
    for g, (ch, dr, hd) in enumerate(inst):
        rs = rows(ch)
        ls = slice(hd * 128, (hd + 1) * 128)
        qg_ref[dr, rs, ls] = qg[g]
        kd_ref[dr, rs, ls] = kd[g]
        kc_ref[dr, rs, ls] = sol[g, :, GDN_DV:].astype(BF16)
        wv_ref[dr, rs, ls] = sol[g, :, :GDN_DV]
        at_ref[dr, rs, ls] = attn[g]
        eg_ref[dr, ch, :, ls] = egt[g]


def _bcast_cols(x, sel):
    hi, mid, lo = _split3(x)
    return _dot(lo, sel) + _dot(mid, sel) + _dot(hi, sel)


def _gdn_prep(qkv, ab, conv_w, ea_row, dtb_row, *, tb):
    b, t, cw = qkv.shape
    nt = t // tb
    nch = t // GDN_CHUNK
    cpb = tb // GDN_CHUNK
    const = lambda shape: pl.BlockSpec(shape, lambda i, j: (0,) * len(shape))
    hw = GDN_WIDTH
    big = lambda: pl.BlockSpec((2, None, tb, hw), lambda i, j: (0, i, j, 0))
    outs = pl.pallas_call(
        functools.partial(_gdn_prep_kernel, tb=tb, nt=nt),
        grid=(b, nt),
        in_specs=_halo_specs(tb, t, cw, 0, lambda j: j) + [
            pl.BlockSpec((None, tb, AB_PAD), lambda i, j: (i, j, 0)),
            const((4, cw)), const((1, AB_PAD)), const((1, AB_PAD))],
        out_specs=[big(), big(), big(), big(), big(),
                   pl.BlockSpec((2, None, cpb, 8, hw), lambda i, j: (0, i, j, 0, 0))],
        out_shape=[jax.ShapeDtypeStruct((2, b, t, hw), BF16),
                   jax.ShapeDtypeStruct((2, b, t, hw), BF16),
                   jax.ShapeDtypeStruct((2, b, t, hw), BF16),
                   jax.ShapeDtypeStruct((2, b, t, hw), F32),
                   jax.ShapeDtypeStruct((2, b, t, hw), BF16),
                   jax.ShapeDtypeStruct((2, b, nch, 8, hw), F32)],
        compiler_params=_cparams(("parallel", "parallel")),
    )(qkv, qkv, qkv, ab, conv_w, ea_row, dtb_row)
    return outs


def _gdn_scan_kernel(qg_f, kd_f, kc_f, wv_f, at_f, eg_f, qg_b, kd_b, kc_b, wv_b, at_b, eg_b, s0_ref,
                     of_ref, ob_ref, sfin_ref, s_scr, *, cps):
    c = GDN_CHUNK
    j = pl.program_id(1)

    @pl.when(j == 0)
    def _():
        s_scr[...] = s0_ref[...]

    dirs = ((qg_f, kd_f, kc_f, wv_f, at_f, eg_f), (qg_b, kd_b, kc_b, wv_b, at_b, eg_b))
    chains = [(dr, hd) for dr in range(2) for hd in range(GDN_HEADS)]
    ls = lambda hd: slice(hd * 128, (hd + 1) * 128)
    s = s_scr[...].reshape(2 * GDN_HEADS, GDN_DK, GDN_DV)
    for q in range(cps):
        chunk = (q, cps - 1 - q)
        rs = lambda dr: slice(chunk[dr] * c, (chunk[dr] + 1) * c)
        stack = lambda f: jnp.stack([f(dirs[dr], rs(dr), ls(hd), chunk[dr]) for dr, hd in chains], axis=0)
        kq = stack(lambda r, t, l, n: jnp.concatenate([r[2][t, l], r[0][t, l]], axis=0))
        wv = stack(lambda r, t, l, n: r[3][t, l])
        at = stack(lambda r, t, l, n: r[4][t, l][:, :c])
        eg = stack(lambda r, t, l, n: r[5][n, 0:1, l])
        r = _bmm(kq, s.astype(BF16))
        vb = (wv - r[:, :c]).astype(BF16)
        o = r[:, c:] + _bmm(at, vb)
        upd = []
        for g, (dr, hd) in enumerate(chains):
            (of_ref, ob_ref)[dr][rs(dr), ls(hd)] = o[g]
            upd.append(_dot_tn(dirs[dr][1][rs(dr), ls(hd)], vb[g]))
        s = s * eg + jnp.stack(upd, axis=0)
    s_scr[...] = s.reshape(2, GDN_HEADS, GDN_DK, GDN_DV)
    sfin_ref[...] = s_scr[...]


def _gdn_scan(prep, s0, *, cps):
    qg, kd, kc, wv, at, eg = prep
    _, b, t, hw = qg.shape
    c = GDN_CHUNK
    nblk = t // (c * cps)
    fw = lambda: pl.BlockSpec((None, None, cps * c, hw), lambda i, j: (0, i, j, 0))
    bw = lambda: pl.BlockSpec((None, None, cps * c, hw), lambda i, j: (1, i, nblk - 1 - j, 0))
    egf = pl.BlockSpec((None, None, cps, 8, hw), lambda i, j: (0, i, j, 0, 0))
    egb = pl.BlockSpec((None, None, cps, 8, hw), lambda i, j: (1, i, nblk - 1 - j, 0, 0))
    st = pl.BlockSpec((None, 2, GDN_HEADS, GDN_DK, GDN_DV), lambda i, j: (i, 0, 0, 0, 0))
    o_f, o_b, s_fin = pl.pallas_call(
        functools.partial(_gdn_scan_kernel, cps=cps),
        grid=(b, nblk),
        in_specs=[fw(), fw(), fw(), fw(), fw(), egf, bw(), bw(), bw(), bw(), bw(), egb, st],
        out_specs=[pl.BlockSpec((None, cps * c, hw), lambda i, j: (i, j, 0)),
                   pl.BlockSpec((None, cps * c, hw), lambda i, j: (i, nblk - 1 - j, 0)),
                   st],
        out_shape=[jax.ShapeDtypeStruct((b, t, hw), F32), jax.ShapeDtypeStruct((b, t, hw), F32),
                   jax.ShapeDtypeStruct((b, 2, GDN_HEADS, GDN_DK, GDN_DV), F32)],
        scratch_shapes=[pltpu.VMEM((2, GDN_HEADS, GDN_DK, GDN_DV), F32)],
        compiler_params=_cparams(("parallel", "arbitrary")),
    )(qg, kd, kc, wv, at, eg, qg, kd, kc, wv, at, eg, s0)
    return o_f, o_b, s_fin


def _out_rg_kernel(x_ref, y_ref, w_ref, gt_ref, o_ref):
    o_ref[...] = x_ref[...] + gt_ref[...] * _dot(y_ref[...], w_ref[...])


def _out_gdn_kernel(x_ref, of_ref, ob_ref, z_ref, ng_ref, w_ref, gt_ref, o_ref, *, ncol, d):
    o = of_ref[...] + ob_ref[...]
    z = z_ref[...]
    parts = []
    for hd in range(GDN_HEADS):
        ls = slice(hd * GDN_DV, (hd + 1) * GDN_DV)
        parts.append(_rms(o[:, ls], ng_ref[...]) * _silu(z[:, ls]))
    y = jnp.concatenate(parts, axis=1).astype(BF16)
    t2 = gt_ref[...] * _dot(y, w_ref[...])
    rows = t2.shape[0] // ncol
    for j in range(ncol):
        o_ref[:, j, :] = x_ref[:, j, :] + t2[j * rows:(j + 1) * rows]


def _out_project(x3, y_rg, o_f, o_b, z, norm_g, w_out, gt, *, tile):
    b, t, d = x3.shape
    w = RG_WIDTH
    vec = lambda: pl.BlockSpec((None, 1, d), lambda i, j: (i, 0, 0))
    x1 = pl.pallas_call(
        _out_rg_kernel,
        grid=(b, t // tile),
        in_specs=[pl.BlockSpec((None, tile, d), lambda i, j: (i, j, 0)),
                  pl.BlockSpec((None, tile, w), lambda i, j: (i, j, 0)),
                  pl.BlockSpec((w, d), lambda i, j: (0, 0)), vec()],
        out_specs=pl.BlockSpec((None, tile, d), lambda i, j: (i, j, 0)),
        out_shape=jax.ShapeDtypeStruct((b, t, d), F32),
        compiler_params=_cparams(("parallel", "parallel")),
    )(x3, y_rg, w_out[:w], gt)
    rows = t // GRID_W
    ncol = tile // rows
    hw = GDN_WIDTH
    x1v = x1.reshape(b, rows, GRID_W, d)
    cm = lambda wd: pl.BlockSpec((None, tile, wd), lambda i, j: (i, j, 0))
    x1v = pl.pallas_call(
        functools.partial(_out_gdn_kernel, ncol=ncol, d=d),
        grid=(b, GRID_W // ncol),
        in_specs=[pl.BlockSpec((None, rows, ncol, d), lambda i, j: (i, 0, j, 0)),
                  cm(hw), cm(hw), cm(hw),
                  pl.BlockSpec((1, GDN_DV), lambda i, j: (0, 0)),
                  pl.BlockSpec((hw, d), lambda i, j: (0, 0)), vec()],
        out_specs=pl.BlockSpec((None, rows, ncol, d), lambda i, j: (i, 0, j, 0)),
        out_shape=jax.ShapeDtypeStruct((b, rows, GRID_W, d), F32),
        input_output_aliases={0: 0},
        compiler_params=_cparams(("parallel", "parallel")),
    )(x1v, o_f, o_b, z, norm_g, w_out[w:], gt)
    return x1v.reshape(b, t, d)


def _fold_kernel(wq_ref, k_ref, o_ref):
    o_ref[...] = lax.dot_general(wq_ref[...], k_ref[...], (((1,), (1,)), ((), ())),
                                 preferred_element_type=F32, precision=lax.Precision.HIGHEST)


def _fold_keys(wq, keys):
    d, n = wq.shape
    nblk = n // PEER_HALF
    return pl.pallas_call(
        _fold_kernel,
        grid=(nblk,),
        in_specs=[pl.BlockSpec((d, PEER_HALF), lambda j: (0, j)),
                  pl.BlockSpec((None, PEER_NKEYS, PEER_HALF), lambda j: (j % 2, 0, 0))],
        out_specs=pl.BlockSpec((d, PEER_NKEYS), lambda j: (0, j)),
        out_shape=jax.ShapeDtypeStruct((d, nblk * PEER_NKEYS), F32),
        compiler_params=_cparams(("parallel",)),
    )(wq, keys)


def _extract16(vals, ids, s_scr, i_scr, *, fast, n_masked=0):
    n, p = vals.shape
    for r in range(PEER_TOPK):
        m = jnp.max(vals, axis=0, keepdims=True)
        eq = vals == m
        if fast:
            idx = _dot(ids, jnp.where(eq, 1.0, 0.0).astype(BF16))[0:1, :].astype(jnp.int32)
            vals = jnp.where(eq, NEG_INF, vals)
        else:
            idx = jnp.min(jnp.where(eq, ids, 1 << 20), axis=0, keepdims=True)
            vals = jnp.where(ids == idx, NEG_INF, vals)
        s_scr[r:r + 1, :] = m
        i_scr[r:r + 1, :] = idx
    if not fast:
        return None
    dropped = jnp.sum(jnp.where(vals == NEG_INF, 1.0, 0.0), axis=0, keepdims=True)
    return jnp.where(dropped == float(PEER_TOPK + n_masked), 0.0, 1.0)


def _peer_sel_kernel(x_ref, g_ref, sh_ref, sc_ref, wt_ref, h_ref, a_ref, b_ref, gate_ref,
                     sc_scr, s1, i1, s2, i2, ts, tp, a_t, b_t, g_t):
    k = PEER_TOPK
    h = _rms(x_ref[...], g_ref[...]) * (1.0 + sc_ref[...]) + sh_ref[...]
    hb = h.astype(BF16)
    h_ref[...] = hb
    sc_scr[...] = _dot_nt(wt_ref[...], hb)
    p = hb.shape[0]

    n_cand = 80
    n_masked = sum(8 - k // (r + 1) for r in range(1, 8))

    def cand_pos(row):
        return jnp.where(row < 16, row, jnp.where(row < 72, ((row - 16) // 8 + 1) * k + (row - 16) % 8,
                                                  (row - 72 + 8) * k))

    def head(hd, bad, fast):
        if fast:
            key_ids = lax.broadcasted_iota(jnp.int32, (8, PEER_NKEYS), 1).astype(F32).astype(BF16)
            pos_ids = cand_pos(lax.broadcasted_iota(jnp.int32, (8, n_cand), 1)).astype(F32).astype(BF16)
        else:
            key_ids = lax.broadcasted_iota(jnp.int32, (PEER_NKEYS, p), 0)
            pos_ids = cand_pos(lax.broadcasted_iota(jnp.int32, (n_cand, p), 0))
        sub = lax.broadcasted_iota(jnp.int32, (8, p), 0)
        base = pl.multiple_of(hd * 2 * PEER_NKEYS, 2 * PEER_NKEYS)
        bad1 = _extract16(sc_scr[pl.ds(base, PEER_NKEYS), :], key_ids, s1, i1, fast=fast)
        bad2 = _extract16(sc_scr[pl.ds(base + PEER_NKEYS, PEER_NKEYS), :], key_ids, s2, i2, fast=fast)
        s1v, s2v = s1[...], s2[...]
        cands = [s1v[0:1, :] + s2v]
        for r in range(1, 8):
            cands.append(jnp.where(sub < k // (r + 1), s1v[r:r + 1, :] + s2v[0:8, :], NEG_INF))
        cands.append(s1v[8:16, :] + s2v[0:1, :])
        bad3 = _extract16(jnp.concatenate(cands, axis=0), pos_ids, ts, tp, fast=fast, n_masked=n_masked)
        if fast:
            bad = bad + bad1 + bad2 + bad3
        top_s, top_p = ts[...], tp[...]
        rr = top_p >> 4
        cc = top_p & (k - 1)
        i1v, i2v = i1[...], i2[...]
        av = jnp.zeros_like(top_p)
        bv = jnp.zeros_like(top_p)
        for q in range(k):
            av = jnp.where(rr == q, i1v[q:q + 1, :], av)
            bv = jnp.where(cc == q, i2v[q:q + 1, :], bv)
        e = jnp.exp(top_s - top_s[0:1, :])
        gate = e / jnp.sum(e, axis=0, keepdims=True)
        o = pl.multiple_of(hd * k, k)
        a_t[pl.ds(o, k), :] = av
        b_t[pl.ds(o, k), :] = bv
        g_t[pl.ds(o, k), :] = gate
        return bad

    bad = lax.fori_loop(0, PEER_HEADS, functools.partial(head, fast=True), jnp.zeros((1, p), F32))

    @pl.when(jnp.max(bad) > 0.0)
    def _():
        lax.fori_loop(0, PEER_HEADS, functools.partial(head, fast=False), jnp.zeros((1, p), F32))

    a_ref[...] = a_t[...].T
    b_ref[...] = b_t[...].T
    gate_ref[...] = g_t[...].T


def _peer_select(x1, g, sh, sc, wt, *, tile):
    b, t, d = x1.shape
    nt = t // tile
    nk = PEER_HEADS * PEER_TOPK
    nrow = wt.shape[0]
    vec = lambda: pl.BlockSpec((None, 1, d), lambda i, j: (i, 0, 0))
    tr = lambda: pl.BlockSpec((None, tile, nk), lambda i, j: (i, j, 0))
    k = PEER_TOPK
    return pl.pallas_call(
        _peer_sel_kernel,
        grid=(b, nt),
        in_specs=[pl.BlockSpec((None, tile, d), lambda i, j: (i, j, 0)),
                  pl.BlockSpec((1, d), lambda i, j: (0, 0)), vec(), vec(),
                  pl.BlockSpec((nrow, d), lambda i, j: (0, 0))],
        out_specs=[pl.BlockSpec((None, tile, d), lambda i, j: (i, j, 0)), tr(), tr(), tr()],
        out_shape=[jax.ShapeDtypeStruct((b, t, d), BF16),
                   jax.ShapeDtypeStruct((b, t, nk), jnp.int32),
                   jax.ShapeDtypeStruct((b, t, nk), jnp.int32),
                   jax.ShapeDtypeStruct((b, t, nk), F32)],
        scratch_shapes=[pltpu.VMEM((nrow, tile), F32),
                        pltpu.VMEM((k, tile), F32), pltpu.VMEM((k, tile), jnp.int32),
                        pltpu.VMEM((k, tile), F32), pltpu.VMEM((k, tile), jnp.int32),
                        pltpu.VMEM((k, tile), F32), pltpu.VMEM((k, tile), jnp.int32),
                        pltpu.VMEM((nk, tile), jnp.int32), pltpu.VMEM((nk, tile), jnp.int32),
                        pltpu.VMEM((nk, tile), F32)],
        compiler_params=_cparams(("parallel", "parallel")),
    )(x1, g.reshape(1, d), sh, sc, wt)


def _peer_mix_kernel(h_ref, a_ref, b_ref, gate_ref, u_ref, v_ref, x_ref, gt_ref, fg_ref, o_ref,
                     m_scr, acc, w0, w1, *, tile, n_steps, pairs, n_total):
    nk = PEER_NKEYS
    half = nk // 2
    g = pl.program_id(0)
    step = jnp.minimum(g, n_total - 1) % n_steps
    step_b = jnp.maximum(g - 1, 0) % n_steps
    hi_mask = jnp.uint32(0xFFFF0000)

    @pl.when(g == 0)
    def _():
        w1[...] = jnp.zeros(w1.shape, BF16)
        acc[...] = jnp.zeros(acc.shape, F32)

    @pl.when((step == 0) & (g < n_total))
    def _():
        sub = lax.broadcasted_iota(jnp.int32, (nk, a_ref.shape[1]), 0)
        a_of_row = jnp.where(sub < half, 2 * sub, 2 * (sub - half) + 1)

        def build(p, carry):
            arow = a_ref[pl.ds(p, 1), :]
            brow = b_ref[pl.ds(p, 1), :]
            grow = gate_ref[pl.ds(p, 1), :]
            xa = jnp.where(a_of_row == arow, 1.0, 0.0).astype(BF16)
            yb = jnp.where(sub == brow, 0.5 * grow, 0.0).astype(BF16)
            m = _dot_nt(xa, yb).astype(BF16).astype(F32)
            bits = lax.bitcast_convert_type(m, jnp.uint32)
            m_scr[pl.ds(pl.multiple_of(p * half, half), half), :] = (bits[half:] & hi_mask) | (bits[:half] >> 16)
            return carry

        lax.fori_loop(0, tile, build, 0, unroll=16)

    def stages(w_out, w_in):
        h = h_ref[...]
        contrib = None
        for q in range(pairs):
            es = slice(q * 2 * nk, (q + 1) * 2 * nk)
            pr = step * pairs + q
            act = _gelu_times_2(_dot_nt(h, u_ref[es, :]))
            w = m_scr[pl.ds(pr, tile, stride=half), :]
            m_even = lax.bitcast_convert_type(w << 16, F32)
            m_odd = lax.bitcast_convert_type(w & hi_mask, F32)
            w_out[:, es] = (act * jnp.concatenate([m_even, m_odd], axis=1)).astype(BF16)
            c_q = _dot(w_in[:, es], v_ref[es, :])
            contrib = c_q if contrib is None else contrib + c_q
        acc[...] = jnp.where(step_b == 0, contrib, acc[...] + contrib)

    @pl.when(g % 2 == 0)
    def _():
        stages(w0, w1)

    @pl.when(g % 2 == 1)
    def _():
        stages(w1, w0)

    @pl.when((step_b == n_steps - 1) & (g > 0))
    def _():
        x2 = x_ref[...] + gt_ref[...] * acc[...]
        o_ref[...] = _rms(x2, fg_ref[...])


def _peer_mix(hb, aidx, bidx, gate, u, v, x1, gt, final_g, *, tile, pairs):
    b, t, d = x1.shape
    nt = t // tile
    nk = PEER_NKEYS
    npk = aidx.shape[-1]
    n_steps = nk // (2 * pairs)
    eb = 2 * nk * pairs
    n_total = b * nt * n_steps
    blk_a = lambda g: jnp.minimum(g, n_total - 1)
    blk_b = lambda g: jnp.maximum(g - 1, 0)
    tile_of = lambda blk: ((blk // n_steps) // nt, (blk // n_steps) % nt)
    tok_a = lambda wd: pl.BlockSpec((None, tile, wd), lambda g: (*tile_of(blk_a(g)), 0))
    tok_b = lambda wd: pl.BlockSpec((None, tile, wd), lambda g: (*tile_of(blk_b(g)), 0))
    return pl.pallas_call(
        functools.partial(_peer_mix_kernel, tile=tile, n_steps=n_steps, pairs=pairs, n_total=n_total),
        grid=(n_total + 1,),
        in_specs=[tok_a(d), tok_a(npk), tok_a(npk), tok_a(npk),
                  pl.BlockSpec((eb, d), lambda g: (blk_a(g) % n_steps, 0)),
                  pl.BlockSpec((eb, d), lambda g: (blk_b(g) % n_steps, 0)),
                  tok_b(d),
                  pl.BlockSpec((None, 1, d), lambda g: (tile_of(blk_b(g))[0], 0, 0)),
                  pl.BlockSpec((1, d), lambda g: (0, 0))],
        out_specs=tok_b(d),
        out_shape=jax.ShapeDtypeStruct((b, t, d), F32),
        scratch_shapes=[pltpu.VMEM((tile * nk // 2, nk), jnp.uint32), pltpu.VMEM((tile, d), F32),
                        pltpu.VMEM((tile, eb), BF16), pltpu.VMEM((tile, eb), BF16)],
        compiler_params=_cparams(("arbitrary",)),
    )(hb, aidx, bidx, gate, u, v, x1, gt, final_g.reshape(1, d))


def _block_diag(w):
    n, e, _ = w.shape
    eye = jnp.eye(n, dtype=w.dtype)
    return (eye[:, None, :, None] * w[:, :, None, :]).reshape(n * e, n * e)


def _mix_sequence(x3, ctx_mode, params, states, mods, *, rg_tb, gdn_tb, proj_tile):
    (norm1_g, w_rg, w_gdn, rg_conv_w, rg_conv_b, wg, gate_b, c_lam, gdn_conv_w, ea_row, dtb_row) = params
    sh1, sc1, mod_row = mods
    rg_h0_f, rg_h0_b, gdn_s0 = states
    (p_rg,) = _project(x3, norm1_g, sh1, sc1, mod_row, w_rg, (2 * RG_WIDTH,), colmajor=False, tile=proj_tile)
    qkv, z, ab = _project(x3, norm1_g, sh1, sc1, mod_row, w_gdn, (3 * GDN_WIDTH, GDN_WIDTH, AB_PAD),
                          colmajor=not ctx_mode, tile=proj_tile)
    h_f, st_f = _rglru_pass(p_rg, rg_conv_w, rg_conv_b, wg[0], gate_b[0], c_lam[0], rg_h0_f, None,
                            reverse=False, tb=rg_tb)
    y_rg, st_b = _rglru_pass(p_rg, rg_conv_w, rg_conv_b, wg[1], gate_b[1], c_lam[1], rg_h0_b, h_f,
                             reverse=True, tb=rg_tb)
    prep = _gdn_prep(qkv, ab, gdn_conv_w, ea_row, dtb_row, tb=gdn_tb)
    o_f, o_b, s_fin = _gdn_scan(prep, gdn_s0, cps=min(4, x3.shape[1] // GDN_CHUNK))
    return y_rg, o_f, o_b, z, (st_f, st_b, s_fin)


def kernel(x, c, ctx, c_ctx, w_mod, b_mod, norm1_g, norm2_g, w_in, rg_conv_w, rg_conv_b, rg_gate_w, rg_gate_b,
           rg_lambda, gdn_conv_w, gdn_a_log, gdn_dt_bias, gdn_norm_g, w_out, peer_wq, peer_keys, peer_u, peer_v,
           final_g):
    b, t, d = x.shape
    depth = w_mod.shape[0]
    assert depth == 1, "context residual stream update is only needed for depth > 1"
    l = 0
    w = RG_WIDTH

    cc = jnp.zeros((16, d), F32).at[:b].set(c).at[b].set(c_ctx)
    w_rg = w_in[l][:, :2 * w].astype(BF16)
    n_ab = w_in.shape[2] - 2 * w - 4 * GDN_WIDTH
    w_gdn = jnp.concatenate([w_in[l][:, 2 * w:2 * w + 4 * GDN_WIDTH],
                             jnp.pad(w_in[l][:, 2 * w + 4 * GDN_WIDTH:], ((0, 0), (0, AB_PAD - n_ab)))],
                            axis=1).astype(BF16)
    wg = jnp.stack([jnp.concatenate([_block_diag(rg_gate_w[l, dr, 0]), _block_diag(rg_gate_w[l, dr, 1])], axis=1)
                    for dr in range(2)]).astype(BF16)
    gate_b = rg_gate_b[l].reshape(2, 1, 2 * w)
    c_lam = (-RG_C * jax.nn.softplus(-rg_lambda[l])).reshape(2, 1, w)
    ea = jnp.exp(gdn_a_log[l])
    pad4 = jnp.zeros((2, GDN_HEADS), F32)
    ea_row = jnp.pad(jnp.concatenate([ea, pad4], axis=1).reshape(1, -1), ((0, 0), (0, AB_PAD - n_ab)))
    dtb_row = jnp.pad(jnp.concatenate([gdn_dt_bias[l], pad4], axis=1).reshape(1, -1), ((0, 0), (0, AB_PAD - n_ab)))
    params = (norm1_g[l], w_rg, w_gdn, rg_conv_w[l], rg_conv_b[l].reshape(1, w), wg, gate_b, c_lam,
              gdn_conv_w[l], ea_row, dtb_row)

    mod = _modulation(cc, w_mod[l], b_mod[l])
    sh1, sc1, gt1, sh2, sc2, gt2 = [mod[:, i * d:(i + 1) * d].reshape(16, 1, d) for i in range(6)]

    tc = ctx.shape[1]
    zero_states = (jnp.zeros((b, 1, w), F32), jnp.zeros((b, 1, w), F32),
                   jnp.zeros((b, 2, GDN_HEADS, GDN_DK, GDN_DV), F32))
    ctx_tb = min(tc, 256)
    _, _, _, _, ctx_states = _mix_sequence(ctx, True, params, zero_states, (sh1, sc1, lambda i: b),
                                           rg_tb=ctx_tb, gdn_tb=min(tc, 128), proj_tile=ctx_tb)

    tile = min(t, 512)
    y_rg, o_f, o_b, z, _ = _mix_sequence(x, False, params, ctx_states, (sh1, sc1, lambda i: i),
                                         rg_tb=min(t, 256), gdn_tb=min(t, 256), proj_tile=tile)
    x1 = _out_project(x, y_rg, o_f, o_b, z, gdn_norm_g[l].reshape(1, GDN_DV), w_out[l].astype(BF16), gt1, tile=tile)

    wfold_t = _fold_keys(peer_wq[l], peer_keys[l]).T.astype(BF16)
    sel_tile = min(t, 256)
    hb, aidx, bidx, gate = _peer_select(x1, norm2_g[l], sh2, sc2, wfold_t, tile=sel_tile)
    out = _peer_mix(hb, aidx, bidx, gate, peer_u[l].astype(BF16), peer_v[l].astype(BF16),
                    x1, gt2, final_g, tile=tile, pairs=8)
    return out
```

```python
import functools
import math

import jax
import jax.numpy as jnp
from jax import lax
from jax.experimental import pallas as pl
from jax.experimental.pallas import tpu as pltpu

F32 = jnp.float32
BF16 = jnp.bfloat16

GRID_W = 64
EPS = 1e-6
RG_WIDTH = 512
RG_BLOCKS = 8
RG_C = 8.0
GDN_HEADS = 4
GDN_DK = 128
GDN_DV = 128
GDN_WIDTH = GDN_HEADS * GDN_DV
GDN_CHUNK = 64
GDN_SUB = 16
AB_PAD = 128
PEER_HEADS = 8
PEER_NKEYS = 128
PEER_HALF = 128
PEER_TOPK = 16
NEG_INF = float("-inf")
MASK_PITCH = 72

VMEM_LIMIT = 58 * 1024 * 1024


def _cparams(sem):
    return pltpu.CompilerParams(dimension_semantics=sem, vmem_limit_bytes=VMEM_LIMIT)


def _dot(a, b):
    return jnp.dot(a, b, preferred_element_type=F32)


def _dot_nt(a, b):
    return lax.dot_general(a, b, (((1,), (1,)), ((), ())), preferred_element_type=F32)


def _dot_tn(a, b):
    return lax.dot_general(a, b, (((0,), (0,)), ((), ())), preferred_element_type=F32)


def _split3(x):
    hi = x.astype(BF16)
    r = x - hi.astype(F32)
    mid = r.astype(BF16)
    lo = (r - mid.astype(F32)).astype(BF16)
    return hi, mid, lo


def _dot_sel(m, x):
    hi, mid, lo = _split3(x)
    return _dot(m, lo) + _dot(m, mid) + _dot(m, hi)


def _silu(x):
    return x * jax.nn.sigmoid(x)


def _softplus(x):
    return jnp.maximum(x, 0.0) + jnp.log1p(jnp.exp(-jnp.abs(x)))


def _gelu_times_2(x):
    c = math.sqrt(2.0 / math.pi)
    return x * (1.0 + jnp.tanh(x * (c + (c * 0.044715) * (x * x))))


def _rms(x, g):
    return x * lax.rsqrt(jnp.mean(x * x, axis=-1, keepdims=True) + EPS) * g


def _mod_kernel(c_ref, w_ref, b_ref, o_ref):
    s = _silu(c_ref[...])
    o_ref[...] = jnp.dot(s, w_ref[...], preferred_element_type=F32,
                         precision=lax.Precision.HIGHEST) + b_ref[...]


def _modulation(cc, w_mod, b_mod):
    m, d = cc.shape
    n = w_mod.shape[1]
    tn = 1536
    return pl.pallas_call(
        _mod_kernel,
        grid=(n // tn,),
        in_specs=[pl.BlockSpec((m, d), lambda j: (0, 0)),
                  pl.BlockSpec((d, tn), lambda j: (0, j)),
                  pl.BlockSpec((1, tn), lambda j: (0, j))],
        out_specs=pl.BlockSpec((m, tn), lambda j: (0, j)),
        out_shape=jax.ShapeDtypeStruct((m, n), F32),
        compiler_params=_cparams(("arbitrary",)),
    )(cc, w_mod, b_mod.reshape(1, n))


def _proj_kernel(x_ref, g_ref, sh_ref, sc_ref, w_ref, *o_refs, ncol, d, widths):
    if ncol > 1:
        x = jnp.concatenate([x_ref[:, j, :] for j in range(ncol)], axis=0)
    else:
        x = x_ref[...]
    h = _rms(x, g_ref[...]) * (1.0 + sc_ref[...]) + sh_ref[...]
    o = _dot(h.astype(BF16), w_ref[...])
    off = 0
    for o_ref, wd in zip(o_refs, widths):
        o_ref[...] = o[:, off:off + wd]
        off += wd


def _project(x3, g, sh, sc, mod_row, w, widths, *, colmajor, tile):
    b, t, d = x3.shape
    n = w.shape[1]
    if colmajor:
        rows = t // GRID_W
        ncol = tile // rows
        xv = x3.reshape(b, rows, GRID_W, d)
        x_spec = pl.BlockSpec((None, rows, ncol, d), lambda i, j: (i, 0, j, 0))
        nt = GRID_W // ncol
    else:
        ncol = 1
        xv = x3
        x_spec = pl.BlockSpec((None, tile, d), lambda i, j: (i, j, 0))
        nt = t // tile
    vec = lambda: pl.BlockSpec((None, 1, d), lambda i, j: (mod_row(i), 0, 0))
    outs = pl.pallas_call(
        functools.partial(_proj_kernel, ncol=ncol, d=d, widths=tuple(widths)),
        grid=(b, nt),
        in_specs=[x_spec, pl.BlockSpec((1, d), lambda i, j: (0, 0)), vec(), vec(),
                  pl.BlockSpec((d, n), lambda i, j: (0, 0))],
        out_specs=[pl.BlockSpec((None, tile, wd), lambda i, j: (i, j, 0)) for wd in widths],
        out_shape=[jax.ShapeDtypeStruct((b, t, wd), F32) for wd in widths],
        compiler_params=_cparams(("parallel", "parallel")),
    )(xv, g.reshape(1, d), sh, sc, w)
    return outs


def _conv4(cur, prev8, nxt8, w_ref, first, last):
    tb, c = cur.shape
    g = tb // 8
    prev8 = jnp.where(first, 0.0, prev8)
    nxt8 = jnp.where(last, 0.0, nxt8)
    ext = jnp.concatenate([prev8, cur, nxt8], axis=0).reshape(g + 2, 8, c)
    sub = lax.broadcasted_iota(jnp.int32, (g, 8, c), 1)

    def back(k):
        r = pltpu.roll(ext, k, 1)
        return jnp.where(sub >= k, r[1:g + 1], r[0:g])

    r = pltpu.roll(ext, 7, 1)
    xp1 = jnp.where(sub < 7, r[1:g + 1], r[2:g + 2])
    y = (w_ref[0:1, :] * back(2) + w_ref[1:2, :] * back(1) + w_ref[2:3, :] * ext[1:g + 1] + w_ref[3:4, :] * xp1)
    return y.reshape(tb, c)


def _halo_specs(tb, t, width, lane_blk, tmap):
    r = tb // 8
    nb8 = t // 8
    return [pl.BlockSpec((None, tb, width), lambda i, j: (i, tmap(j), lane_blk)),
            pl.BlockSpec((None, 8, width), lambda i, j: (i, jnp.maximum(tmap(j) * r - 1, 0), lane_blk)),
            pl.BlockSpec((None, 8, width), lambda i, j: (i, jnp.minimum((tmap(j) + 1) * r, nb8 - 1), lane_blk))]


def _rglru_kernel(*refs, tb, nt, reverse):
    if reverse:
        (u_ref, up_ref, un_ref, cw_ref, cb_ref, wg_ref, gb_ref, cl_ref, h0_ref, gate_ref, hf_ref,
         y_ref, st_ref, carry) = refs
    else:
        (u_ref, up_ref, un_ref, cw_ref, cb_ref, wg_ref, gb_ref, cl_ref, h0_ref,
         y_ref, st_ref, carry) = refs
    j = pl.program_id(1)
    tblk = (nt - 1 - j) if reverse else j
    w = u_ref.shape[-1]

    @pl.when(j == 0)
    def _():
        carry[...] = h0_ref[...]

    u = u_ref[...]
    xc = _conv4(u, up_ref[...], un_ref[...], cw_ref, tblk == 0, tblk == nt - 1) + cb_ref[...]
    gates = _dot(xc.astype(BF16), wg_ref[...]) + gb_ref[...]
    r = jax.nn.sigmoid(gates[:, :w])
    ig = jax.nn.sigmoid(gates[:, w:])
    log_a = r * cl_ref[...]
    a = jnp.exp(log_a)
    th = jnp.tanh(log_a)
    bb = jnp.sqrt(-2.0 * th / (1.0 - th)) * (ig * xc)

    n_grp = tb // 8
    a = a.reshape(n_grp, 8, w)
    bb = bb.reshape(n_grp, 8, w)
    sub = lax.broadcasted_iota(jnp.int32, (n_grp, 8, w), 1)
    s = 1
    while s < 8:
        shift, ok = (8 - s, sub < 8 - s) if reverse else (s, sub >= s)
        a_sh = jnp.where(ok, pltpu.roll(a, shift, 1), 1.0)
        b_sh = jnp.where(ok, pltpu.roll(bb, shift, 1), 0.0)
        bb = a * b_sh + bb
        a = a * a_sh
        s *= 2
    h_prev = carry[...]
    hs = [None] * n_grp
    for g in (range(n_grp - 1, -1, -1) if reverse else range(n_grp)):
        hg = a[g] * h_prev + bb[g]
        h_prev = hg[0:1, :] if reverse else hg[7:8, :]
        hs[g] = hg
    h = jnp.concatenate(hs, axis=0)
    carry[...] = h_prev
    st_ref[...] = carry[...]
    if reverse:
        y_ref[...] = ((hf_ref[...] + h) * jax.nn.gelu(gate_ref[...])).astype(y_ref.dtype)
    else:
        y_ref[...] = h


def _rglru_pass(p_rg, conv_w, conv_b, wg, gate_b, c_lam, h0, hf, *, reverse, tb):
    b, t, w2 = p_rg.shape
    w = w2 // 2
    nt = t // tb
    tmap = (lambda j: nt - 1 - j) if reverse else (lambda j: j)
    const = lambda shape: pl.BlockSpec(shape, lambda i, j: (0,) * len(shape))
    in_specs = _halo_specs(tb, t, w, 0, tmap) + [
        const((4, w)), const((1, w)), const((w, 2 * w)), const((1, 2 * w)), const((1, w)),
        pl.BlockSpec((None, 1, w), lambda i, j: (i, 0, 0))]
    args = [p_rg, p_rg, p_rg, conv_w, conv_b, wg, gate_b, c_lam, h0]
    if reverse:
        in_specs += [pl.BlockSpec((None, tb, w), lambda i, j: (i, tmap(j), 1)),
                     pl.BlockSpec((None, tb, w), lambda i, j: (i, tmap(j), 0))]
        args += [p_rg, hf]
    y, st = pl.pallas_call(
        functools.partial(_rglru_kernel, tb=tb, nt=nt, reverse=reverse),
        grid=(b, nt),
        in_specs=in_specs,
        out_specs=[pl.BlockSpec((None, tb, w), lambda i, j: (i, tmap(j), 0)),
                   pl.BlockSpec((None, 1, w), lambda i, j: (i, 0, 0))],
        out_shape=[jax.ShapeDtypeStruct((b, t, w), BF16 if reverse else F32),
                   jax.ShapeDtypeStruct((b, 1, w), F32)],
        scratch_shapes=[pltpu.VMEM((1, w), F32)],
        compiler_params=_cparams(("parallel", "arbitrary")),
    )(*args)
    return y, st


def _bmm(a, b):
    return lax.dot_general(a, b, (((2,), (1,)), ((0,), (0,))), preferred_element_type=F32)


def _bmm_nt(a, b):
    return lax.dot_general(a, b, (((2,), (2,)), ((0,), (0,))), preferred_element_type=F32)


def _unit_tri_inverse(a, diag_mask):
    c = a.shape[-1]
    eye = (lax.broadcasted_iota(jnp.int32, (1, c, c), 1) == lax.broadcasted_iota(jnp.int32, (1, c, c), 2)).astype(F32)
    mm = lambda p, q: _bmm(p.astype(BF16), q.astype(BF16))
    ad = jnp.where(diag_mask, a, 0.0)
    x = eye - ad
    pw = ad
    k = 2
    while k < GDN_SUB:
        pw = mm(pw, pw)
        x = x + mm(x, pw)
        k *= 2
    n = mm(x, a - ad)
    nblk = c // GDN_SUB
    y = eye - n
    pw = n
    k = 2
    while k < nblk:
        pw = mm(pw, pw)
        y = y + mm(y, pw)
        k *= 2
    return mm(y, x)


def _gdn_prep_kernel(qkv_ref, qp_ref, qn_ref, ab_ref, cw_ref, ea_ref, dtb_ref,
                     qg_ref, kd_ref, kc_ref, wv_ref, at_ref, eg_ref, *, tb, nt):
    c = GDN_CHUNK
    j = pl.program_id(1)
    hw = GDN_WIDTH
    x = _silu(_conv4(qkv_ref[...], qp_ref[...], qn_ref[...], cw_ref, j == 0, j == nt - 1))

    ab = ab_ref[...]
    col = lax.broadcasted_iota(jnp.int32, (tb, AB_PAD), 1)
    is_a = (col & 4) == 0
    gbv = jnp.where(is_a, -ea_ref[...] * _softplus(ab + dtb_ref[...]), jax.nn.sigmoid(ab))

    ri = lax.broadcasted_iota(jnp.int32, (tb, tb), 0)
    ci = lax.broadcasted_iota(jnp.int32, (tb, tb), 1)
    same = (ri // c) == (ci // c)
    l_f = (same & (ci <= ri)).astype(BF16)
    l_b = (same & (ci >= ri)).astype(BF16)
    l_t = same.astype(BF16)
    gcum = jnp.where(col < 8, _dot_sel(l_f, gbv), _dot_sel(l_b, gbv))
    gtot = _dot_sel(l_t, gbv)

    cols_a = [dr * 8 + hd for dr in range(2) for hd in range(GDN_HEADS)]
    bcast = lambda v, cols: jnp.concatenate([jnp.broadcast_to(v[:, cc:cc + 1], (tb, 128)) for cc in cols], axis=1)
    gc_all = bcast(gcum, cols_a)
    gt_all = bcast(gtot, cols_a)
    be_all = bcast(gbv, [cc + 4 for cc in cols_a])

    qs, ks, vs = [], [], []
    for hd in range(GDN_HEADS):
        q = x[:, hd * GDN_DK:(hd + 1) * GDN_DK]
        k = x[:, hw + hd * GDN_DK: hw + (hd + 1) * GDN_DK]
        qs.append(q * lax.rsqrt(jnp.sum(q * q, axis=-1, keepdims=True) + EPS) * (GDN_DK ** -0.5))
        ks.append(k * lax.rsqrt(jnp.sum(k * k, axis=-1, keepdims=True) + EPS))
        vs.append(x[:, 2 * hw + hd * GDN_DV: 2 * hw + (hd + 1) * GDN_DV])
    inst = [(ch, dr, hd) for ch in range(tb // c) for dr in range(2) for hd in range(GDN_HEADS)]
    n_inst = len(inst)
    rows = lambda ch: slice(ch * c, (ch + 1) * c)
    lanes = lambda dr, hd: slice((dr * GDN_HEADS + hd) * 128, (dr * GDN_HEADS + hd + 1) * 128)
    stack = lambda f: jnp.stack([f(ch, dr, hd) for ch, dr, hd in inst], axis=0)
    qq = stack(lambda ch, dr, hd: qs[hd][rows(ch)])
    kk = stack(lambda ch, dr, hd: ks[hd][rows(ch)])
    vv = stack(lambda ch, dr, hd: vs[hd][rows(ch)])
    gcb = stack(lambda ch, dr, hd: gc_all[rows(ch), lanes(dr, hd)])
    gtb = stack(lambda ch, dr, hd: gt_all[rows(ch), lanes(dr, hd)])
    beb = stack(lambda ch, dr, hd: be_all[rows(ch), lanes(dr, hd)])

    fwd = ((lax.broadcasted_iota(jnp.int32, (n_inst, c, 128), 0) // GDN_HEADS) % 2) == 0
    ii = lax.broadcasted_iota(jnp.int32, (n_inst, c, 128), 1)
    jj = lax.broadcasted_iota(jnp.int32, (n_inst, c, 128), 2)
    incl = (fwd & (jj <= ii)) | (jnp.logical_not(fwd) & (jj >= ii) & (jj < c))
    fwd_c = ((lax.broadcasted_iota(jnp.int32, (n_inst, c, c), 0) // GDN_HEADS) % 2) == 0
    i64 = lax.broadcasted_iota(jnp.int32, (n_inst, c, c), 1)
    j64 = lax.broadcasted_iota(jnp.int32, (n_inst, c, c), 2)
    strict = (fwd_c & (j64 < i64)) | (jnp.logical_not(fwd_c) & (j64 > i64))
    diag_blk = (i64 // GDN_SUB) == (j64 // GDN_SUB)

    eg = jnp.exp(gcb)
    kb = kk * beb
    diag = jnp.where(ii == jj, gcb, 0.0)
    gcr = _dot_sel(jnp.ones((c, c), BF16), jnp.concatenate([diag[g] for g in range(n_inst)], axis=1))
    gcr = jnp.stack([gcr[:, g * 128:(g + 1) * 128] for g in range(n_inst)], axis=0)
    decay = jnp.where(incl, jnp.exp(jnp.where(incl, gcb - gcr, 0.0)), 0.0)
    kpad = jnp.concatenate([kk.astype(BF16), jnp.zeros((n_inst, c, GDN_DK), BF16)], axis=1)
    qk = _bmm_nt(jnp.concatenate([qq, kb], axis=1).astype(BF16), kpad)
    attn = (qk[:, :c] * decay).astype(BF16)
    a_mat = jnp.where(strict, (qk[:, c:] * decay)[:, :, :c], 0.0)
    tinv = _unit_tri_inverse(a_mat, diag_blk)
    rhs = jnp.concatenate([vv * beb, kb * eg], axis=2)
    sol = _bmm(tinv.astype(BF16), rhs.astype(BF16))
    qg = (qq * eg).astype(BF16)
    kd = (kk * jnp.exp(gtb - gcb)).astype(BF16)
    egt = jnp.exp(gtb[:, 0:8, :])
    for g, (ch, dr, hd) in enumerate(inst):
        rs = rows(ch)
        ls = slice(hd * 128, (hd + 1) * 128)
        qg_ref[dr, rs, ls] = qg[g]
        kd_ref[dr, rs, ls] = kd[g]
        kc_ref[dr, rs, ls] = sol[g, :, GDN_DV:].astype(BF16)
        wv_ref[dr, rs, ls] = sol[g, :, :GDN_DV]
        at_ref[dr, rs, ls] = attn[g]
        eg_ref[dr, ch, :, ls] = egt[g]


def _bcast_cols(x, sel):
    hi, mid, lo = _split3(x)
    return _dot(lo, sel) + _dot(mid, sel) + _dot(hi, sel)


def _gdn_prep(qkv, ab, conv_w, ea_row, dtb_row, *, tb):
    b, t, cw = qkv.shape
    nt = t // tb
    nch = t // GDN_CHUNK
    cpb = tb // GDN_CHUNK
    const = lambda shape: pl.BlockSpec(shape, lambda i, j: (0,) * len(shape))
    hw = GDN_WIDTH
    big = lambda: pl.BlockSpec((2, None, tb, hw), lambda i, j: (0, i, j, 0))
    outs = pl.pallas_call(
        functools.partial(_gdn_prep_kernel, tb=tb, nt=nt),
        grid=(b, nt),
        in_specs=_halo_specs(tb, t, cw, 0, lambda j: j) + [
            pl.BlockSpec((None, tb, AB_PAD), lambda i, j: (i, j, 0)),
            const((4, cw)), const((1, AB_PAD)), const((1, AB_PAD))],
        out_specs=[big(), big(), big(), big(), big(),
                   pl.BlockSpec((2, None, cpb, 8, hw), lambda i, j: (0, i, j, 0, 0))],
        out_shape=[jax.ShapeDtypeStruct((2, b, t, hw), BF16),
                   jax.ShapeDtypeStruct((2, b, t, hw), BF16),
                   jax.ShapeDtypeStruct((2, b, t, hw), BF16),
                   jax.ShapeDtypeStruct((2, b, t, hw), F32),
                   jax.ShapeDtypeStruct((2, b, t, hw), BF16),
                   jax.ShapeDtypeStruct((2, b, nch, 8, hw), F32)],
        compiler_params=_cparams(("parallel", "parallel")),
    )(qkv, qkv, qkv, ab, conv_w, ea_row, dtb_row)
    return outs


def _gdn_scan_kernel(qg_f, kd_f, kc_f, wv_f, at_f, eg_f, qg_b, kd_b, kc_b, wv_b, at_b, eg_b, s0_ref,
                     of_ref, ob_ref, sfin_ref, s_scr, *, cps):
    c = GDN_CHUNK
    j = pl.program_id(1)

    @pl.when(j == 0)
    def _():
        s_scr[...] = s0_ref[...]

    dirs = ((qg_f, kd_f, kc_f, wv_f, at_f, eg_f), (qg_b, kd_b, kc_b, wv_b, at_b, eg_b))
    chains = [(dr, hd) for dr in range(2) for hd in range(GDN_HEADS)]
    ls = lambda hd: slice(hd * 128, (hd + 1) * 128)
    s = s_scr[...].reshape(2 * GDN_HEADS, GDN_DK, GDN_DV)
    for q in range(cps):
        chunk = (q, cps - 1 - q)
        rs = lambda dr: slice(chunk[dr] * c, (chunk[dr] + 1) * c)
        stack = lambda f: jnp.stack([f(dirs[dr], rs(dr), ls(hd), chunk[dr]) for dr, hd in chains], axis=0)
        kq = stack(lambda r, t, l, n: jnp.concatenate([r[2][t, l], r[0][t, l]], axis=0))
        wv = stack(lambda r, t, l, n: r[3][t, l])
        at = stack(lambda r, t, l, n: r[4][t, l][:, :c])
        eg = stack(lambda r, t, l, n: r[5][n, 0:1, l])
        r = _bmm(kq, s.astype(BF16))
        vb = (wv - r[:, :c]).astype(BF16)
        o = r[:, c:] + _bmm(at, vb)
        upd = []
        for g, (dr, hd) in enumerate(chains):
            (of_ref, ob_ref)[dr][rs(dr), ls(hd)] = o[g]
            upd.append(_dot_tn(dirs[dr][1][rs(dr), ls(hd)], vb[g]))
        s = s * eg + jnp.stack(upd, axis=0)
    s_scr[...] = s.reshape(2, GDN_HEADS, GDN_DK, GDN_DV)
    sfin_ref[...] = s_scr[...]


def _gdn_scan(prep, s0, *, cps):
    qg, kd, kc, wv, at, eg = prep
    _, b, t, hw = qg.shape
    c = GDN_CHUNK
    nblk = t // (c * cps)
    fw = lambda: pl.BlockSpec((None, None, cps * c, hw), lambda i, j: (0, i, j, 0))
    bw = lambda: pl.BlockSpec((None, None, cps * c, hw), lambda i, j: (1, i, nblk - 1 - j, 0))
    egf = pl.BlockSpec((None, None, cps, 8, hw), lambda i, j: (0, i, j, 0, 0))
    egb = pl.BlockSpec((None, None, cps, 8, hw), lambda i, j: (1, i, nblk - 1 - j, 0, 0))
    st = pl.BlockSpec((None, 2, GDN_HEADS, GDN_DK, GDN_DV), lambda i, j: (i, 0, 0, 0, 0))
    o_f, o_b, s_fin = pl.pallas_call(
        functools.partial(_gdn_scan_kernel, cps=cps),
        grid=(b, nblk),
        in_specs=[fw(), fw(), fw(), fw(), fw(), egf, bw(), bw(), bw(), bw(), bw(), egb, st],
        out_specs=[pl.BlockSpec((None, cps * c, hw), lambda i, j: (i, j, 0)),
                   pl.BlockSpec((None, cps * c, hw), lambda i, j: (i, nblk - 1 - j, 0)),
                   st],
        out_shape=[jax.ShapeDtypeStruct((b, t, hw), F32), jax.ShapeDtypeStruct((b, t, hw), F32),
                   jax.ShapeDtypeStruct((b, 2, GDN_HEADS, GDN_DK, GDN_DV), F32)],
        scratch_shapes=[pltpu.VMEM((2, GDN_HEADS, GDN_DK, GDN_DV), F32)],
        compiler_params=_cparams(("parallel", "arbitrary")),
    )(qg, kd, kc, wv, at, eg, qg, kd, kc, wv, at, eg, s0)
    return o_f, o_b, s_fin


def _out_rg_kernel(x_ref, y_ref, w_ref, gt_ref, o_ref):
    o_ref[...] = x_ref[...] + gt_ref[...] * _dot(y_ref[...], w_ref[...])


def _out_gdn_kernel(x_ref, of_ref, ob_ref, z_ref, ng_ref, w_ref, gt_ref, o_ref, *, ncol, d):
    o = of_ref[...] + ob_ref[...]
    z = z_ref[...]
    parts = []
    for hd in range(GDN_HEADS):
        ls = slice(hd * GDN_DV, (hd + 1) * GDN_DV)
        parts.append(_rms(o[:, ls], ng_ref[...]) * _silu(z[:, ls]))
    y = jnp.concatenate(parts, axis=1).astype(BF16)
    t2 = gt_ref[...] * _dot(y, w_ref[...])
    rows = t2.shape[0] // ncol
    for j in range(ncol):
        o_ref[:, j, :] = x_ref[:, j, :] + t2[j * rows:(j + 1) * rows]


def _out_project(x3, y_rg, o_f, o_b, z, norm_g, w_out, gt, *, tile):
    b, t, d = x3.shape
    w = RG_WIDTH
    vec = lambda: pl.BlockSpec((None, 1, d), lambda i, j: (i, 0, 0))
    x1 = pl.pallas_call(
        _out_rg_kernel,
        grid=(b, t // tile),
        in_specs=[pl.BlockSpec((None, tile, d), lambda i, j: (i, j, 0)),
                  pl.BlockSpec((None, tile, w), lambda i, j: (i, j, 0)),
                  pl.BlockSpec((w, d), lambda i, j: (0, 0)), vec()],
        out_specs=pl.BlockSpec((None, tile, d), lambda i, j: (i, j, 0)),
        out_shape=jax.ShapeDtypeStruct((b, t, d), F32),
        compiler_params=_cparams(("parallel", "parallel")),
    )(x3, y_rg, w_out[:w], gt)
    rows = t // GRID_W
    ncol = tile // rows
    hw = GDN_WIDTH
    x1v = x1.reshape(b, rows, GRID_W, d)
    cm = lambda wd: pl.BlockSpec((None, tile, wd), lambda i, j: (i, j, 0))
    x1v = pl.pallas_call(
        functools.partial(_out_gdn_kernel, ncol=ncol, d=d),
        grid=(b, GRID_W // ncol),
        in_specs=[pl.BlockSpec((None, rows, ncol, d), lambda i, j: (i, 0, j, 0)),
                  cm(hw), cm(hw), cm(hw),
                  pl.BlockSpec((1, GDN_DV), lambda i, j: (0, 0)),
                  pl.BlockSpec((hw, d), lambda i, j: (0, 0)), vec()],
        out_specs=pl.BlockSpec((None, rows, ncol, d), lambda i, j: (i, 0, j, 0)),
        out_shape=jax.ShapeDtypeStruct((b, rows, GRID_W, d), F32),
        input_output_aliases={0: 0},
        compiler_params=_cparams(("parallel", "parallel")),
    )(x1v, o_f, o_b, z, norm_g, w_out[w:], gt)
    return x1v.reshape(b, t, d)


def _fold_kernel(wq_ref, k_ref, o_ref):
    o_ref[...] = lax.dot_general(wq_ref[...], k_ref[...], (((1,), (1,)), ((), ())),
                                 preferred_element_type=F32, precision=lax.Precision.HIGHEST)


def _fold_keys(wq, keys):
    d, n = wq.shape
    nblk = n // PEER_HALF
    return pl.pallas_call(
        _fold_kernel,
        grid=(nblk,),
        in_specs=[pl.BlockSpec((d, PEER_HALF), lambda j: (0, j)),
                  pl.BlockSpec((None, PEER_NKEYS, PEER_HALF), lambda j: (j % 2, 0, 0))],
        out_specs=pl.BlockSpec((d, PEER_NKEYS), lambda j: (0, j)),
        out_shape=jax.ShapeDtypeStruct((d, nblk * PEER_NKEYS), F32),
        compiler_params=_cparams(("parallel",)),
    )(wq, keys)


def _extract16(vals, ids, s_scr, i_scr, *, fast, n_masked=0):
    n, p = vals.shape
    for r in range(PEER_TOPK):
        m = jnp.max(vals, axis=0, keepdims=True)
        eq = vals == m
        if fast:
            idx = _dot(ids, jnp.where(eq, 1.0, 0.0).astype(BF16))[0:1, :].astype(jnp.int32)
            vals = jnp.where(eq, NEG_INF, vals)
        else:
            idx = jnp.min(jnp.where(eq, ids, 1 << 20), axis=0, keepdims=True)
            vals = jnp.where(ids == idx, NEG_INF, vals)
        s_scr[r:r + 1, :] = m
        i_scr[r:r + 1, :] = idx
    if not fast:
        return None
    dropped = jnp.sum(jnp.where(vals == NEG_INF, 1.0, 0.0), axis=0, keepdims=True)
    return jnp.where(dropped == float(PEER_TOPK + n_masked), 0.0, 1.0)


def _peer_sel_kernel(x_ref, g_ref, sh_ref, sc_ref, wt_ref, h_ref, a_ref, b_ref, gate_ref,
                     sc_scr, s1, i1, s2, i2, ts, tp, a_t, b_t, g_t):
    k = PEER_TOPK
    h = _rms(x_ref[...], g_ref[...]) * (1.0 + sc_ref[...]) + sh_ref[...]
    hb = h.astype(BF16)
    h_ref[...] = hb
    sc_scr[...] = _dot_nt(wt_ref[...], hb)
    p = hb.shape[0]

    n_cand = 80
    n_masked = sum(8 - k // (r + 1) for r in range(1, 8))

    def cand_pos(row):
        return jnp.where(row < 16, row, jnp.where(row < 72, ((row - 16) // 8 + 1) * k + (row - 16) % 8,
                                                  (row - 72 + 8) * k))

    def head(hd, bad, fast):
        if fast:
            key_ids = lax.broadcasted_iota(jnp.int32, (8, PEER_NKEYS), 1).astype(F32).astype(BF16)
            pos_ids = cand_pos(lax.broadcasted_iota(jnp.int32, (8, n_cand), 1)).astype(F32).astype(BF16)
        else:
            key_ids = lax.broadcasted_iota(jnp.int32, (PEER_NKEYS, p), 0)
            pos_ids = cand_pos(lax.broadcasted_iota(jnp.int32, (n_cand, p), 0))
        sub = lax.broadcasted_iota(jnp.int32, (8, p), 0)
        base = pl.multiple_of(hd * 2 * PEER_NKEYS, 2 * PEER_NKEYS)
        bad1 = _extract16(sc_scr[pl.ds(base, PEER_NKEYS), :], key_ids, s1, i1, fast=fast)
        bad2 = _extract16(sc_scr[pl.ds(base + PEER_NKEYS, PEER_NKEYS), :], key_ids, s2, i2, fast=fast)
        s1v, s2v = s1[...], s2[...]
        cands = [s1v[0:1, :] + s2v]
        for r in range(1, 8):
            cands.append(jnp.where(sub < k // (r + 1), s1v[r:r + 1, :] + s2v[0:8, :], NEG_INF))
        cands.append(s1v[8:16, :] + s2v[0:1, :])
        bad3 = _extract16(jnp.concatenate(cands, axis=0), pos_ids, ts, tp, fast=fast, n_masked=n_masked)
        if fast:
            bad = bad + bad1 + bad2 + bad3
        top_s, top_p = ts[...], tp[...]
        rr = top_p >> 4
        cc = top_p & (k - 1)
        i1v, i2v = i1[...], i2[...]
        av = jnp.zeros_like(top_p)
        bv = jnp.zeros_like(top_p)
        for q in range(k):
            av = jnp.where(rr == q, i1v[q:q + 1, :], av)
            bv = jnp.where(cc == q, i2v[q:q + 1, :], bv)
        e = jnp.exp(top_s - top_s[0:1, :])
        gate = e / jnp.sum(e, axis=0, keepdims=True)
        o = pl.multiple_of(hd * k, k)
        a_t[pl.ds(o, k), :] = av
        b_t[pl.ds(o, k), :] = bv
        g_t[pl.ds(o, k), :] = gate
        return bad

    bad = lax.fori_loop(0, PEER_HEADS, functools.partial(head, fast=True), jnp.zeros((1, p), F32))

    @pl.when(jnp.max(bad) > 0.0)
    def _():
        lax.fori_loop(0, PEER_HEADS, functools.partial(head, fast=False), jnp.zeros((1, p), F32))

    a_ref[...] = a_t[...].T
    b_ref[...] = b_t[...].T
    gate_ref[...] = g_t[...].T


def _peer_select(x1, g, sh, sc, wt, *, tile):
    b, t, d = x1.shape
    nt = t // tile
    nk = PEER_HEADS * PEER_TOPK
    nrow = wt.shape[0]
    vec = lambda: pl.BlockSpec((None, 1, d), lambda i, j: (i, 0, 0))
    tr = lambda: pl.BlockSpec((None, tile, nk), lambda i, j: (i, j, 0))
    k = PEER_TOPK
    return pl.pallas_call(
        _peer_sel_kernel,
        grid=(b, nt),
        in_specs=[pl.BlockSpec((None, tile, d), lambda i, j: (i, j, 0)),
                  pl.BlockSpec((1, d), lambda i, j: (0, 0)), vec(), vec(),
                  pl.BlockSpec((nrow, d), lambda i, j: (0, 0))],
        out_specs=[pl.BlockSpec((None, tile, d), lambda i, j: (i, j, 0)), tr(), tr(), tr()],
        out_shape=[jax.ShapeDtypeStruct((b, t, d), BF16),
                   jax.ShapeDtypeStruct((b, t, nk), jnp.int32),
                   jax.ShapeDtypeStruct((b, t, nk), jnp.int32),
                   jax.ShapeDtypeStruct((b, t, nk), F32)],
        scratch_shapes=[pltpu.VMEM((nrow, tile), F32),
                        pltpu.VMEM((k, tile), F32), pltpu.VMEM((k, tile), jnp.int32),
                        pltpu.VMEM((k, tile), F32), pltpu.VMEM((k, tile), jnp.int32),
                        pltpu.VMEM((k, tile), F32), pltpu.VMEM((k, tile), jnp.int32),
                        pltpu.VMEM((nk, tile), jnp.int32), pltpu.VMEM((nk, tile), jnp.int32),
                        pltpu.VMEM((nk, tile), F32)],
        compiler_params=_cparams(("parallel", "parallel")),
    )(x1, g.reshape(1, d), sh, sc, wt)


def _peer_mix_kernel(h_ref, a_ref, b_ref, gate_ref, ut_ref, v_ref, x_ref, gt_ref, fg_ref, o_ref,
                     m_scr, acc, *, tile, n_steps, pairs):
    nk = PEER_NKEYS
    half = nk // 2
    step = pl.program_id(2)
    hi_mask = jnp.uint32(0xFFFF0000)

    @pl.when(step == 0)
    def _():
        sub = lax.broadcasted_iota(jnp.int32, (nk, a_ref.shape[1]), 0)
        a_of_row = jnp.where(sub < half, 2 * sub, 2 * (sub - half) + 1)

        def build(p, carry):
            arow = a_ref[pl.ds(p, 1), :]
            brow = b_ref[pl.ds(p, 1), :]
            grow = gate_ref[pl.ds(p, 1), :]
            xa = jnp.where(a_of_row == arow, 1.0, 0.0).astype(BF16)
            yb = jnp.where(sub == brow, 0.5 * grow, 0.0).astype(BF16)
            m = _dot_nt(xa, yb).astype(BF16).astype(F32)
            bits = lax.bitcast_convert_type(m, jnp.uint32)
            m_scr[pl.ds(pl.multiple_of(p * MASK_PITCH, 8), half), :] = (bits[half:] & hi_mask) | (bits[:half] >> 16)
            return carry

        lax.fori_loop(0, tile, build, 0, unroll=16)

    h = h_ref[...]
    parts = []
    for q in range(pairs):
        pr = step * pairs + q
        act = _gelu_times_2(_dot(h, ut_ref[:, q * 2 * nk:(q + 1) * 2 * nk]))
        w = m_scr[pl.ds(pr, tile, stride=MASK_PITCH), :]
        m_even = lax.bitcast_convert_type(w << 16, F32)
        m_odd = lax.bitcast_convert_type(w & hi_mask, F32)
        parts.append((act * jnp.concatenate([m_even, m_odd], axis=1)).astype(BF16))
    contrib = _dot(jnp.concatenate(parts, axis=1), v_ref[...])

    @pl.when(step == 0)
    def _():
        acc[...] = contrib

    @pl.when(step > 0)
    def _():
        acc[...] += contrib

    @pl.when(step == n_steps - 1)
    def _():
        x2 = x_ref[...] + gt_ref[...] * acc[...]
        o_ref[...] = _rms(x2, fg_ref[...])


def _peer_mix(hb, aidx, bidx, gate, ut, v, x1, gt, final_g, *, tile, pairs):
    b, t, d = x1.shape
    nt = t // tile
    nk = PEER_NKEYS
    npk = aidx.shape[-1]
    n_steps = nk // (2 * pairs)
    eb = 2 * nk * pairs
    tok = lambda wd: pl.BlockSpec((None, tile, wd), lambda i, j, s: (i, j, 0))
    return pl.pallas_call(
        functools.partial(_peer_mix_kernel, tile=tile, n_steps=n_steps, pairs=pairs),
        grid=(b, nt, n_steps),
        in_specs=[tok(d), tok(npk), tok(npk), tok(npk),
                  pl.BlockSpec((d, eb), lambda i, j, s: (0, s)),
                  pl.BlockSpec((eb, d), lambda i, j, s: (s, 0)), tok(d),
                  pl.BlockSpec((None, 1, d), lambda i, j, s: (i, 0, 0)),
                  pl.BlockSpec((1, d), lambda i, j, s: (0, 0))],
        out_specs=tok(d),
        out_shape=jax.ShapeDtypeStruct((b, t, d), F32),
        scratch_shapes=[pltpu.VMEM((tile * MASK_PITCH, nk), jnp.uint32), pltpu.VMEM((tile, d), F32)],
        compiler_params=_cparams(("parallel", "parallel", "arbitrary")),
    )(hb, aidx, bidx, gate, ut, v, x1, gt, final_g.reshape(1, d))


def _block_diag(w):
    n, e, _ = w.shape
    eye = jnp.eye(n, dtype=w.dtype)
    return (eye[:, None, :, None] * w[:, :, None, :]).reshape(n * e, n * e)


def _mix_sequence(x3, ctx_mode, params, states, mods, *, rg_tb, gdn_tb, proj_tile):
    (norm1_g, w_rg, w_gdn, rg_conv_w, rg_conv_b, wg, gate_b, c_lam, gdn_conv_w, ea_row, dtb_row) = params
    sh1, sc1, mod_row = mods
    rg_h0_f, rg_h0_b, gdn_s0 = states
    (p_rg,) = _project(x3, norm1_g, sh1, sc1, mod_row, w_rg, (2 * RG_WIDTH,), colmajor=False, tile=proj_tile)
    qkv, z, ab = _project(x3, norm1_g, sh1, sc1, mod_row, w_gdn, (3 * GDN_WIDTH, GDN_WIDTH, AB_PAD),
                          colmajor=not ctx_mode, tile=proj_tile)
    h_f, st_f = _rglru_pass(p_rg, rg_conv_w, rg_conv_b, wg[0], gate_b[0], c_lam[0], rg_h0_f, None,
                            reverse=False, tb=rg_tb)
    y_rg, st_b = _rglru_pass(p_rg, rg_conv_w, rg_conv_b, wg[1], gate_b[1], c_lam[1], rg_h0_b, h_f,
                             reverse=True, tb=rg_tb)
    prep = _gdn_prep(qkv, ab, gdn_conv_w, ea_row, dtb_row, tb=gdn_tb)
    o_f, o_b, s_fin = _gdn_scan(prep, gdn_s0, cps=min(4, x3.shape[1] // GDN_CHUNK))
    return y_rg, o_f, o_b, z, (st_f, st_b, s_fin)


def kernel(x, c, ctx, c_ctx, w_mod, b_mod, norm1_g, norm2_g, w_in, rg_conv_w, rg_conv_b, rg_gate_w, rg_gate_b,
           rg_lambda, gdn_conv_w, gdn_a_log, gdn_dt_bias, gdn_norm_g, w_out, peer_wq, peer_keys, peer_u, peer_v,
           final_g):
    b, t, d = x.shape
    depth = w_mod.shape[0]
    assert depth == 1, "context residual stream update is only needed for depth > 1"
    l = 0
    w = RG_WIDTH

    cc = jnp.zeros((16, d), F32).at[:b].set(c).at[b].set(c_ctx)
    w_rg = w_in[l][:, :2 * w].astype(BF16)
    n_ab = w_in.shape[2] - 2 * w - 4 * GDN_WIDTH
    w_gdn = jnp.concatenate([w_in[l][:, 2 * w:2 * w + 4 * GDN_WIDTH],
                             jnp.pad(w_in[l][:, 2 * w + 4 * GDN_WIDTH:], ((0, 0), (0, AB_PAD - n_ab)))],
                            axis=1).astype(BF16)
    wg = jnp.stack([jnp.concatenate([_block_diag(rg_gate_w[l, dr, 0]), _block_diag(rg_gate_w[l, dr, 1])], axis=1)
                    for dr in range(2)]).astype(BF16)
    gate_b = rg_gate_b[l].reshape(2, 1, 2 * w)
    c_lam = (-RG_C * jax.nn.softplus(-rg_lambda[l])).reshape(2, 1, w)
    ea = jnp.exp(gdn_a_log[l])
    pad4 = jnp.zeros((2, GDN_HEADS), F32)
    ea_row = jnp.pad(jnp.concatenate([ea, pad4], axis=1).reshape(1, -1), ((0, 0), (0, AB_PAD - n_ab)))
    dtb_row = jnp.pad(jnp.concatenate([gdn_dt_bias[l], pad4], axis=1).reshape(1, -1), ((0, 0), (0, AB_PAD - n_ab)))
    params = (norm1_g[l], w_rg, w_gdn, rg_conv_w[l], rg_conv_b[l].reshape(1, w), wg, gate_b, c_lam,
              gdn_conv_w[l], ea_row, dtb_row)

    mod = _modulation(cc, w_mod[l], b_mod[l])
    sh1, sc1, gt1, sh2, sc2, gt2 = [mod[:, i * d:(i + 1) * d].reshape(16, 1, d) for i in range(6)]

    tc = ctx.shape[1]
    zero_states = (jnp.zeros((b, 1, w), F32), jnp.zeros((b, 1, w), F32),
                   jnp.zeros((b, 2, GDN_HEADS, GDN_DK, GDN_DV), F32))
    ctx_tb = min(tc, 256)
    _, _, _, _, ctx_states = _mix_sequence(ctx, True, params, zero_states, (sh1, sc1, lambda i: b),
                                           rg_tb=ctx_tb, gdn_tb=min(tc, 128), proj_tile=ctx_tb)

    tile = min(t, 512)
    y_rg, o_f, o_b, z, _ = _mix_sequence(x, False, params, ctx_states, (sh1, sc1, lambda i: i),
                                         rg_tb=min(t, 256), gdn_tb=min(t, 256), proj_tile=tile)
    x1 = _out_project(x, y_rg, o_f, o_b, z, gdn_norm_g[l].reshape(1, GDN_DV), w_out[l].astype(BF16), gt1, tile=tile)

    wfold_t = _fold_keys(peer_wq[l], peer_keys[l]).T.astype(BF16)
    sel_tile = min(t, 256)
    hb, aidx, bidx, gate = _peer_select(x1, norm2_g[l], sh2, sc2, wfold_t, tile=sel_tile)
    out = _peer_mix(hb, aidx, bidx, gate, peer_u[l].astype(BF16).T, peer_v[l].astype(BF16),
                    x1, gt2, final_g, tile=tile, pairs=8)
    return out
```

```python
import functools
import math

import jax
import jax.numpy as jnp
from jax import lax
from jax.experimental import pallas as pl
from jax.experimental.pallas import tpu as pltpu

F32 = jnp.float32
BF16 = jnp.bfloat16

GRID_W = 64
EPS = 1e-6
RG_WIDTH = 512
RG_BLOCKS = 8
RG_C = 8.0
GDN_HEADS = 4
GDN_DK = 128
GDN_DV = 128
GDN_WIDTH = GDN_HEADS * GDN_DV
GDN_CHUNK = 64
GDN_SUB = 16
AB_PAD = 128
PEER_HEADS = 8
PEER_NKEYS = 128
PEER_HALF = 128
PEER_TOPK = 16
NEG_INF = float("-inf")
MASK_PITCH = 72

VMEM_LIMIT = 58 * 1024 * 1024


def _cparams(sem):
    return pltpu.CompilerParams(dimension_semantics=sem, vmem_limit_bytes=VMEM_LIMIT)


def _dot(a, b):
    return jnp.dot(a, b, preferred_element_type=F32)


def _dot_nt(a, b):
    return lax.dot_general(a, b, (((1,), (1,)), ((), ())), preferred_element_type=F32)


def _dot_tn(a, b):
    return lax.dot_general(a, b, (((0,), (0,)), ((), ())), preferred_element_type=F32)


def _split3(x):
    hi = x.astype(BF16)
    r = x - hi.astype(F32)
    mid = r.astype(BF16)
    lo = (r - mid.astype(F32)).astype(BF16)
    return hi, mid, lo


def _dot_sel(m, x):
    hi, mid, lo = _split3(x)
    return _dot(m, lo) + _dot(m, mid) + _dot(m, hi)


def _silu(x):
    return x * jax.nn.sigmoid(x)


def _softplus(x):
    return jnp.maximum(x, 0.0) + jnp.log1p(jnp.exp(-jnp.abs(x)))


def _gelu_times_2(x):
    c = math.sqrt(2.0 / math.pi)
    return x * (1.0 + jnp.tanh(x * (c + (c * 0.044715) * (x * x))))


def _rms(x, g):
    return x * lax.rsqrt(jnp.mean(x * x, axis=-1, keepdims=True) + EPS) * g


def _mod_kernel(c_ref, w_ref, b_ref, o_ref):
    s = _silu(c_ref[...])
    o_ref[...] = jnp.dot(s, w_ref[...], preferred_element_type=F32,
                         precision=lax.Precision.HIGHEST) + b_ref[...]


def _modulation(cc, w_mod, b_mod):
    m, d = cc.shape
    n = w_mod.shape[1]
    tn = 1536
    return pl.pallas_call(
        _mod_kernel,
        grid=(n // tn,),
        in_specs=[pl.BlockSpec((m, d), lambda j: (0, 0)),
                  pl.BlockSpec((d, tn), lambda j: (0, j)),
                  pl.BlockSpec((1, tn), lambda j: (0, j))],
        out_specs=pl.BlockSpec((m, tn), lambda j: (0, j)),
        out_shape=jax.ShapeDtypeStruct((m, n), F32),
        compiler_params=_cparams(("arbitrary",)),
    )(cc, w_mod, b_mod.reshape(1, n))


def _proj_kernel(x_ref, g_ref, sh_ref, sc_ref, w_ref, *o_refs, ncol, d, widths):
    if ncol > 1:
        x = jnp.concatenate([x_ref[:, j, :] for j in range(ncol)], axis=0)
    else:
        x = x_ref[...]
    h = _rms(x, g_ref[...]) * (1.0 + sc_ref[...]) + sh_ref[...]
    o = _dot(h.astype(BF16), w_ref[...])
    off = 0
    for o_ref, wd in zip(o_refs, widths):
        o_ref[...] = o[:, off:off + wd]
        off += wd


def _project(x3, g, sh, sc, mod_row, w, widths, *, colmajor, tile):
    b, t, d = x3.shape
    n = w.shape[1]
    if colmajor:
        rows = t // GRID_W
        ncol = tile // rows
        xv = x3.reshape(b, rows, GRID_W, d)
        x_spec = pl.BlockSpec((None, rows, ncol, d), lambda i, j: (i, 0, j, 0))
        nt = GRID_W // ncol
    else:
        ncol = 1
        xv = x3
        x_spec = pl.BlockSpec((None, tile, d), lambda i, j: (i, j, 0))
        nt = t // tile
    vec = lambda: pl.BlockSpec((None, 1, d), lambda i, j: (mod_row(i), 0, 0))
    outs = pl.pallas_call(
        functools.partial(_proj_kernel, ncol=ncol, d=d, widths=tuple(widths)),
        grid=(b, nt),
        in_specs=[x_spec, pl.BlockSpec((1, d), lambda i, j: (0, 0)), vec(), vec(),
                  pl.BlockSpec((d, n), lambda i, j: (0, 0))],
        out_specs=[pl.BlockSpec((None, tile, wd), lambda i, j: (i, j, 0)) for wd in widths],
        out_shape=[jax.ShapeDtypeStruct((b, t, wd), F32) for wd in widths],
        compiler_params=_cparams(("parallel", "parallel")),
    )(xv, g.reshape(1, d), sh, sc, w)
    return outs


def _conv4(cur, prev8, nxt8, w_ref, first, last):
    tb, c = cur.shape
    g = tb // 8
    prev8 = jnp.where(first, 0.0, prev8)
    nxt8 = jnp.where(last, 0.0, nxt8)
    ext = jnp.concatenate([prev8, cur, nxt8], axis=0).reshape(g + 2, 8, c)
    sub = lax.broadcasted_iota(jnp.int32, (g, 8, c), 1)

    def back(k):
        r = pltpu.roll(ext, k, 1)
        return jnp.where(sub >= k, r[1:g + 1], r[0:g])

    r = pltpu.roll(ext, 7, 1)
    xp1 = jnp.where(sub < 7, r[1:g + 1], r[2:g + 2])
    y = (w_ref[0:1, :] * back(2) + w_ref[1:2, :] * back(1) + w_ref[2:3, :] * ext[1:g + 1] + w_ref[3:4, :] * xp1)
    return y.reshape(tb, c)


def _halo_specs(tb, t, width, lane_blk, tmap):
    r = tb // 8
    nb8 = t // 8
    return [pl.BlockSpec((None, tb, width), lambda i, j: (i, tmap(j), lane_blk)),
            pl.BlockSpec((None, 8, width), lambda i, j: (i, jnp.maximum(tmap(j) * r - 1, 0), lane_blk)),
            pl.BlockSpec((None, 8, width), lambda i, j: (i, jnp.minimum((tmap(j) + 1) * r, nb8 - 1), lane_blk))]


def _rglru_kernel(*refs, tb, nt, reverse):
    if reverse:
        (u_ref, up_ref, un_ref, cw_ref, cb_ref, wg_ref, gb_ref, cl_ref, h0_ref, gate_ref, hf_ref,
         y_ref, st_ref, carry) = refs
    else:
        (u_ref, up_ref, un_ref, cw_ref, cb_ref, wg_ref, gb_ref, cl_ref, h0_ref,
         y_ref, st_ref, carry) = refs
    j = pl.program_id(1)
    tblk = (nt - 1 - j) if reverse else j
    w = u_ref.shape[-1]

    @pl.when(j == 0)
    def _():
        carry[...] = h0_ref[...]

    u = u_ref[...]
    xc = _conv4(u, up_ref[...], un_ref[...], cw_ref, tblk == 0, tblk == nt - 1) + cb_ref[...]
    gates = _dot(xc.astype(BF16), wg_ref[...]) + gb_ref[...]
    r = jax.nn.sigmoid(gates[:, :w])
    ig = jax.nn.sigmoid(gates[:, w:])
    log_a = r * cl_ref[...]
    a = jnp.exp(log_a)
    th = jnp.tanh(log_a)
    bb = jnp.sqrt(-2.0 * th / (1.0 - th)) * (ig * xc)

    n_grp = tb // 8
    a = a.reshape(n_grp, 8, w)
    bb = bb.reshape(n_grp, 8, w)
    sub = lax.broadcasted_iota(jnp.int32, (n_grp, 8, w), 1)
    s = 1
    while s < 8:
        shift, ok = (8 - s, sub < 8 - s) if reverse else (s, sub >= s)
        a_sh = jnp.where(ok, pltpu.roll(a, shift, 1), 1.0)
        b_sh = jnp.where(ok, pltpu.roll(bb, shift, 1), 0.0)
        bb = a * b_sh + bb
        a = a * a_sh
        s *= 2
    h_prev = carry[...]
    hs = [None] * n_grp
    for g in (range(n_grp - 1, -1, -1) if reverse else range(n_grp)):
        hg = a[g] * h_prev + bb[g]
        h_prev = hg[0:1, :] if reverse else hg[7:8, :]
        hs[g] = hg
    h = jnp.concatenate(hs, axis=0)
    carry[...] = h_prev
    st_ref[...] = carry[...]
    if reverse:
        y_ref[...] = ((hf_ref[...] + h) * jax.nn.gelu(gate_ref[...])).astype(y_ref.dtype)
    else:
        y_ref[...] = h


def _rglru_pass(p_rg, conv_w, conv_b, wg, gate_b, c_lam, h0, hf, *, reverse, tb):
    b, t, w2 = p_rg.shape
    w = w2 // 2
    nt = t // tb
    tmap = (lambda j: nt - 1 - j) if reverse else (lambda j: j)
    const = lambda shape: pl.BlockSpec(shape, lambda i, j: (0,) * len(shape))
    in_specs = _halo_specs(tb, t, w, 0, tmap) + [
        const((4, w)), const((1, w)), const((w, 2 * w)), const((1, 2 * w)), const((1, w)),
        pl.BlockSpec((None, 1, w), lambda i, j: (i, 0, 0))]
    args = [p_rg, p_rg, p_rg, conv_w, conv_b, wg, gate_b, c_lam, h0]
    if reverse:
        in_specs += [pl.BlockSpec((None, tb, w), lambda i, j: (i, tmap(j), 1)),
                     pl.BlockSpec((None, tb, w), lambda i, j: (i, tmap(j), 0))]
        args += [p_rg, hf]
    y, st = pl.pallas_call(
        functools.partial(_rglru_kernel, tb=tb, nt=nt, reverse=reverse),
        grid=(b, nt),
        in_specs=in_specs,
        out_specs=[pl.BlockSpec((None, tb, w), lambda i, j: (i, tmap(j), 0)),
                   pl.BlockSpec((None, 1, w), lambda i, j: (i, 0, 0))],
        out_shape=[jax.ShapeDtypeStruct((b, t, w), BF16 if reverse else F32),
                   jax.ShapeDtypeStruct((b, 1, w), F32)],
        scratch_shapes=[pltpu.VMEM((1, w), F32)],
        compiler_params=_cparams(("parallel", "arbitrary")),
    )(*args)
    return y, st


def _bmm(a, b):
    return lax.dot_general(a, b, (((2,), (1,)), ((0,), (0,))), preferred_element_type=F32)


def _bmm_nt(a, b):
    return lax.dot_general(a, b, (((2,), (2,)), ((0,), (0,))), preferred_element_type=F32)


def _unit_tri_inverse(a, diag_mask):
    c = a.shape[-1]
    eye = (lax.broadcasted_iota(jnp.int32, (1, c, c), 1) == lax.broadcasted_iota(jnp.int32, (1, c, c), 2)).astype(F32)
    mm = lambda p, q: _bmm(p.astype(BF16), q.astype(BF16))
    ad = jnp.where(diag_mask, a, 0.0)
    x = eye - ad
    pw = ad
    k = 2
    while k < GDN_SUB:
        pw = mm(pw, pw)
        x = x + mm(x, pw)
        k *= 2
    n = mm(x, a - ad)
    nblk = c // GDN_SUB
    y = eye - n
    pw = n
    k = 2
    while k < nblk:
        pw = mm(pw, pw)
        y = y + mm(y, pw)
        k *= 2
    return mm(y, x)


def _gdn_prep_kernel(qkv_ref, qp_ref, qn_ref, ab_ref, cw_ref, ea_ref, dtb_ref,
                     qg_ref, kd_ref, kc_ref, wv_ref, at_ref, eg_ref, *, tb, nt):
    c = GDN_CHUNK
    j = pl.program_id(1)
    hw = GDN_WIDTH
    x = _silu(_conv4(qkv_ref[...], qp_ref[...], qn_ref[...], cw_ref, j == 0, j == nt - 1))

    ab = ab_ref[...]
    col = lax.broadcasted_iota(jnp.int32, (tb, AB_PAD), 1)
    is_a = (col & 4) == 0
    gbv = jnp.where(is_a, -ea_ref[...] * _softplus(ab + dtb_ref[...]), jax.nn.sigmoid(ab))

    ri = lax.broadcasted_iota(jnp.int32, (tb, tb), 0)
    ci = lax.broadcasted_iota(jnp.int32, (tb, tb), 1)
    same = (ri // c) == (ci // c)
    l_f = (same & (ci <= ri)).astype(BF16)
    l_b = (same & (ci >= ri)).astype(BF16)
    l_t = same.astype(BF16)
    gcum = jnp.where(col < 8, _dot_sel(l_f, gbv), _dot_sel(l_b, gbv))
    gtot = _dot_sel(l_t, gbv)

    cols_a = [dr * 8 + hd for dr in range(2) for hd in range(GDN_HEADS)]
    bcast = lambda v, cols: jnp.concatenate([jnp.broadcast_to(v[:, cc:cc + 1], (tb, 128)) for cc in cols], axis=1)
    gc_all = bcast(gcum, cols_a)
    gt_all = bcast(gtot, cols_a)
    be_all = bcast(gbv, [cc + 4 for cc in cols_a])

    qs, ks, vs = [], [], []
    for hd in range(GDN_HEADS):
        q = x[:, hd * GDN_DK:(hd + 1) * GDN_DK]
        k = x[:, hw + hd * GDN_DK: hw + (hd + 1) * GDN_DK]
        qs.append(q * lax.rsqrt(jnp.sum(q * q, axis=-1, keepdims=True) + EPS) * (GDN_DK ** -0.5))
        ks.append(k * lax.rsqrt(jnp.sum(k * k, axis=-1, keepdims=True) + EPS))
        vs.append(x[:, 2 * hw + hd * GDN_DV: 2 * hw + (hd + 1) * GDN_DV])
    inst = [(ch, dr, hd) for ch in range(tb // c) for dr in range(2) for hd in range(GDN_HEADS)]
    n_inst = len(inst)
    rows = lambda ch: slice(ch * c, (ch + 1) * c)
    lanes = lambda dr, hd: slice((dr * GDN_HEADS + hd) * 128, (dr * GDN_HEADS + hd + 1) * 128)
    stack = lambda f: jnp.stack([f(ch, dr, hd) for ch, dr, hd in inst], axis=0)
    qq = stack(lambda ch, dr, hd: qs[hd][rows(ch)])
    kk = stack(lambda ch, dr, hd: ks[hd][rows(ch)])
    vv = stack(lambda ch, dr, hd: vs[hd][rows(ch)])
    gcb = stack(lambda ch, dr, hd: gc_all[rows(ch), lanes(dr, hd)])
    gtb = stack(lambda ch, dr, hd: gt_all[rows(ch), lanes(dr, hd)])
    beb = stack(lambda ch, dr, hd: be_all[rows(ch), lanes(dr, hd)])

    fwd = ((lax.broadcasted_iota(jnp.int32, (n_inst, c, 128), 0) // GDN_HEADS) % 2) == 0
    ii = lax.broadcasted_iota(jnp.int32, (n_inst, c, 128), 1)
    jj = lax.broadcasted_iota(jnp.int32, (n_inst, c, 128), 2)
    incl = (fwd & (jj <= ii)) | (jnp.logical_not(fwd) & (jj >= ii) & (jj < c))
    fwd_c = ((lax.broadcasted_iota(jnp.int32, (n_inst, c, c), 0) // GDN_HEADS) % 2) == 0
    i64 = lax.broadcasted_iota(jnp.int32, (n_inst, c, c), 1)
    j64 = lax.broadcasted_iota(jnp.int32, (n_inst, c, c), 2)
    strict = (fwd_c & (j64 < i64)) | (jnp.logical_not(fwd_c) & (j64 > i64))
    diag_blk = (i64 // GDN_SUB) == (j64 // GDN_SUB)

    eg = jnp.exp(gcb)
    kb = kk * beb
    diag = jnp.where(ii == jj, gcb, 0.0)
    gcr = _dot_sel(jnp.ones((c, c), BF16), jnp.concatenate([diag[g] for g in range(n_inst)], axis=1))
    gcr = jnp.stack([gcr[:, g * 128:(g + 1) * 128] for g in range(n_inst)], axis=0)
    decay = jnp.where(incl, jnp.exp(jnp.where(incl, gcb - gcr, 0.0)), 0.0)
    kpad = jnp.concatenate([kk.astype(BF16), jnp.zeros((n_inst, c, GDN_DK), BF16)], axis=1)
    qk = _bmm_nt(jnp.concatenate([qq, kb], axis=1).astype(BF16), kpad)
    attn = (qk[:, :c] * decay).astype(BF16)
    a_mat = jnp.where(strict, (qk[:, c:] * decay)[:, :, :c], 0.0)
    tinv = _unit_tri_inverse(a_mat, diag_blk)
    rhs = jnp.concatenate([vv * beb, kb * eg], axis=2)
    sol = _bmm(tinv.astype(BF16), rhs.astype(BF16))
    qg = (qq * eg).astype(BF16)
    kd = (kk * jnp.exp(gtb - gcb)).astype(BF16)
    egt = jnp.exp(gtb[:, 0:8, :])
    for g, (ch, dr, hd) in enumerate(inst):
        rs = rows(ch)
        ls = slice(hd * 128, (hd + 1) * 128)
        qg_ref[dr, rs, ls] = qg[g]
        kd_ref[dr, rs, ls] = kd[g]
        kc_ref[dr, rs, ls] = sol[g, :, GDN_DV:].astype(BF16)
        wv_ref[dr, rs, ls] = sol[g, :, :GDN_DV]
        at_ref[dr, rs, ls] = attn[g]
        eg_ref[dr, ch, :, ls] = egt[g]


def _bcast_cols(x, sel):
    hi, mid, lo = _split3(x)
    return _dot(lo, sel) + _dot(mid, sel) + _dot(hi, sel)


def _gdn_prep(qkv, ab, conv_w, ea_row, dtb_row, *, tb):
    b, t, cw = qkv.shape
    nt = t // tb
    nch = t // GDN_CHUNK
    cpb = tb // GDN_CHUNK
    const = lambda shape: pl.BlockSpec(shape, lambda i, j: (0,) * len(shape))
    hw = GDN_WIDTH
    big = lambda: pl.BlockSpec((2, None, tb, hw), lambda i, j: (0, i, j, 0))
    outs = pl.pallas_call(
        functools.partial(_gdn_prep_kernel, tb=tb, nt=nt),
        grid=(b, nt),
        in_specs=_halo_specs(tb, t, cw, 0, lambda j: j) + [
            pl.BlockSpec((None, tb, AB_PAD), lambda i, j: (i, j, 0)),
            const((4, cw)), const((1, AB_PAD)), const((1, AB_PAD))],
        out_specs=[big(), big(), big(), big(), big(),
                   pl.BlockSpec((2, None, cpb, 8, hw), lambda i, j: (0, i, j, 0, 0))],
        out_shape=[jax.ShapeDtypeStruct((2, b, t, hw), BF16),
                   jax.ShapeDtypeStruct((2, b, t, hw), BF16),
                   jax.ShapeDtypeStruct((2, b, t, hw), BF16),
                   jax.ShapeDtypeStruct((2, b, t, hw), F32),
                   jax.ShapeDtypeStruct((2, b, t, hw), BF16),
                   jax.ShapeDtypeStruct((2, b, nch, 8, hw), F32)],
        compiler_params=_cparams(("parallel", "parallel")),
    )(qkv, qkv, qkv, ab, conv_w, ea_row, dtb_row)
    return outs


def _gdn_scan_kernel(qg_f, kd_f, kc_f, wv_f, at_f, eg_f, qg_b, kd_b, kc_b, wv_b, at_b, eg_b, s0_ref,
                     of_ref, ob_ref, sfin_ref, s_scr, *, cps):
    c = GDN_CHUNK
    j = pl.program_id(1)

    @pl.when(j == 0)
    def _():
        s_scr[...] = s0_ref[...]

    dirs = ((qg_f, kd_f, kc_f, wv_f, at_f, eg_f), (qg_b, kd_b, kc_b, wv_b, at_b, eg_b))
    chains = [(dr, hd) for dr in range(2) for hd in range(GDN_HEADS)]
    ls = lambda hd: slice(hd * 128, (hd + 1) * 128)
    s = s_scr[...].reshape(2 * GDN_HEADS, GDN_DK, GDN_DV)
    for q in range(cps):
        chunk = (q, cps - 1 - q)
        rs = lambda dr: slice(chunk[dr] * c, (chunk[dr] + 1) * c)
        stack = lambda f: jnp.stack([f(dirs[dr], rs(dr), ls(hd), chunk[dr]) for dr, hd in chains], axis=0)
        kq = stack(lambda r, t, l, n: jnp.concatenate([r[2][t, l], r[0][t, l]], axis=0))
        wv = stack(lambda r, t, l, n: r[3][t, l])
        at = stack(lambda r, t, l, n: r[4][t, l][:, :c])
        eg = stack(lambda r, t, l, n: r[5][n, 0:1, l])
        r = _bmm(kq, s.astype(BF16))
        vb = (wv - r[:, :c]).astype(BF16)
        o = r[:, c:] + _bmm(at, vb)
        upd = []
        for g, (dr, hd) in enumerate(chains):
            (of_ref, ob_ref)[dr][rs(dr), ls(hd)] = o[g]
            upd.append(_dot_tn(dirs[dr][1][rs(dr), ls(hd)], vb[g]))
        s = s * eg + jnp.stack(upd, axis=0)
    s_scr[...] = s.reshape(2, GDN_HEADS, GDN_DK, GDN_DV)
    sfin_ref[...] = s_scr[...]


def _gdn_scan(prep, s0, *, cps):
    qg, kd, kc, wv, at, eg = prep
    _, b, t, hw = qg.shape
    c = GDN_CHUNK
    nblk = t // (c * cps)
    fw = lambda: pl.BlockSpec((None, None, cps * c, hw), lambda i, j: (0, i, j, 0))
    bw = lambda: pl.BlockSpec((None, None, cps * c, hw), lambda i, j: (1, i, nblk - 1 - j, 0))
    egf = pl.BlockSpec((None, None, cps, 8, hw), lambda i, j: (0, i, j, 0, 0))
    egb = pl.BlockSpec((None, None, cps, 8, hw), lambda i, j: (1, i, nblk - 1 - j, 0, 0))
    st = pl.BlockSpec((None, 2, GDN_HEADS, GDN_DK, GDN_DV), lambda i, j: (i, 0, 0, 0, 0))
    o_f, o_b, s_fin = pl.pallas_call(
        functools.partial(_gdn_scan_kernel, cps=cps),
        grid=(b, nblk),
        in_specs=[fw(), fw(), fw(), fw(), fw(), egf, bw(), bw(), bw(), bw(), bw(), egb, st],
        out_specs=[pl.BlockSpec((None, cps * c, hw), lambda i, j: (i, j, 0)),
                   pl.BlockSpec((None, cps * c, hw), lambda i, j: (i, nblk - 1 - j, 0)),
                   st],
        out_shape=[jax.ShapeDtypeStruct((b, t, hw), F32), jax.ShapeDtypeStruct((b, t, hw), F32),
                   jax.ShapeDtypeStruct((b, 2, GDN_HEADS, GDN_DK, GDN_DV), F32)],
        scratch_shapes=[pltpu.VMEM((2, GDN_HEADS, GDN_DK, GDN_DV), F32)],
        compiler_params=_cparams(("parallel", "arbitrary")),
    )(qg, kd, kc, wv, at, eg, qg, kd, kc, wv, at, eg, s0)
    return o_f, o_b, s_fin


def _out_rg_kernel(x_ref, y_ref, w_ref, gt_ref, o_ref):
    o_ref[...] = x_ref[...] + gt_ref[...] * _dot(y_ref[...], w_ref[...])


def _out_gdn_kernel(x_ref, of_ref, ob_ref, z_ref, ng_ref, w_ref, gt_ref, o_ref, *, ncol, d):
    o = of_ref[...] + ob_ref[...]
    z = z_ref[...]
    parts = []
    for hd in range(GDN_HEADS):
        ls = slice(hd * GDN_DV, (hd + 1) * GDN_DV)
        parts.append(_rms(o[:, ls], ng_ref[...]) * _silu(z[:, ls]))
    y = jnp.concatenate(parts, axis=1).astype(BF16)
    t2 = gt_ref[...] * _dot(y, w_ref[...])
    rows = t2.shape[0] // ncol
    for j in range(ncol):
        o_ref[:, j, :] = x_ref[:, j, :] + t2[j * rows:(j + 1) * rows]


def _out_project(x3, y_rg, o_f, o_b, z, norm_g, w_out, gt, *, tile):
    b, t, d = x3.shape
    w = RG_WIDTH
    vec = lambda: pl.BlockSpec((None, 1, d), lambda i, j: (i, 0, 0))
    x1 = pl.pallas_call(
        _out_rg_kernel,
        grid=(b, t // tile),
        in_specs=[pl.BlockSpec((None, tile, d), lambda i, j: (i, j, 0)),
                  pl.BlockSpec((None, tile, w), lambda i, j: (i, j, 0)),
                  pl.BlockSpec((w, d), lambda i, j: (0, 0)), vec()],
        out_specs=pl.BlockSpec((None, tile, d), lambda i, j: (i, j, 0)),
        out_shape=jax.ShapeDtypeStruct((b, t, d), F32),
        compiler_params=_cparams(("parallel", "parallel")),
    )(x3, y_rg, w_out[:w], gt)
    rows = t // GRID_W
    ncol = tile // rows
    hw = GDN_WIDTH
    x1v = x1.reshape(b, rows, GRID_W, d)
    cm = lambda wd: pl.BlockSpec((None, tile, wd), lambda i, j: (i, j, 0))
    x1v = pl.pallas_call(
        functools.partial(_out_gdn_kernel, ncol=ncol, d=d),
        grid=(b, GRID_W // ncol),
        in_specs=[pl.BlockSpec((None, rows, ncol, d), lambda i, j: (i, 0, j, 0)),
                  cm(hw), cm(hw), cm(hw),
                  pl.BlockSpec((1, GDN_DV), lambda i, j: (0, 0)),
                  pl.BlockSpec((hw, d), lambda i, j: (0, 0)), vec()],
        out_specs=pl.BlockSpec((None, rows, ncol, d), lambda i, j: (i, 0, j, 0)),
        out_shape=jax.ShapeDtypeStruct((b, rows, GRID_W, d), F32),
        input_output_aliases={0: 0},
        compiler_params=_cparams(("parallel", "parallel")),
    )(x1v, o_f, o_b, z, norm_g, w_out[w:], gt)
    return x1v.reshape(b, t, d)


def _fold_kernel(wq_ref, k_ref, o_ref):
    o_ref[...] = lax.dot_general(wq_ref[...], k_ref[...], (((1,), (1,)), ((), ())),
                                 preferred_element_type=F32, precision=lax.Precision.HIGHEST)


def _fold_keys(wq, keys):
    d, n = wq.shape
    nblk = n // PEER_HALF
    return pl.pallas_call(
        _fold_kernel,
        grid=(nblk,),
        in_specs=[pl.BlockSpec((d, PEER_HALF), lambda j: (0, j)),
                  pl.BlockSpec((None, PEER_NKEYS, PEER_HALF), lambda j: (j % 2, 0, 0))],
        out_specs=pl.BlockSpec((d, PEER_NKEYS), lambda j: (0, j)),
        out_shape=jax.ShapeDtypeStruct((d, nblk * PEER_NKEYS), F32),
        compiler_params=_cparams(("parallel",)),
    )(wq, keys)


def _extract16(vals, ids, s_scr, i_scr, *, fast, n_masked=0):
    n, p = vals.shape
    for r in range(PEER_TOPK):
        m = jnp.max(vals, axis=0, keepdims=True)
        eq = vals == m
        if fast:
            idx = _dot(ids, jnp.where(eq, 1.0, 0.0).astype(BF16))[0:1, :].astype(jnp.int32)
            vals = jnp.where(eq, NEG_INF, vals)
        else:
            idx = jnp.min(jnp.where(eq, ids, 1 << 20), axis=0, keepdims=True)
            vals = jnp.where(ids == idx, NEG_INF, vals)
        s_scr[r:r + 1, :] = m
        i_scr[r:r + 1, :] = idx
    if not fast:
        return None
    dropped = jnp.sum(jnp.where(vals == NEG_INF, 1.0, 0.0), axis=0, keepdims=True)
    return jnp.where(dropped == float(PEER_TOPK + n_masked), 0.0, 1.0)


def _peer_sel_kernel(x_ref, g_ref, sh_ref, sc_ref, wt_ref, h_ref, a_ref, b_ref, gate_ref,
                     sc_scr, s1, i1, s2, i2, ts, tp, a_t, b_t, g_t):
    k = PEER_TOPK
    h = _rms(x_ref[...], g_ref[...]) * (1.0 + sc_ref[...]) + sh_ref[...]
    hb = h.astype(BF16)
    h_ref[...] = hb
    sc_scr[...] = _dot_nt(wt_ref[...], hb)
    p = hb.shape[0]

    n_cand = 80
    n_masked = sum(8 - k // (r + 1) for r in range(1, 8))

    def cand_pos(row):
        return jnp.where(row < 16, row, jnp.where(row < 72, ((row - 16) // 8 + 1) * k + (row - 16) % 8,
                                                  (row - 72 + 8) * k))

    def head(hd, fast):
        if fast:
            key_ids = lax.broadcasted_iota(jnp.int32, (8, PEER_NKEYS), 1).astype(F32).astype(BF16)
            pos_ids = cand_pos(lax.broadcasted_iota(jnp.int32, (8, n_cand), 1)).astype(F32).astype(BF16)
        else:
            key_ids = lax.broadcasted_iota(jnp.int32, (PEER_NKEYS, p), 0)
            pos_ids = cand_pos(lax.broadcasted_iota(jnp.int32, (n_cand, p), 0))
        sub = lax.broadcasted_iota(jnp.int32, (8, p), 0)
        base = pl.multiple_of(hd * 2 * PEER_NKEYS, 2 * PEER_NKEYS)
        bad1 = _extract16(sc_scr[pl.ds(base, PEER_NKEYS), :], key_ids, s1, i1, fast=fast)
        bad2 = _extract16(sc_scr[pl.ds(base + PEER_NKEYS, PEER_NKEYS), :], key_ids, s2, i2, fast=fast)
        s1v, s2v = s1[...], s2[...]
        cands = [s1v[0:1, :] + s2v]
        for r in range(1, 8):
            cands.append(jnp.where(sub < k // (r + 1), s1v[r:r + 1, :] + s2v[0:8, :], NEG_INF))
        cands.append(s1v[8:16, :] + s2v[0:1, :])
        bad3 = _extract16(jnp.concatenate(cands, axis=0), pos_ids, ts, tp, fast=fast, n_masked=n_masked)
        top_s, top_p = ts[...], tp[...]
        rr = top_p >> 4
        cc = top_p & (k - 1)
        i1v, i2v = i1[...], i2[...]
        av = jnp.zeros_like(top_p)
        bv = jnp.zeros_like(top_p)
        for q in range(k):
            av = jnp.where(rr == q, i1v[q:q + 1, :], av)
            bv = jnp.where(cc == q, i2v[q:q + 1, :], bv)
        e = jnp.exp(top_s - top_s[0:1, :])
        gate = e / jnp.sum(e, axis=0, keepdims=True)
        o = pl.multiple_of(hd * k, k)
        a_t[pl.ds(o, k), :] = av
        b_t[pl.ds(o, k), :] = bv
        g_t[pl.ds(o, k), :] = gate
        return (bad1 + bad2 + bad3) if fast else None

    def head_step(hd, carry):
        bad = head(hd, True)

        @pl.when(jnp.max(bad) > 0.0)
        def _():
            head(hd, False)

        return carry

    lax.fori_loop(0, PEER_HEADS, head_step, 0)

    a_ref[...] = a_t[...].T
    b_ref[...] = b_t[...].T
    gate_ref[...] = g_t[...].T


def _peer_select(x1, g, sh, sc, wt, *, tile):
    b, t, d = x1.shape
    nt = t // tile
    nk = PEER_HEADS * PEER_TOPK
    nrow = wt.shape[0]
    vec = lambda: pl.BlockSpec((None, 1, d), lambda i, j: (i, 0, 0))
    tr = lambda: pl.BlockSpec((None, tile, nk), lambda i, j: (i, j, 0))
    k = PEER_TOPK
    return pl.pallas_call(
        _peer_sel_kernel,
        grid=(b, nt),
        in_specs=[pl.BlockSpec((None, tile, d), lambda i, j: (i, j, 0)),
                  pl.BlockSpec((1, d), lambda i, j: (0, 0)), vec(), vec(),
                  pl.BlockSpec((nrow, d), lambda i, j: (0, 0))],
        out_specs=[pl.BlockSpec((None, tile, d), lambda i, j: (i, j, 0)), tr(), tr(), tr()],
        out_shape=[jax.ShapeDtypeStruct((b, t, d), BF16),
                   jax.ShapeDtypeStruct((b, t, nk), jnp.int32),
                   jax.ShapeDtypeStruct((b, t, nk), jnp.int32),
                   jax.ShapeDtypeStruct((b, t, nk), F32)],
        scratch_shapes=[pltpu.VMEM((nrow, tile), F32),
                        pltpu.VMEM((k, tile), F32), pltpu.VMEM((k, tile), jnp.int32),
                        pltpu.VMEM((k, tile), F32), pltpu.VMEM((k, tile), jnp.int32),
                        pltpu.VMEM((k, tile), F32), pltpu.VMEM((k, tile), jnp.int32),
                        pltpu.VMEM((nk, tile), jnp.int32), pltpu.VMEM((nk, tile), jnp.int32),
                        pltpu.VMEM((nk, tile), F32)],
        compiler_params=_cparams(("parallel", "parallel")),
    )(x1, g.reshape(1, d), sh, sc, wt)


def _peer_mix_kernel(h_ref, a_ref, b_ref, gate_ref, ut_ref, v_ref, x_ref, gt_ref, fg_ref, o_ref,
                     m_scr, acc, *, tile, n_steps, pairs):
    nk = PEER_NKEYS
    half = nk // 2
    step = pl.program_id(2)
    hi_mask = jnp.uint32(0xFFFF0000)

    @pl.when(step == 0)
    def _():
        sub = lax.broadcasted_iota(jnp.int32, (nk, a_ref.shape[1]), 0)
        a_of_row = jnp.where(sub < half, 2 * sub, 2 * (sub - half) + 1)

        def build(p, carry):
            arow = a_ref[pl.ds(p, 1), :]
            brow = b_ref[pl.ds(p, 1), :]
            grow = gate_ref[pl.ds(p, 1), :]
            xa = jnp.where(a_of_row == arow, 1.0, 0.0).astype(BF16)
            yb = jnp.where(sub == brow, 0.5 * grow, 0.0).astype(BF16)
            m = _dot_nt(xa, yb).astype(BF16).astype(F32)
            bits = lax.bitcast_convert_type(m, jnp.uint32)
            m_scr[pl.ds(pl.multiple_of(p * MASK_PITCH, 8), half), :] = (bits[half:] & hi_mask) | (bits[:half] >> 16)
            return carry

        lax.fori_loop(0, tile, build, 0, unroll=32)

    h = h_ref[...]
    parts = []
    for q in range(pairs):
        pr = step * pairs + q
        act = _gelu_times_2(_dot(h, ut_ref[:, q * 2 * nk:(q + 1) * 2 * nk]))
        w = m_scr[pl.ds(pr, tile, stride=MASK_PITCH), :]
        m_even = lax.bitcast_convert_type(w << 16, F32)
        m_odd = lax.bitcast_convert_type(w & hi_mask, F32)
        parts.append((act * jnp.concatenate([m_even, m_odd], axis=1)).astype(BF16))
    contrib = _dot(jnp.concatenate(parts, axis=1), v_ref[...])

    @pl.when(step == 0)
    def _():
        acc[...] = contrib

    @pl.when(step > 0)
    def _():
        acc[...] += contrib

    @pl.when(step == n_steps - 1)
    def _():
        x2 = x_ref[...] + gt_ref[...] * acc[...]
        o_ref[...] = _rms(x2, fg_ref[...])


def _peer_mix(hb, aidx, bidx, gate, ut, v, x1, gt, final_g, *, tile, pairs):
    b, t, d = x1.shape
    nt = t // tile
    nk = PEER_NKEYS
    npk = aidx.shape[-1]
    n_steps = nk // (2 * pairs)
    eb = 2 * nk * pairs
    tok = lambda wd: pl.BlockSpec((None, tile, wd), lambda i, j, s: (i, j, 0))
    return pl.pallas_call(
        functools.partial(_peer_mix_kernel, tile=tile, n_steps=n_steps, pairs=pairs),
        grid=(b, nt, n_steps),
        in_specs=[tok(d), tok(npk), tok(npk), tok(npk),
                  pl.BlockSpec((d, eb), lambda i, j, s: (0, s)),
                  pl.BlockSpec((eb, d), lambda i, j, s: (s, 0)), tok(d),
                  pl.BlockSpec((None, 1, d), lambda i, j, s: (i, 0, 0)),
                  pl.BlockSpec((1, d), lambda i, j, s: (0, 0))],
        out_specs=tok(d),
        out_shape=jax.ShapeDtypeStruct((b, t, d), F32),
        scratch_shapes=[pltpu.VMEM((tile * MASK_PITCH, nk), jnp.uint32), pltpu.VMEM((tile, d), F32)],
        compiler_params=_cparams(("parallel", "parallel", "arbitrary")),
    )(hb, aidx, bidx, gate, ut, v, x1, gt, final_g.reshape(1, d))


def _block_diag(w):
    n, e, _ = w.shape
    eye = jnp.eye(n, dtype=w.dtype)
    return (eye[:, None, :, None] * w[:, :, None, :]).reshape(n * e, n * e)


def _mix_sequence(x3, ctx_mode, params, states, mods, *, rg_tb, gdn_tb, proj_tile):
    (norm1_g, w_rg, w_gdn, rg_conv_w, rg_conv_b, wg, gate_b, c_lam, gdn_conv_w, ea_row, dtb_row) = params
    sh1, sc1, mod_row = mods
    rg_h0_f, rg_h0_b, gdn_s0 = states
    (p_rg,) = _project(x3, norm1_g, sh1, sc1, mod_row, w_rg, (2 * RG_WIDTH,), colmajor=False, tile=proj_tile)
    qkv, z, ab = _project(x3, norm1_g, sh1, sc1, mod_row, w_gdn, (3 * GDN_WIDTH, GDN_WIDTH, AB_PAD),
                          colmajor=not ctx_mode, tile=proj_tile)
    h_f, st_f = _rglru_pass(p_rg, rg_conv_w, rg_conv_b, wg[0], gate_b[0], c_lam[0], rg_h0_f, None,
                            reverse=False, tb=rg_tb)
    y_rg, st_b = _rglru_pass(p_rg, rg_conv_w, rg_conv_b, wg[1], gate_b[1], c_lam[1], rg_h0_b, h_f,
                             reverse=True, tb=rg_tb)
    prep = _gdn_prep(qkv, ab, gdn_conv_w, ea_row, dtb_row, tb=gdn_tb)
    o_f, o_b, s_fin = _gdn_scan(prep, gdn_s0, cps=min(8, x3.shape[1] // GDN_CHUNK))
    return y_rg, o_f, o_b, z, (st_f, st_b, s_fin)


def kernel(x, c, ctx, c_ctx, w_mod, b_mod, norm1_g, norm2_g, w_in, rg_conv_w, rg_conv_b, rg_gate_w, rg_gate_b,
           rg_lambda, gdn_conv_w, gdn_a_log, gdn_dt_bias, gdn_norm_g, w_out, peer_wq, peer_keys, peer_u, peer_v,
           final_g):
    b, t, d = x.shape
    depth = w_mod.shape[0]
    assert depth == 1, "context residual stream update is only needed for depth > 1"
    l = 0
    w = RG_WIDTH

    cc = jnp.zeros((16, d), F32).at[:b].set(c).at[b].set(c_ctx)
    w_rg = w_in[l][:, :2 * w].astype(BF16)
    n_ab = w_in.shape[2] - 2 * w - 4 * GDN_WIDTH
    w_gdn = jnp.concatenate([w_in[l][:, 2 * w:2 * w + 4 * GDN_WIDTH],
                             jnp.pad(w_in[l][:, 2 * w + 4 * GDN_WIDTH:], ((0, 0), (0, AB_PAD - n_ab)))],
                            axis=1).astype(BF16)
    wg = jnp.stack([jnp.concatenate([_block_diag(rg_gate_w[l, dr, 0]), _block_diag(rg_gate_w[l, dr, 1])], axis=1)
                    for dr in range(2)]).astype(BF16)
    gate_b = rg_gate_b[l].reshape(2, 1, 2 * w)
    c_lam = (-RG_C * jax.nn.softplus(-rg_lambda[l])).reshape(2, 1, w)
    ea = jnp.exp(gdn_a_log[l])
    pad4 = jnp.zeros((2, GDN_HEADS), F32)
    ea_row = jnp.pad(jnp.concatenate([ea, pad4], axis=1).reshape(1, -1), ((0, 0), (0, AB_PAD - n_ab)))
    dtb_row = jnp.pad(jnp.concatenate([gdn_dt_bias[l], pad4], axis=1).reshape(1, -1), ((0, 0), (0, AB_PAD - n_ab)))
    params = (norm1_g[l], w_rg, w_gdn, rg_conv_w[l], rg_conv_b[l].reshape(1, w), wg, gate_b, c_lam,
              gdn_conv_w[l], ea_row, dtb_row)

    mod = _modulation(cc, w_mod[l], b_mod[l])
    sh1, sc1, gt1, sh2, sc2, gt2 = [mod[:, i * d:(i + 1) * d].reshape(16, 1, d) for i in range(6)]

    tc = ctx.shape[1]
    zero_states = (jnp.zeros((b, 1, w), F32), jnp.zeros((b, 1, w), F32),
                   jnp.zeros((b, 2, GDN_HEADS, GDN_DK, GDN_DV), F32))
    ctx_tb = min(tc, 256)
    _, _, _, _, ctx_states = _mix_sequence(ctx, True, params, zero_states, (sh1, sc1, lambda i: b),
                                           rg_tb=ctx_tb, gdn_tb=min(tc, 128), proj_tile=ctx_tb)

    tile = min(t, 512)
    y_rg, o_f, o_b, z, _ = _mix_sequence(x, False, params, ctx_states, (sh1, sc1, lambda i: i),
                                         rg_tb=min(t, 256), gdn_tb=min(t, 256), proj_tile=tile)
    x1 = _out_project(x, y_rg, o_f, o_b, z, gdn_norm_g[l].reshape(1, GDN_DV), w_out[l].astype(BF16), gt1, tile=tile)

    wfold_t = _fold_keys(peer_wq[l], peer_keys[l]).T.astype(BF16)
    sel_tile = min(t, 256)
    hb, aidx, bidx, gate = _peer_select(x1, norm2_g[l], sh2, sc2, wfold_t, tile=sel_tile)
    out = _peer_mix(hb, aidx, bidx, gate, peer_u[l].astype(BF16).T, peer_v[l].astype(BF16),
                    x1, gt2, final_g, tile=tile, pairs=8)
    return out
```

```python
import functools
import math

import jax
import jax.numpy as jnp
from jax import lax
from jax.experimental import pallas as pl
from jax.experimental.pallas import tpu as pltpu

F32 = jnp.float32
BF16 = jnp.bfloat16

GRID_W = 64
EPS = 1e-6
RG_WIDTH = 512
RG_BLOCKS = 8
RG_C = 8.0
GDN_HEADS = 4
GDN_DK = 128
GDN_DV = 128
GDN_WIDTH = GDN_HEADS * GDN_DV
GDN_CHUNK = 64
GDN_SUB = 16
AB_PAD = 128
PEER_HEADS = 8
PEER_NKEYS = 128
PEER_HALF = 128
PEER_TOPK = 16
NEG_INF = float("-inf")
MASK_PITCH = 72

VMEM_LIMIT = 58 * 1024 * 1024


def _cparams(sem):
    return pltpu.CompilerParams(dimension_semantics=sem, vmem_limit_bytes=VMEM_LIMIT)


def _dot(a, b):
    return jnp.dot(a, b, preferred_element_type=F32)


def _dot_nt(a, b):
    return lax.dot_general(a, b, (((1,), (1,)), ((), ())), preferred_element_type=F32)


def _dot_tn(a, b):
    return lax.dot_general(a, b, (((0,), (0,)), ((), ())), preferred_element_type=F32)


def _split3(x):
    hi = x.astype(BF16)
    r = x - hi.astype(F32)
    mid = r.astype(BF16)
    lo = (r - mid.astype(F32)).astype(BF16)
    return hi, mid, lo


def _dot_sel(m, x):
    hi, mid, lo = _split3(x)
    return _dot(m, lo) + _dot(m, mid) + _dot(m, hi)


def _silu(x):
    return x * jax.nn.sigmoid(x)


def _softplus(x):
    return jnp.maximum(x, 0.0) + jnp.log1p(jnp.exp(-jnp.abs(x)))


def _gelu_times_2(x):
    c = math.sqrt(2.0 / math.pi)
    return x * (1.0 + jnp.tanh(x * (c + (c * 0.044715) * (x * x))))


def _rms(x, g):
    return x * lax.rsqrt(jnp.mean(x * x, axis=-1, keepdims=True) + EPS) * g


def _mod_kernel(c_ref, w_ref, b_ref, o_ref):
    s = _silu(c_ref[...])
    o_ref[...] = jnp.dot(s, w_ref[...], preferred_element_type=F32,
                         precision=lax.Precision.HIGHEST) + b_ref[...]


def _modulation(cc, w_mod, b_mod):
    m, d = cc.shape
    n = w_mod.shape[1]
    tn = 1536
    return pl.pallas_call(
        _mod_kernel,
        grid=(n // tn,),
        in_specs=[pl.BlockSpec((m, d), lambda j: (0, 0)),
                  pl.BlockSpec((d, tn), lambda j: (0, j)),
                  pl.BlockSpec((1, tn), lambda j: (0, j))],
        out_specs=pl.BlockSpec((m, tn), lambda j: (0, j)),
        out_shape=jax.ShapeDtypeStruct((m, n), F32),
        compiler_params=_cparams(("arbitrary",)),
    )(cc, w_mod, b_mod.reshape(1, n))


def _proj_kernel(x_ref, g_ref, sh_ref, sc_ref, w_ref, *o_refs, ncol, widths):
    if ncol > 1:
        x = jnp.concatenate([x_ref[:, j, :] for j in range(ncol)], axis=0)
    else:
        x = x_ref[...]
    h = _rms(x, g_ref[...]) * (1.0 + sc_ref[...]) + sh_ref[...]
    o = _dot(h.astype(BF16), w_ref[...])
    off = 0
    for n_out, (o_ref, wd) in enumerate(zip(o_refs, widths)):
        if ncol > 1 and n_out == 0:
            rows = o.shape[0] // ncol
            for j in range(ncol):
                o_ref[:, j, :] = o[j * rows:(j + 1) * rows, off:off + wd]
        else:
            o_ref[...] = o[:, off:off + wd]
        off += wd


def _project(x3, g, sh, sc, mod_row, w, widths, *, colmajor, tile):
    b, t, d = x3.shape
    n = w.shape[1]
    flat = lambda wd: pl.BlockSpec((None, tile, wd), lambda i, j: (i, j, 0))
    if colmajor:
        rows = t // GRID_W
        ncol = tile // rows
        grid_view = lambda wd: pl.BlockSpec((None, rows, ncol, wd), lambda i, j: (i, 0, j, 0))
        xv = x3.reshape(b, rows, GRID_W, d)
        x_spec = grid_view(d)
        nt = GRID_W // ncol
        out_specs = [grid_view(widths[0])] + [flat(wd) for wd in widths[1:]]
        out_shape = ([jax.ShapeDtypeStruct((b, rows, GRID_W, widths[0]), F32)]
                     + [jax.ShapeDtypeStruct((b, t, wd), F32) for wd in widths[1:]])
    else:
        ncol = 1
        xv = x3
        x_spec = flat(d)
        nt = t // tile
        out_specs = [flat(wd) for wd in widths]
        out_shape = [jax.ShapeDtypeStruct((b, t, wd), F32) for wd in widths]
    vec = lambda: pl.BlockSpec((None, 1, d), lambda i, j: (mod_row(i), 0, 0))
    outs = pl.pallas_call(
        functools.partial(_proj_kernel, ncol=ncol, widths=tuple(widths)),
        grid=(b, nt),
        in_specs=[x_spec, pl.BlockSpec((1, d), lambda i, j: (0, 0)), vec(), vec(),
                  pl.BlockSpec((d, n), lambda i, j: (0, 0))],
        out_specs=out_specs,
        out_shape=out_shape,
        compiler_params=_cparams(("parallel", "parallel")),
    )(xv, g.reshape(1, d), sh, sc, w)
    return [outs[0].reshape(b, t, widths[0])] + list(outs[1:])


def _conv4(cur, prev8, nxt8, w_ref, first, last):
    tb, c = cur.shape
    g = tb // 8
    prev8 = jnp.where(first, 0.0, prev8)
    nxt8 = jnp.where(last, 0.0, nxt8)
    ext = jnp.concatenate([prev8, cur, nxt8], axis=0).reshape(g + 2, 8, c)
    sub = lax.broadcasted_iota(jnp.int32, (g, 8, c), 1)

    def back(k):
        r = pltpu.roll(ext, k, 1)
        return jnp.where(sub >= k, r[1:g + 1], r[0:g])

    r = pltpu.roll(ext, 7, 1)
    xp1 = jnp.where(sub < 7, r[1:g + 1], r[2:g + 2])
    y = (w_ref[0:1, :] * back(2) + w_ref[1:2, :] * back(1) + w_ref[2:3, :] * ext[1:g + 1] + w_ref[3:4, :] * xp1)
    return y.reshape(tb, c)


def _halo_specs(tb, t, width, lane_blk, tmap):
    r = tb // 8
    nb8 = t // 8
    return [pl.BlockSpec((None, tb, width), lambda i, j: (i, tmap(j), lane_blk)),
            pl.BlockSpec((None, 8, width), lambda i, j: (i, jnp.maximum(tmap(j) * r - 1, 0), lane_blk)),
            pl.BlockSpec((None, 8, width), lambda i, j: (i, jnp.minimum((tmap(j) + 1) * r, nb8 - 1), lane_blk))]


def _rglru_kernel(*refs, tb, nt, reverse):
    if reverse:
        (u_ref, up_ref, un_ref, cw_ref, cb_ref, wg_ref, gb_ref, cl_ref, h0_ref, gate_ref, hf_ref,
         y_ref, st_ref, carry) = refs
    else:
        (u_ref, up_ref, un_ref, cw_ref, cb_ref, wg_ref, gb_ref, cl_ref, h0_ref,
         y_ref, st_ref, carry) = refs
    j = pl.program_id(1)
    tblk = (nt - 1 - j) if reverse else j
    w = u_ref.shape[-1]

    @pl.when(j == 0)
    def _():
        carry[...] = h0_ref[...]

    u = u_ref[...]
    xc = _conv4(u, up_ref[...], un_ref[...], cw_ref, tblk == 0, tblk == nt - 1) + cb_ref[...]
    gates = _dot(xc.astype(BF16), wg_ref[...]) + gb_ref[...]
    r = jax.nn.sigmoid(gates[:, :w])
    ig = jax.nn.sigmoid(gates[:, w:])
    log_a = r * cl_ref[...]
    a = jnp.exp(log_a)
    th = jnp.tanh(log_a)
    bb = jnp.sqrt(-2.0 * th / (1.0 - th)) * (ig * xc)

    n_grp = tb // 8
    a = a.reshape(n_grp, 8, w)
    bb = bb.reshape(n_grp, 8, w)
    sub = lax.broadcasted_iota(jnp.int32, (n_grp, 8, w), 1)
    s = 1
    while s < 8:
        shift, ok = (8 - s, sub < 8 - s) if reverse else (s, sub >= s)
        a_sh = jnp.where(ok, pltpu.roll(a, shift, 1), 1.0)
        b_sh = jnp.where(ok, pltpu.roll(bb, shift, 1), 0.0)
        bb = a * b_sh + bb
        a = a * a_sh
        s *= 2
    h_prev = carry[...]
    hs = [None] * n_grp
    for g in (range(n_grp - 1, -1, -1) if reverse else range(n_grp)):
        hg = a[g] * h_prev + bb[g]
        h_prev = hg[0:1, :] if reverse else hg[7:8, :]
        hs[g] = hg
    h = jnp.concatenate(hs, axis=0)
    carry[...] = h_prev
    st_ref[...] = carry[...]
    if reverse:
        y_ref[...] = ((hf_ref[...] + h) * jax.nn.gelu(gate_ref[...])).astype(y_ref.dtype)
    else:
        y_ref[...] = h


def _rglru_pass(p_rg, conv_w, conv_b, wg, gate_b, c_lam, h0, hf, *, reverse, tb):
    b, t, w2 = p_rg.shape
    w = w2 // 2
    nt = t // tb
    tmap = (lambda j: nt - 1 - j) if reverse else (lambda j: j)
    const = lambda shape: pl.BlockSpec(shape, lambda i, j: (0,) * len(shape))
    in_specs = _halo_specs(tb, t, w, 0, tmap) + [
        const((4, w)), const((1, w)), const((w, 2 * w)), const((1, 2 * w)), const((1, w)),
        pl.BlockSpec((None, 1, w), lambda i, j: (i, 0, 0))]
    args = [p_rg, p_rg, p_rg, conv_w, conv_b, wg, gate_b, c_lam, h0]
    if reverse:
        in_specs += [pl.BlockSpec((None, tb, w), lambda i, j: (i, tmap(j), 1)),
                     pl.BlockSpec((None, tb, w), lambda i, j: (i, tmap(j), 0))]
        args += [p_rg, hf]
    y, st = pl.pallas_call(
        functools.partial(_rglru_kernel, tb=tb, nt=nt, reverse=reverse),
        grid=(b, nt),
        in_specs=in_specs,
        out_specs=[pl.BlockSpec((None, tb, w), lambda i, j: (i, tmap(j), 0)),
                   pl.BlockSpec((None, 1, w), lambda i, j: (i, 0, 0))],
        out_shape=[jax.ShapeDtypeStruct((b, t, w), F32),
                   jax.ShapeDtypeStruct((b, 1, w), F32)],
        scratch_shapes=[pltpu.VMEM((1, w), F32)],
        compiler_params=_cparams(("parallel", "arbitrary")),
    )(*args)
    return y, st


def _bmm(a, b):
    return lax.dot_general(a, b, (((2,), (1,)), ((0,), (0,))), preferred_element_type=F32)


def _bmm_nt(a, b):
    return lax.dot_general(a, b, (((2,), (2,)), ((0,), (0,))), preferred_element_type=F32)


def _unit_tri_inverse(a, diag_mask):
    c = a.shape[-1]
    eye = (lax.broadcasted_iota(jnp.int32, (1, c, c), 1) == lax.broadcasted_iota(jnp.int32, (1, c, c), 2)).astype(F32)
    mm = lambda p, q: _bmm(p.astype(BF16), q.astype(BF16))
    ad = jnp.where(diag_mask, a, 0.0)
    x = eye - ad
    pw = ad
    k = 2
    while k < GDN_SUB:
        pw = mm(pw, pw)
        x = x + mm(x, pw)
        k *= 2
    n = mm(x, a - ad)
    nblk = c // GDN_SUB
    y = eye - n
    pw = n
    k = 2
    while k < nblk:
        pw = mm(pw, pw)
        y = y + mm(y, pw)
        k *= 2
    return mm(y, x)


def _gdn_prep_kernel(qkv_ref, qp_ref, qn_ref, ab_ref, cw_ref, ea_ref, dtb_ref,
                     qg_ref, kd_ref, kc_ref, wv_ref, at_ref, eg_ref, *, tb, nt):
    c = GDN_CHUNK
    j = pl.program_id(1)
    hw = GDN_WIDTH
    x = _silu(_conv4(qkv_ref[...], qp_ref[...], qn_ref[...], cw_ref, j == 0, j == nt - 1))

    ab = ab_ref[...]
    col = lax.broadcasted_iota(jnp.int32, (tb, AB_PAD), 1)
    is_a = (col & 4) == 0
    gbv = jnp.where(is_a, -ea_ref[...] * _softplus(ab + dtb_ref[...]), jax.nn.sigmoid(ab))

    ri = lax.broadcasted_iota(jnp.int32, (tb, tb), 0)
    ci = lax.broadcasted_iota(jnp.int32, (tb, tb), 1)
    same = (ri // c) == (ci // c)
    l_f = (same & (ci <= ri)).astype(BF16)
    l_b = (same & (ci >= ri)).astype(BF16)
    l_t = same.astype(BF16)
    gcum = jnp.where(col < 8, _dot_sel(l_f, gbv), _dot_sel(l_b, gbv))
    gtot = _dot_sel(l_t, gbv)

    cols_a = [dr * 8 + hd for dr in range(2) for hd in range(GDN_HEADS)]
    bcast = lambda v, cols: jnp.concatenate([jnp.broadcast_to(v[:, cc:cc + 1], (tb, 128)) for cc in cols], axis=1)
    gc_all = bcast(gcum, cols_a)
    gt_all = bcast(gtot, cols_a)
    be_all = bcast(gbv, [cc + 4 for cc in cols_a])

    qs, ks, vs = [], [], []
    for hd in range(GDN_HEADS):
        q = x[:, hd * GDN_DK:(hd + 1) * GDN_DK]
        k = x[:, hw + hd * GDN_DK: hw + (hd + 1) * GDN_DK]
        qs.append(q * lax.rsqrt(jnp.sum(q * q, axis=-1, keepdims=True) + EPS) * (GDN_DK ** -0.5))
        ks.append(k * lax.rsqrt(jnp.sum(k * k, axis=-1, keepdims=True) + EPS))
        vs.append(x[:, 2 * hw + hd * GDN_DV: 2 * hw + (hd + 1) * GDN_DV])
    inst = [(ch, dr, hd) for ch in range(tb // c) for dr in range(2) for hd in range(GDN_HEADS)]
    n_inst = len(inst)
    rows = lambda ch: slice(ch * c, (ch + 1) * c)
    lanes = lambda dr, hd: slice((dr * GDN_HEADS + hd) * 128, (dr * GDN_HEADS + hd + 1) * 128)
    stack = lambda f: jnp.stack([f(ch, dr, hd) for ch, dr, hd in inst], axis=0)
    qq = stack(lambda ch, dr, hd: qs[hd][rows(ch)])
    kk = stack(lambda ch, dr, hd: ks[hd][rows(ch)])
    vv = stack(lambda ch, dr, hd: vs[hd][rows(ch)])
    gcb = stack(lambda ch, dr, hd: gc_all[rows(ch), lanes(dr, hd)])
    gtb = stack(lambda ch, dr, hd: gt_all[rows(ch), lanes(dr, hd)])
    beb = stack(lambda ch, dr, hd: be_all[rows(ch), lanes(dr, hd)])

    fwd = ((lax.broadcasted_iota(jnp.int32, (n_inst, c, 128), 0) // GDN_HEADS) % 2) == 0
    ii = lax.broadcasted_iota(jnp.int32, (n_inst, c, 128), 1)
    jj = lax.broadcasted_iota(jnp.int32, (n_inst, c, 128), 2)
    incl = (fwd & (jj <= ii)) | (jnp.logical_not(fwd) & (jj >= ii) & (jj < c))
    fwd_c = ((lax.broadcasted_iota(jnp.int32, (n_inst, c, c), 0) // GDN_HEADS) % 2) == 0
    i64 = lax.broadcasted_iota(jnp.int32, (n_inst, c, c), 1)
    j64 = lax.broadcasted_iota(jnp.int32, (n_inst, c, c), 2)
    strict = (fwd_c & (j64 < i64)) | (jnp.logical_not(fwd_c) & (j64 > i64))
    diag_blk = (i64 // GDN_SUB) == (j64 // GDN_SUB)

    eg = jnp.exp(gcb)
    kb = kk * beb
    diag = jnp.where(ii == jj, gcb, 0.0)
    gcr = _dot_sel(jnp.ones((c, c), BF16), jnp.concatenate([diag[g] for g in range(n_inst)], axis=1))
    gcr = jnp.stack([gcr[:, g * 128:(g + 1) * 128] for g in range(n_inst)], axis=0)
    decay = jnp.where(incl, jnp.exp(jnp.where(incl, gcb - gcr, 0.0)), 0.0)
    kpad = jnp.concatenate([kk.astype(BF16), jnp.zeros((n_inst, c, GDN_DK), BF16)], axis=1)
    qk = _bmm_nt(jnp.concatenate([qq, kb], axis=1).astype(BF16), kpad)
    attn = (qk[:, :c] * decay).astype(BF16)
    a_mat = jnp.where(strict, (qk[:, c:] * decay)[:, :, :c], 0.0)
    tinv = _unit_tri_inverse(a_mat, diag_blk)
    rhs = jnp.concatenate([vv * beb, kb * eg], axis=2)
    sol = _bmm(tinv.astype(BF16), rhs.astype(BF16))
    qg = (qq * eg).astype(BF16)
    kd = (kk * jnp.exp(gtb - gcb)).astype(BF16)
    egt = jnp.exp(gtb[:, 0:8, :])
    for g, (ch, dr, hd) in enumerate(inst):
        rs = rows(ch)
        ls = slice(hd * 128, (hd + 1) * 128)
        qg_ref[dr, rs, ls] = qg[g]
        kd_ref[dr, rs, ls] = kd[g]
        kc_ref[dr, rs, ls] = sol[g, :, GDN_DV:].astype(BF16)
        wv_ref[dr, rs, ls] = sol[g, :, :GDN_DV]
        at_ref[dr, rs, ls] = attn[g]
        eg_ref[dr, ch, :, ls] = egt[g]


def _bcast_cols(x, sel):
    hi, mid, lo = _split3(x)
    return _dot(lo, sel) + _dot(mid, sel) + _dot(hi, sel)


def _gdn_prep(qkv, ab, conv_w, ea_row, dtb_row, *, tb):
    b, t, cw = qkv.shape
    nt = t // tb
    nch = t // GDN_CHUNK
    cpb = tb // GDN_CHUNK
    const = lambda shape: pl.BlockSpec(shape, lambda i, j: (0,) * len(shape))
    hw = GDN_WIDTH
    big = lambda: pl.BlockSpec((2, None, tb, hw), lambda i, j: (0, i, j, 0))
    outs = pl.pallas_call(
        functools.partial(_gdn_prep_kernel, tb=tb, nt=nt),
        grid=(b, nt),
        in_specs=_halo_specs(tb, t, cw, 0, lambda j: j) + [
            pl.BlockSpec((None, tb, AB_PAD), lambda i, j: (i, j, 0)),
            const((4, cw)), const((1, AB_PAD)), const((1, AB_PAD))],
        out_specs=[big(), big(), big(), big(), big(),
                   pl.BlockSpec((2, None, cpb, 8, hw), lambda i, j: (0, i, j, 0, 0))],
        out_shape=[jax.ShapeDtypeStruct((2, b, t, hw), BF16),
                   jax.ShapeDtypeStruct((2, b, t, hw), BF16),
                   jax.ShapeDtypeStruct((2, b, t, hw), BF16),
                   jax.ShapeDtypeStruct((2, b, t, hw), F32),
                   jax.ShapeDtypeStruct((2, b, t, hw), BF16),
                   jax.ShapeDtypeStruct((2, b, nch, 8, hw), F32)],
        compiler_params=_cparams(("parallel", "parallel")),
    )(qkv, qkv, qkv, ab, conv_w, ea_row, dtb_row)
    return outs


def _gdn_scan_kernel(qg_f, kd_f, kc_f, wv_f, at_f, eg_f, qg_b, kd_b, kc_b, wv_b, at_b, eg_b, s0_ref,
                     of_ref, ob_ref, sfin_ref, s_scr, *, cps):
    c = GDN_CHUNK
    j = pl.program_id(1)

    @pl.when(j == 0)
    def _():
        s_scr[...] = s0_ref[...]

    dirs = ((qg_f, kd_f, kc_f, wv_f, at_f, eg_f), (qg_b, kd_b, kc_b, wv_b, at_b, eg_b))
    chains = [(dr, hd) for dr in range(2) for hd in range(GDN_HEADS)]
    ls = lambda hd: slice(hd * 128, (hd + 1) * 128)
    s = s_scr[...].reshape(2 * GDN_HEADS, GDN_DK, GDN_DV)
    for q in range(cps):
        chunk = (q, cps - 1 - q)
        rs = lambda dr: slice(chunk[dr] * c, (chunk[dr] + 1) * c)
        stack = lambda f: jnp.stack([f(dirs[dr], rs(dr), ls(hd), chunk[dr]) for dr, hd in chains], axis=0)
        kq = stack(lambda r, t, l, n: jnp.concatenate([r[2][t, l], r[0][t, l]], axis=0))
        wv = stack(lambda r, t, l, n: r[3][t, l])
        at = stack(lambda r, t, l, n: r[4][t, l][:, :c])
        eg = stack(lambda r, t, l, n: r[5][n, 0:1, l])
        r = _bmm(kq, s.astype(BF16))
        vb = (wv - r[:, :c]).astype(BF16)
        o = r[:, c:] + _bmm(at, vb)
        upd = []
        for g, (dr, hd) in enumerate(chains):
            (of_ref, ob_ref)[dr][rs(dr), ls(hd)] = o[g]
            upd.append(_dot_tn(dirs[dr][1][rs(dr), ls(hd)], vb[g]))
        s = s * eg + jnp.stack(upd, axis=0)
    s_scr[...] = s.reshape(2, GDN_HEADS, GDN_DK, GDN_DV)
    sfin_ref[...] = s_scr[...]


def _gdn_scan(prep, s0, *, cps):
    qg, kd, kc, wv, at, eg = prep
    _, b, t, hw = qg.shape
    c = GDN_CHUNK
    nblk = t // (c * cps)
    fw = lambda: pl.BlockSpec((None, None, cps * c, hw), lambda i, j: (0, i, j, 0))
    bw = lambda: pl.BlockSpec((None, None, cps * c, hw), lambda i, j: (1, i, nblk - 1 - j, 0))
    egf = pl.BlockSpec((None, None, cps, 8, hw), lambda i, j: (0, i, j, 0, 0))
    egb = pl.BlockSpec((None, None, cps, 8, hw), lambda i, j: (1, i, nblk - 1 - j, 0, 0))
    st = pl.BlockSpec((None, 2, GDN_HEADS, GDN_DK, GDN_DV), lambda i, j: (i, 0, 0, 0, 0))
    o_f, o_b, s_fin = pl.pallas_call(
        functools.partial(_gdn_scan_kernel, cps=cps),
        grid=(b, nblk),
        in_specs=[fw(), fw(), fw(), fw(), fw(), egf, bw(), bw(), bw(), bw(), bw(), egb, st],
        out_specs=[pl.BlockSpec((None, cps * c, hw), lambda i, j: (i, j, 0)),
                   pl.BlockSpec((None, cps * c, hw), lambda i, j: (i, nblk - 1 - j, 0)),
                   st],
        out_shape=[jax.ShapeDtypeStruct((b, t, hw), F32), jax.ShapeDtypeStruct((b, t, hw), F32),
                   jax.ShapeDtypeStruct((b, 2, GDN_HEADS, GDN_DK, GDN_DV), F32)],
        scratch_shapes=[pltpu.VMEM((2, GDN_HEADS, GDN_DK, GDN_DV), F32)],
        compiler_params=_cparams(("parallel", "arbitrary")),
    )(qg, kd, kc, wv, at, eg, qg, kd, kc, wv, at, eg, s0)
    return o_f, o_b, s_fin


def _out_kernel(x_ref, yrg_ref, of_ref, ob_ref, z_ref, ng_ref, w_ref, gt_ref, o_ref, *, ncol):
    o = of_ref[...] + ob_ref[...]
    z = z_ref[...]
    parts = [jnp.concatenate([yrg_ref[:, j, :] for j in range(ncol)], axis=0)]
    for hd in range(GDN_HEADS):
        ls = slice(hd * GDN_DV, (hd + 1) * GDN_DV)
        parts.append(_rms(o[:, ls], ng_ref[...]) * _silu(z[:, ls]))
    y = jnp.concatenate(parts, axis=1).astype(BF16)
    t2 = gt_ref[...] * _dot(y, w_ref[...])
    rows = t2.shape[0] // ncol
    for j in range(ncol):
        o_ref[:, j, :] = x_ref[:, j, :] + t2[j * rows:(j + 1) * rows]


def _out_project(x3, y_rg, o_f, o_b, z, norm_g, w_out, gt, *, tile):
    b, t, d = x3.shape
    w = RG_WIDTH
    rows = t // GRID_W
    ncol = tile // rows
    hw = GDN_WIDTH
    grid_view = lambda wd: pl.BlockSpec((None, rows, ncol, wd), lambda i, j: (i, 0, j, 0))
    cm = lambda wd: pl.BlockSpec((None, tile, wd), lambda i, j: (i, j, 0))
    x1 = pl.pallas_call(
        functools.partial(_out_kernel, ncol=ncol),
        grid=(b, GRID_W // ncol),
        in_specs=[grid_view(d), grid_view(w), cm(hw), cm(hw), cm(hw),
                  pl.BlockSpec((1, GDN_DV), lambda i, j: (0, 0)),
                  pl.BlockSpec((w + hw, d), lambda i, j: (0, 0)),
                  pl.BlockSpec((None, 1, d), lambda i, j: (i, 0, 0))],
        out_specs=grid_view(d),
        out_shape=jax.ShapeDtypeStruct((b, rows, GRID_W, d), F32),
        compiler_params=_cparams(("parallel", "parallel")),
    )(x3.reshape(b, rows, GRID_W, d), y_rg.reshape(b, rows, GRID_W, w), o_f, o_b, z, norm_g, w_out, gt)
    return x1.reshape(b, t, d)


def _fold_kernel(wq_ref, k_ref, o_ref):
    o_ref[...] = lax.dot_general(wq_ref[...], k_ref[...], (((1,), (1,)), ((), ())),
                                 preferred_element_type=F32, precision=lax.Precision.HIGHEST)


def _fold_keys(wq, keys):
    d, n = wq.shape
    nblk = n // PEER_HALF
    return pl.pallas_call(
        _fold_kernel,
        grid=(nblk,),
        in_specs=[pl.BlockSpec((d, PEER_HALF), lambda j: (0, j)),
                  pl.BlockSpec((None, PEER_NKEYS, PEER_HALF), lambda j: (j % 2, 0, 0))],
        out_specs=pl.BlockSpec((d, PEER_NKEYS), lambda j: (0, j)),
        out_shape=jax.ShapeDtypeStruct((d, nblk * PEER_NKEYS), F32),
        compiler_params=_cparams(("parallel",)),
    )(wq, keys)


def _extract16(vals, ids, s_scr, i_scr, *, fast, n_masked=0):
    n, p = vals.shape
    for r in range(PEER_TOPK):
        m = jnp.max(vals, axis=0, keepdims=True)
        eq = vals == m
        if fast:
            idx = _dot(ids, jnp.where(eq, 1.0, 0.0).astype(BF16))[0:1, :].astype(jnp.int32)
            vals = jnp.where(eq, NEG_INF, vals)
        else:
            idx = jnp.min(jnp.where(eq, ids, 1 << 20), axis=0, keepdims=True)
            vals = jnp.where(ids == idx, NEG_INF, vals)
        s_scr[r:r + 1, :] = m
        i_scr[r:r + 1, :] = idx
    if not fast:
        return None
    dropped = jnp.sum(jnp.where(vals == NEG_INF, 1.0, 0.0), axis=0, keepdims=True)
    return jnp.where(dropped == float(PEER_TOPK + n_masked), 0.0, 1.0)


def _peer_sel_kernel(x_ref, g_ref, sh_ref, sc_ref, wt_ref, h_ref, a_ref, b_ref, gate_ref,
                     sc_scr, s1, i1, s2, i2, ts, tp, a_t, b_t, g_t):
    k = PEER_TOPK
    h = _rms(x_ref[...], g_ref[...]) * (1.0 + sc_ref[...]) + sh_ref[...]
    hb = h.astype(BF16)
    h_ref[...] = hb
    sc_scr[...] = _dot_nt(wt_ref[...], hb)
    p = hb.shape[0]

    n_cand = 80
    n_masked = sum(8 - k // (r + 1) for r in range(1, 8))

    def cand_pos(row):
        return jnp.where(row < 16, row, jnp.where(row < 72, ((row - 16) // 8 + 1) * k + (row - 16) % 8,
                                                  (row - 72 + 8) * k))

    def head(hd, fast):
        if fast:
            key_ids = lax.broadcasted_iota(jnp.int32, (8, PEER_NKEYS), 1).astype(F32).astype(BF16)
            pos_ids = cand_pos(lax.broadcasted_iota(jnp.int32, (8, n_cand), 1)).astype(F32).astype(BF16)
        else:
            key_ids = lax.broadcasted_iota(jnp.int32, (PEER_NKEYS, p), 0)
            pos_ids = cand_pos(lax.broadcasted_iota(jnp.int32, (n_cand, p), 0))
        sub = lax.broadcasted_iota(jnp.int32, (8, p), 0)
        base = pl.multiple_of(hd * 2 * PEER_NKEYS, 2 * PEER_NKEYS)
        bad1 = _extract16(sc_scr[pl.ds(base, PEER_NKEYS), :], key_ids, s1, i1, fast=fast)
        bad2 = _extract16(sc_scr[pl.ds(base + PEER_NKEYS, PEER_NKEYS), :], key_ids, s2, i2, fast=fast)
        s1v, s2v = s1[...], s2[...]
        cands = [s1v[0:1, :] + s2v]
        for r in range(1, 8):
            cands.append(jnp.where(sub < k // (r + 1), s1v[r:r + 1, :] + s2v[0:8, :], NEG_INF))
        cands.append(s1v[8:16, :] + s2v[0:1, :])
        bad3 = _extract16(jnp.concatenate(cands, axis=0), pos_ids, ts, tp, fast=fast, n_masked=n_masked)
        top_s, top_p = ts[...], tp[...]
        rr = top_p >> 4
        cc = top_p & (k - 1)
        i1v, i2v = i1[...], i2[...]
        av = jnp.zeros_like(top_p)
        bv = jnp.zeros_like(top_p)
        for q in range(k):
            av = jnp.where(rr == q, i1v[q:q + 1, :], av)
            bv = jnp.where(cc == q, i2v[q:q + 1, :], bv)
        e = jnp.exp(top_s - top_s[0:1, :])
        gate = e / jnp.sum(e, axis=0, keepdims=True)
        o = pl.multiple_of(hd * k, k)
        a_t[pl.ds(o, k), :] = av
        b_t[pl.ds(o, k), :] = bv
        g_t[pl.ds(o, k), :] = gate
        return (bad1 + bad2 + bad3) if fast else None

    def head_step(hd, carry):
        bad = head(hd, True)

        @pl.when(jnp.max(bad) > 0.0)
        def _():
            head(hd, False)

        return carry

    lax.fori_loop(0, PEER_HEADS, head_step, 0)

    a_ref[...] = a_t[...].T
    b_ref[...] = b_t[...].T
    gate_ref[...] = g_t[...].T


def _peer_select(x1, g, sh, sc, wt, *, tile):
    b, t, d = x1.shape
    nt = t // tile
    nk = PEER_HEADS * PEER_TOPK
    nrow = wt.shape[0]
    vec = lambda: pl.BlockSpec((None, 1, d), lambda i, j: (i, 0, 0))
    tr = lambda: pl.BlockSpec((None, tile, nk), lambda i, j: (i, j, 0))
    k = PEER_TOPK
    return pl.pallas_call(
        _peer_sel_kernel,
        grid=(b, nt),
        in_specs=[pl.BlockSpec((None, tile, d), lambda i, j: (i, j, 0)),
                  pl.BlockSpec((1, d), lambda i, j: (0, 0)), vec(), vec(),
                  pl.BlockSpec((nrow, d), lambda i, j: (0, 0))],
        out_specs=[pl.BlockSpec((None, tile, d), lambda i, j: (i, j, 0)), tr(), tr(), tr()],
        out_shape=[jax.ShapeDtypeStruct((b, t, d), BF16),
                   jax.ShapeDtypeStruct((b, t, nk), jnp.int32),
                   jax.ShapeDtypeStruct((b, t, nk), jnp.int32),
                   jax.ShapeDtypeStruct((b, t, nk), F32)],
        scratch_shapes=[pltpu.VMEM((nrow, tile), F32),
                        pltpu.VMEM((k, tile), F32), pltpu.VMEM((k, tile), jnp.int32),
                        pltpu.VMEM((k, tile), F32), pltpu.VMEM((k, tile), jnp.int32),
                        pltpu.VMEM((k, tile), F32), pltpu.VMEM((k, tile), jnp.int32),
                        pltpu.VMEM((nk, tile), jnp.int32), pltpu.VMEM((nk, tile), jnp.int32),
                        pltpu.VMEM((nk, tile), F32)],
        compiler_params=_cparams(("parallel", "parallel")),
    )(x1, g.reshape(1, d), sh, sc, wt)


def _peer_mix_kernel(h_ref, a_ref, b_ref, gate_ref, ut_ref, v_ref, x_ref, gt_ref, fg_ref, o_ref,
                     m_scr, acc, *, tile, n_steps, pairs):
    nk = PEER_NKEYS
    half = nk // 2
    step = pl.program_id(2)
    hi_mask = jnp.uint32(0xFFFF0000)

    @pl.when(step == 0)
    def _():
        sub = lax.broadcasted_iota(jnp.int32, (nk, a_ref.shape[1]), 0)
        a_of_row = jnp.where(sub < half, 2 * sub, 2 * (sub - half) + 1)

        def build(p, carry):
            arow = a_ref[pl.ds(p, 1), :]
            brow = b_ref[pl.ds(p, 1), :]
            grow = gate_ref[pl.ds(p, 1), :]
            xa = jnp.where(a_of_row == arow, 1.0, 0.0).astype(BF16)
            yb = jnp.where(sub == brow, 0.5 * grow, 0.0).astype(BF16)
            m = _dot_nt(xa, yb).astype(BF16).astype(F32)
            bits = lax.bitcast_convert_type(m, jnp.uint32)
            m_scr[pl.ds(pl.multiple_of(p * MASK_PITCH, 8), half), :] = (bits[half:] & hi_mask) | (bits[:half] >> 16)
            return carry

        lax.fori_loop(0, tile, build, 0, unroll=32)

    h = h_ref[...]
    parts = []
    for q in range(pairs):
        pr = step * pairs + q
        act = _gelu_times_2(_dot(h, ut_ref[:, q * 2 * nk:(q + 1) * 2 * nk]))
        w = m_scr[pl.ds(pr, tile, stride=MASK_PITCH), :]
        m_even = lax.bitcast_convert_type(w << 16, F32)
        m_odd = lax.bitcast_convert_type(w & hi_mask, F32)
        parts.append((act * jnp.concatenate([m_even, m_odd], axis=1)).astype(BF16))
    contrib = _dot(jnp.concatenate(parts, axis=1), v_ref[...])

    @pl.when(step == 0)
    def _():
        acc[...] = contrib

    @pl.when(step > 0)
    def _():
        acc[...] += contrib

    @pl.when(step == n_steps - 1)
    def _():
        x2 = x_ref[...] + gt_ref[...] * acc[...]
        o_ref[...] = _rms(x2, fg_ref[...])


def _peer_mix(hb, aidx, bidx, gate, ut, v, x1, gt, final_g, *, tile, pairs):
    b, t, d = x1.shape
    nt = t // tile
    nk = PEER_NKEYS
    npk = aidx.shape[-1]
    n_steps = nk // (2 * pairs)
    eb = 2 * nk * pairs
    tok = lambda wd: pl.BlockSpec((None, tile, wd), lambda i, j, s: (i, j, 0))
    return pl.pallas_call(
        functools.partial(_peer_mix_kernel, tile=tile, n_steps=n_steps, pairs=pairs),
        grid=(b, nt, n_steps),
        in_specs=[tok(d), tok(npk), tok(npk), tok(npk),
                  pl.BlockSpec((d, eb), lambda i, j, s: (0, s)),
                  pl.BlockSpec((eb, d), lambda i, j, s: (s, 0)), tok(d),
                  pl.BlockSpec((None, 1, d), lambda i, j, s: (i, 0, 0)),
                  pl.BlockSpec((1, d), lambda i, j, s: (0, 0))],
        out_specs=tok(d),
        out_shape=jax.ShapeDtypeStruct((b, t, d), F32),
        scratch_shapes=[pltpu.VMEM((tile * MASK_PITCH, nk), jnp.uint32), pltpu.VMEM((tile, d), F32)],
        compiler_params=_cparams(("parallel", "parallel", "arbitrary")),
    )(hb, aidx, bidx, gate, ut, v, x1, gt, final_g.reshape(1, d))


def _block_diag(w):
    n, e, _ = w.shape
    eye = jnp.eye(n, dtype=w.dtype)
    return (eye[:, None, :, None] * w[:, :, None, :]).reshape(n * e, n * e)


def _mix_sequence(x3, ctx_mode, params, states, mods, *, rg_tb, gdn_tb, proj_tile):
    (norm1_g, w_rg, w_gdn, rg_conv_w, rg_conv_b, wg, gate_b, c_lam, gdn_conv_w, ea_row, dtb_row) = params
    sh1, sc1, mod_row = mods
    rg_h0_f, rg_h0_b, gdn_s0 = states
    p_rg, qkv, z, ab = _project(x3, norm1_g, sh1, sc1, mod_row, jnp.concatenate([w_rg, w_gdn], axis=1),
                                (2 * RG_WIDTH, 3 * GDN_WIDTH, GDN_WIDTH, AB_PAD), colmajor=not ctx_mode, tile=proj_tile)
    h_f, st_f = _rglru_pass(p_rg, rg_conv_w, rg_conv_b, wg[0], gate_b[0], c_lam[0], rg_h0_f, None,
                            reverse=False, tb=rg_tb)
    y_rg, st_b = _rglru_pass(p_rg, rg_conv_w, rg_conv_b, wg[1], gate_b[1], c_lam[1], rg_h0_b, h_f,
                             reverse=True, tb=rg_tb)
    prep = _gdn_prep(qkv, ab, gdn_conv_w, ea_row, dtb_row, tb=gdn_tb)
    o_f, o_b, s_fin = _gdn_scan(prep, gdn_s0, cps=min(8, x3.shape[1] // GDN_CHUNK))
    return y_rg, o_f, o_b, z, (st_f, st_b, s_fin)


def kernel(x, c, ctx, c_ctx, w_mod, b_mod, norm1_g, norm2_g, w_in, rg_conv_w, rg_conv_b, rg_gate_w, rg_gate_b,
           rg_lambda, gdn_conv_w, gdn_a_log, gdn_dt_bias, gdn_norm_g, w_out, peer_wq, peer_keys, peer_u, peer_v,
           final_g):
    b, t, d = x.shape
    depth = w_mod.shape[0]
    assert depth == 1, "context residual stream update is only needed for depth > 1"
    l = 0
    w = RG_WIDTH

    cc = jnp.zeros((16, d), F32).at[:b].set(c).at[b].set(c_ctx)
    w_rg = w_in[l][:, :2 * w].astype(BF16)
    n_ab = w_in.shape[2] - 2 * w - 4 * GDN_WIDTH
    w_gdn = jnp.concatenate([w_in[l][:, 2 * w:2 * w + 4 * GDN_WIDTH],
                             jnp.pad(w_in[l][:, 2 * w + 4 * GDN_WIDTH:], ((0, 0), (0, AB_PAD - n_ab)))],
                            axis=1).astype(BF16)
    wg = jnp.stack([jnp.concatenate([_block_diag(rg_gate_w[l, dr, 0]), _block_diag(rg_gate_w[l, dr, 1])], axis=1)
                    for dr in range(2)]).astype(BF16)
    gate_b = rg_gate_b[l].reshape(2, 1, 2 * w)
    c_lam = (-RG_C * jax.nn.softplus(-rg_lambda[l])).reshape(2, 1, w)
    ea = jnp.exp(gdn_a_log[l])
    pad4 = jnp.zeros((2, GDN_HEADS), F32)
    ea_row = jnp.pad(jnp.concatenate([ea, pad4], axis=1).reshape(1, -1), ((0, 0), (0, AB_PAD - n_ab)))
    dtb_row = jnp.pad(jnp.concatenate([gdn_dt_bias[l], pad4], axis=1).reshape(1, -1), ((0, 0), (0, AB_PAD - n_ab)))
    params = (norm1_g[l], w_rg, w_gdn, rg_conv_w[l], rg_conv_b[l].reshape(1, w), wg, gate_b, c_lam,
              gdn_conv_w[l], ea_row, dtb_row)

    mod = _modulation(cc, w_mod[l], b_mod[l])
    sh1, sc1, gt1, sh2, sc2, gt2 = [mod[:, i * d:(i + 1) * d].reshape(16, 1, d) for i in range(6)]

    tc = ctx.shape[1]
    zero_states = (jnp.zeros((b, 1, w), F32), jnp.zeros((b, 1, w), F32),
                   jnp.zeros((b, 2, GDN_HEADS, GDN_DK, GDN_DV), F32))
    ctx_tb = min(tc, 256)
    _, _, _, _, ctx_states = _mix_sequence(ctx, True, params, zero_states, (sh1, sc1, lambda i: b),
                                           rg_tb=ctx_tb, gdn_tb=min(tc, 128), proj_tile=ctx_tb)

    tile = min(t, 512)
    y_rg, o_f, o_b, z, _ = _mix_sequence(x, False, params, ctx_states, (sh1, sc1, lambda i: i),
                                         rg_tb=min(t, 256), gdn_tb=min(t, 256), proj_tile=tile)
    x1 = _out_project(x, y_rg, o_f, o_b, z, gdn_norm_g[l].reshape(1, GDN_DV), w_out[l].astype(BF16), gt1, tile=tile)

    wfold_t = _fold_keys(peer_wq[l], peer_keys[l]).T.astype(BF16)
    sel_tile = min(t, 256)
    hb, aidx, bidx, gate = _peer_select(x1, norm2_g[l], sh2, sc2, wfold_t, tile=sel_tile)
    out = _peer_mix(hb, aidx, bidx, gate, peer_u[l].astype(BF16).T, peer_v[l].astype(BF16),
                    x1, gt2, final_g, tile=tile, pairs=8)
    return out
```

```python
import functools
import math

import jax
import jax.numpy as jnp
from jax import lax
from jax.experimental import pallas as pl
from jax.experimental.pallas import tpu as pltpu

F32 = jnp.float32
BF16 = jnp.bfloat16

GRID_W = 64
EPS = 1e-6
RG_WIDTH = 512
RG_BLOCKS = 8
RG_C = 8.0
GDN_HEADS = 4
GDN_DK = 128
GDN_DV = 128
GDN_WIDTH = GDN_HEADS * GDN_DV
GDN_CHUNK = 64
GDN_SUB = 16
AB_PAD = 128
PEER_HEADS = 8
PEER_NKEYS = 128
PEER_HALF = 128
PEER_TOPK = 16
NEG_INF = float("-inf")
MASK_PITCH = 72

VMEM_LIMIT = 58 * 1024 * 1024


def _cparams(sem):
    return pltpu.CompilerParams(dimension_semantics=sem, vmem_limit_bytes=VMEM_LIMIT)


def _dot(a, b):
    return jnp.dot(a, b, preferred_element_type=F32)


def _dot_nt(a, b):
    return lax.dot_general(a, b, (((1,), (1,)), ((), ())), preferred_element_type=F32)


def _dot_tn(a, b):
    return lax.dot_general(a, b, (((0,), (0,)), ((), ())), preferred_element_type=F32)


def _split3(x):
    hi = x.astype(BF16)
    r = x - hi.astype(F32)
    mid = r.astype(BF16)
    lo = (r - mid.astype(F32)).astype(BF16)
    return hi, mid, lo


def _dot_sel(m, x):
    hi, mid, lo = _split3(x)
    return _dot(m, lo) + _dot(m, mid) + _dot(m, hi)


def _silu(x):
    return x * jax.nn.sigmoid(x)


def _softplus(x):
    return jnp.maximum(x, 0.0) + jnp.log1p(jnp.exp(-jnp.abs(x)))


def _gelu_times_2(x):
    c = math.sqrt(2.0 / math.pi)
    return x * (1.0 + jnp.tanh(x * (c + (c * 0.044715) * (x * x))))


def _rms(x, g):
    return x * lax.rsqrt(jnp.mean(x * x, axis=-1, keepdims=True) + EPS) * g


def _mod_kernel(c_ref, w_ref, b_ref, o_ref):
    s = _silu(c_ref[...])
    o_ref[...] = jnp.dot(s, w_ref[...], preferred_element_type=F32,
                         precision=lax.Precision.HIGHEST) + b_ref[...]


def _modulation(cc, w_mod, b_mod):
    m, d = cc.shape
    n = w_mod.shape[1]
    tn = 1536
    return pl.pallas_call(
        _mod_kernel,
        grid=(n // tn,),
        in_specs=[pl.BlockSpec((m, d), lambda j: (0, 0)),
                  pl.BlockSpec((d, tn), lambda j: (0, j)),
                  pl.BlockSpec((1, tn), lambda j: (0, j))],
        out_specs=pl.BlockSpec((m, tn), lambda j: (0, j)),
        out_shape=jax.ShapeDtypeStruct((m, n), F32),
        compiler_params=_cparams(("arbitrary",)),
    )(cc, w_mod, b_mod.reshape(1, n))


def _proj_kernel(x_ref, g_ref, sh_ref, sc_ref, w_ref, *o_refs, ncol, widths):
    if ncol > 1:
        x = jnp.concatenate([x_ref[:, j, :] for j in range(ncol)], axis=0)
    else:
        x = x_ref[...]
    h = _rms(x, g_ref[...]) * (1.0 + sc_ref[...]) + sh_ref[...]
    o = _dot(h.astype(BF16), w_ref[...])
    off = 0
    for n_out, (o_ref, wd) in enumerate(zip(o_refs, widths)):
        if ncol > 1 and n_out == 0:
            rows = o.shape[0] // ncol
            for j in range(ncol):
                o_ref[:, j, :] = o[j * rows:(j + 1) * rows, off:off + wd]
        else:
            o_ref[...] = o[:, off:off + wd]
        off += wd


def _project(x3, g, sh, sc, mod_row, w, widths, *, colmajor, tile):
    b, t, d = x3.shape
    n = w.shape[1]
    flat = lambda wd: pl.BlockSpec((None, tile, wd), lambda i, j: (i, j, 0))
    if colmajor:
        rows = t // GRID_W
        ncol = tile // rows
        grid_view = lambda wd: pl.BlockSpec((None, rows, ncol, wd), lambda i, j: (i, 0, j, 0))
        xv = x3.reshape(b, rows, GRID_W, d)
        x_spec = grid_view(d)
        nt = GRID_W // ncol
        out_specs = [grid_view(widths[0])] + [flat(wd) for wd in widths[1:]]
        out_shape = ([jax.ShapeDtypeStruct((b, rows, GRID_W, widths[0]), F32)]
                     + [jax.ShapeDtypeStruct((b, t, wd), F32) for wd in widths[1:]])
    else:
        ncol = 1
        xv = x3
        x_spec = flat(d)
        nt = t // tile
        out_specs = [flat(wd) for wd in widths]
        out_shape = [jax.ShapeDtypeStruct((b, t, wd), F32) for wd in widths]
    vec = lambda: pl.BlockSpec((None, 1, d), lambda i, j: (mod_row(i), 0, 0))
    outs = pl.pallas_call(
        functools.partial(_proj_kernel, ncol=ncol, widths=tuple(widths)),
        grid=(b, nt),
        in_specs=[x_spec, pl.BlockSpec((1, d), lambda i, j: (0, 0)), vec(), vec(),
                  pl.BlockSpec((d, n), lambda i, j: (0, 0))],
        out_specs=out_specs,
        out_shape=out_shape,
        compiler_params=_cparams(("parallel", "parallel")),
    )(xv, g.reshape(1, d), sh, sc, w)
    return [outs[0].reshape(b, t, widths[0])] + list(outs[1:])


def _conv4(cur, prev8, nxt8, w_ref, first, last):
    tb, c = cur.shape
    g = tb // 8
    prev8 = jnp.where(first, 0.0, prev8)
    nxt8 = jnp.where(last, 0.0, nxt8)
    ext = jnp.concatenate([prev8, cur, nxt8], axis=0).reshape(g + 2, 8, c)
    sub = lax.broadcasted_iota(jnp.int32, (g, 8, c), 1)

    def back(k):
        r = pltpu.roll(ext, k, 1)
        return jnp.where(sub >= k, r[1:g + 1], r[0:g])

    r = pltpu.roll(ext, 7, 1)
    xp1 = jnp.where(sub < 7, r[1:g + 1], r[2:g + 2])
    y = (w_ref[0:1, :] * back(2) + w_ref[1:2, :] * back(1) + w_ref[2:3, :] * ext[1:g + 1] + w_ref[3:4, :] * xp1)
    return y.reshape(tb, c)


def _halo_specs(tb, t, width, lane_blk, tmap):
    r = tb // 8
    nb8 = t // 8
    return [pl.BlockSpec((None, tb, width), lambda i, j: (i, tmap(j), lane_blk)),
            pl.BlockSpec((None, 8, width), lambda i, j: (i, jnp.maximum(tmap(j) * r - 1, 0), lane_blk)),
            pl.BlockSpec((None, 8, width), lambda i, j: (i, jnp.minimum((tmap(j) + 1) * r, nb8 - 1), lane_blk))]


def _rglru_kernel(*refs, tb, nt, reverse):
    if reverse:
        (u_ref, up_ref, un_ref, cw_ref, cb_ref, wg_ref, gb_ref, cl_ref, h0_ref, gate_ref, hf_ref,
         y_ref, st_ref, carry) = refs
    else:
        (u_ref, up_ref, un_ref, cw_ref, cb_ref, wg_ref, gb_ref, cl_ref, h0_ref,
         y_ref, st_ref, carry) = refs
    j = pl.program_id(1)
    tblk = (nt - 1 - j) if reverse else j
    w = u_ref.shape[-1]

    @pl.when(j == 0)
    def _():
        carry[...] = h0_ref[...]

    u = u_ref[...]
    xc = _conv4(u, up_ref[...], un_ref[...], cw_ref, tblk == 0, tblk == nt - 1) + cb_ref[...]
    gates = _dot(xc.astype(BF16), wg_ref[...]) + gb_ref[...]
    r = jax.nn.sigmoid(gates[:, :w])
    ig = jax.nn.sigmoid(gates[:, w:])
    log_a = r * cl_ref[...]
    a = jnp.exp(log_a)
    th = jnp.tanh(log_a)
    bb = jnp.sqrt(-2.0 * th / (1.0 - th)) * (ig * xc)

    n_grp = tb // 8
    a = a.reshape(n_grp, 8, w)
    bb = bb.reshape(n_grp, 8, w)
    sub = lax.broadcasted_iota(jnp.int32, (n_grp, 8, w), 1)
    s = 1
    while s < 8:
        shift, ok = (8 - s, sub < 8 - s) if reverse else (s, sub >= s)
        a_sh = jnp.where(ok, pltpu.roll(a, shift, 1), 1.0)
        b_sh = jnp.where(ok, pltpu.roll(bb, shift, 1), 0.0)
        bb = a * b_sh + bb
        a = a * a_sh
        s *= 2
    h_prev = carry[...]
    hs = [None] * n_grp
    for g in (range(n_grp - 1, -1, -1) if reverse else range(n_grp)):
        hg = a[g] * h_prev + bb[g]
        h_prev = hg[0:1, :] if reverse else hg[7:8, :]
        hs[g] = hg
    h = jnp.concatenate(hs, axis=0)
    carry[...] = h_prev
    st_ref[...] = carry[...]
    if reverse:
        y_ref[...] = ((hf_ref[...] + h) * jax.nn.gelu(gate_ref[...])).astype(y_ref.dtype)
    else:
        y_ref[...] = h


def _rglru_pass(p_rg, conv_w, conv_b, wg, gate_b, c_lam, h0, hf, *, reverse, tb):
    b, t, w2 = p_rg.shape
    w = w2 // 2
    nt = t // tb
    tmap = (lambda j: nt - 1 - j) if reverse else (lambda j: j)
    const = lambda shape: pl.BlockSpec(shape, lambda i, j: (0,) * len(shape))
    in_specs = _halo_specs(tb, t, w, 0, tmap) + [
        const((4, w)), const((1, w)), const((w, 2 * w)), const((1, 2 * w)), const((1, w)),
        pl.BlockSpec((None, 1, w), lambda i, j: (i, 0, 0))]
    args = [p_rg, p_rg, p_rg, conv_w, conv_b, wg, gate_b, c_lam, h0]
    if reverse:
        in_specs += [pl.BlockSpec((None, tb, w), lambda i, j: (i, tmap(j), 1)),
                     pl.BlockSpec((None, tb, w), lambda i, j: (i, tmap(j), 0))]
        args += [p_rg, hf]
    y, st = pl.pallas_call(
        functools.partial(_rglru_kernel, tb=tb, nt=nt, reverse=reverse),
        grid=(b, nt),
        in_specs=in_specs,
        out_specs=[pl.BlockSpec((None, tb, w), lambda i, j: (i, tmap(j), 0)),
                   pl.BlockSpec((None, 1, w), lambda i, j: (i, 0, 0))],
        out_shape=[jax.ShapeDtypeStruct((b, t, w), F32),
                   jax.ShapeDtypeStruct((b, 1, w), F32)],
        scratch_shapes=[pltpu.VMEM((1, w), F32)],
        compiler_params=_cparams(("parallel", "arbitrary")),
    )(*args)
    return y, st


def _bmm(a, b):
    return lax.dot_general(a, b, (((2,), (1,)), ((0,), (0,))), preferred_element_type=F32)


def _bmm_nt(a, b):
    return lax.dot_general(a, b, (((2,), (2,)), ((0,), (0,))), preferred_element_type=F32)


def _unit_tri_inverse(a, diag_mask):
    c = a.shape[-1]
    eye = (lax.broadcasted_iota(jnp.int32, (1, c, c), 1) == lax.broadcasted_iota(jnp.int32, (1, c, c), 2)).astype(F32)
    mm = lambda p, q: _bmm(p.astype(BF16), q.astype(BF16))
    ad = jnp.where(diag_mask, a, 0.0)
    x = eye - ad
    pw = ad
    k = 2
    while k < GDN_SUB:
        pw = mm(pw, pw)
        x = x + mm(x, pw)
        k *= 2
    n = mm(x, a - ad)
    nblk = c // GDN_SUB
    y = eye - n
    pw = n
    k = 2
    while k < nblk:
        pw = mm(pw, pw)
        y = y + mm(y, pw)
        k *= 2
    return mm(y, x)


def _gdn_prep_kernel(qkv_ref, qp_ref, qn_ref, ab_ref, cw_ref, ea_ref, dtb_ref,
                     qg_ref, kd_ref, kc_ref, wv_ref, at_ref, eg_ref, *, tb, nt):
    c = GDN_CHUNK
    j = pl.program_id(1)
    hw = GDN_WIDTH
    x = _silu(_conv4(qkv_ref[...], qp_ref[...], qn_ref[...], cw_ref, j == 0, j == nt - 1))

    ab = ab_ref[...]
    col = lax.broadcasted_iota(jnp.int32, (tb, AB_PAD), 1)
    is_a = (col & 4) == 0
    gbv = jnp.where(is_a, -ea_ref[...] * _softplus(ab + dtb_ref[...]), jax.nn.sigmoid(ab))

    ri = lax.broadcasted_iota(jnp.int32, (tb, tb), 0)
    ci = lax.broadcasted_iota(jnp.int32, (tb, tb), 1)
    same = (ri // c) == (ci // c)
    l_f = (same & (ci <= ri)).astype(BF16)
    l_b = (same & (ci >= ri)).astype(BF16)
    l_t = same.astype(BF16)
    gcum = jnp.where(col < 8, _dot_sel(l_f, gbv), _dot_sel(l_b, gbv))
    gtot = _dot_sel(l_t, gbv)

    cols_a = [dr * 8 + hd for dr in range(2) for hd in range(GDN_HEADS)]
    bcast = lambda v, cols: jnp.concatenate([jnp.broadcast_to(v[:, cc:cc + 1], (tb, 128)) for cc in cols], axis=1)
    gc_all = bcast(gcum, cols_a)
    gt_all = bcast(gtot, cols_a)
    be_all = bcast(gbv, [cc + 4 for cc in cols_a])

    qs, ks, vs = [], [], []
    for hd in range(GDN_HEADS):
        q = x[:, hd * GDN_DK:(hd + 1) * GDN_DK]
        k = x[:, hw + hd * GDN_DK: hw + (hd + 1) * GDN_DK]
        qs.append(q * lax.rsqrt(jnp.sum(q * q, axis=-1, keepdims=True) + EPS) * (GDN_DK ** -0.5))
        ks.append(k * lax.rsqrt(jnp.sum(k * k, axis=-1, keepdims=True) + EPS))
        vs.append(x[:, 2 * hw + hd * GDN_DV: 2 * hw + (hd + 1) * GDN_DV])
    inst = [(ch, dr, hd) for ch in range(tb // c) for dr in range(2) for hd in range(GDN_HEADS)]
    n_inst = len(inst)
    rows = lambda ch: slice(ch * c, (ch + 1) * c)
    lanes = lambda dr, hd: slice((dr * GDN_HEADS + hd) * 128, (dr * GDN_HEADS + hd + 1) * 128)
    stack = lambda f: jnp.stack([f(ch, dr, hd) for ch, dr, hd in inst], axis=0)
    qq = stack(lambda ch, dr, hd: qs[hd][rows(ch)])
    kk = stack(lambda ch, dr, hd: ks[hd][rows(ch)])
    vv = stack(lambda ch, dr, hd: vs[hd][rows(ch)])
    gcb = stack(lambda ch, dr, hd: gc_all[rows(ch), lanes(dr, hd)])
    gtb = stack(lambda ch, dr, hd: gt_all[rows(ch), lanes(dr, hd)])
    beb = stack(lambda ch, dr, hd: be_all[rows(ch), lanes(dr, hd)])

    fwd = ((lax.broadcasted_iota(jnp.int32, (n_inst, c, 128), 0) // GDN_HEADS) % 2) == 0
    ii = lax.broadcasted_iota(jnp.int32, (n_inst, c, 128), 1)
    jj = lax.broadcasted_iota(jnp.int32, (n_inst, c, 128), 2)
    incl = (fwd & (jj <= ii)) | (jnp.logical_not(fwd) & (jj >= ii) & (jj < c))
    fwd_c = ((lax.broadcasted_iota(jnp.int32, (n_inst, c, c), 0) // GDN_HEADS) % 2) == 0
    i64 = lax.broadcasted_iota(jnp.int32, (n_inst, c, c), 1)
    j64 = lax.broadcasted_iota(jnp.int32, (n_inst, c, c), 2)
    strict = (fwd_c & (j64 < i64)) | (jnp.logical_not(fwd_c) & (j64 > i64))
    diag_blk = (i64 // GDN_SUB) == (j64 // GDN_SUB)

    eg = jnp.exp(gcb)
    kb = kk * beb
    diag = jnp.where(ii == jj, gcb, 0.0)
    gcr = _dot_sel(jnp.ones((c, c), BF16), jnp.concatenate([diag[g] for g in range(n_inst)], axis=1))
    gcr = jnp.stack([gcr[:, g * 128:(g + 1) * 128] for g in range(n_inst)], axis=0)
    decay = jnp.where(incl, jnp.exp(jnp.where(incl, gcb - gcr, 0.0)), 0.0)
    kpad = jnp.concatenate([kk.astype(BF16), jnp.zeros((n_inst, c, GDN_DK), BF16)], axis=1)
    qk = _bmm_nt(jnp.concatenate([qq, kb], axis=1).astype(BF16), kpad)
    attn = (qk[:, :c] * decay).astype(BF16)
    a_mat = jnp.where(strict, (qk[:, c:] * decay)[:, :, :c], 0.0)
    tinv = _unit_tri_inverse(a_mat, diag_blk)
    rhs = jnp.concatenate([vv * beb, kb * eg], axis=2)
    sol = _bmm(tinv.astype(BF16), rhs.astype(BF16))
    qg = (qq * eg).astype(BF16)
    kd = (kk * jnp.exp(gtb - gcb)).astype(BF16)
    egt = jnp.exp(gtb[:, 0:8, :])
    for g, (ch, dr, hd) in enumerate(inst):
        rs = rows(ch)
        ls = slice(hd * 128, (hd + 1) * 128)
        qg_ref[dr, rs, ls] = qg[g]
        kd_ref[dr, rs, ls] = kd[g]
        kc_ref[dr, rs, ls] = sol[g, :, GDN_DV:].astype(BF16)
        wv_ref[dr, rs, ls] = sol[g, :, :GDN_DV]
        at_ref[dr, rs, ls] = attn[g]
        eg_ref[dr, ch, :, ls] = egt[g]


def _bcast_cols(x, sel):
    hi, mid, lo = _split3(x)
    return _dot(lo, sel) + _dot(mid, sel) + _dot(hi, sel)


def _gdn_prep(qkv, ab, conv_w, ea_row, dtb_row, *, tb):
    b, t, cw = qkv.shape
    nt = t // tb
    nch = t // GDN_CHUNK
    cpb = tb // GDN_CHUNK
    const = lambda shape: pl.BlockSpec(shape, lambda i, j: (0,) * len(shape))
    hw = GDN_WIDTH
    big = lambda: pl.BlockSpec((2, None, tb, hw), lambda i, j: (0, i, j, 0))
    outs = pl.pallas_call(
        functools.partial(_gdn_prep_kernel, tb=tb, nt=nt),
        grid=(b, nt),
        in_specs=_halo_specs(tb, t, cw, 0, lambda j: j) + [
            pl.BlockSpec((None, tb, AB_PAD), lambda i, j: (i, j, 0)),
            const((4, cw)), const((1, AB_PAD)), const((1, AB_PAD))],
        out_specs=[big(), big(), big(), big(), big(),
                   pl.BlockSpec((2, None, cpb, 8, hw), lambda i, j: (0, i, j, 0, 0))],
        out_shape=[jax.ShapeDtypeStruct((2, b, t, hw), BF16),
                   jax.ShapeDtypeStruct((2, b, t, hw), BF16),
                   jax.ShapeDtypeStruct((2, b, t, hw), BF16),
                   jax.ShapeDtypeStruct((2, b, t, hw), F32),
                   jax.ShapeDtypeStruct((2, b, t, hw), BF16),
                   jax.ShapeDtypeStruct((2, b, nch, 8, hw), F32)],
        compiler_params=_cparams(("parallel", "parallel")),
    )(qkv, qkv, qkv, ab, conv_w, ea_row, dtb_row)
    return outs


def _gdn_scan_kernel(qg_f, kd_f, kc_f, wv_f, at_f, eg_f, qg_b, kd_b, kc_b, wv_b, at_b, eg_b, s0_ref,
                     of_ref, ob_ref, sfin_ref, s_scr, *, cps):
    c = GDN_CHUNK
    j = pl.program_id(1)

    @pl.when(j == 0)
    def _():
        s_scr[...] = s0_ref[...]

    dirs = ((qg_f, kd_f, kc_f, wv_f, at_f, eg_f), (qg_b, kd_b, kc_b, wv_b, at_b, eg_b))
    chains = [(dr, hd) for dr in range(2) for hd in range(GDN_HEADS)]
    ls = lambda hd: slice(hd * 128, (hd + 1) * 128)
    s = s_scr[...].reshape(2 * GDN_HEADS, GDN_DK, GDN_DV)
    for q in range(cps):
        chunk = (q, cps - 1 - q)
        rs = lambda dr: slice(chunk[dr] * c, (chunk[dr] + 1) * c)
        stack = lambda f: jnp.stack([f(dirs[dr], rs(dr), ls(hd), chunk[dr]) for dr, hd in chains], axis=0)
        kq = stack(lambda r, t, l, n: jnp.concatenate([r[2][t, l], r[0][t, l]], axis=0))
        wv = stack(lambda r, t, l, n: r[3][t, l])
        at = stack(lambda r, t, l, n: r[4][t, l][:, :c])
        eg = stack(lambda r, t, l, n: r[5][n, 0:1, l])
        r = _bmm(kq, s.astype(BF16))
        vb = (wv - r[:, :c]).astype(BF16)
        o = r[:, c:] + _bmm(at, vb)
        upd = []
        for g, (dr, hd) in enumerate(chains):
            (of_ref, ob_ref)[dr][rs(dr), ls(hd)] = o[g]
            upd.append(_dot_tn(dirs[dr][1][rs(dr), ls(hd)], vb[g]))
        s = s * eg + jnp.stack(upd, axis=0)
    s_scr[...] = s.reshape(2, GDN_HEADS, GDN_DK, GDN_DV)
    sfin_ref[...] = s_scr[...]


def _gdn_scan(prep, s0, *, cps):
    qg, kd, kc, wv, at, eg = prep
    _, b, t, hw = qg.shape
    c = GDN_CHUNK
    nblk = t // (c * cps)
    fw = lambda: pl.BlockSpec((None, None, cps * c, hw), lambda i, j: (0, i, j, 0))
    bw = lambda: pl.BlockSpec((None, None, cps * c, hw), lambda i, j: (1, i, nblk - 1 - j, 0))
    egf = pl.BlockSpec((None, None, cps, 8, hw), lambda i, j: (0, i, j, 0, 0))
    egb = pl.BlockSpec((None, None, cps, 8, hw), lambda i, j: (1, i, nblk - 1 - j, 0, 0))
    st = pl.BlockSpec((None, 2, GDN_HEADS, GDN_DK, GDN_DV), lambda i, j: (i, 0, 0, 0, 0))
    o_f, o_b, s_fin = pl.pallas_call(
        functools.partial(_gdn_scan_kernel, cps=cps),
        grid=(b, nblk),
        in_specs=[fw(), fw(), fw(), fw(), fw(), egf, bw(), bw(), bw(), bw(), bw(), egb, st],
        out_specs=[pl.BlockSpec((None, cps * c, hw), lambda i, j: (i, j, 0)),
                   pl.BlockSpec((None, cps * c, hw), lambda i, j: (i, nblk - 1 - j, 0)),
                   st],
        out_shape=[jax.ShapeDtypeStruct((b, t, hw), F32), jax.ShapeDtypeStruct((b, t, hw), F32),
                   jax.ShapeDtypeStruct((b, 2, GDN_HEADS, GDN_DK, GDN_DV), F32)],
        scratch_shapes=[pltpu.VMEM((2, GDN_HEADS, GDN_DK, GDN_DV), F32)],
        compiler_params=_cparams(("parallel", "arbitrary")),
    )(qg, kd, kc, wv, at, eg, qg, kd, kc, wv, at, eg, s0)
    return o_f, o_b, s_fin


def _out_kernel(x_ref, yrg_ref, of_ref, ob_ref, z_ref, ng_ref, w_ref, gt_ref, o_ref, *, ncol):
    o = of_ref[...] + ob_ref[...]
    z = z_ref[...]
    parts = [jnp.concatenate([yrg_ref[:, j, :] for j in range(ncol)], axis=0)]
    for hd in range(GDN_HEADS):
        ls = slice(hd * GDN_DV, (hd + 1) * GDN_DV)
        parts.append(_rms(o[:, ls], ng_ref[...]) * _silu(z[:, ls]))
    y = jnp.concatenate(parts, axis=1).astype(BF16)
    t2 = gt_ref[...] * _dot(y, w_ref[...])
    rows = t2.shape[0] // ncol
    for j in range(ncol):
        o_ref[:, j, :] = x_ref[:, j, :] + t2[j * rows:(j + 1) * rows]


def _out_project(x3, y_rg, o_f, o_b, z, norm_g, w_out, gt, *, tile):
    b, t, d = x3.shape
    w = RG_WIDTH
    rows = t // GRID_W
    ncol = tile // rows
    hw = GDN_WIDTH
    grid_view = lambda wd: pl.BlockSpec((None, rows, ncol, wd), lambda i, j: (i, 0, j, 0))
    cm = lambda wd: pl.BlockSpec((None, tile, wd), lambda i, j: (i, j, 0))
    x1 = pl.pallas_call(
        functools.partial(_out_kernel, ncol=ncol),
        grid=(b, GRID_W // ncol),
        in_specs=[grid_view(d), grid_view(w), cm(hw), cm(hw), cm(hw),
                  pl.BlockSpec((1, GDN_DV), lambda i, j: (0, 0)),
                  pl.BlockSpec((w + hw, d), lambda i, j: (0, 0)),
                  pl.BlockSpec((None, 1, d), lambda i, j: (i, 0, 0))],
        out_specs=grid_view(d),
        out_shape=jax.ShapeDtypeStruct((b, rows, GRID_W, d), F32),
        compiler_params=_cparams(("parallel", "parallel")),
    )(x3.reshape(b, rows, GRID_W, d), y_rg.reshape(b, rows, GRID_W, w), o_f, o_b, z, norm_g, w_out, gt)
    return x1.reshape(b, t, d)


def _fold_kernel(wq_ref, k_ref, o_ref):
    o_ref[...] = lax.dot_general(wq_ref[...], k_ref[...], (((1,), (1,)), ((), ())),
                                 preferred_element_type=F32, precision=lax.Precision.HIGHEST)


def _fold_keys(wq, keys):
    d, n = wq.shape
    nblk = n // PEER_HALF
    return pl.pallas_call(
        _fold_kernel,
        grid=(nblk,),
        in_specs=[pl.BlockSpec((d, PEER_HALF), lambda j: (0, j)),
                  pl.BlockSpec((None, PEER_NKEYS, PEER_HALF), lambda j: (j % 2, 0, 0))],
        out_specs=pl.BlockSpec((d, PEER_NKEYS), lambda j: (0, j)),
        out_shape=jax.ShapeDtypeStruct((d, nblk * PEER_NKEYS), F32),
        compiler_params=_cparams(("parallel",)),
    )(wq, keys)


def _extract16(vals, ids, s_scr, i_scr, *, fast, n_masked=0):
    n, p = vals.shape
    for r in range(PEER_TOPK):
        m = jnp.max(vals, axis=0, keepdims=True)
        eq = vals == m
        if fast:
            idx = _dot(ids, jnp.where(eq, 1.0, 0.0).astype(BF16))[0:1, :].astype(jnp.int32)
            vals = jnp.where(eq, NEG_INF, vals)
        else:
            idx = jnp.min(jnp.where(eq, ids, 1 << 20), axis=0, keepdims=True)
            vals = jnp.where(ids == idx, NEG_INF, vals)
        s_scr[r:r + 1, :] = m
        i_scr[r:r + 1, :] = idx
    if not fast:
        return None
    dropped = jnp.sum(jnp.where(vals == NEG_INF, 1.0, 0.0), axis=0, keepdims=True)
    return jnp.where(dropped == float(PEER_TOPK + n_masked), 0.0, 1.0)


def _peer_sel_kernel(x_ref, g_ref, sh_ref, sc_ref, wt_ref, h_ref, a_ref, b_ref, gate_ref,
                     sc_scr, s1, i1, s2, i2, ts, tp, a_t, b_t, g_t):
    k = PEER_TOPK
    h = _rms(x_ref[...], g_ref[...]) * (1.0 + sc_ref[...]) + sh_ref[...]
    hb = h.astype(BF16)
    h_ref[...] = hb
    sc_scr[...] = _dot_nt(wt_ref[...], hb)
    p = hb.shape[0]

    n_cand = 80
    n_masked = sum(8 - k // (r + 1) for r in range(1, 8))

    def cand_pos(row):
        return jnp.where(row < 16, row, jnp.where(row < 72, ((row - 16) // 8 + 1) * k + (row - 16) % 8,
                                                  (row - 72 + 8) * k))

    def head(hd, fast):
        if fast:
            key_ids = lax.broadcasted_iota(jnp.int32, (8, PEER_NKEYS), 1).astype(F32).astype(BF16)
            pos_ids = cand_pos(lax.broadcasted_iota(jnp.int32, (8, n_cand), 1)).astype(F32).astype(BF16)
        else:
            key_ids = lax.broadcasted_iota(jnp.int32, (PEER_NKEYS, p), 0)
            pos_ids = cand_pos(lax.broadcasted_iota(jnp.int32, (n_cand, p), 0))
        sub = lax.broadcasted_iota(jnp.int32, (8, p), 0)
        base = pl.multiple_of(hd * 2 * PEER_NKEYS, 2 * PEER_NKEYS)
        bad1 = _extract16(sc_scr[pl.ds(base, PEER_NKEYS), :], key_ids, s1, i1, fast=fast)
        bad2 = _extract16(sc_scr[pl.ds(base + PEER_NKEYS, PEER_NKEYS), :], key_ids, s2, i2, fast=fast)
        s1v, s2v = s1[...], s2[...]
        cands = [s1v[0:1, :] + s2v]
        for r in range(1, 8):
            cands.append(jnp.where(sub < k // (r + 1), s1v[r:r + 1, :] + s2v[0:8, :], NEG_INF))
        cands.append(s1v[8:16, :] + s2v[0:1, :])
        bad3 = _extract16(jnp.concatenate(cands, axis=0), pos_ids, ts, tp, fast=fast, n_masked=n_masked)
        top_s, top_p = ts[...], tp[...]
        rr = top_p >> 4
        cc = top_p & (k - 1)
        i1v, i2v = i1[...], i2[...]
        av = jnp.zeros_like(top_p)
        bv = jnp.zeros_like(top_p)
        for q in range(k):
            av = jnp.where(rr == q, i1v[q:q + 1, :], av)
            bv = jnp.where(cc == q, i2v[q:q + 1, :], bv)
        e = jnp.exp(top_s - top_s[0:1, :])
        gate = e / jnp.sum(e, axis=0, keepdims=True)
        o = pl.multiple_of(hd * k, k)
        a_t[pl.ds(o, k), :] = av
        b_t[pl.ds(o, k), :] = bv
        g_t[pl.ds(o, k), :] = gate
        return (bad1 + bad2 + bad3) if fast else None

    def head_step(hd, carry):
        bad = head(hd, True)

        @pl.when(jnp.max(bad) > 0.0)
        def _():
            head(hd, False)

        return carry

    lax.fori_loop(0, PEER_HEADS, head_step, 0)

    a_ref[...] = a_t[...].T
    b_ref[...] = b_t[...].T
    gate_ref[...] = g_t[...].T


def _peer_select(x1, g, sh, sc, wt, *, tile):
    b, t, d = x1.shape
    nt = t // tile
    nk = PEER_HEADS * PEER_TOPK
    nrow = wt.shape[0]
    vec = lambda: pl.BlockSpec((None, 1, d), lambda i, j: (i, 0, 0))
    tr = lambda: pl.BlockSpec((None, tile, nk), lambda i, j: (i, j, 0))
    k = PEER_TOPK
    return pl.pallas_call(
        _peer_sel_kernel,
        grid=(b, nt),
        in_specs=[pl.BlockSpec((None, tile, d), lambda i, j: (i, j, 0)),
                  pl.BlockSpec((1, d), lambda i, j: (0, 0)), vec(), vec(),
                  pl.BlockSpec((nrow, d), lambda i, j: (0, 0))],
        out_specs=[pl.BlockSpec((None, tile, d), lambda i, j: (i, j, 0)), tr(), tr(), tr()],
        out_shape=[jax.ShapeDtypeStruct((b, t, d), BF16),
                   jax.ShapeDtypeStruct((b, t, nk), jnp.int32),
                   jax.ShapeDtypeStruct((b, t, nk), jnp.int32),
                   jax.ShapeDtypeStruct((b, t, nk), F32)],
        scratch_shapes=[pltpu.VMEM((nrow, tile), F32),
                        pltpu.VMEM((k, tile), F32), pltpu.VMEM((k, tile), jnp.int32),
                        pltpu.VMEM((k, tile), F32), pltpu.VMEM((k, tile), jnp.int32),
                        pltpu.VMEM((k, tile), F32), pltpu.VMEM((k, tile), jnp.int32),
                        pltpu.VMEM((nk, tile), jnp.int32), pltpu.VMEM((nk, tile), jnp.int32),
                        pltpu.VMEM((nk, tile), F32)],
        compiler_params=_cparams(("parallel", "parallel")),
    )(x1, g.reshape(1, d), sh, sc, wt)


def _peer_mix_kernel(h_ref, a_ref, b_ref, gate_ref, ut_ref, v_ref, x_ref, gt_ref, fg_ref, o_ref,
                     m_scr, acc, *, tile, n_steps, pairs):
    nk = PEER_NKEYS
    half = nk // 2
    step = pl.program_id(2)
    hi_mask = jnp.uint32(0xFFFF0000)

    @pl.when(step == 0)
    def _():
        sub = lax.broadcasted_iota(jnp.int32, (nk, a_ref.shape[1]), 0)
        a_of_row = jnp.where(sub < half, 2 * sub, 2 * (sub - half) + 1)

        def build(p, carry):
            arow = a_ref[pl.ds(p, 1), :]
            brow = b_ref[pl.ds(p, 1), :]
            grow = gate_ref[pl.ds(p, 1), :]
            xa = jnp.where(a_of_row == arow, 1.0, 0.0).astype(BF16)
            yb = jnp.where(sub == brow, 0.5 * grow, 0.0).astype(BF16)
            m = _dot_nt(xa, yb).astype(BF16).astype(F32)
            bits = lax.bitcast_convert_type(m, jnp.uint32)
            m_scr[pl.ds(pl.multiple_of(p * MASK_PITCH, 8), half), :] = (bits[half:] & hi_mask) | (bits[:half] >> 16)
            return carry

        lax.fori_loop(0, tile, build, 0, unroll=64)

    h = h_ref[...]
    parts = []
    for q in range(pairs):
        pr = step * pairs + q
        act = _gelu_times_2(_dot(h, ut_ref[:, q * 2 * nk:(q + 1) * 2 * nk]))
        w = m_scr[pl.ds(pr, tile, stride=MASK_PITCH), :]
        m_even = lax.bitcast_convert_type(w << 16, F32)
        m_odd = lax.bitcast_convert_type(w & hi_mask, F32)
        parts.append((act * jnp.concatenate([m_even, m_odd], axis=1)).astype(BF16))
    contrib = _dot(jnp.concatenate(parts, axis=1), v_ref[...])

    @pl.when(step == 0)
    def _():
        acc[...] = contrib

    @pl.when(step > 0)
    def _():
        acc[...] += contrib

    @pl.when(step == n_steps - 1)
    def _():
        x2 = x_ref[...] + gt_ref[...] * acc[...]
        o_ref[...] = _rms(x2, fg_ref[...])


def _peer_mix(hb, aidx, bidx, gate, ut, v, x1, gt, final_g, *, tile, pairs):
    b, t, d = x1.shape
    nt = t // tile
    nk = PEER_NKEYS
    npk = aidx.shape[-1]
    n_steps = nk // (2 * pairs)
    eb = 2 * nk * pairs
    tok = lambda wd: pl.BlockSpec((None, tile, wd), lambda i, j, s: (i, j, 0))
    return pl.pallas_call(
        functools.partial(_peer_mix_kernel, tile=tile, n_steps=n_steps, pairs=pairs),
        grid=(b, nt, n_steps),
        in_specs=[tok(d), tok(npk), tok(npk), tok(npk),
                  pl.BlockSpec((d, eb), lambda i, j, s: (0, s)),
                  pl.BlockSpec((eb, d), lambda i, j, s: (s, 0)), tok(d),
                  pl.BlockSpec((None, 1, d), lambda i, j, s: (i, 0, 0)),
                  pl.BlockSpec((1, d), lambda i, j, s: (0, 0))],
        out_specs=tok(d),
        out_shape=jax.ShapeDtypeStruct((b, t, d), F32),
        scratch_shapes=[pltpu.VMEM((tile * MASK_PITCH, nk), jnp.uint32), pltpu.VMEM((tile, d), F32)],
        compiler_params=_cparams(("parallel", "parallel", "arbitrary")),
    )(hb, aidx, bidx, gate, ut, v, x1, gt, final_g.reshape(1, d))


def _block_diag(w):
    n, e, _ = w.shape
    eye = jnp.eye(n, dtype=w.dtype)
    return (eye[:, None, :, None] * w[:, :, None, :]).reshape(n * e, n * e)


def _mix_sequence(x3, ctx_mode, params, states, mods, *, rg_tb, gdn_tb, proj_tile):
    (norm1_g, w_rg, w_gdn, rg_conv_w, rg_conv_b, wg, gate_b, c_lam, gdn_conv_w, ea_row, dtb_row) = params
    sh1, sc1, mod_row = mods
    rg_h0_f, rg_h0_b, gdn_s0 = states
    p_rg, qkv, z, ab = _project(x3, norm1_g, sh1, sc1, mod_row, jnp.concatenate([w_rg, w_gdn], axis=1),
                                (2 * RG_WIDTH, 3 * GDN_WIDTH, GDN_WIDTH, AB_PAD), colmajor=not ctx_mode, tile=proj_tile)
    h_f, st_f = _rglru_pass(p_rg, rg_conv_w, rg_conv_b, wg[0], gate_b[0], c_lam[0], rg_h0_f, None,
                            reverse=False, tb=rg_tb)
    y_rg, st_b = _rglru_pass(p_rg, rg_conv_w, rg_conv_b, wg[1], gate_b[1], c_lam[1], rg_h0_b, h_f,
                             reverse=True, tb=rg_tb)
    prep = _gdn_prep(qkv, ab, gdn_conv_w, ea_row, dtb_row, tb=gdn_tb)
    o_f, o_b, s_fin = _gdn_scan(prep, gdn_s0, cps=min(8, x3.shape[1] // GDN_CHUNK))
    return y_rg, o_f, o_b, z, (st_f, st_b, s_fin)


def kernel(x, c, ctx, c_ctx, w_mod, b_mod, norm1_g, norm2_g, w_in, rg_conv_w, rg_conv_b, rg_gate_w, rg_gate_b,
           rg_lambda, gdn_conv_w, gdn_a_log, gdn_dt_bias, gdn_norm_g, w_out, peer_wq, peer_keys, peer_u, peer_v,
           final_g):
    b, t, d = x.shape
    depth = w_mod.shape[0]
    assert depth == 1, "context residual stream update is only needed for depth > 1"
    l = 0
    w = RG_WIDTH

    cc = jnp.zeros((16, d), F32).at[:b].set(c).at[b].set(c_ctx)
    w_rg = w_in[l][:, :2 * w].astype(BF16)
    n_ab = w_in.shape[2] - 2 * w - 4 * GDN_WIDTH
    w_gdn = jnp.concatenate([w_in[l][:, 2 * w:2 * w + 4 * GDN_WIDTH],
                             jnp.pad(w_in[l][:, 2 * w + 4 * GDN_WIDTH:], ((0, 0), (0, AB_PAD - n_ab)))],
                            axis=1).astype(BF16)
    wg = jnp.stack([jnp.concatenate([_block_diag(rg_gate_w[l, dr, 0]), _block_diag(rg_gate_w[l, dr, 1])], axis=1)
                    for dr in range(2)]).astype(BF16)
    gate_b = rg_gate_b[l].reshape(2, 1, 2 * w)
    c_lam = (-RG_C * jax.nn.softplus(-rg_lambda[l])).reshape(2, 1, w)
    ea = jnp.exp(gdn_a_log[l])
    pad4 = jnp.zeros((2, GDN_HEADS), F32)
    ea_row = jnp.pad(jnp.concatenate([ea, pad4], axis=1).reshape(1, -1), ((0, 0), (0, AB_PAD - n_ab)))
    dtb_row = jnp.pad(jnp.concatenate([gdn_dt_bias[l], pad4], axis=1).reshape(1, -1), ((0, 0), (0, AB_PAD - n_ab)))
    params = (norm1_g[l], w_rg, w_gdn, rg_conv_w[l], rg_conv_b[l].reshape(1, w), wg, gate_b, c_lam,
              gdn_conv_w[l], ea_row, dtb_row)

    mod = _modulation(cc, w_mod[l], b_mod[l])
    sh1, sc1, gt1, sh2, sc2, gt2 = [mod[:, i * d:(i + 1) * d].reshape(16, 1, d) for i in range(6)]

    tc = ctx.shape[1]
    zero_states = (jnp.zeros((b, 1, w), F32), jnp.zeros((b, 1, w), F32),
                   jnp.zeros((b, 2, GDN_HEADS, GDN_DK, GDN_DV), F32))
    ctx_tb = min(tc, 256)
    _, _, _, _, ctx_states = _mix_sequence(ctx, True, params, zero_states, (sh1, sc1, lambda i: b),
                                           rg_tb=ctx_tb, gdn_tb=ctx_tb, proj_tile=ctx_tb)

    tile = min(t, 512)
    y_rg, o_f, o_b, z, _ = _mix_sequence(x, False, params, ctx_states, (sh1, sc1, lambda i: i),
                                         rg_tb=min(t, 256), gdn_tb=min(t, 256), proj_tile=tile)
    x1 = _out_project(x, y_rg, o_f, o_b, z, gdn_norm_g[l].reshape(1, GDN_DV), w_out[l].astype(BF16), gt1, tile=tile)

    wfold_t = _fold_keys(peer_wq[l], peer_keys[l]).T.astype(BF16)
    sel_tile = min(t, 512)
    hb, aidx, bidx, gate = _peer_select(x1, norm2_g[l], sh2, sc2, wfold_t, tile=sel_tile)
    out = _peer_mix(hb, aidx, bidx, gate, peer_u[l].astype(BF16).T, peer_v[l].astype(BF16),
                    x1, gt2, final_g, tile=tile, pairs=8)
    return out
```

```python
import functools
import math

import jax
import jax.numpy as jnp
from jax import lax
from jax.experimental import pallas as pl
from jax.experimental.pallas import tpu as pltpu

F32 = jnp.float32
BF16 = jnp.bfloat16

GRID_W = 64
EPS = 1e-6
RG_WIDTH = 512
RG_C = 8.0
GDN_HEADS = 4
GDN_DK = 128
GDN_DV = 128
GDN_WIDTH = GDN_HEADS * GDN_DV
GDN_CHUNK = 64
GDN_SUB = 16
AB_PAD = 128
PEER_HEADS = 8
PEER_NKEYS = 128
PEER_HALF = 128
PEER_TOPK = 16
NEG_INF = float("-inf")
MASK_PITCH = 72

VMEM_LIMIT = 58 * 1024 * 1024

TOKEN_TILE = 512
RG_BLOCK = 512
GDN_BLOCK = 256
GDN_SCAN_CHUNKS = 8
MIX_PAIRS = 8


def _cparams(sem):
    return pltpu.CompilerParams(dimension_semantics=sem, vmem_limit_bytes=VMEM_LIMIT)


def _dot(a, b):
    return jnp.dot(a, b, preferred_element_type=F32)


def _dot_nt(a, b):
    return lax.dot_general(a, b, (((1,), (1,)), ((), ())), preferred_element_type=F32)


def _dot_tn(a, b):
    return lax.dot_general(a, b, (((0,), (0,)), ((), ())), preferred_element_type=F32)


def _split3(x):
    hi = x.astype(BF16)
    r = x - hi.astype(F32)
    mid = r.astype(BF16)
    lo = (r - mid.astype(F32)).astype(BF16)
    return hi, mid, lo


def _dot_sel(m, x):
    hi, mid, lo = _split3(x)
    return _dot(m, lo) + _dot(m, mid) + _dot(m, hi)


def _silu(x):
    return x * jax.nn.sigmoid(x)


def _softplus(x):
    return jnp.maximum(x, 0.0) + jnp.log1p(jnp.exp(-jnp.abs(x)))


def _gelu_times_2(x):
    c = math.sqrt(2.0 / math.pi)
    return x * (1.0 + jnp.tanh(x * (c + (c * 0.044715) * (x * x))))


def _rms(x, g):
    return x * lax.rsqrt(jnp.mean(x * x, axis=-1, keepdims=True) + EPS) * g


def _mod_kernel(c_ref, w_ref, b_ref, o_ref):
    s = _silu(c_ref[...])
    o_ref[...] = jnp.dot(s, w_ref[...], preferred_element_type=F32,
                         precision=lax.Precision.HIGHEST) + b_ref[...]


def _modulation(cc, w_mod, b_mod):
    m, d = cc.shape
    n = w_mod.shape[1]
    tn = 1536
    return pl.pallas_call(
        _mod_kernel,
        grid=(n // tn,),
        in_specs=[pl.BlockSpec((m, d), lambda j: (0, 0)),
                  pl.BlockSpec((d, tn), lambda j: (0, j)),
                  pl.BlockSpec((1, tn), lambda j: (0, j))],
        out_specs=pl.BlockSpec((m, tn), lambda j: (0, j)),
        out_shape=jax.ShapeDtypeStruct((m, n), F32),
        compiler_params=_cparams(("arbitrary",)),
    )(cc, w_mod, b_mod.reshape(1, n))


def _proj_kernel(x_ref, g_ref, sh_ref, sc_ref, w_ref, *o_refs, ncol, widths):
    if ncol > 1:
        x = jnp.concatenate([x_ref[:, j, :] for j in range(ncol)], axis=0)
    else:
        x = x_ref[...]
    h = _rms(x, g_ref[...]) * (1.0 + sc_ref[...]) + sh_ref[...]
    o = _dot(h.astype(BF16), w_ref[...])
    off = 0
    for n_out, (o_ref, wd) in enumerate(zip(o_refs, widths)):
        if ncol > 1 and n_out == 0:
            rows = o.shape[0] // ncol
            for j in range(ncol):
                o_ref[:, j, :] = o[j * rows:(j + 1) * rows, off:off + wd]
        else:
            o_ref[...] = o[:, off:off + wd]
        off += wd


def _project(x3, g, sh, sc, mod_row, w, widths, *, colmajor, tile):
    b, t, d = x3.shape
    n = w.shape[1]
    flat = lambda wd: pl.BlockSpec((None, tile, wd), lambda i, j: (i, j, 0))
    if colmajor:
        rows = t // GRID_W
        ncol = tile // rows
        grid_view = lambda wd: pl.BlockSpec((None, rows, ncol, wd), lambda i, j: (i, 0, j, 0))
        xv = x3.reshape(b, rows, GRID_W, d)
        x_spec = grid_view(d)
        nt = GRID_W // ncol
        out_specs = [grid_view(widths[0])] + [flat(wd) for wd in widths[1:]]
        out_shape = ([jax.ShapeDtypeStruct((b, rows, GRID_W, widths[0]), F32)]
                     + [jax.ShapeDtypeStruct((b, t, wd), F32) for wd in widths[1:]])
    else:
        ncol = 1
        xv = x3
        x_spec = flat(d)
        nt = t // tile
        out_specs = [flat(wd) for wd in widths]
        out_shape = [jax.ShapeDtypeStruct((b, t, wd), F32) for wd in widths]
    vec = lambda: pl.BlockSpec((None, 1, d), lambda i, j: (mod_row(i), 0, 0))
    outs = pl.pallas_call(
        functools.partial(_proj_kernel, ncol=ncol, widths=tuple(widths)),
        grid=(b, nt),
        in_specs=[x_spec, pl.BlockSpec((1, d), lambda i, j: (0, 0)), vec(), vec(),
                  pl.BlockSpec((d, n), lambda i, j: (0, 0))],
        out_specs=out_specs,
        out_shape=out_shape,
        compiler_params=_cparams(("parallel", "parallel")),
    )(xv, g.reshape(1, d), sh, sc, w)
    return [outs[0].reshape(b, t, widths[0])] + list(outs[1:])


def _conv4(cur, prev8, nxt8, w_ref, first, last):
    tb, c = cur.shape
    g = tb // 8
    prev8 = jnp.where(first, 0.0, prev8)
    nxt8 = jnp.where(last, 0.0, nxt8)
    ext = jnp.concatenate([prev8, cur, nxt8], axis=0).reshape(g + 2, 8, c)
    sub = lax.broadcasted_iota(jnp.int32, (g, 8, c), 1)

    def back(k):
        r = pltpu.roll(ext, k, 1)
        return jnp.where(sub >= k, r[1:g + 1], r[0:g])

    r = pltpu.roll(ext, 7, 1)
    xp1 = jnp.where(sub < 7, r[1:g + 1], r[2:g + 2])
    y = (w_ref[0:1, :] * back(2) + w_ref[1:2, :] * back(1) + w_ref[2:3, :] * ext[1:g + 1] + w_ref[3:4, :] * xp1)
    return y.reshape(tb, c)


def _halo_specs(tb, t, width, lane_blk, tmap):
    r = tb // 8
    nb8 = t // 8
    return [pl.BlockSpec((None, tb, width), lambda i, j: (i, tmap(j), lane_blk)),
            pl.BlockSpec((None, 8, width), lambda i, j: (i, jnp.maximum(tmap(j) * r - 1, 0), lane_blk)),
            pl.BlockSpec((None, 8, width), lambda i, j: (i, jnp.minimum((tmap(j) + 1) * r, nb8 - 1), lane_blk))]


def _rglru_kernel(*refs, tb, nt, reverse):
    if reverse:
        (u_ref, up_ref, un_ref, cw_ref, cb_ref, wg_ref, gb_ref, cl_ref, h0_ref, gate_ref, hf_ref,
         y_ref, st_ref, carry) = refs
    else:
        (u_ref, up_ref, un_ref, cw_ref, cb_ref, wg_ref, gb_ref, cl_ref, h0_ref,
         y_ref, st_ref, carry) = refs
    j = pl.program_id(1)
    tblk = (nt - 1 - j) if reverse else j
    w = u_ref.shape[-1]

    @pl.when(j == 0)
    def _():
        carry[...] = h0_ref[...]

    u = u_ref[...]
    xc = _conv4(u, up_ref[...], un_ref[...], cw_ref, tblk == 0, tblk == nt - 1) + cb_ref[...]
    gates = _dot(xc.astype(BF16), wg_ref[...]) + gb_ref[...]
    r = jax.nn.sigmoid(gates[:, :w])
    ig = jax.nn.sigmoid(gates[:, w:])
    log_a = r * cl_ref[...]
    a = jnp.exp(log_a)
    th = jnp.tanh(log_a)
    bb = jnp.sqrt(-2.0 * th / (1.0 - th)) * (ig * xc)

    n_grp = tb // 8
    a = a.reshape(n_grp, 8, w)
    bb = bb.reshape(n_grp, 8, w)
    sub = lax.broadcasted_iota(jnp.int32, (n_grp, 8, w), 1)
    s = 1
    while s < 8:
        shift, ok = (8 - s, sub < 8 - s) if reverse else (s, sub >= s)
        a_sh = jnp.where(ok, pltpu.roll(a, shift, 1), 1.0)
        b_sh = jnp.where(ok, pltpu.roll(bb, shift, 1), 0.0)
        bb = a * b_sh + bb
        a = a * a_sh
        s *= 2
    h_prev = carry[...]
    hs = [None] * n_grp
    for g in (range(n_grp - 1, -1, -1) if reverse else range(n_grp)):
        hg = a[g] * h_prev + bb[g]
        h_prev = hg[0:1, :] if reverse else hg[7:8, :]
        hs[g] = hg
    h = jnp.concatenate(hs, axis=0)
    carry[...] = h_prev
    st_ref[...] = carry[...]
    if reverse:
        y_ref[...] = ((hf_ref[...] + h) * jax.nn.gelu(gate_ref[...])).astype(y_ref.dtype)
    else:
        y_ref[...] = h


def _rglru_pass(p_rg, conv_w, conv_b, wg, gate_b, c_lam, h0, hf, *, reverse, tb):
    b, t, w2 = p_rg.shape
    w = w2 // 2
    nt = t // tb
    tmap = (lambda j: nt - 1 - j) if reverse else (lambda j: j)
    const = lambda shape: pl.BlockSpec(shape, lambda i, j: (0,) * len(shape))
    in_specs = _halo_specs(tb, t, w, 0, tmap) + [
        const((4, w)), const((1, w)), const((w, 2 * w)), const((1, 2 * w)), const((1, w)),
        pl.BlockSpec((None, 1, w), lambda i, j: (i, 0, 0))]
    args = [p_rg, p_rg, p_rg, conv_w, conv_b, wg, gate_b, c_lam, h0]
    if reverse:
        in_specs += [pl.BlockSpec((None, tb, w), lambda i, j: (i, tmap(j), 1)),
                     pl.BlockSpec((None, tb, w), lambda i, j: (i, tmap(j), 0))]
        args += [p_rg, hf]
    y, st = pl.pallas_call(
        functools.partial(_rglru_kernel, tb=tb, nt=nt, reverse=reverse),
        grid=(b, nt),
        in_specs=in_specs,
        out_specs=[pl.BlockSpec((None, tb, w), lambda i, j: (i, tmap(j), 0)),
                   pl.BlockSpec((None, 1, w), lambda i, j: (i, 0, 0))],
        out_shape=[jax.ShapeDtypeStruct((b, t, w), F32),
                   jax.ShapeDtypeStruct((b, 1, w), F32)],
        scratch_shapes=[pltpu.VMEM((1, w), F32)],
        compiler_params=_cparams(("parallel", "arbitrary")),
    )(*args)
    return y, st


def _bmm(a, b):
    return lax.dot_general(a, b, (((2,), (1,)), ((0,), (0,))), preferred_element_type=F32)


def _bmm_nt(a, b):
    return lax.dot_general(a, b, (((2,), (2,)), ((0,), (0,))), preferred_element_type=F32)


def _unit_tri_inverse(a, diag_mask):
    c = a.shape[-1]
    eye = (lax.broadcasted_iota(jnp.int32, (1, c, c), 1) == lax.broadcasted_iota(jnp.int32, (1, c, c), 2)).astype(F32)
    mm = lambda p, q: _bmm(p.astype(BF16), q.astype(BF16))
    ad = jnp.where(diag_mask, a, 0.0)
    x = eye - ad
    pw = ad
    k = 2
    while k < GDN_SUB:
        pw = mm(pw, pw)
        x = x + mm(x, pw)
        k *= 2
    n = mm(x, a - ad)
    nblk = c // GDN_SUB
    y = eye - n
    pw = n
    k = 2
    while k < nblk:
        pw = mm(pw, pw)
        y = y + mm(y, pw)
        k *= 2
    return mm(y, x)


def _gdn_prep_kernel(qkv_ref, qp_ref, qn_ref, ab_ref, cw_ref, ea_ref, dtb_ref,
                     qg_ref, kd_ref, kc_ref, wv_ref, at_ref, eg_ref, *, tb, nt):
    c = GDN_CHUNK
    j = pl.program_id(1)
    hw = GDN_WIDTH
    x = _silu(_conv4(qkv_ref[...], qp_ref[...], qn_ref[...], cw_ref, j == 0, j == nt - 1))

    ab = ab_ref[...]
    col = lax.broadcasted_iota(jnp.int32, (tb, AB_PAD), 1)
    is_a = (col & 4) == 0
    gbv = jnp.where(is_a, -ea_ref[...] * _softplus(ab + dtb_ref[...]), jax.nn.sigmoid(ab))

    ri = lax.broadcasted_iota(jnp.int32, (tb, tb), 0)
    ci = lax.broadcasted_iota(jnp.int32, (tb, tb), 1)
    same = (ri // c) == (ci // c)
    l_f = (same & (ci <= ri)).astype(BF16)
    l_b = (same & (ci >= ri)).astype(BF16)
    l_t = same.astype(BF16)
    gcum = jnp.where(col < 8, _dot_sel(l_f, gbv), _dot_sel(l_b, gbv))
    gtot = _dot_sel(l_t, gbv)

    cols_a = [dr * 8 + hd for dr in range(2) for hd in range(GDN_HEADS)]
    bcast = lambda v, cols: jnp.concatenate([jnp.broadcast_to(v[:, cc:cc + 1], (tb, 128)) for cc in cols], axis=1)
    gc_all = bcast(gcum, cols_a)
    gt_all = bcast(gtot, cols_a)
    be_all = bcast(gbv, [cc + 4 for cc in cols_a])

    qs, ks, vs = [], [], []
    for hd in range(GDN_HEADS):
        q = x[:, hd * GDN_DK:(hd + 1) * GDN_DK]
        k = x[:, hw + hd * GDN_DK: hw + (hd + 1) * GDN_DK]
        qs.append(q * lax.rsqrt(jnp.sum(q * q, axis=-1, keepdims=True) + EPS) * (GDN_DK ** -0.5))
        ks.append(k * lax.rsqrt(jnp.sum(k * k, axis=-1, keepdims=True) + EPS))
        vs.append(x[:, 2 * hw + hd * GDN_DV: 2 * hw + (hd + 1) * GDN_DV])
    inst = [(ch, dr, hd) for ch in range(tb // c) for dr in range(2) for hd in range(GDN_HEADS)]
    n_inst = len(inst)
    rows = lambda ch: slice(ch * c, (ch + 1) * c)
    lanes = lambda dr, hd: slice((dr * GDN_HEADS + hd) * 128, (dr * GDN_HEADS + hd + 1) * 128)
    stack = lambda f: jnp.stack([f(ch, dr, hd) for ch, dr, hd in inst], axis=0)
    qq = stack(lambda ch, dr, hd: qs[hd][rows(ch)])
    kk = stack(lambda ch, dr, hd: ks[hd][rows(ch)])
    vv = stack(lambda ch, dr, hd: vs[hd][rows(ch)])
    gcb = stack(lambda ch, dr, hd: gc_all[rows(ch), lanes(dr, hd)])
    gtb = stack(lambda ch, dr, hd: gt_all[rows(ch), lanes(dr, hd)])
    beb = stack(lambda ch, dr, hd: be_all[rows(ch), lanes(dr, hd)])

    fwd = ((lax.broadcasted_iota(jnp.int32, (n_inst, c, 128), 0) // GDN_HEADS) % 2) == 0
    ii = lax.broadcasted_iota(jnp.int32, (n_inst, c, 128), 1)
    jj = lax.broadcasted_iota(jnp.int32, (n_inst, c, 128), 2)
    incl = (fwd & (jj <= ii)) | (jnp.logical_not(fwd) & (jj >= ii) & (jj < c))
    fwd_c = ((lax.broadcasted_iota(jnp.int32, (n_inst, c, c), 0) // GDN_HEADS) % 2) == 0
    i64 = lax.broadcasted_iota(jnp.int32, (n_inst, c, c), 1)
    j64 = lax.broadcasted_iota(jnp.int32, (n_inst, c, c), 2)
    strict = (fwd_c & (j64 < i64)) | (jnp.logical_not(fwd_c) & (j64 > i64))
    diag_blk = (i64 // GDN_SUB) == (j64 // GDN_SUB)

    eg = jnp.exp(gcb)
    kb = kk * beb
    diag = jnp.where(ii == jj, gcb, 0.0)
    gcr = _dot_sel(jnp.ones((c, c), BF16), jnp.concatenate([diag[g] for g in range(n_inst)], axis=1))
    gcr = jnp.stack([gcr[:, g * 128:(g + 1) * 128] for g in range(n_inst)], axis=0)
    decay = jnp.where(incl, jnp.exp(jnp.where(incl, gcb - gcr, 0.0)), 0.0)
    kpad = jnp.concatenate([kk.astype(BF16), jnp.zeros((n_inst, c, GDN_DK), BF16)], axis=1)
    qk = _bmm_nt(jnp.concatenate([qq, kb], axis=1).astype(BF16), kpad)
    attn = (qk[:, :c] * decay).astype(BF16)
    a_mat = jnp.where(strict, (qk[:, c:] * decay)[:, :, :c], 0.0)
    tinv = _unit_tri_inverse(a_mat, diag_blk)
    rhs = jnp.concatenate([vv * beb, kb * eg], axis=2)
    sol = _bmm(tinv.astype(BF16), rhs.astype(BF16))
    qg = (qq * eg).astype(BF16)
    kd = (kk * jnp.exp(gtb - gcb)).astype(BF16)
    egt = jnp.exp(gtb[:, 0:8, :])
    for g, (ch, dr, hd) in enumerate(inst):
        rs = rows(ch)
        ls = slice(hd * 128, (hd + 1) * 128)
        qg_ref[dr, rs, ls] = qg[g]
        kd_ref[dr, rs, ls] = kd[g]
        kc_ref[dr, rs, ls] = sol[g, :, GDN_DV:].astype(BF16)
        wv_ref[dr, rs, ls] = sol[g, :, :GDN_DV]
        at_ref[dr, rs, ls] = attn[g]
        eg_ref[dr, ch, :, ls] = egt[g]


def _gdn_prep(qkv, ab, conv_w, ea_row, dtb_row, *, tb):
    b, t, cw = qkv.shape
    nt = t // tb
    nch = t // GDN_CHUNK
    cpb = tb // GDN_CHUNK
    const = lambda shape: pl.BlockSpec(shape, lambda i, j: (0,) * len(shape))
    hw = GDN_WIDTH
    big = lambda: pl.BlockSpec((2, None, tb, hw), lambda i, j: (0, i, j, 0))
    outs = pl.pallas_call(
        functools.partial(_gdn_prep_kernel, tb=tb, nt=nt),
        grid=(b, nt),
        in_specs=_halo_specs(tb, t, cw, 0, lambda j: j) + [
            pl.BlockSpec((None, tb, AB_PAD), lambda i, j: (i, j, 0)),
            const((4, cw)), const((1, AB_PAD)), const((1, AB_PAD))],
        out_specs=[big(), big(), big(), big(), big(),
                   pl.BlockSpec((2, None, cpb, 8, hw), lambda i, j: (0, i, j, 0, 0))],
        out_shape=[jax.ShapeDtypeStruct((2, b, t, hw), BF16),
                   jax.ShapeDtypeStruct((2, b, t, hw), BF16),
                   jax.ShapeDtypeStruct((2, b, t, hw), BF16),
                   jax.ShapeDtypeStruct((2, b, t, hw), F32),
                   jax.ShapeDtypeStruct((2, b, t, hw), BF16),
                   jax.ShapeDtypeStruct((2, b, nch, 8, hw), F32)],
        compiler_params=_cparams(("parallel", "parallel")),
    )(qkv, qkv, qkv, ab, conv_w, ea_row, dtb_row)
    return outs


def _gdn_scan_kernel(qg_f, kd_f, kc_f, wv_f, at_f, eg_f, qg_b, kd_b, kc_b, wv_b, at_b, eg_b, s0_ref,
                     of_ref, ob_ref, sfin_ref, s_scr, *, cps):
    c = GDN_CHUNK
    j = pl.program_id(1)

    @pl.when(j == 0)
    def _():
        s_scr[...] = s0_ref[...]

    dirs = ((qg_f, kd_f, kc_f, wv_f, at_f, eg_f), (qg_b, kd_b, kc_b, wv_b, at_b, eg_b))
    chains = [(dr, hd) for dr in range(2) for hd in range(GDN_HEADS)]
    ls = lambda hd: slice(hd * 128, (hd + 1) * 128)
    s = s_scr[...].reshape(2 * GDN_HEADS, GDN_DK, GDN_DV)
    for q in range(cps):
        chunk = (q, cps - 1 - q)
        rs = lambda dr: slice(chunk[dr] * c, (chunk[dr] + 1) * c)
        stack = lambda f: jnp.stack([f(dirs[dr], rs(dr), ls(hd), chunk[dr]) for dr, hd in chains], axis=0)
        kq = stack(lambda r, t, l, n: jnp.concatenate([r[2][t, l], r[0][t, l]], axis=0))
        wv = stack(lambda r, t, l, n: r[3][t, l])
        at = stack(lambda r, t, l, n: r[4][t, l][:, :c])
        eg = stack(lambda r, t, l, n: r[5][n, 0:1, l])
        r = _bmm(kq, s.astype(BF16))
        vb = (wv - r[:, :c]).astype(BF16)
        o = r[:, c:] + _bmm(at, vb)
        upd = []
        for g, (dr, hd) in enumerate(chains):
            (of_ref, ob_ref)[dr][rs(dr), ls(hd)] = o[g]
            upd.append(_dot_tn(dirs[dr][1][rs(dr), ls(hd)], vb[g]))
        s = s * eg + jnp.stack(upd, axis=0)
    s_scr[...] = s.reshape(2, GDN_HEADS, GDN_DK, GDN_DV)
    sfin_ref[...] = s_scr[...]


def _gdn_scan(prep, s0, *, cps):
    qg, kd, kc, wv, at, eg = prep
    _, b, t, hw = qg.shape
    c = GDN_CHUNK
    nblk = t // (c * cps)
    fw = lambda: pl.BlockSpec((None, None, cps * c, hw), lambda i, j: (0, i, j, 0))
    bw = lambda: pl.BlockSpec((None, None, cps * c, hw), lambda i, j: (1, i, nblk - 1 - j, 0))
    egf = pl.BlockSpec((None, None, cps, 8, hw), lambda i, j: (0, i, j, 0, 0))
    egb = pl.BlockSpec((None, None, cps, 8, hw), lambda i, j: (1, i, nblk - 1 - j, 0, 0))
    st = pl.BlockSpec((None, 2, GDN_HEADS, GDN_DK, GDN_DV), lambda i, j: (i, 0, 0, 0, 0))
    o_f, o_b, s_fin = pl.pallas_call(
        functools.partial(_gdn_scan_kernel, cps=cps),
        grid=(b, nblk),
        in_specs=[fw(), fw(), fw(), fw(), fw(), egf, bw(), bw(), bw(), bw(), bw(), egb, st],
        out_specs=[pl.BlockSpec((None, cps * c, hw), lambda i, j: (i, j, 0)),
                   pl.BlockSpec((None, cps * c, hw), lambda i, j: (i, nblk - 1 - j, 0)),
                   st],
        out_shape=[jax.ShapeDtypeStruct((b, t, hw), F32), jax.ShapeDtypeStruct((b, t, hw), F32),
                   jax.ShapeDtypeStruct((b, 2, GDN_HEADS, GDN_DK, GDN_DV), F32)],
        scratch_shapes=[pltpu.VMEM((2, GDN_HEADS, GDN_DK, GDN_DV), F32)],
        compiler_params=_cparams(("parallel", "arbitrary")),
    )(qg, kd, kc, wv, at, eg, qg, kd, kc, wv, at, eg, s0)
    return o_f, o_b, s_fin


def _out_kernel(x_ref, yrg_ref, of_ref, ob_ref, z_ref, ng_ref, w_ref, gt_ref, o_ref, *, ncol):
    o = of_ref[...] + ob_ref[...]
    z = z_ref[...]
    parts = [jnp.concatenate([yrg_ref[:, j, :] for j in range(ncol)], axis=0)]
    for hd in range(GDN_HEADS):
        ls = slice(hd * GDN_DV, (hd + 1) * GDN_DV)
        parts.append(_rms(o[:, ls], ng_ref[...]) * _silu(z[:, ls]))
    y = jnp.concatenate(parts, axis=1).astype(BF16)
    t2 = gt_ref[...] * _dot(y, w_ref[...])
    rows = t2.shape[0] // ncol
    for j in range(ncol):
        o_ref[:, j, :] = x_ref[:, j, :] + t2[j * rows:(j + 1) * rows]


def _out_project(x3, y_rg, o_f, o_b, z, norm_g, w_out, gt, *, tile):
    b, t, d = x3.shape
    w = RG_WIDTH
    rows = t // GRID_W
    ncol = tile // rows
    hw = GDN_WIDTH
    grid_view = lambda wd: pl.BlockSpec((None, rows, ncol, wd), lambda i, j: (i, 0, j, 0))
    cm = lambda wd: pl.BlockSpec((None, tile, wd), lambda i, j: (i, j, 0))
    x1 = pl.pallas_call(
        functools.partial(_out_kernel, ncol=ncol),
        grid=(b, GRID_W // ncol),
        in_specs=[grid_view(d), grid_view(w), cm(hw), cm(hw), cm(hw),
                  pl.BlockSpec((1, GDN_DV), lambda i, j: (0, 0)),
                  pl.BlockSpec((w + hw, d), lambda i, j: (0, 0)),
                  pl.BlockSpec((None, 1, d), lambda i, j: (i, 0, 0))],
        out_specs=grid_view(d),
        out_shape=jax.ShapeDtypeStruct((b, rows, GRID_W, d), F32),
        compiler_params=_cparams(("parallel", "parallel")),
    )(x3.reshape(b, rows, GRID_W, d), y_rg.reshape(b, rows, GRID_W, w), o_f, o_b, z, norm_g, w_out, gt)
    return x1.reshape(b, t, d)


def _fold_kernel(wq_ref, k_ref, o_ref):
    o_ref[...] = lax.dot_general(wq_ref[...], k_ref[...], (((1,), (1,)), ((), ())),
                                 preferred_element_type=F32, precision=lax.Precision.HIGHEST)


def _fold_keys(wq, keys):
    d, n = wq.shape
    nblk = n // PEER_HALF
    return pl.pallas_call(
        _fold_kernel,
        grid=(nblk,),
        in_specs=[pl.BlockSpec((d, PEER_HALF), lambda j: (0, j)),
                  pl.BlockSpec((None, PEER_NKEYS, PEER_HALF), lambda j: (j % 2, 0, 0))],
        out_specs=pl.BlockSpec((d, PEER_NKEYS), lambda j: (0, j)),
        out_shape=jax.ShapeDtypeStruct((d, nblk * PEER_NKEYS), F32),
        compiler_params=_cparams(("parallel",)),
    )(wq, keys)


def _extract16(vals, ids, s_scr, i_scr, *, fast, n_masked=0):
    n, p = vals.shape
    for r in range(PEER_TOPK):
        m = jnp.max(vals, axis=0, keepdims=True)
        eq = vals == m
        if fast:
            idx = _dot(ids, jnp.where(eq, 1.0, 0.0).astype(BF16))[0:1, :].astype(jnp.int32)
            vals = jnp.where(eq, NEG_INF, vals)
        else:
            idx = jnp.min(jnp.where(eq, ids, 1 << 20), axis=0, keepdims=True)
            vals = jnp.where(ids == idx, NEG_INF, vals)
        s_scr[r:r + 1, :] = m
        i_scr[r:r + 1, :] = idx
    if not fast:
        return None
    dropped = jnp.sum(jnp.where(vals == NEG_INF, 1.0, 0.0), axis=0, keepdims=True)
    return jnp.where(dropped == float(PEER_TOPK + n_masked), 0.0, 1.0)


def _peer_sel_kernel(x_ref, g_ref, sh_ref, sc_ref, wt_ref, h_ref, a_ref, b_ref, gate_ref,
                     sc_scr, s1, i1, s2, i2, ts, tp, a_t, b_t, g_t):
    k = PEER_TOPK
    h = _rms(x_ref[...], g_ref[...]) * (1.0 + sc_ref[...]) + sh_ref[...]
    hb = h.astype(BF16)
    h_ref[...] = hb
    sc_scr[...] = _dot_nt(wt_ref[...], hb)
    p = hb.shape[0]

    n_cand = 80
    n_masked = sum(8 - k // (r + 1) for r in range(1, 8))

    def cand_pos(row):
        return jnp.where(row < 16, row, jnp.where(row < 72, ((row - 16) // 8 + 1) * k + (row - 16) % 8,
                                                  (row - 72 + 8) * k))

    def head(hd, fast):
        if fast:
            key_ids = lax.broadcasted_iota(jnp.int32, (8, PEER_NKEYS), 1).astype(F32).astype(BF16)
            pos_ids = cand_pos(lax.broadcasted_iota(jnp.int32, (8, n_cand), 1)).astype(F32).astype(BF16)
        else:
            key_ids = lax.broadcasted_iota(jnp.int32, (PEER_NKEYS, p), 0)
            pos_ids = cand_pos(lax.broadcasted_iota(jnp.int32, (n_cand, p), 0))
        sub = lax.broadcasted_iota(jnp.int32, (8, p), 0)
        base = pl.multiple_of(hd * 2 * PEER_NKEYS, 2 * PEER_NKEYS)
        bad1 = _extract16(sc_scr[pl.ds(base, PEER_NKEYS), :], key_ids, s1, i1, fast=fast)
        bad2 = _extract16(sc_scr[pl.ds(base + PEER_NKEYS, PEER_NKEYS), :], key_ids, s2, i2, fast=fast)
        s1v, s2v = s1[...], s2[...]
        cands = [s1v[0:1, :] + s2v]
        for r in range(1, 8):
            cands.append(jnp.where(sub < k // (r + 1), s1v[r:r + 1, :] + s2v[0:8, :], NEG_INF))
        cands.append(s1v[8:16, :] + s2v[0:1, :])
        bad3 = _extract16(jnp.concatenate(cands, axis=0), pos_ids, ts, tp, fast=fast, n_masked=n_masked)
        top_s, top_p = ts[...], tp[...]
        rr = top_p >> 4
        cc = top_p & (k - 1)
        i1v, i2v = i1[...], i2[...]
        av = jnp.zeros_like(top_p)
        bv = jnp.zeros_like(top_p)
        for q in range(k):
            av = jnp.where(rr == q, i1v[q:q + 1, :], av)
            bv = jnp.where(cc == q, i2v[q:q + 1, :], bv)
        e = jnp.exp(top_s - top_s[0:1, :])
        gate = e / jnp.sum(e, axis=0, keepdims=True)
        o = pl.multiple_of(hd * k, k)
        a_t[pl.ds(o, k), :] = av
        b_t[pl.ds(o, k), :] = bv
        g_t[pl.ds(o, k), :] = gate
        return (bad1 + bad2 + bad3) if fast else None

    def head_step(hd, carry):
        bad = head(hd, True)

        @pl.when(jnp.max(bad) > 0.0)
        def _():
            head(hd, False)

        return carry

    lax.fori_loop(0, PEER_HEADS, head_step, 0)

    a_ref[...] = a_t[...].T
    b_ref[...] = b_t[...].T
    gate_ref[...] = g_t[...].T


def _peer_select(x1, g, sh, sc, wt, *, tile):
    b, t, d = x1.shape
    nt = t // tile
    nk = PEER_HEADS * PEER_TOPK
    nrow = wt.shape[0]
    vec = lambda: pl.BlockSpec((None, 1, d), lambda i, j: (i, 0, 0))
    tr = lambda: pl.BlockSpec((None, tile, nk), lambda i, j: (i, j, 0))
    k = PEER_TOPK
    return pl.pallas_call(
        _peer_sel_kernel,
        grid=(b, nt),
        in_specs=[pl.BlockSpec((None, tile, d), lambda i, j: (i, j, 0)),
                  pl.BlockSpec((1, d), lambda i, j: (0, 0)), vec(), vec(),
                  pl.BlockSpec((nrow, d), lambda i, j: (0, 0))],
        out_specs=[pl.BlockSpec((None, tile, d), lambda i, j: (i, j, 0)), tr(), tr(), tr()],
        out_shape=[jax.ShapeDtypeStruct((b, t, d), BF16),
                   jax.ShapeDtypeStruct((b, t, nk), jnp.int32),
                   jax.ShapeDtypeStruct((b, t, nk), jnp.int32),
                   jax.ShapeDtypeStruct((b, t, nk), F32)],
        scratch_shapes=[pltpu.VMEM((nrow, tile), F32),
                        pltpu.VMEM((k, tile), F32), pltpu.VMEM((k, tile), jnp.int32),
                        pltpu.VMEM((k, tile), F32), pltpu.VMEM((k, tile), jnp.int32),
                        pltpu.VMEM((k, tile), F32), pltpu.VMEM((k, tile), jnp.int32),
                        pltpu.VMEM((nk, tile), jnp.int32), pltpu.VMEM((nk, tile), jnp.int32),
                        pltpu.VMEM((nk, tile), F32)],
        compiler_params=_cparams(("parallel", "parallel")),
    )(x1, g.reshape(1, d), sh, sc, wt)


def _peer_mix_kernel(h_ref, a_ref, b_ref, gate_ref, ut_ref, v_ref, x_ref, gt_ref, fg_ref, o_ref,
                     m_scr, acc, *, tile, n_steps, pairs):
    nk = PEER_NKEYS
    half = nk // 2
    step = pl.program_id(2)
    hi_mask = jnp.uint32(0xFFFF0000)

    @pl.when(step == 0)
    def _():
        sub = lax.broadcasted_iota(jnp.int32, (nk, a_ref.shape[1]), 0)
        a_of_row = jnp.where(sub < half, 2 * sub, 2 * (sub - half) + 1)

        def build(p, carry):
            arow = a_ref[pl.ds(p, 1), :]
            brow = b_ref[pl.ds(p, 1), :]
            grow = gate_ref[pl.ds(p, 1), :]
            xa = jnp.where(a_of_row == arow, 1.0, 0.0).astype(BF16)
            yb = jnp.where(sub == brow, 0.5 * grow, 0.0).astype(BF16)
            m = _dot_nt(xa, yb).astype(BF16).astype(F32)
            bits = lax.bitcast_convert_type(m, jnp.uint32)
            m_scr[pl.ds(pl.multiple_of(p * MASK_PITCH, 8), half), :] = (bits[half:] & hi_mask) | (bits[:half] >> 16)
            return carry

        lax.fori_loop(0, tile, build, 0, unroll=64)

    h = h_ref[...]
    parts = []
    for q in range(pairs):
        pr = step * pairs + q
        act = _gelu_times_2(_dot(h, ut_ref[:, q * 2 * nk:(q + 1) * 2 * nk]))
        w = m_scr[pl.ds(pr, tile, stride=MASK_PITCH), :]
        m_even = lax.bitcast_convert_type(w << 16, F32)
        m_odd = lax.bitcast_convert_type(w & hi_mask, F32)
        parts.append((act * jnp.concatenate([m_even, m_odd], axis=1)).astype(BF16))
    contrib = _dot(jnp.concatenate(parts, axis=1), v_ref[...])

    @pl.when(step == 0)
    def _():
        acc[...] = contrib

    @pl.when(step > 0)
    def _():
        acc[...] += contrib

    @pl.when(step == n_steps - 1)
    def _():
        x2 = x_ref[...] + gt_ref[...] * acc[...]
        o_ref[...] = _rms(x2, fg_ref[...])


def _peer_mix(hb, aidx, bidx, gate, ut, v, x1, gt, final_g, *, tile, pairs):
    b, t, d = x1.shape
    nt = t // tile
    nk = PEER_NKEYS
    npk = aidx.shape[-1]
    n_steps = nk // (2 * pairs)
    eb = 2 * nk * pairs
    tok = lambda wd: pl.BlockSpec((None, tile, wd), lambda i, j, s: (i, j, 0))
    return pl.pallas_call(
        functools.partial(_peer_mix_kernel, tile=tile, n_steps=n_steps, pairs=pairs),
        grid=(b, nt, n_steps),
        in_specs=[tok(d), tok(npk), tok(npk), tok(npk),
                  pl.BlockSpec((d, eb), lambda i, j, s: (0, s)),
                  pl.BlockSpec((eb, d), lambda i, j, s: (s, 0)), tok(d),
                  pl.BlockSpec((None, 1, d), lambda i, j, s: (i, 0, 0)),
                  pl.BlockSpec((1, d), lambda i, j, s: (0, 0))],
        out_specs=tok(d),
        out_shape=jax.ShapeDtypeStruct((b, t, d), F32),
        scratch_shapes=[pltpu.VMEM((tile * MASK_PITCH, nk), jnp.uint32), pltpu.VMEM((tile, d), F32)],
        compiler_params=_cparams(("parallel", "parallel", "arbitrary")),
    )(hb, aidx, bidx, gate, ut, v, x1, gt, final_g.reshape(1, d))


def _block_diag(w):
    n, e, _ = w.shape
    eye = jnp.eye(n, dtype=w.dtype)
    return (eye[:, None, :, None] * w[:, :, None, :]).reshape(n * e, n * e)


def _mix_sequence(x3, ctx_mode, params, states, mods, *, rg_tb, gdn_tb, proj_tile):
    (norm1_g, w_rg, w_gdn, rg_conv_w, rg_conv_b, wg, gate_b, c_lam, gdn_conv_w, ea_row, dtb_row) = params
    sh1, sc1, mod_row = mods
    rg_h0_f, rg_h0_b, gdn_s0 = states
    p_rg, qkv, z, ab = _project(x3, norm1_g, sh1, sc1, mod_row, jnp.concatenate([w_rg, w_gdn], axis=1),
                                (2 * RG_WIDTH, 3 * GDN_WIDTH, GDN_WIDTH, AB_PAD), colmajor=not ctx_mode, tile=proj_tile)
    h_f, st_f = _rglru_pass(p_rg, rg_conv_w, rg_conv_b, wg[0], gate_b[0], c_lam[0], rg_h0_f, None,
                            reverse=False, tb=rg_tb)
    y_rg, st_b = _rglru_pass(p_rg, rg_conv_w, rg_conv_b, wg[1], gate_b[1], c_lam[1], rg_h0_b, h_f,
                             reverse=True, tb=rg_tb)
    prep = _gdn_prep(qkv, ab, gdn_conv_w, ea_row, dtb_row, tb=gdn_tb)
    o_f, o_b, s_fin = _gdn_scan(prep, gdn_s0, cps=min(GDN_SCAN_CHUNKS, x3.shape[1] // GDN_CHUNK))
    return y_rg, o_f, o_b, z, (st_f, st_b, s_fin)


def kernel(x, c, ctx, c_ctx, w_mod, b_mod, norm1_g, norm2_g, w_in, rg_conv_w, rg_conv_b, rg_gate_w, rg_gate_b,
           rg_lambda, gdn_conv_w, gdn_a_log, gdn_dt_bias, gdn_norm_g, w_out, peer_wq, peer_keys, peer_u, peer_v,
           final_g):
    b, t, d = x.shape
    depth = w_mod.shape[0]
    assert depth == 1, "context residual stream update is only needed for depth > 1"
    l = 0
    w = RG_WIDTH

    cc = jnp.zeros((16, d), F32).at[:b].set(c).at[b].set(c_ctx)
    w_rg = w_in[l][:, :2 * w].astype(BF16)
    n_ab = w_in.shape[2] - 2 * w - 4 * GDN_WIDTH
    w_gdn = jnp.concatenate([w_in[l][:, 2 * w:2 * w + 4 * GDN_WIDTH],
                             jnp.pad(w_in[l][:, 2 * w + 4 * GDN_WIDTH:], ((0, 0), (0, AB_PAD - n_ab)))],
                            axis=1).astype(BF16)
    wg = jnp.stack([jnp.concatenate([_block_diag(rg_gate_w[l, dr, 0]), _block_diag(rg_gate_w[l, dr, 1])], axis=1)
                    for dr in range(2)]).astype(BF16)
    gate_b = rg_gate_b[l].reshape(2, 1, 2 * w)
    c_lam = (-RG_C * jax.nn.softplus(-rg_lambda[l])).reshape(2, 1, w)
    ea = jnp.exp(gdn_a_log[l])
    pad4 = jnp.zeros((2, GDN_HEADS), F32)
    ea_row = jnp.pad(jnp.concatenate([ea, pad4], axis=1).reshape(1, -1), ((0, 0), (0, AB_PAD - n_ab)))
    dtb_row = jnp.pad(jnp.concatenate([gdn_dt_bias[l], pad4], axis=1).reshape(1, -1), ((0, 0), (0, AB_PAD - n_ab)))
    params = (norm1_g[l], w_rg, w_gdn, rg_conv_w[l], rg_conv_b[l].reshape(1, w), wg, gate_b, c_lam,
              gdn_conv_w[l], ea_row, dtb_row)

    mod = _modulation(cc, w_mod[l], b_mod[l])
    sh1, sc1, gt1, sh2, sc2, gt2 = [mod[:, i * d:(i + 1) * d].reshape(16, 1, d) for i in range(6)]

    tc = ctx.shape[1]
    zero_states = (jnp.zeros((b, 1, w), F32), jnp.zeros((b, 1, w), F32),
                   jnp.zeros((b, 2, GDN_HEADS, GDN_DK, GDN_DV), F32))
    _, _, _, _, ctx_states = _mix_sequence(ctx, True, params, zero_states, (sh1, sc1, lambda i: b),
                                           rg_tb=min(tc, RG_BLOCK), gdn_tb=min(tc, GDN_BLOCK),
                                           proj_tile=min(tc, TOKEN_TILE))

    tile = min(t, TOKEN_TILE)
    y_rg, o_f, o_b, z, _ = _mix_sequence(x, False, params, ctx_states, (sh1, sc1, lambda i: i),
                                         rg_tb=min(t, RG_BLOCK), gdn_tb=min(t, GDN_BLOCK), proj_tile=tile)
    x1 = _out_project(x, y_rg, o_f, o_b, z, gdn_norm_g[l].reshape(1, GDN_DV), w_out[l].astype(BF16), gt1, tile=tile)

    wfold_t = _fold_keys(peer_wq[l], peer_keys[l]).T.astype(BF16)
    hb, aidx, bidx, gate = _peer_select(x1, norm2_g[l], sh2, sc2, wfold_t, tile=tile)
    out = _peer_mix(hb, aidx, bidx, gate, peer_u[l].astype(BF16).T, peer_v[l].astype(BF16),
                    x1, gt2, final_g, tile=tile, pairs=MIX_PAIRS)
    return out
```

```python
import functools
import math

import jax
import jax.numpy as jnp
from jax import lax
from jax.experimental import pallas as pl
from jax.experimental.pallas import tpu as pltpu

F32 = jnp.float32
BF16 = jnp.bfloat16

GRID_W = 64
EPS = 1e-6
RG_WIDTH = 512
RG_C = 8.0
GDN_HEADS = 4
GDN_DK = 128
GDN_DV = 128
GDN_WIDTH = GDN_HEADS * GDN_DV
GDN_CHUNK = 64
GDN_SUB = 16
AB_PAD = 128
PEER_HEADS = 8
PEER_NKEYS = 128
PEER_HALF = 128
PEER_TOPK = 16
NEG_INF = float("-inf")
MASK_PITCH = 72

VMEM_LIMIT = 58 * 1024 * 1024

TOKEN_TILE = 512
RG_BLOCK = 512
GDN_BLOCK = 256
GDN_SCAN_CHUNKS = 8
MIX_PAIRS = 8


def _cparams(sem):
    return pltpu.CompilerParams(dimension_semantics=sem, vmem_limit_bytes=VMEM_LIMIT)


def _dot(a, b):
    return jnp.dot(a, b, preferred_element_type=F32)


def _dot_nt(a, b):
    return lax.dot_general(a, b, (((1,), (1,)), ((), ())), preferred_element_type=F32)


def _dot_tn(a, b):
    return lax.dot_general(a, b, (((0,), (0,)), ((), ())), preferred_element_type=F32)


def _split3(x):
    hi = x.astype(BF16)
    r = x - hi.astype(F32)
    mid = r.astype(BF16)
    lo = (r - mid.astype(F32)).astype(BF16)
    return hi, mid, lo


def _dot_sel(m, x):
    hi, mid, lo = _split3(x)
    return _dot(m, lo) + _dot(m, mid) + _dot(m, hi)


def _silu(x):
    return x * jax.nn.sigmoid(x)


def _softplus(x):
    return jnp.maximum(x, 0.0) + jnp.log1p(jnp.exp(-jnp.abs(x)))


def _gelu_times_2(x):
    c = math.sqrt(2.0 / math.pi)
    return x * (1.0 + jnp.tanh(x * (c + (c * 0.044715) * (x * x))))


def _rms(x, g):
    return x * lax.rsqrt(jnp.mean(x * x, axis=-1, keepdims=True) + EPS) * g


def _mod_kernel(c_ref, w_ref, b_ref, o_ref):
    s = _silu(c_ref[...])
    o_ref[...] = jnp.dot(s, w_ref[...], preferred_element_type=F32,
                         precision=lax.Precision.HIGHEST) + b_ref[...]


def _modulation(cc, w_mod, b_mod):
    m, d = cc.shape
    n = w_mod.shape[1]
    tn = 1536
    return pl.pallas_call(
        _mod_kernel,
        grid=(n // tn,),
        in_specs=[pl.BlockSpec((m, d), lambda j: (0, 0)),
                  pl.BlockSpec((d, tn), lambda j: (0, j)),
                  pl.BlockSpec((1, tn), lambda j: (0, j))],
        out_specs=pl.BlockSpec((m, tn), lambda j: (0, j)),
        out_shape=jax.ShapeDtypeStruct((m, n), F32),
        compiler_params=_cparams(("arbitrary",)),
    )(cc, w_mod, b_mod.reshape(1, n))


def _proj_kernel(x_ref, g_ref, sh_ref, sc_ref, w_ref, *o_refs, ncol, widths):
    if ncol > 1:
        x = jnp.concatenate([x_ref[:, j, :] for j in range(ncol)], axis=0)
    else:
        x = x_ref[...]
    h = _rms(x, g_ref[...]) * (1.0 + sc_ref[...]) + sh_ref[...]
    o = _dot(h.astype(BF16), w_ref[...])
    off = 0
    for n_out, (o_ref, wd) in enumerate(zip(o_refs, widths)):
        if ncol > 1 and n_out == 0:
            rows = o.shape[0] // ncol
            for j in range(ncol):
                o_ref[:, j, :] = o[j * rows:(j + 1) * rows, off:off + wd]
        else:
            o_ref[...] = o[:, off:off + wd]
        off += wd


def _project(x3, g, sh, sc, mod_row, w, widths, *, colmajor, tile):
    b, t, d = x3.shape
    n = w.shape[1]
    flat = lambda wd: pl.BlockSpec((None, tile, wd), lambda i, j: (i, j, 0))
    if colmajor:
        rows = t // GRID_W
        ncol = tile // rows
        grid_view = lambda wd: pl.BlockSpec((None, rows, ncol, wd), lambda i, j: (i, 0, j, 0))
        xv = x3.reshape(b, rows, GRID_W, d)
        x_spec = grid_view(d)
        nt = GRID_W // ncol
        out_specs = [grid_view(widths[0])] + [flat(wd) for wd in widths[1:]]
        out_shape = ([jax.ShapeDtypeStruct((b, rows, GRID_W, widths[0]), F32)]
                     + [jax.ShapeDtypeStruct((b, t, wd), F32) for wd in widths[1:]])
    else:
        ncol = 1
        xv = x3
        x_spec = flat(d)
        nt = t // tile
        out_specs = [flat(wd) for wd in widths]
        out_shape = [jax.ShapeDtypeStruct((b, t, wd), F32) for wd in widths]
    vec = lambda: pl.BlockSpec((None, 1, d), lambda i, j: (mod_row(i), 0, 0))
    outs = pl.pallas_call(
        functools.partial(_proj_kernel, ncol=ncol, widths=tuple(widths)),
        grid=(b, nt),
        in_specs=[x_spec, pl.BlockSpec((1, d), lambda i, j: (0, 0)), vec(), vec(),
                  pl.BlockSpec((d, n), lambda i, j: (0, 0))],
        out_specs=out_specs,
        out_shape=out_shape,
        compiler_params=_cparams(("parallel", "parallel")),
    )(xv, g.reshape(1, d), sh, sc, w)
    return [outs[0].reshape(b, t, widths[0])] + list(outs[1:])


def _conv4(cur, prev8, nxt8, w_ref, first, last):
    tb, c = cur.shape
    g = tb // 8
    prev8 = jnp.where(first, 0.0, prev8)
    nxt8 = jnp.where(last, 0.0, nxt8)
    ext = jnp.concatenate([prev8, cur, nxt8], axis=0).reshape(g + 2, 8, c)
    sub = lax.broadcasted_iota(jnp.int32, (g, 8, c), 1)

    def back(k):
        r = pltpu.roll(ext, k, 1)
        return jnp.where(sub >= k, r[1:g + 1], r[0:g])

    r = pltpu.roll(ext, 7, 1)
    xp1 = jnp.where(sub < 7, r[1:g + 1], r[2:g + 2])
    y = (w_ref[0:1, :] * back(2) + w_ref[1:2, :] * back(1) + w_ref[2:3, :] * ext[1:g + 1] + w_ref[3:4, :] * xp1)
    return y.reshape(tb, c)


def _halo_specs(tb, t, width, lane_blk, tmap):
    r = tb // 8
    nb8 = t // 8
    return [pl.BlockSpec((None, tb, width), lambda i, j: (i, tmap(j), lane_blk)),
            pl.BlockSpec((None, 8, width), lambda i, j: (i, jnp.maximum(tmap(j) * r - 1, 0), lane_blk)),
            pl.BlockSpec((None, 8, width), lambda i, j: (i, jnp.minimum((tmap(j) + 1) * r, nb8 - 1), lane_blk))]


def _rglru_kernel(*refs, tb, nt, reverse):
    if reverse:
        (u_ref, up_ref, un_ref, cw_ref, cb_ref, wg_ref, gb_ref, cl_ref, h0_ref, gate_ref, hf_ref,
         y_ref, st_ref, carry) = refs
    else:
        (u_ref, up_ref, un_ref, cw_ref, cb_ref, wg_ref, gb_ref, cl_ref, h0_ref,
         y_ref, st_ref, carry) = refs
    j = pl.program_id(1)
    tblk = (nt - 1 - j) if reverse else j
    w = u_ref.shape[-1]

    @pl.when(j == 0)
    def _():
        carry[...] = h0_ref[...]

    u = u_ref[...]
    xc = _conv4(u, up_ref[...], un_ref[...], cw_ref, tblk == 0, tblk == nt - 1) + cb_ref[...]
    gates = _dot(xc.astype(BF16), wg_ref[...]) + gb_ref[...]
    r = jax.nn.sigmoid(gates[:, :w])
    ig = jax.nn.sigmoid(gates[:, w:])
    log_a = r * cl_ref[...]
    a = jnp.exp(log_a)
    th = jnp.tanh(log_a)
    bb = jnp.sqrt(-2.0 * th / (1.0 - th)) * (ig * xc)

    n_grp = tb // 8
    a = a.reshape(n_grp, 8, w)
    bb = bb.reshape(n_grp, 8, w)
    sub = lax.broadcasted_iota(jnp.int32, (n_grp, 8, w), 1)
    s = 1
    while s < 8:
        shift, ok = (8 - s, sub < 8 - s) if reverse else (s, sub >= s)
        a_sh = jnp.where(ok, pltpu.roll(a, shift, 1), 1.0)
        b_sh = jnp.where(ok, pltpu.roll(bb, shift, 1), 0.0)
        bb = a * b_sh + bb
        a = a * a_sh
        s *= 2
    h_prev = carry[...]
    hs = [None] * n_grp
    for g in (range(n_grp - 1, -1, -1) if reverse else range(n_grp)):
        hg = a[g] * h_prev + bb[g]
        h_prev = hg[0:1, :] if reverse else hg[7:8, :]
        hs[g] = hg
    h = jnp.concatenate(hs, axis=0)
    carry[...] = h_prev
    st_ref[...] = carry[...]
    if reverse:
        y_ref[...] = ((hf_ref[...] + h) * jax.nn.gelu(gate_ref[...])).astype(y_ref.dtype)
    else:
        y_ref[...] = h


def _rglru_pass(p_rg, conv_w, conv_b, wg, gate_b, c_lam, h0, hf, *, reverse, tb):
    b, t, w2 = p_rg.shape
    w = w2 // 2
    nt = t // tb
    tmap = (lambda j: nt - 1 - j) if reverse else (lambda j: j)
    const = lambda shape: pl.BlockSpec(shape, lambda i, j: (0,) * len(shape))
    in_specs = _halo_specs(tb, t, w, 0, tmap) + [
        const((4, w)), const((1, w)), const((w, 2 * w)), const((1, 2 * w)), const((1, w)),
        pl.BlockSpec((None, 1, w), lambda i, j: (i, 0, 0))]
    args = [p_rg, p_rg, p_rg, conv_w, conv_b, wg, gate_b, c_lam, h0]
    if reverse:
        in_specs += [pl.BlockSpec((None, tb, w), lambda i, j: (i, tmap(j), 1)),
                     pl.BlockSpec((None, tb, w), lambda i, j: (i, tmap(j), 0))]
        args += [p_rg, hf]
    y, st = pl.pallas_call(
        functools.partial(_rglru_kernel, tb=tb, nt=nt, reverse=reverse),
        grid=(b, nt),
        in_specs=in_specs,
        out_specs=[pl.BlockSpec((None, tb, w), lambda i, j: (i, tmap(j), 0)),
                   pl.BlockSpec((None, 1, w), lambda i, j: (i, 0, 0))],
        out_shape=[jax.ShapeDtypeStruct((b, t, w), F32),
                   jax.ShapeDtypeStruct((b, 1, w), F32)],
        scratch_shapes=[pltpu.VMEM((1, w), F32)],
        compiler_params=_cparams(("parallel", "arbitrary")),
    )(*args)
    return y, st


def _bmm(a, b):
    return lax.dot_general(a, b, (((2,), (1,)), ((0,), (0,))), preferred_element_type=F32)


def _bmm_nt(a, b):
    return lax.dot_general(a, b, (((2,), (2,)), ((0,), (0,))), preferred_element_type=F32)


def _unit_tri_inverse(a, diag_mask):
    c = a.shape[-1]
    eye = (lax.broadcasted_iota(jnp.int32, (1, c, c), 1) == lax.broadcasted_iota(jnp.int32, (1, c, c), 2)).astype(F32)
    mm = lambda p, q: _bmm(p.astype(BF16), q.astype(BF16))
    ad = jnp.where(diag_mask, a, 0.0)
    x = eye - ad
    pw = ad
    k = 2
    while k < GDN_SUB:
        pw = mm(pw, pw)
        x = x + mm(x, pw)
        k *= 2
    n = mm(x, a - ad)
    nblk = c // GDN_SUB
    y = eye - n
    pw = n
    k = 2
    while k < nblk:
        pw = mm(pw, pw)
        y = y + mm(y, pw)
        k *= 2
    return mm(y, x)


def _gdn_prep_kernel(qkv_ref, qp_ref, qn_ref, ab_ref, cw_ref, ea_ref, dtb_ref,
                     qg_ref, kd_ref, kc_ref, wv_ref, at_ref, eg_ref, *, tb, nt):
    c = GDN_CHUNK
    j = pl.program_id(1)
    hw = GDN_WIDTH
    x = _silu(_conv4(qkv_ref[...], qp_ref[...], qn_ref[...], cw_ref, j == 0, j == nt - 1))

    ab = ab_ref[...]
    col = lax.broadcasted_iota(jnp.int32, (tb, AB_PAD), 1)
    is_a = (col & 4) == 0
    gbv = jnp.where(is_a, -ea_ref[...] * _softplus(ab + dtb_ref[...]), jax.nn.sigmoid(ab))

    ri = lax.broadcasted_iota(jnp.int32, (tb, tb), 0)
    ci = lax.broadcasted_iota(jnp.int32, (tb, tb), 1)
    same = (ri // c) == (ci // c)
    l_f = (same & (ci <= ri)).astype(BF16)
    l_b = (same & (ci >= ri)).astype(BF16)
    l_t = same.astype(BF16)
    gcum = jnp.where(col < 8, _dot_sel(l_f, gbv), _dot_sel(l_b, gbv))
    gtot = _dot_sel(l_t, gbv)

    cols_a = [dr * 8 + hd for dr in range(2) for hd in range(GDN_HEADS)]
    bcast = lambda v, cols: jnp.concatenate([jnp.broadcast_to(v[:, cc:cc + 1], (tb, 128)) for cc in cols], axis=1)
    gc_all = bcast(gcum, cols_a)
    gt_all = bcast(gtot, cols_a)
    be_all = bcast(gbv, [cc + 4 for cc in cols_a])

    qs, ks, vs = [], [], []
    for hd in range(GDN_HEADS):
        q = x[:, hd * GDN_DK:(hd + 1) * GDN_DK]
        k = x[:, hw + hd * GDN_DK: hw + (hd + 1) * GDN_DK]
        qs.append(q * lax.rsqrt(jnp.sum(q * q, axis=-1, keepdims=True) + EPS) * (GDN_DK ** -0.5))
        ks.append(k * lax.rsqrt(jnp.sum(k * k, axis=-1, keepdims=True) + EPS))
        vs.append(x[:, 2 * hw + hd * GDN_DV: 2 * hw + (hd + 1) * GDN_DV])
    inst = [(ch, dr, hd) for ch in range(tb // c) for dr in range(2) for hd in range(GDN_HEADS)]
    n_inst = len(inst)
    rows = lambda ch: slice(ch * c, (ch + 1) * c)
    lanes = lambda dr, hd: slice((dr * GDN_HEADS + hd) * 128, (dr * GDN_HEADS + hd + 1) * 128)
    stack = lambda f: jnp.stack([f(ch, dr, hd) for ch, dr, hd in inst], axis=0)
    qq = stack(lambda ch, dr, hd: qs[hd][rows(ch)])
    kk = stack(lambda ch, dr, hd: ks[hd][rows(ch)])
    vv = stack(lambda ch, dr, hd: vs[hd][rows(ch)])
    gcb = stack(lambda ch, dr, hd: gc_all[rows(ch), lanes(dr, hd)])
    gtb = stack(lambda ch, dr, hd: gt_all[rows(ch), lanes(dr, hd)])
    beb = stack(lambda ch, dr, hd: be_all[rows(ch), lanes(dr, hd)])

    fwd = ((lax.broadcasted_iota(jnp.int32, (n_inst, c, 128), 0) // GDN_HEADS) % 2) == 0
    ii = lax.broadcasted_iota(jnp.int32, (n_inst, c, 128), 1)
    jj = lax.broadcasted_iota(jnp.int32, (n_inst, c, 128), 2)
    incl = (fwd & (jj <= ii)) | (jnp.logical_not(fwd) & (jj >= ii) & (jj < c))
    fwd_c = ((lax.broadcasted_iota(jnp.int32, (n_inst, c, c), 0) // GDN_HEADS) % 2) == 0
    i64 = lax.broadcasted_iota(jnp.int32, (n_inst, c, c), 1)
    j64 = lax.broadcasted_iota(jnp.int32, (n_inst, c, c), 2)
    strict = (fwd_c & (j64 < i64)) | (jnp.logical_not(fwd_c) & (j64 > i64))
    diag_blk = (i64 // GDN_SUB) == (j64 // GDN_SUB)

    eg = jnp.exp(gcb)
    kb = kk * beb
    diag = jnp.where(ii == jj, gcb, 0.0)
    gcr = _dot_sel(jnp.ones((c, c), BF16), jnp.concatenate([diag[g] for g in range(n_inst)], axis=1))
    gcr = jnp.stack([gcr[:, g * 128:(g + 1) * 128] for g in range(n_inst)], axis=0)
    decay = jnp.where(incl, jnp.exp(jnp.where(incl, gcb - gcr, 0.0)), 0.0)
    kpad = jnp.concatenate([kk.astype(BF16), jnp.zeros((n_inst, c, GDN_DK), BF16)], axis=1)
    qk = _bmm_nt(jnp.concatenate([qq, kb], axis=1).astype(BF16), kpad)
    attn = (qk[:, :c] * decay).astype(BF16)
    a_mat = jnp.where(strict, (qk[:, c:] * decay)[:, :, :c], 0.0)
    tinv = _unit_tri_inverse(a_mat, diag_blk)
    rhs = jnp.concatenate([vv * beb, kb * eg], axis=2)
    sol = _bmm(tinv.astype(BF16), rhs.astype(BF16))
    qg = (qq * eg).astype(BF16)
    kd = (kk * jnp.exp(gtb - gcb)).astype(BF16)
    egt = jnp.exp(gtb[:, 0:8, :])
    for g, (ch, dr, hd) in enumerate(inst):
        rs = rows(ch)
        ls = slice(hd * 128, (hd + 1) * 128)
        qg_ref[dr, rs, ls] = qg[g]
        kd_ref[dr, rs, ls] = kd[g]
        kc_ref[dr, rs, ls] = sol[g, :, GDN_DV:].astype(BF16)
        wv_ref[dr, rs, ls] = sol[g, :, :GDN_DV]
        at_ref[dr, rs, ls] = attn[g]
        eg_ref[dr, ch, :, ls] = egt[g]


def _gdn_prep(qkv, ab, conv_w, ea_row, dtb_row, *, tb):
    b, t, cw = qkv.shape
    nt = t // tb
    nch = t // GDN_CHUNK
    cpb = tb // GDN_CHUNK
    const = lambda shape: pl.BlockSpec(shape, lambda i, j: (0,) * len(shape))
    hw = GDN_WIDTH
    big = lambda: pl.BlockSpec((2, None, tb, hw), lambda i, j: (0, i, j, 0))
    outs = pl.pallas_call(
        functools.partial(_gdn_prep_kernel, tb=tb, nt=nt),
        grid=(b, nt),
        in_specs=_halo_specs(tb, t, cw, 0, lambda j: j) + [
            pl.BlockSpec((None, tb, AB_PAD), lambda i, j: (i, j, 0)),
            const((4, cw)), const((1, AB_PAD)), const((1, AB_PAD))],
        out_specs=[big(), big(), big(), big(), big(),
                   pl.BlockSpec((2, None, cpb, 8, hw), lambda i, j: (0, i, j, 0, 0))],
        out_shape=[jax.ShapeDtypeStruct((2, b, t, hw), BF16),
                   jax.ShapeDtypeStruct((2, b, t, hw), BF16),
                   jax.ShapeDtypeStruct((2, b, t, hw), BF16),
                   jax.ShapeDtypeStruct((2, b, t, hw), F32),
                   jax.ShapeDtypeStruct((2, b, t, hw), BF16),
                   jax.ShapeDtypeStruct((2, b, nch, 8, hw), F32)],
        compiler_params=_cparams(("parallel", "parallel")),
    )(qkv, qkv, qkv, ab, conv_w, ea_row, dtb_row)
    return outs


def _gdn_scan_kernel(qg_f, kd_f, kc_f, wv_f, at_f, eg_f, qg_b, kd_b, kc_b, wv_b, at_b, eg_b, s0_ref,
                     of_ref, ob_ref, sfin_ref, s_scr, *, cps):
    c = GDN_CHUNK
    j = pl.program_id(1)

    @pl.when(j == 0)
    def _():
        s_scr[...] = s0_ref[...]

    dirs = ((qg_f, kd_f, kc_f, wv_f, at_f, eg_f), (qg_b, kd_b, kc_b, wv_b, at_b, eg_b))
    chains = [(dr, hd) for dr in range(2) for hd in range(GDN_HEADS)]
    ls = lambda hd: slice(hd * 128, (hd + 1) * 128)
    s = s_scr[...].reshape(2 * GDN_HEADS, GDN_DK, GDN_DV)
    for q in range(cps):
        chunk = (q, cps - 1 - q)
        rs = lambda dr: slice(chunk[dr] * c, (chunk[dr] + 1) * c)
        stack = lambda f: jnp.stack([f(dirs[dr], rs(dr), ls(hd), chunk[dr]) for dr, hd in chains], axis=0)
        kq = stack(lambda r, t, l, n: jnp.concatenate([r[2][t, l], r[0][t, l]], axis=0))
        wv = stack(lambda r, t, l, n: r[3][t, l])
        at = stack(lambda r, t, l, n: r[4][t, l][:, :c])
        eg = stack(lambda r, t, l, n: r[5][n, 0:1, l])
        r = _bmm(kq, s.astype(BF16))
        vb = (wv - r[:, :c]).astype(BF16)
        o = r[:, c:] + _bmm(at, vb)
        upd = []
        for g, (dr, hd) in enumerate(chains):
            (of_ref, ob_ref)[dr][rs(dr), ls(hd)] = o[g]
            upd.append(_dot_tn(dirs[dr][1][rs(dr), ls(hd)], vb[g]))
        s = s * eg + jnp.stack(upd, axis=0)
    s_scr[...] = s.reshape(2, GDN_HEADS, GDN_DK, GDN_DV)
    sfin_ref[...] = s_scr[...]


def _gdn_scan(prep, s0, *, cps):
    qg, kd, kc, wv, at, eg = prep
    _, b, t, hw = qg.shape
    c = GDN_CHUNK
    nblk = t // (c * cps)
    fw = lambda: pl.BlockSpec((None, None, cps * c, hw), lambda i, j: (0, i, j, 0))
    bw = lambda: pl.BlockSpec((None, None, cps * c, hw), lambda i, j: (1, i, nblk - 1 - j, 0))
    egf = pl.BlockSpec((None, None, cps, 8, hw), lambda i, j: (0, i, j, 0, 0))
    egb = pl.BlockSpec((None, None, cps, 8, hw), lambda i, j: (1, i, nblk - 1 - j, 0, 0))
    st = pl.BlockSpec((None, 2, GDN_HEADS, GDN_DK, GDN_DV), lambda i, j: (i, 0, 0, 0, 0))
    o_f, o_b, s_fin = pl.pallas_call(
        functools.partial(_gdn_scan_kernel, cps=cps),
        grid=(b, nblk),
        in_specs=[fw(), fw(), fw(), fw(), fw(), egf, bw(), bw(), bw(), bw(), bw(), egb, st],
        out_specs=[pl.BlockSpec((None, cps * c, hw), lambda i, j: (i, j, 0)),
                   pl.BlockSpec((None, cps * c, hw), lambda i, j: (i, nblk - 1 - j, 0)),
                   st],
        out_shape=[jax.ShapeDtypeStruct((b, t, hw), F32), jax.ShapeDtypeStruct((b, t, hw), F32),
                   jax.ShapeDtypeStruct((b, 2, GDN_HEADS, GDN_DK, GDN_DV), F32)],
        scratch_shapes=[pltpu.VMEM((2, GDN_HEADS, GDN_DK, GDN_DV), F32)],
        compiler_params=_cparams(("parallel", "arbitrary")),
    )(qg, kd, kc, wv, at, eg, qg, kd, kc, wv, at, eg, s0)
    return o_f, o_b, s_fin


def _out_kernel(x_ref, yrg_ref, of_ref, ob_ref, z_ref, ng_ref, w_ref, gt_ref, o_ref, *, ncol):
    o = of_ref[...] + ob_ref[...]
    z = z_ref[...]
    parts = [jnp.concatenate([yrg_ref[:, j, :] for j in range(ncol)], axis=0)]
    for hd in range(GDN_HEADS):
        ls = slice(hd * GDN_DV, (hd + 1) * GDN_DV)
        parts.append(_rms(o[:, ls], ng_ref[...]) * _silu(z[:, ls]))
    y = jnp.concatenate(parts, axis=1).astype(BF16)
    t2 = gt_ref[...] * _dot(y, w_ref[...])
    rows = t2.shape[0] // ncol
    for j in range(ncol):
        o_ref[:, j, :] = x_ref[:, j, :] + t2[j * rows:(j + 1) * rows]


def _out_project(x3, y_rg, o_f, o_b, z, norm_g, w_out, gt, *, tile):
    b, t, d = x3.shape
    w = RG_WIDTH
    rows = t // GRID_W
    ncol = tile // rows
    hw = GDN_WIDTH
    grid_view = lambda wd: pl.BlockSpec((None, rows, ncol, wd), lambda i, j: (i, 0, j, 0))
    cm = lambda wd: pl.BlockSpec((None, tile, wd), lambda i, j: (i, j, 0))
    x1 = pl.pallas_call(
        functools.partial(_out_kernel, ncol=ncol),
        grid=(b, GRID_W // ncol),
        in_specs=[grid_view(d), grid_view(w), cm(hw), cm(hw), cm(hw),
                  pl.BlockSpec((1, GDN_DV), lambda i, j: (0, 0)),
                  pl.BlockSpec((w + hw, d), lambda i, j: (0, 0)),
                  pl.BlockSpec((None, 1, d), lambda i, j: (i, 0, 0))],
        out_specs=grid_view(d),
        out_shape=jax.ShapeDtypeStruct((b, rows, GRID_W, d), F32),
        compiler_params=_cparams(("parallel", "parallel")),
    )(x3.reshape(b, rows, GRID_W, d), y_rg.reshape(b, rows, GRID_W, w), o_f, o_b, z, norm_g, w_out, gt)
    return x1.reshape(b, t, d)


def _fold_kernel(wq_ref, k_ref, o_ref):
    o_ref[...] = lax.dot_general(wq_ref[...], k_ref[...], (((1,), (1,)), ((), ())),
                                 preferred_element_type=F32, precision=lax.Precision.HIGHEST)


def _fold_keys(wq, keys):
    d, n = wq.shape
    nblk = n // PEER_HALF
    return pl.pallas_call(
        _fold_kernel,
        grid=(nblk,),
        in_specs=[pl.BlockSpec((d, PEER_HALF), lambda j: (0, j)),
                  pl.BlockSpec((None, PEER_NKEYS, PEER_HALF), lambda j: (j % 2, 0, 0))],
        out_specs=pl.BlockSpec((d, PEER_NKEYS), lambda j: (0, j)),
        out_shape=jax.ShapeDtypeStruct((d, nblk * PEER_NKEYS), F32),
        compiler_params=_cparams(("parallel",)),
    )(wq, keys)


def _extract16(vals, ids, s_scr, i_scr, *, fast, n_masked=0):
    n, p = vals.shape
    for r in range(PEER_TOPK):
        m = jnp.max(vals, axis=0, keepdims=True)
        eq = vals == m
        if fast:
            idx = _dot(ids, jnp.where(eq, 1.0, 0.0).astype(BF16))[0:1, :].astype(jnp.int32)
            vals = jnp.where(eq, NEG_INF, vals)
        else:
            idx = jnp.min(jnp.where(eq, ids, 1 << 20), axis=0, keepdims=True)
            vals = jnp.where(ids == idx, NEG_INF, vals)
        s_scr[r:r + 1, :] = m
        i_scr[r:r + 1, :] = idx
    if not fast:
        return None
    dropped = jnp.sum(jnp.where(vals == NEG_INF, 1.0, 0.0), axis=0, keepdims=True)
    return jnp.where(dropped == float(PEER_TOPK + n_masked), 0.0, 1.0)


def _peer_sel_kernel(x_ref, g_ref, sh_ref, sc_ref, wt_ref, h_ref, a_ref, b_ref, gate_ref,
                     sc_scr, s1, i1, s2, i2, ts, tp, a_t, b_t, g_t):
    k = PEER_TOPK
    h = _rms(x_ref[...], g_ref[...]) * (1.0 + sc_ref[...]) + sh_ref[...]
    hb = h.astype(BF16)
    h_ref[...] = hb
    sc_scr[...] = _dot_nt(wt_ref[...], hb)
    p = hb.shape[0]

    n_cand = 80
    n_masked = sum(8 - k // (r + 1) for r in range(1, 8))

    def cand_pos(row):
        return jnp.where(row < 16, row, jnp.where(row < 72, ((row - 16) // 8 + 1) * k + (row - 16) % 8,
                                                  (row - 72 + 8) * k))

    def head(hd, fast):
        if fast:
            key_ids = lax.broadcasted_iota(jnp.int32, (8, PEER_NKEYS), 1).astype(F32).astype(BF16)
            pos_ids = cand_pos(lax.broadcasted_iota(jnp.int32, (8, n_cand), 1)).astype(F32).astype(BF16)
        else:
            key_ids = lax.broadcasted_iota(jnp.int32, (PEER_NKEYS, p), 0)
            pos_ids = cand_pos(lax.broadcasted_iota(jnp.int32, (n_cand, p), 0))
        sub = lax.broadcasted_iota(jnp.int32, (8, p), 0)
        base = pl.multiple_of(hd * 2 * PEER_NKEYS, 2 * PEER_NKEYS)
        bad1 = _extract16(sc_scr[pl.ds(base, PEER_NKEYS), :], key_ids, s1, i1, fast=fast)
        bad2 = _extract16(sc_scr[pl.ds(base + PEER_NKEYS, PEER_NKEYS), :], key_ids, s2, i2, fast=fast)
        s1v, s2v = s1[...], s2[...]
        cands = [s1v[0:1, :] + s2v]
        for r in range(1, 8):
            cands.append(jnp.where(sub < k // (r + 1), s1v[r:r + 1, :] + s2v[0:8, :], NEG_INF))
        cands.append(s1v[8:16, :] + s2v[0:1, :])
        bad3 = _extract16(jnp.concatenate(cands, axis=0), pos_ids, ts, tp, fast=fast, n_masked=n_masked)
        top_s, top_p = ts[...], tp[...]
        rr = top_p >> 4
        cc = top_p & (k - 1)
        i1v, i2v = i1[...], i2[...]
        av = jnp.zeros_like(top_p)
        bv = jnp.zeros_like(top_p)
        for q in range(k):
            av = jnp.where(rr == q, i1v[q:q + 1, :], av)
            bv = jnp.where(cc == q, i2v[q:q + 1, :], bv)
        e = jnp.exp(top_s - top_s[0:1, :])
        gate = e / jnp.sum(e, axis=0, keepdims=True)
        o = pl.multiple_of(hd * k, k)
        a_t[pl.ds(o, k), :] = av
        b_t[pl.ds(o, k), :] = bv
        g_t[pl.ds(o, k), :] = gate
        return (bad1 + bad2 + bad3) if fast else None

    def head_step(hd, carry):
        bad = head(hd, True)

        @pl.when(jnp.max(bad) > 0.0)
        def _():
            head(hd, False)

        return carry

    lax.fori_loop(0, PEER_HEADS, head_step, 0)

    a_ref[...] = a_t[...].T
    b_ref[...] = b_t[...].T
    gate_ref[...] = g_t[...].T


def _peer_select(x1, g, sh, sc, wt, *, tile):
    b, t, d = x1.shape
    nt = t // tile
    nk = PEER_HEADS * PEER_TOPK
    nrow = wt.shape[0]
    vec = lambda: pl.BlockSpec((None, 1, d), lambda i, j: (i, 0, 0))
    tr = lambda: pl.BlockSpec((None, tile, nk), lambda i, j: (i, j, 0))
    k = PEER_TOPK
    return pl.pallas_call(
        _peer_sel_kernel,
        grid=(b, nt),
        in_specs=[pl.BlockSpec((None, tile, d), lambda i, j: (i, j, 0)),
                  pl.BlockSpec((1, d), lambda i, j: (0, 0)), vec(), vec(),
                  pl.BlockSpec((nrow, d), lambda i, j: (0, 0))],
        out_specs=[pl.BlockSpec((None, tile, d), lambda i, j: (i, j, 0)), tr(), tr(), tr()],
        out_shape=[jax.ShapeDtypeStruct((b, t, d), BF16),
                   jax.ShapeDtypeStruct((b, t, nk), jnp.int32),
                   jax.ShapeDtypeStruct((b, t, nk), jnp.int32),
                   jax.ShapeDtypeStruct((b, t, nk), F32)],
        scratch_shapes=[pltpu.VMEM((nrow, tile), F32),
                        pltpu.VMEM((k, tile), F32), pltpu.VMEM((k, tile), jnp.int32),
                        pltpu.VMEM((k, tile), F32), pltpu.VMEM((k, tile), jnp.int32),
                        pltpu.VMEM((k, tile), F32), pltpu.VMEM((k, tile), jnp.int32),
                        pltpu.VMEM((nk, tile), jnp.int32), pltpu.VMEM((nk, tile), jnp.int32),
                        pltpu.VMEM((nk, tile), F32)],
        compiler_params=_cparams(("parallel", "parallel")),
    )(x1, g.reshape(1, d), sh, sc, wt)


def _peer_mix_kernel(h_ref, a_ref, b_ref, gate_ref, ut_ref, v_ref, x_ref, gt_ref, fg_ref, o_ref,
                     m_scr, acc, *, tile, n_steps, pairs):
    nk = PEER_NKEYS
    half = nk // 2
    step = pl.program_id(2)
    hi_mask = jnp.uint32(0xFFFF0000)

    @pl.when((pl.program_id(0) == 0) & (pl.program_id(1) == 0) & (step == 0))
    def _():
        acc[...] = jnp.zeros(acc.shape, F32)

    @pl.when(step == 0)
    def _():
        sub = lax.broadcasted_iota(jnp.int32, (nk, a_ref.shape[1]), 0)
        a_of_row = jnp.where(sub < half, 2 * sub, 2 * (sub - half) + 1)

        def build(p, carry):
            arow = a_ref[pl.ds(p, 1), :]
            brow = b_ref[pl.ds(p, 1), :]
            grow = gate_ref[pl.ds(p, 1), :]
            xa = jnp.where(a_of_row == arow, 1.0, 0.0).astype(BF16)
            yb = jnp.where(sub == brow, 0.5 * grow, 0.0).astype(BF16)
            m = _dot_nt(xa, yb).astype(BF16).astype(F32)
            bits = lax.bitcast_convert_type(m, jnp.uint32)
            m_scr[pl.ds(pl.multiple_of(p * MASK_PITCH, 8), half), :] = (bits[half:] & hi_mask) | (bits[:half] >> 16)
            return carry

        lax.fori_loop(0, tile, build, 0, unroll=64)

    h = h_ref[...]
    parts = []
    for q in range(pairs):
        pr = step * pairs + q
        act = _gelu_times_2(_dot(h, ut_ref[:, q * 2 * nk:(q + 1) * 2 * nk]))
        w = m_scr[pl.ds(pr, tile, stride=MASK_PITCH), :]
        m_even = lax.bitcast_convert_type(w << 16, F32)
        m_odd = lax.bitcast_convert_type(w & hi_mask, F32)
        parts.append((act * jnp.concatenate([m_even, m_odd], axis=1)).astype(BF16))
    contrib = _dot(jnp.concatenate(parts, axis=1), v_ref[...])
    acc[...] = jnp.where(step == 0, contrib, acc[...] + contrib)

    @pl.when(step == n_steps - 1)
    def _():
        x2 = x_ref[...] + gt_ref[...] * acc[...]
        o_ref[...] = _rms(x2, fg_ref[...])


def _peer_mix(hb, aidx, bidx, gate, ut, v, x1, gt, final_g, *, tile, pairs):
    b, t, d = x1.shape
    nt = t // tile
    nk = PEER_NKEYS
    npk = aidx.shape[-1]
    n_steps = nk // (2 * pairs)
    eb = 2 * nk * pairs
    tok = lambda wd: pl.BlockSpec((None, tile, wd), lambda i, j, s: (i, j, 0))
    return pl.pallas_call(
        functools.partial(_peer_mix_kernel, tile=tile, n_steps=n_steps, pairs=pairs),
        grid=(b, nt, n_steps),
        in_specs=[tok(d), tok(npk), tok(npk), tok(npk),
                  pl.BlockSpec((d, eb), lambda i, j, s: (0, s)),
                  pl.BlockSpec((eb, d), lambda i, j, s: (s, 0)), tok(d),
                  pl.BlockSpec((None, 1, d), lambda i, j, s: (i, 0, 0)),
                  pl.BlockSpec((1, d), lambda i, j, s: (0, 0))],
        out_specs=tok(d),
        out_shape=jax.ShapeDtypeStruct((b, t, d), F32),
        scratch_shapes=[pltpu.VMEM((tile * MASK_PITCH, nk), jnp.uint32), pltpu.VMEM((tile, d), F32)],
        compiler_params=_cparams(("parallel", "parallel", "arbitrary")),
    )(hb, aidx, bidx, gate, ut, v, x1, gt, final_g.reshape(1, d))


def _block_diag(w):
    n, e, _ = w.shape
    eye = jnp.eye(n, dtype=w.dtype)
    return (eye[:, None, :, None] * w[:, :, None, :]).reshape(n * e, n * e)


def _mix_sequence(x3, ctx_mode, params, states, mods, *, rg_tb, gdn_tb, proj_tile):
    (norm1_g, w_rg, w_gdn, rg_conv_w, rg_conv_b, wg, gate_b, c_lam, gdn_conv_w, ea_row, dtb_row) = params
    sh1, sc1, mod_row = mods
    rg_h0_f, rg_h0_b, gdn_s0 = states
    p_rg, qkv, z, ab = _project(x3, norm1_g, sh1, sc1, mod_row, jnp.concatenate([w_rg, w_gdn], axis=1),
                                (2 * RG_WIDTH, 3 * GDN_WIDTH, GDN_WIDTH, AB_PAD), colmajor=not ctx_mode, tile=proj_tile)
    h_f, st_f = _rglru_pass(p_rg, rg_conv_w, rg_conv_b, wg[0], gate_b[0], c_lam[0], rg_h0_f, None,
                            reverse=False, tb=rg_tb)
    y_rg, st_b = _rglru_pass(p_rg, rg_conv_w, rg_conv_b, wg[1], gate_b[1], c_lam[1], rg_h0_b, h_f,
                             reverse=True, tb=rg_tb)
    prep = _gdn_prep(qkv, ab, gdn_conv_w, ea_row, dtb_row, tb=gdn_tb)
    o_f, o_b, s_fin = _gdn_scan(prep, gdn_s0, cps=min(GDN_SCAN_CHUNKS, x3.shape[1] // GDN_CHUNK))
    return y_rg, o_f, o_b, z, (st_f, st_b, s_fin)


def kernel(x, c, ctx, c_ctx, w_mod, b_mod, norm1_g, norm2_g, w_in, rg_conv_w, rg_conv_b, rg_gate_w, rg_gate_b,
           rg_lambda, gdn_conv_w, gdn_a_log, gdn_dt_bias, gdn_norm_g, w_out, peer_wq, peer_keys, peer_u, peer_v,
           final_g):
    b, t, d = x.shape
    depth = w_mod.shape[0]
    assert depth == 1, "context residual stream update is only needed for depth > 1"
    l = 0
    w = RG_WIDTH

    cc = jnp.zeros((16, d), F32).at[:b].set(c).at[b].set(c_ctx)
    w_rg = w_in[l][:, :2 * w].astype(BF16)
    n_ab = w_in.shape[2] - 2 * w - 4 * GDN_WIDTH
    w_gdn = jnp.concatenate([w_in[l][:, 2 * w:2 * w + 4 * GDN_WIDTH],
                             jnp.pad(w_in[l][:, 2 * w + 4 * GDN_WIDTH:], ((0, 0), (0, AB_PAD - n_ab)))],
                            axis=1).astype(BF16)
    wg = jnp.stack([jnp.concatenate([_block_diag(rg_gate_w[l, dr, 0]), _block_diag(rg_gate_w[l, dr, 1])], axis=1)
                    for dr in range(2)]).astype(BF16)
    gate_b = rg_gate_b[l].reshape(2, 1, 2 * w)
    c_lam = (-RG_C * jax.nn.softplus(-rg_lambda[l])).reshape(2, 1, w)
    ea = jnp.exp(gdn_a_log[l])
    pad4 = jnp.zeros((2, GDN_HEADS), F32)
    ea_row = jnp.pad(jnp.concatenate([ea, pad4], axis=1).reshape(1, -1), ((0, 0), (0, AB_PAD - n_ab)))
    dtb_row = jnp.pad(jnp.concatenate([gdn_dt_bias[l], pad4], axis=1).reshape(1, -1), ((0, 0), (0, AB_PAD - n_ab)))
    params = (norm1_g[l], w_rg, w_gdn, rg_conv_w[l], rg_conv_b[l].reshape(1, w), wg, gate_b, c_lam,
              gdn_conv_w[l], ea_row, dtb_row)

    mod = _modulation(cc, w_mod[l], b_mod[l])
    sh1, sc1, gt1, sh2, sc2, gt2 = [mod[:, i * d:(i + 1) * d].reshape(16, 1, d) for i in range(6)]

    tc = ctx.shape[1]
    zero_states = (jnp.zeros((b, 1, w), F32), jnp.zeros((b, 1, w), F32),
                   jnp.zeros((b, 2, GDN_HEADS, GDN_DK, GDN_DV), F32))
    _, _, _, _, ctx_states = _mix_sequence(ctx, True, params, zero_states, (sh1, sc1, lambda i: b),
                                           rg_tb=min(tc, RG_BLOCK), gdn_tb=min(tc, GDN_BLOCK),
                                           proj_tile=min(tc, TOKEN_TILE))

    tile = min(t, TOKEN_TILE)
    y_rg, o_f, o_b, z, _ = _mix_sequence(x, False, params, ctx_states, (sh1, sc1, lambda i: i),
                                         rg_tb=min(t, RG_BLOCK), gdn_tb=min(t, GDN_BLOCK), proj_tile=tile)
    x1 = _out_project(x, y_rg, o_f, o_b, z, gdn_norm_g[l].reshape(1, GDN_DV), w_out[l].astype(BF16), gt1, tile=tile)

    wfold_t = _fold_keys(peer_wq[l], peer_keys[l]).T.astype(BF16)
    hb, aidx, bidx, gate = _peer_select(x1, norm2_g[l], sh2, sc2, wfold_t, tile=tile)
    out = _peer_mix(hb, aidx, bidx, gate, peer_u[l].astype(BF16).T, peer_v[l].astype(BF16),
                    x1, gt2, final_g, tile=tile, pairs=MIX_PAIRS)
    return out
```

```python
import functools
import math

import jax
import jax.numpy as jnp
from jax import lax
from jax.experimental import pallas as pl
from jax.experimental.pallas import tpu as pltpu

F32 = jnp.float32
BF16 = jnp.bfloat16

GRID_W = 64
EPS = 1e-6
RG_WIDTH = 512
RG_C = 8.0
GDN_HEADS = 4
GDN_DK = 128
GDN_DV = 128
GDN_WIDTH = GDN_HEADS * GDN_DV
GDN_CHUNK = 64
GDN_SUB = 16
AB_PAD = 128
PEER_HEADS = 8
PEER_NKEYS = 128
PEER_HALF = 128
PEER_TOPK = 16
NEG_INF = float("-inf")
MASK_PITCH = 72

VMEM_LIMIT = 58 * 1024 * 1024

TOKEN_TILE = 512
RG_BLOCK = 512
GDN_BLOCK = 256
GDN_SCAN_CHUNKS = 8
MIX_PAIRS = 8


def _cparams(sem):
    return pltpu.CompilerParams(dimension_semantics=sem, vmem_limit_bytes=VMEM_LIMIT)


def _dot(a, b):
    return jnp.dot(a, b, preferred_element_type=F32)


def _dot_nt(a, b):
    return lax.dot_general(a, b, (((1,), (1,)), ((), ())), preferred_element_type=F32)


def _dot_tn(a, b):
    return lax.dot_general(a, b, (((0,), (0,)), ((), ())), preferred_element_type=F32)


def _split3(x):
    hi = x.astype(BF16)
    r = x - hi.astype(F32)
    mid = r.astype(BF16)
    lo = (r - mid.astype(F32)).astype(BF16)
    return hi, mid, lo


def _dot_sel(m, x):
    hi, mid, lo = _split3(x)
    return _dot(m, lo) + _dot(m, mid) + _dot(m, hi)


def _silu(x):
    return x * jax.nn.sigmoid(x)


def _softplus(x):
    return jnp.maximum(x, 0.0) + jnp.log1p(jnp.exp(-jnp.abs(x)))


def _gelu_times_2(x):
    c = math.sqrt(2.0 / math.pi)
    return x * (1.0 + jnp.tanh(x * (c + (c * 0.044715) * (x * x))))


def _rms(x, g):
    return x * lax.rsqrt(jnp.mean(x * x, axis=-1, keepdims=True) + EPS) * g


def _mod_kernel(c_ref, w_ref, b_ref, o_ref):
    s = _silu(c_ref[...])
    o_ref[...] = jnp.dot(s, w_ref[...], preferred_element_type=F32,
                         precision=lax.Precision.HIGHEST) + b_ref[...]


def _modulation(cc, w_mod, b_mod):
    m, d = cc.shape
    n = w_mod.shape[1]
    tn = 1536
    return pl.pallas_call(
        _mod_kernel,
        grid=(n // tn,),
        in_specs=[pl.BlockSpec((m, d), lambda j: (0, 0)),
                  pl.BlockSpec((d, tn), lambda j: (0, j)),
                  pl.BlockSpec((1, tn), lambda j: (0, j))],
        out_specs=pl.BlockSpec((m, tn), lambda j: (0, j)),
        out_shape=jax.ShapeDtypeStruct((m, n), F32),
        compiler_params=_cparams(("arbitrary",)),
    )(cc, w_mod, b_mod.reshape(1, n))


def _proj_kernel(x_ref, g_ref, sh_ref, sc_ref, w_ref, *o_refs, ncol, widths):
    if ncol > 1:
        x = jnp.concatenate([x_ref[:, j, :] for j in range(ncol)], axis=0)
    else:
        x = x_ref[...]
    h = _rms(x, g_ref[...]) * (1.0 + sc_ref[...]) + sh_ref[...]
    o = _dot(h.astype(BF16), w_ref[...])
    off = 0
    for n_out, (o_ref, wd) in enumerate(zip(o_refs, widths)):
        if ncol > 1 and n_out == 0:
            rows = o.shape[0] // ncol
            for j in range(ncol):
                o_ref[:, j, :] = o[j * rows:(j + 1) * rows, off:off + wd]
        else:
            o_ref[...] = o[:, off:off + wd]
        off += wd


def _project(x3, g, sh, sc, mod_row, w, widths, *, colmajor, tile):
    b, t, d = x3.shape
    n = w.shape[1]
    flat = lambda wd: pl.BlockSpec((None, tile, wd), lambda i, j: (i, j, 0))
    if colmajor:
        rows = t // GRID_W
        ncol = tile // rows
        grid_view = lambda wd: pl.BlockSpec((None, rows, ncol, wd), lambda i, j: (i, 0, j, 0))
        xv = x3.reshape(b, rows, GRID_W, d)
        x_spec = grid_view(d)
        nt = GRID_W // ncol
        out_specs = [grid_view(widths[0])] + [flat(wd) for wd in widths[1:]]
        out_shape = ([jax.ShapeDtypeStruct((b, rows, GRID_W, widths[0]), F32)]
                     + [jax.ShapeDtypeStruct((b, t, wd), F32) for wd in widths[1:]])
    else:
        ncol = 1
        xv = x3
        x_spec = flat(d)
        nt = t // tile
        out_specs = [flat(wd) for wd in widths]
        out_shape = [jax.ShapeDtypeStruct((b, t, wd), F32) for wd in widths]
    vec = lambda: pl.BlockSpec((None, 1, d), lambda i, j: (mod_row(i), 0, 0))
    outs = pl.pallas_call(
        functools.partial(_proj_kernel, ncol=ncol, widths=tuple(widths)),
        grid=(b, nt),
        in_specs=[x_spec, pl.BlockSpec((1, d), lambda i, j: (0, 0)), vec(), vec(),
                  pl.BlockSpec((d, n), lambda i, j: (0, 0))],
        out_specs=out_specs,
        out_shape=out_shape,
        compiler_params=_cparams(("parallel", "parallel")),
    )(xv, g.reshape(1, d), sh, sc, w)
    return [outs[0].reshape(b, t, widths[0])] + list(outs[1:])


def _conv4(cur, prev8, nxt8, w_ref, first, last):
    tb, c = cur.shape
    g = tb // 8
    prev8 = jnp.where(first, 0.0, prev8)
    nxt8 = jnp.where(last, 0.0, nxt8)
    ext = jnp.concatenate([prev8, cur, nxt8], axis=0).reshape(g + 2, 8, c)
    sub = lax.broadcasted_iota(jnp.int32, (g, 8, c), 1)

    def back(k):
        r = pltpu.roll(ext, k, 1)
        return jnp.where(sub >= k, r[1:g + 1], r[0:g])

    r = pltpu.roll(ext, 7, 1)
    xp1 = jnp.where(sub < 7, r[1:g + 1], r[2:g + 2])
    y = (w_ref[0:1, :] * back(2) + w_ref[1:2, :] * back(1) + w_ref[2:3, :] * ext[1:g + 1] + w_ref[3:4, :] * xp1)
    return y.reshape(tb, c)


def _halo_specs(tb, t, width, lane_blk, tmap):
    r = tb // 8
    nb8 = t // 8
    return [pl.BlockSpec((None, tb, width), lambda i, j: (i, tmap(j), lane_blk)),
            pl.BlockSpec((None, 8, width), lambda i, j: (i, jnp.maximum(tmap(j) * r - 1, 0), lane_blk)),
            pl.BlockSpec((None, 8, width), lambda i, j: (i, jnp.minimum((tmap(j) + 1) * r, nb8 - 1), lane_blk))]


def _rglru_kernel(*refs, tb, nt, reverse):
    if reverse:
        (u_ref, up_ref, un_ref, cw_ref, cb_ref, wg_ref, gb_ref, cl_ref, h0_ref, gate_ref, hf_ref,
         y_ref, st_ref, carry) = refs
    else:
        (u_ref, up_ref, un_ref, cw_ref, cb_ref, wg_ref, gb_ref, cl_ref, h0_ref,
         y_ref, st_ref, carry) = refs
    j = pl.program_id(1)
    tblk = (nt - 1 - j) if reverse else j
    w = u_ref.shape[-1]

    @pl.when(j == 0)
    def _():
        carry[...] = h0_ref[...]

    u = u_ref[...]
    xc = _conv4(u, up_ref[...], un_ref[...], cw_ref, tblk == 0, tblk == nt - 1) + cb_ref[...]
    gates = _dot(xc.astype(BF16), wg_ref[...]) + gb_ref[...]
    r = jax.nn.sigmoid(gates[:, :w])
    ig = jax.nn.sigmoid(gates[:, w:])
    log_a = r * cl_ref[...]
    a = jnp.exp(log_a)
    th = jnp.tanh(log_a)
    bb = jnp.sqrt(-2.0 * th / (1.0 - th)) * (ig * xc)

    n_grp = tb // 8
    a = a.reshape(n_grp, 8, w)
    bb = bb.reshape(n_grp, 8, w)
    sub = lax.broadcasted_iota(jnp.int32, (n_grp, 8, w), 1)
    s = 1
    while s < 8:
        shift, ok = (8 - s, sub < 8 - s) if reverse else (s, sub >= s)
        a_sh = jnp.where(ok, pltpu.roll(a, shift, 1), 1.0)
        b_sh = jnp.where(ok, pltpu.roll(bb, shift, 1), 0.0)
        bb = a * b_sh + bb
        a = a * a_sh
        s *= 2
    h_prev = carry[...]
    hs = [None] * n_grp
    for g in (range(n_grp - 1, -1, -1) if reverse else range(n_grp)):
        hg = a[g] * h_prev + bb[g]
        h_prev = hg[0:1, :] if reverse else hg[7:8, :]
        hs[g] = hg
    h = jnp.concatenate(hs, axis=0)
    carry[...] = h_prev
    st_ref[...] = carry[...]
    if reverse:
        y_ref[...] = ((hf_ref[...] + h) * jax.nn.gelu(gate_ref[...])).astype(y_ref.dtype)
    else:
        y_ref[...] = h


def _rglru_pass(p_rg, conv_w, conv_b, wg, gate_b, c_lam, h0, hf, *, reverse, tb):
    b, t, w2 = p_rg.shape
    w = w2 // 2
    nt = t // tb
    tmap = (lambda j: nt - 1 - j) if reverse else (lambda j: j)
    const = lambda shape: pl.BlockSpec(shape, lambda i, j: (0,) * len(shape))
    in_specs = _halo_specs(tb, t, w, 0, tmap) + [
        const((4, w)), const((1, w)), const((w, 2 * w)), const((1, 2 * w)), const((1, w)),
        pl.BlockSpec((None, 1, w), lambda i, j: (i, 0, 0))]
    args = [p_rg, p_rg, p_rg, conv_w, conv_b, wg, gate_b, c_lam, h0]
    if reverse:
        in_specs += [pl.BlockSpec((None, tb, w), lambda i, j: (i, tmap(j), 1)),
                     pl.BlockSpec((None, tb, w), lambda i, j: (i, tmap(j), 0))]
        args += [p_rg, hf]
    y, st = pl.pallas_call(
        functools.partial(_rglru_kernel, tb=tb, nt=nt, reverse=reverse),
        grid=(b, nt),
        in_specs=in_specs,
        out_specs=[pl.BlockSpec((None, tb, w), lambda i, j: (i, tmap(j), 0)),
                   pl.BlockSpec((None, 1, w), lambda i, j: (i, 0, 0))],
        out_shape=[jax.ShapeDtypeStruct((b, t, w), F32),
                   jax.ShapeDtypeStruct((b, 1, w), F32)],
        scratch_shapes=[pltpu.VMEM((1, w), F32)],
        compiler_params=_cparams(("parallel", "arbitrary")),
    )(*args)
    return y, st


def _bmm(a, b):
    return lax.dot_general(a, b, (((2,), (1,)), ((0,), (0,))), preferred_element_type=F32)


def _bmm_nt(a, b):
    return lax.dot_general(a, b, (((2,), (2,)), ((0,), (0,))), preferred_element_type=F32)


def _unit_tri_inverse(a, diag_mask):
    c = a.shape[-1]
    eye = (lax.broadcasted_iota(jnp.int32, (1, c, c), 1) == lax.broadcasted_iota(jnp.int32, (1, c, c), 2)).astype(F32)
    mm = lambda p, q: _bmm(p.astype(BF16), q.astype(BF16))
    ad = a * diag_mask
    x = eye - ad
    pw = ad
    k = 2
    while k < GDN_SUB:
        pw = mm(pw, pw)
        x = x + mm(x, pw)
        k *= 2
    n = mm(x, a - ad)
    nblk = c // GDN_SUB
    y = eye - n
    pw = n
    k = 2
    while k < nblk:
        pw = mm(pw, pw)
        y = y + mm(y, pw)
        k *= 2
    return mm(y, x)


def _gdn_prep_kernel(qkv_ref, qp_ref, qn_ref, ab_ref, cw_ref, ea_ref, dtb_ref,
                     qg_ref, kd_ref, kc_ref, wv_ref, at_ref, eg_ref, *, tb, nt):
    c = GDN_CHUNK
    j = pl.program_id(1)
    hw = GDN_WIDTH
    x = _silu(_conv4(qkv_ref[...], qp_ref[...], qn_ref[...], cw_ref, j == 0, j == nt - 1))

    ab = ab_ref[...]
    col = lax.broadcasted_iota(jnp.int32, (tb, AB_PAD), 1)
    is_a = (col & 4) == 0
    gbv = jnp.where(is_a, -ea_ref[...] * _softplus(ab + dtb_ref[...]), jax.nn.sigmoid(ab))

    ri = lax.broadcasted_iota(jnp.int32, (tb, tb), 0)
    ci = lax.broadcasted_iota(jnp.int32, (tb, tb), 1)
    same = (ri // c) == (ci // c)
    l_f = (same & (ci <= ri)).astype(BF16)
    l_b = (same & (ci >= ri)).astype(BF16)
    l_t = same.astype(BF16)
    gcum = jnp.where(col < 8, _dot_sel(l_f, gbv), _dot_sel(l_b, gbv))
    gtot = _dot_sel(l_t, gbv)

    cols_a = [dr * 8 + hd for dr in range(2) for hd in range(GDN_HEADS)]
    bcast = lambda v, cols: jnp.concatenate([jnp.broadcast_to(v[:, cc:cc + 1], (tb, 128)) for cc in cols], axis=1)
    gc_all = bcast(gcum, cols_a)
    gt_all = bcast(gtot, cols_a)
    be_all = bcast(gbv, [cc + 4 for cc in cols_a])

    qs, ks, vs = [], [], []
    for hd in range(GDN_HEADS):
        q = x[:, hd * GDN_DK:(hd + 1) * GDN_DK]
        k = x[:, hw + hd * GDN_DK: hw + (hd + 1) * GDN_DK]
        qs.append(q * lax.rsqrt(jnp.sum(q * q, axis=-1, keepdims=True) + EPS) * (GDN_DK ** -0.5))
        ks.append(k * lax.rsqrt(jnp.sum(k * k, axis=-1, keepdims=True) + EPS))
        vs.append(x[:, 2 * hw + hd * GDN_DV: 2 * hw + (hd + 1) * GDN_DV])
    inst = [(ch, dr, hd) for ch in range(tb // c) for dr in range(2) for hd in range(GDN_HEADS)]
    n_inst = len(inst)
    rows = lambda ch: slice(ch * c, (ch + 1) * c)
    lanes = lambda dr, hd: slice((dr * GDN_HEADS + hd) * 128, (dr * GDN_HEADS + hd + 1) * 128)
    stack = lambda f: jnp.stack([f(ch, dr, hd) for ch, dr, hd in inst], axis=0)
    qq = stack(lambda ch, dr, hd: qs[hd][rows(ch)])
    kk = stack(lambda ch, dr, hd: ks[hd][rows(ch)])
    vv = stack(lambda ch, dr, hd: vs[hd][rows(ch)])
    gcb = stack(lambda ch, dr, hd: gc_all[rows(ch), lanes(dr, hd)])
    gtb = stack(lambda ch, dr, hd: gt_all[rows(ch), lanes(dr, hd)])
    beb = stack(lambda ch, dr, hd: be_all[rows(ch), lanes(dr, hd)])

    ii = lax.broadcasted_iota(jnp.int32, (c, 128), 0)
    jj = lax.broadcasted_iota(jnp.int32, (c, 128), 1)
    incl_d = ((jj <= ii).astype(F32), ((jj >= ii) & (jj < c)).astype(F32))
    i64 = lax.broadcasted_iota(jnp.int32, (c, c), 0)
    j64 = lax.broadcasted_iota(jnp.int32, (c, c), 1)
    strict_d = ((j64 < i64).astype(F32), (j64 > i64).astype(F32))
    incl = stack(lambda ch, dr, hd: incl_d[dr])
    strict = stack(lambda ch, dr, hd: strict_d[dr])
    diag_blk = ((i64 // GDN_SUB) == (j64 // GDN_SUB)).astype(F32)[None]

    eg = jnp.exp(gcb)
    kb = kk * beb
    diag = gcb * (ii == jj).astype(F32)[None]
    gcr = _dot_sel(jnp.ones((c, c), BF16), jnp.concatenate([diag[g] for g in range(n_inst)], axis=1))
    gcr = jnp.stack([gcr[:, g * 128:(g + 1) * 128] for g in range(n_inst)], axis=0)
    decay = jnp.exp((gcb - gcr) * incl) * incl
    kpad = jnp.concatenate([kk.astype(BF16), jnp.zeros((n_inst, c, GDN_DK), BF16)], axis=1)
    qk = _bmm_nt(jnp.concatenate([qq, kb], axis=1).astype(BF16), kpad)
    attn = (qk[:, :c] * decay).astype(BF16)
    a_mat = (qk[:, c:] * decay)[:, :, :c] * strict
    tinv = _unit_tri_inverse(a_mat, diag_blk)
    rhs = jnp.concatenate([vv * beb, kb * eg], axis=2)
    sol = _bmm(tinv.astype(BF16), rhs.astype(BF16))
    qg = (qq * eg).astype(BF16)
    kd = (kk * jnp.exp(gtb - gcb)).astype(BF16)
    egt = jnp.exp(gtb[:, 0:8, :])
    for g, (ch, dr, hd) in enumerate(inst):
        rs = rows(ch)
        ls = slice(hd * 128, (hd + 1) * 128)
        qg_ref[dr, rs, ls] = qg[g]
        kd_ref[dr, rs, ls] = kd[g]
        kc_ref[dr, rs, ls] = sol[g, :, GDN_DV:].astype(BF16)
        wv_ref[dr, rs, ls] = sol[g, :, :GDN_DV]
        at_ref[dr, rs, ls] = attn[g]
        eg_ref[dr, ch, :, ls] = egt[g]


def _gdn_prep(qkv, ab, conv_w, ea_row, dtb_row, *, tb):
    b, t, cw = qkv.shape
    nt = t // tb
    nch = t // GDN_CHUNK
    cpb = tb // GDN_CHUNK
    const = lambda shape: pl.BlockSpec(shape, lambda i, j: (0,) * len(shape))
    hw = GDN_WIDTH
    big = lambda: pl.BlockSpec((2, None, tb, hw), lambda i, j: (0, i, j, 0))
    outs = pl.pallas_call(
        functools.partial(_gdn_prep_kernel, tb=tb, nt=nt),
        grid=(b, nt),
        in_specs=_halo_specs(tb, t, cw, 0, lambda j: j) + [
            pl.BlockSpec((None, tb, AB_PAD), lambda i, j: (i, j, 0)),
            const((4, cw)), const((1, AB_PAD)), const((1, AB_PAD))],
        out_specs=[big(), big(), big(), big(), big(),
                   pl.BlockSpec((2, None, cpb, 8, hw), lambda i, j: (0, i, j, 0, 0))],
        out_shape=[jax.ShapeDtypeStruct((2, b, t, hw), BF16),
                   jax.ShapeDtypeStruct((2, b, t, hw), BF16),
                   jax.ShapeDtypeStruct((2, b, t, hw), BF16),
                   jax.ShapeDtypeStruct((2, b, t, hw), F32),
                   jax.ShapeDtypeStruct((2, b, t, hw), BF16),
                   jax.ShapeDtypeStruct((2, b, nch, 8, hw), F32)],
        compiler_params=_cparams(("parallel", "parallel")),
    )(qkv, qkv, qkv, ab, conv_w, ea_row, dtb_row)
    return outs


def _gdn_scan_kernel(qg_f, kd_f, kc_f, wv_f, at_f, eg_f, qg_b, kd_b, kc_b, wv_b, at_b, eg_b, s0_ref,
                     of_ref, ob_ref, sfin_ref, s_scr, *, cps):
    c = GDN_CHUNK
    j = pl.program_id(1)

    @pl.when(j == 0)
    def _():
        s_scr[...] = s0_ref[...]

    dirs = ((qg_f, kd_f, kc_f, wv_f, at_f, eg_f), (qg_b, kd_b, kc_b, wv_b, at_b, eg_b))
    chains = [(dr, hd) for dr in range(2) for hd in range(GDN_HEADS)]
    ls = lambda hd: slice(hd * 128, (hd + 1) * 128)
    s = s_scr[...].reshape(2 * GDN_HEADS, GDN_DK, GDN_DV)
    for q in range(cps):
        chunk = (q, cps - 1 - q)
        rs = lambda dr: slice(chunk[dr] * c, (chunk[dr] + 1) * c)
        stack = lambda f: jnp.stack([f(dirs[dr], rs(dr), ls(hd), chunk[dr]) for dr, hd in chains], axis=0)
        kq = stack(lambda r, t, l, n: jnp.concatenate([r[2][t, l], r[0][t, l]], axis=0))
        wv = stack(lambda r, t, l, n: r[3][t, l])
        at = stack(lambda r, t, l, n: r[4][t, l][:, :c])
        eg = stack(lambda r, t, l, n: r[5][n, 0:1, l])
        r = _bmm(kq, s.astype(BF16))
        vb = (wv - r[:, :c]).astype(BF16)
        o = r[:, c:] + _bmm(at, vb)
        upd = []
        for g, (dr, hd) in enumerate(chains):
            (of_ref, ob_ref)[dr][rs(dr), ls(hd)] = o[g]
            upd.append(_dot_tn(dirs[dr][1][rs(dr), ls(hd)], vb[g]))
        s = s * eg + jnp.stack(upd, axis=0)
    s_scr[...] = s.reshape(2, GDN_HEADS, GDN_DK, GDN_DV)
    sfin_ref[...] = s_scr[...]


def _gdn_scan(prep, s0, *, cps):
    qg, kd, kc, wv, at, eg = prep
    _, b, t, hw = qg.shape
    c = GDN_CHUNK
    nblk = t // (c * cps)
    fw = lambda: pl.BlockSpec((None, None, cps * c, hw), lambda i, j: (0, i, j, 0))
    bw = lambda: pl.BlockSpec((None, None, cps * c, hw), lambda i, j: (1, i, nblk - 1 - j, 0))
    egf = pl.BlockSpec((None, None, cps, 8, hw), lambda i, j: (0, i, j, 0, 0))
    egb = pl.BlockSpec((None, None, cps, 8, hw), lambda i, j: (1, i, nblk - 1 - j, 0, 0))
    st = pl.BlockSpec((None, 2, GDN_HEADS, GDN_DK, GDN_DV), lambda i, j: (i, 0, 0, 0, 0))
    o_f, o_b, s_fin = pl.pallas_call(
        functools.partial(_gdn_scan_kernel, cps=cps),
        grid=(b, nblk),
        in_specs=[fw(), fw(), fw(), fw(), fw(), egf, bw(), bw(), bw(), bw(), bw(), egb, st],
        out_specs=[pl.BlockSpec((None, cps * c, hw), lambda i, j: (i, j, 0)),
                   pl.BlockSpec((None, cps * c, hw), lambda i, j: (i, nblk - 1 - j, 0)),
                   st],
        out_shape=[jax.ShapeDtypeStruct((b, t, hw), F32), jax.ShapeDtypeStruct((b, t, hw), F32),
                   jax.ShapeDtypeStruct((b, 2, GDN_HEADS, GDN_DK, GDN_DV), F32)],
        scratch_shapes=[pltpu.VMEM((2, GDN_HEADS, GDN_DK, GDN_DV), F32)],
        compiler_params=_cparams(("parallel", "arbitrary")),
    )(qg, kd, kc, wv, at, eg, qg, kd, kc, wv, at, eg, s0)
    return o_f, o_b, s_fin


def _out_kernel(x_ref, yrg_ref, of_ref, ob_ref, z_ref, ng_ref, w_ref, gt_ref, o_ref, *, ncol):
    o = of_ref[...] + ob_ref[...]
    z = z_ref[...]
    parts = [jnp.concatenate([yrg_ref[:, j, :] for j in range(ncol)], axis=0)]
    for hd in range(GDN_HEADS):
        ls = slice(hd * GDN_DV, (hd + 1) * GDN_DV)
        parts.append(_rms(o[:, ls], ng_ref[...]) * _silu(z[:, ls]))
    y = jnp.concatenate(parts, axis=1).astype(BF16)
    t2 = gt_ref[...] * _dot(y, w_ref[...])
    rows = t2.shape[0] // ncol
    for j in range(ncol):
        o_ref[:, j, :] = x_ref[:, j, :] + t2[j * rows:(j + 1) * rows]


def _out_project(x3, y_rg, o_f, o_b, z, norm_g, w_out, gt, *, tile):
    b, t, d = x3.shape
    w = RG_WIDTH
    rows = t // GRID_W
    ncol = tile // rows
    hw = GDN_WIDTH
    grid_view = lambda wd: pl.BlockSpec((None, rows, ncol, wd), lambda i, j: (i, 0, j, 0))
    cm = lambda wd: pl.BlockSpec((None, tile, wd), lambda i, j: (i, j, 0))
    x1 = pl.pallas_call(
        functools.partial(_out_kernel, ncol=ncol),
        grid=(b, GRID_W // ncol),
        in_specs=[grid_view(d), grid_view(w), cm(hw), cm(hw), cm(hw),
                  pl.BlockSpec((1, GDN_DV), lambda i, j: (0, 0)),
                  pl.BlockSpec((w + hw, d), lambda i, j: (0, 0)),
                  pl.BlockSpec((None, 1, d), lambda i, j: (i, 0, 0))],
        out_specs=grid_view(d),
        out_shape=jax.ShapeDtypeStruct((b, rows, GRID_W, d), F32),
        compiler_params=_cparams(("parallel", "parallel")),
    )(x3.reshape(b, rows, GRID_W, d), y_rg.reshape(b, rows, GRID_W, w), o_f, o_b, z, norm_g, w_out, gt)
    return x1.reshape(b, t, d)


def _fold_kernel(wq_ref, k_ref, o_ref):
    o_ref[...] = lax.dot_general(wq_ref[...], k_ref[...], (((1,), (1,)), ((), ())),
                                 preferred_element_type=F32, precision=lax.Precision.HIGHEST)


def _fold_keys(wq, keys):
    d, n = wq.shape
    nblk = n // PEER_HALF
    return pl.pallas_call(
        _fold_kernel,
        grid=(nblk,),
        in_specs=[pl.BlockSpec((d, PEER_HALF), lambda j: (0, j)),
                  pl.BlockSpec((None, PEER_NKEYS, PEER_HALF), lambda j: (j % 2, 0, 0))],
        out_specs=pl.BlockSpec((d, PEER_NKEYS), lambda j: (0, j)),
        out_shape=jax.ShapeDtypeStruct((d, nblk * PEER_NKEYS), F32),
        compiler_params=_cparams(("parallel",)),
    )(wq, keys)


def _extract16(vals, ids, s_scr, i_scr, *, fast, n_masked=0):
    n, p = vals.shape
    for r in range(PEER_TOPK):
        m = jnp.max(vals, axis=0, keepdims=True)
        eq = vals == m
        if fast:
            idx = _dot(ids, jnp.where(eq, 1.0, 0.0).astype(BF16))[0:1, :].astype(jnp.int32)
            vals = jnp.where(eq, NEG_INF, vals)
        else:
            idx = jnp.min(jnp.where(eq, ids, 1 << 20), axis=0, keepdims=True)
            vals = jnp.where(ids == idx, NEG_INF, vals)
        s_scr[r:r + 1, :] = m
        i_scr[r:r + 1, :] = idx
    if not fast:
        return None
    dropped = jnp.sum(jnp.where(vals == NEG_INF, 1.0, 0.0), axis=0, keepdims=True)
    return jnp.where(dropped == float(PEER_TOPK + n_masked), 0.0, 1.0)


def _peer_sel_kernel(x_ref, g_ref, sh_ref, sc_ref, wt_ref, h_ref, a_ref, b_ref, gate_ref,
                     sc_scr, s1, i1, s2, i2, ts, tp, a_t, b_t, g_t):
    k = PEER_TOPK
    h = _rms(x_ref[...], g_ref[...]) * (1.0 + sc_ref[...]) + sh_ref[...]
    hb = h.astype(BF16)
    h_ref[...] = hb
    sc_scr[...] = _dot_nt(wt_ref[...], hb)
    p = hb.shape[0]

    n_cand = 80
    n_masked = sum(8 - k // (r + 1) for r in range(1, 8))

    def cand_pos(row):
        return jnp.where(row < 16, row, jnp.where(row < 72, ((row - 16) // 8 + 1) * k + (row - 16) % 8,
                                                  (row - 72 + 8) * k))

    def head(hd, fast):
        if fast:
            key_ids = lax.broadcasted_iota(jnp.int32, (8, PEER_NKEYS), 1).astype(F32).astype(BF16)
            pos_ids = cand_pos(lax.broadcasted_iota(jnp.int32, (8, n_cand), 1)).astype(F32).astype(BF16)
        else:
            key_ids = lax.broadcasted_iota(jnp.int32, (PEER_NKEYS, p), 0)
            pos_ids = cand_pos(lax.broadcasted_iota(jnp.int32, (n_cand, p), 0))
        sub = lax.broadcasted_iota(jnp.int32, (8, p), 0)
        base = pl.multiple_of(hd * 2 * PEER_NKEYS, 2 * PEER_NKEYS)
        bad1 = _extract16(sc_scr[pl.ds(base, PEER_NKEYS), :], key_ids, s1, i1, fast=fast)
        bad2 = _extract16(sc_scr[pl.ds(base + PEER_NKEYS, PEER_NKEYS), :], key_ids, s2, i2, fast=fast)
        s1v, s2v = s1[...], s2[...]
        cands = [s1v[0:1, :] + s2v]
        for r in range(1, 8):
            cands.append(jnp.where(sub < k // (r + 1), s1v[r:r + 1, :] + s2v[0:8, :], NEG_INF))
        cands.append(s1v[8:16, :] + s2v[0:1, :])
        bad3 = _extract16(jnp.concatenate(cands, axis=0), pos_ids, ts, tp, fast=fast, n_masked=n_masked)
        top_s, top_p = ts[...], tp[...]
        rr = top_p >> 4
        cc = top_p & (k - 1)
        i1v, i2v = i1[...], i2[...]
        av = jnp.zeros_like(top_p)
        bv = jnp.zeros_like(top_p)
        for q in range(k):
            av = jnp.where(rr == q, i1v[q:q + 1, :], av)
            bv = jnp.where(cc == q, i2v[q:q + 1, :], bv)
        e = jnp.exp(top_s - top_s[0:1, :])
        gate = e / jnp.sum(e, axis=0, keepdims=True)
        o = pl.multiple_of(hd * k, k)
        a_t[pl.ds(o, k), :] = av
        b_t[pl.ds(o, k), :] = bv
        g_t[pl.ds(o, k), :] = gate
        return (bad1 + bad2 + bad3) if fast else None

    def head_step(hd, carry):
        bad = head(hd, True)

        @pl.when(jnp.max(bad) > 0.0)
        def _():
            head(hd, False)

        return carry

    lax.fori_loop(0, PEER_HEADS, head_step, 0)

    a_ref[...] = a_t[...].T
    b_ref[...] = b_t[...].T
    gate_ref[...] = g_t[...].T


def _peer_select(x1, g, sh, sc, wt, *, tile):
    b, t, d = x1.shape
    nt = t // tile
    nk = PEER_HEADS * PEER_TOPK
    nrow = wt.shape[0]
    vec = lambda: pl.BlockSpec((None, 1, d), lambda i, j: (i, 0, 0))
    tr = lambda: pl.BlockSpec((None, tile, nk), lambda i, j: (i, j, 0))
    k = PEER_TOPK
    return pl.pallas_call(
        _peer_sel_kernel,
        grid=(b, nt),
        in_specs=[pl.BlockSpec((None, tile, d), lambda i, j: (i, j, 0)),
                  pl.BlockSpec((1, d), lambda i, j: (0, 0)), vec(), vec(),
                  pl.BlockSpec((nrow, d), lambda i, j: (0, 0))],
        out_specs=[pl.BlockSpec((None, tile, d), lambda i, j: (i, j, 0)), tr(), tr(), tr()],
        out_shape=[jax.ShapeDtypeStruct((b, t, d), BF16),
                   jax.ShapeDtypeStruct((b, t, nk), jnp.int32),
                   jax.ShapeDtypeStruct((b, t, nk), jnp.int32),
                   jax.ShapeDtypeStruct((b, t, nk), F32)],
        scratch_shapes=[pltpu.VMEM((nrow, tile), F32),
                        pltpu.VMEM((k, tile), F32), pltpu.VMEM((k, tile), jnp.int32),
                        pltpu.VMEM((k, tile), F32), pltpu.VMEM((k, tile), jnp.int32),
                        pltpu.VMEM((k, tile), F32), pltpu.VMEM((k, tile), jnp.int32),
                        pltpu.VMEM((nk, tile), jnp.int32), pltpu.VMEM((nk, tile), jnp.int32),
                        pltpu.VMEM((nk, tile), F32)],
        compiler_params=_cparams(("parallel", "parallel")),
    )(x1, g.reshape(1, d), sh, sc, wt)


def _peer_mix_kernel(h_ref, a_ref, b_ref, gate_ref, ut_ref, v_ref, x_ref, gt_ref, fg_ref, o_ref,
                     m_scr, acc, *, tile, n_steps, pairs):
    nk = PEER_NKEYS
    half = nk // 2
    step = pl.program_id(2)
    hi_mask = jnp.uint32(0xFFFF0000)

    @pl.when((pl.program_id(0) == 0) & (pl.program_id(1) == 0) & (step == 0))
    def _():
        acc[...] = jnp.zeros(acc.shape, F32)

    @pl.when(step == 0)
    def _():
        sub = lax.broadcasted_iota(jnp.int32, (nk, a_ref.shape[1]), 0)
        a_of_row = jnp.where(sub < half, 2 * sub, 2 * (sub - half) + 1)

        def build(p, carry):
            arow = a_ref[pl.ds(p, 1), :]
            brow = b_ref[pl.ds(p, 1), :]
            grow = gate_ref[pl.ds(p, 1), :]
            xa = jnp.where(a_of_row == arow, 1.0, 0.0).astype(BF16)
            yb = jnp.where(sub == brow, 0.5 * grow, 0.0).astype(BF16)
            m = _dot_nt(xa, yb).astype(BF16).astype(F32)
            bits = lax.bitcast_convert_type(m, jnp.uint32)
            m_scr[pl.ds(pl.multiple_of(p * MASK_PITCH, 8), half), :] = (bits[half:] & hi_mask) | (bits[:half] >> 16)
            return carry

        lax.fori_loop(0, tile, build, 0, unroll=128)

    h = h_ref[...]
    parts = []
    for q in range(pairs):
        pr = step * pairs + q
        act = _gelu_times_2(_dot(h, ut_ref[:, q * 2 * nk:(q + 1) * 2 * nk]))
        w = m_scr[pl.ds(pr, tile, stride=MASK_PITCH), :]
        m_even = lax.bitcast_convert_type(w << 16, F32)
        m_odd = lax.bitcast_convert_type(w & hi_mask, F32)
        parts.append((act * jnp.concatenate([m_even, m_odd], axis=1)).astype(BF16))
    contrib = _dot(jnp.concatenate(parts, axis=1), v_ref[...])
    acc[...] = jnp.where(step == 0, contrib, acc[...] + contrib)

    @pl.when(step == n_steps - 1)
    def _():
        x2 = x_ref[...] + gt_ref[...] * acc[...]
        o_ref[...] = _rms(x2, fg_ref[...])


def _peer_mix(hb, aidx, bidx, gate, ut, v, x1, gt, final_g, *, tile, pairs):
    b, t, d = x1.shape
    nt = t // tile
    nk = PEER_NKEYS
    npk = aidx.shape[-1]
    n_steps = nk // (2 * pairs)
    eb = 2 * nk * pairs
    tok = lambda wd: pl.BlockSpec((None, tile, wd), lambda i, j, s: (i, j, 0))
    return pl.pallas_call(
        functools.partial(_peer_mix_kernel, tile=tile, n_steps=n_steps, pairs=pairs),
        grid=(b, nt, n_steps),
        in_specs=[tok(d), tok(npk), tok(npk), tok(npk),
                  pl.BlockSpec((d, eb), lambda i, j, s: (0, s)),
                  pl.BlockSpec((eb, d), lambda i, j, s: (s, 0)), tok(d),
                  pl.BlockSpec((None, 1, d), lambda i, j, s: (i, 0, 0)),
                  pl.BlockSpec((1, d), lambda i, j, s: (0, 0))],
        out_specs=tok(d),
        out_shape=jax.ShapeDtypeStruct((b, t, d), F32),
        scratch_shapes=[pltpu.VMEM((tile * MASK_PITCH, nk), jnp.uint32), pltpu.VMEM((tile, d), F32)],
        compiler_params=_cparams(("parallel", "parallel", "arbitrary")),
    )(hb, aidx, bidx, gate, ut, v, x1, gt, final_g.reshape(1, d))


def _block_diag(w):
    n, e, _ = w.shape
    eye = jnp.eye(n, dtype=w.dtype)
    return (eye[:, None, :, None] * w[:, :, None, :]).reshape(n * e, n * e)


def _mix_sequence(x3, ctx_mode, params, states, mods, *, rg_tb, gdn_tb, proj_tile):
    (norm1_g, w_rg, w_gdn, rg_conv_w, rg_conv_b, wg, gate_b, c_lam, gdn_conv_w, ea_row, dtb_row) = params
    sh1, sc1, mod_row = mods
    rg_h0_f, rg_h0_b, gdn_s0 = states
    p_rg, qkv, z, ab = _project(x3, norm1_g, sh1, sc1, mod_row, jnp.concatenate([w_rg, w_gdn], axis=1),
                                (2 * RG_WIDTH, 3 * GDN_WIDTH, GDN_WIDTH, AB_PAD), colmajor=not ctx_mode, tile=proj_tile)
    h_f, st_f = _rglru_pass(p_rg, rg_conv_w, rg_conv_b, wg[0], gate_b[0], c_lam[0], rg_h0_f, None,
                            reverse=False, tb=rg_tb)
    y_rg, st_b = _rglru_pass(p_rg, rg_conv_w, rg_conv_b, wg[1], gate_b[1], c_lam[1], rg_h0_b, h_f,
                             reverse=True, tb=rg_tb)
    prep = _gdn_prep(qkv, ab, gdn_conv_w, ea_row, dtb_row, tb=gdn_tb)
    o_f, o_b, s_fin = _gdn_scan(prep, gdn_s0, cps=min(GDN_SCAN_CHUNKS, x3.shape[1] // GDN_CHUNK))
    return y_rg, o_f, o_b, z, (st_f, st_b, s_fin)


def kernel(x, c, ctx, c_ctx, w_mod, b_mod, norm1_g, norm2_g, w_in, rg_conv_w, rg_conv_b, rg_gate_w, rg_gate_b,
           rg_lambda, gdn_conv_w, gdn_a_log, gdn_dt_bias, gdn_norm_g, w_out, peer_wq, peer_keys, peer_u, peer_v,
           final_g):
    b, t, d = x.shape
    depth = w_mod.shape[0]
    assert depth == 1, "context residual stream update is only needed for depth > 1"
    l = 0
    w = RG_WIDTH

    cc = jnp.zeros((16, d), F32).at[:b].set(c).at[b].set(c_ctx)
    w_rg = w_in[l][:, :2 * w].astype(BF16)
    n_ab = w_in.shape[2] - 2 * w - 4 * GDN_WIDTH
    w_gdn = jnp.concatenate([w_in[l][:, 2 * w:2 * w + 4 * GDN_WIDTH],
                             jnp.pad(w_in[l][:, 2 * w + 4 * GDN_WIDTH:], ((0, 0), (0, AB_PAD - n_ab)))],
                            axis=1).astype(BF16)
    wg = jnp.stack([jnp.concatenate([_block_diag(rg_gate_w[l, dr, 0]), _block_diag(rg_gate_w[l, dr, 1])], axis=1)
                    for dr in range(2)]).astype(BF16)
    gate_b = rg_gate_b[l].reshape(2, 1, 2 * w)
    c_lam = (-RG_C * jax.nn.softplus(-rg_lambda[l])).reshape(2, 1, w)
    ea = jnp.exp(gdn_a_log[l])
    pad4 = jnp.zeros((2, GDN_HEADS), F32)
    ea_row = jnp.pad(jnp.concatenate([ea, pad4], axis=1).reshape(1, -1), ((0, 0), (0, AB_PAD - n_ab)))
    dtb_row = jnp.pad(jnp.concatenate([gdn_dt_bias[l], pad4], axis=1).reshape(1, -1), ((0, 0), (0, AB_PAD - n_ab)))
    params = (norm1_g[l], w_rg, w_gdn, rg_conv_w[l], rg_conv_b[l].reshape(1, w), wg, gate_b, c_lam,
              gdn_conv_w[l], ea_row, dtb_row)

    mod = _modulation(cc, w_mod[l], b_mod[l])
    sh1, sc1, gt1, sh2, sc2, gt2 = [mod[:, i * d:(i + 1) * d].reshape(16, 1, d) for i in range(6)]

    tc = ctx.shape[1]
    zero_states = (jnp.zeros((b, 1, w), F32), jnp.zeros((b, 1, w), F32),
                   jnp.zeros((b, 2, GDN_HEADS, GDN_DK, GDN_DV), F32))
    _, _, _, _, ctx_states = _mix_sequence(ctx, True, params, zero_states, (sh1, sc1, lambda i: b),
                                           rg_tb=min(tc, RG_BLOCK), gdn_tb=min(tc, GDN_BLOCK),
                                           proj_tile=min(tc, TOKEN_TILE))

    tile = min(t, TOKEN_TILE)
    y_rg, o_f, o_b, z, _ = _mix_sequence(x, False, params, ctx_states, (sh1, sc1, lambda i: i),
                                         rg_tb=min(t, RG_BLOCK), gdn_tb=min(t, GDN_BLOCK), proj_tile=tile)
    x1 = _out_project(x, y_rg, o_f, o_b, z, gdn_norm_g[l].reshape(1, GDN_DV), w_out[l].astype(BF16), gt1, tile=tile)

    wfold_t = _fold_keys(peer_wq[l], peer_keys[l]).T.astype(BF16)
    hb, aidx, bidx, gate = _peer_select(x1, norm2_g[l], sh2, sc2, wfold_t, tile=tile)
    out = _peer_mix(hb, aidx, bidx, gate, peer_u[l].astype(BF16).T, peer_v[l].astype(BF16),
                    x1, gt2, final_g, tile=tile, pairs=MIX_PAIRS)
    return out
```

```python
import functools
import math

import jax
import jax.numpy as jnp
from jax import lax
from jax.experimental import pallas as pl
from jax.experimental.pallas import tpu as pltpu

F32 = jnp.float32
BF16 = jnp.bfloat16

GRID_W = 64
EPS = 1e-6
RG_WIDTH = 512
RG_C = 8.0
GDN_HEADS = 4
GDN_DK = 128
GDN_DV = 128
GDN_WIDTH = GDN_HEADS * GDN_DV
GDN_CHUNK = 64
GDN_SUB = 16
AB_PAD = 128
PEER_HEADS = 8
PEER_NKEYS = 128
PEER_HALF = 128
PEER_TOPK = 16
NEG_INF = float("-inf")
MASK_PITCH = 72

VMEM_LIMIT = 58 * 1024 * 1024

TOKEN_TILE = 512
RG_BLOCK = 512
GDN_BLOCK = 256
GDN_SCAN_CHUNKS = 8
MIX_PAIRS = 8


def _cparams(sem):
    return pltpu.CompilerParams(dimension_semantics=sem, vmem_limit_bytes=VMEM_LIMIT)


def _dot(a, b):
    return jnp.dot(a, b, preferred_element_type=F32)


def _dot_nt(a, b):
    return lax.dot_general(a, b, (((1,), (1,)), ((), ())), preferred_element_type=F32)


def _dot_tn(a, b):
    return lax.dot_general(a, b, (((0,), (0,)), ((), ())), preferred_element_type=F32)


def _split3(x):
    hi = x.astype(BF16)
    r = x - hi.astype(F32)
    mid = r.astype(BF16)
    lo = (r - mid.astype(F32)).astype(BF16)
    return hi, mid, lo


def _dot_sel(m, x):
    hi, mid, lo = _split3(x)
    return _dot(m, lo) + _dot(m, mid) + _dot(m, hi)


def _silu(x):
    return x * jax.nn.sigmoid(x)


def _softplus(x):
    return jnp.maximum(x, 0.0) + jnp.log1p(jnp.exp(-jnp.abs(x)))


def _gelu_times_2(x):
    c = math.sqrt(2.0 / math.pi)
    return x * (1.0 + jnp.tanh(x * (c + (c * 0.044715) * (x * x))))


def _rms(x, g):
    return x * lax.rsqrt(jnp.mean(x * x, axis=-1, keepdims=True) + EPS) * g


def _mod_kernel(c_ref, w_ref, b_ref, o_ref):
    s = _silu(c_ref[...])
    o_ref[...] = jnp.dot(s, w_ref[...], preferred_element_type=F32,
                         precision=lax.Precision.HIGHEST) + b_ref[...]


def _modulation(cc, w_mod, b_mod):
    m, d = cc.shape
    n = w_mod.shape[1]
    tn = 1536
    return pl.pallas_call(
        _mod_kernel,
        grid=(n // tn,),
        in_specs=[pl.BlockSpec((m, d), lambda j: (0, 0)),
                  pl.BlockSpec((d, tn), lambda j: (0, j)),
                  pl.BlockSpec((1, tn), lambda j: (0, j))],
        out_specs=pl.BlockSpec((m, tn), lambda j: (0, j)),
        out_shape=jax.ShapeDtypeStruct((m, n), F32),
        compiler_params=_cparams(("arbitrary",)),
    )(cc, w_mod, b_mod.reshape(1, n))


def _proj_kernel(x_ref, g_ref, sh_ref, sc_ref, w_ref, *o_refs, ncol, widths):
    if ncol > 1:
        x = jnp.concatenate([x_ref[:, j, :] for j in range(ncol)], axis=0)
    else:
        x = x_ref[...]
    h = _rms(x, g_ref[...]) * (1.0 + sc_ref[...]) + sh_ref[...]
    o = _dot(h.astype(BF16), w_ref[...])
    off = 0
    for n_out, (o_ref, wd) in enumerate(zip(o_refs, widths)):
        if ncol > 1 and n_out == 0:
            rows = o.shape[0] // ncol
            for j in range(ncol):
                o_ref[:, j, :] = o[j * rows:(j + 1) * rows, off:off + wd]
        else:
            o_ref[...] = o[:, off:off + wd]
        off += wd


def _project(x3, g, sh, sc, mod_row, w, widths, *, colmajor, tile):
    b, t, d = x3.shape
    n = w.shape[1]
    flat = lambda wd: pl.BlockSpec((None, tile, wd), lambda i, j: (i, j, 0))
    if colmajor:
        rows = t // GRID_W
        ncol = tile // rows
        grid_view = lambda wd: pl.BlockSpec((None, rows, ncol, wd), lambda i, j: (i, 0, j, 0))
        xv = x3.reshape(b, rows, GRID_W, d)
        x_spec = grid_view(d)
        nt = GRID_W // ncol
        out_specs = [grid_view(widths[0])] + [flat(wd) for wd in widths[1:]]
        out_shape = ([jax.ShapeDtypeStruct((b, rows, GRID_W, widths[0]), F32)]
                     + [jax.ShapeDtypeStruct((b, t, wd), F32) for wd in widths[1:]])
    else:
        ncol = 1
        xv = x3
        x_spec = flat(d)
        nt = t // tile
        out_specs = [flat(wd) for wd in widths]
        out_shape = [jax.ShapeDtypeStruct((b, t, wd), F32) for wd in widths]
    vec = lambda: pl.BlockSpec((None, 1, d), lambda i, j: (mod_row(i), 0, 0))
    outs = pl.pallas_call(
        functools.partial(_proj_kernel, ncol=ncol, widths=tuple(widths)),
        grid=(b, nt),
        in_specs=[x_spec, pl.BlockSpec((1, d), lambda i, j: (0, 0)), vec(), vec(),
                  pl.BlockSpec((d, n), lambda i, j: (0, 0))],
        out_specs=out_specs,
        out_shape=out_shape,
        compiler_params=_cparams(("parallel", "parallel")),
    )(xv, g.reshape(1, d), sh, sc, w)
    return [outs[0].reshape(b, t, widths[0])] + list(outs[1:])


def _conv4(cur, prev8, nxt8, w_ref, first, last):
    tb, c = cur.shape
    g = tb // 8
    prev8 = jnp.where(first, 0.0, prev8)
    nxt8 = jnp.where(last, 0.0, nxt8)
    ext = jnp.concatenate([prev8, cur, nxt8], axis=0).reshape(g + 2, 8, c)
    sub = lax.broadcasted_iota(jnp.int32, (g, 8, c), 1)

    def back(k):
        r = pltpu.roll(ext, k, 1)
        return jnp.where(sub >= k, r[1:g + 1], r[0:g])

    r = pltpu.roll(ext, 7, 1)
    xp1 = jnp.where(sub < 7, r[1:g + 1], r[2:g + 2])
    y = (w_ref[0:1, :] * back(2) + w_ref[1:2, :] * back(1) + w_ref[2:3, :] * ext[1:g + 1] + w_ref[3:4, :] * xp1)
    return y.reshape(tb, c)


def _halo_specs(tb, t, width, lane_blk, tmap):
    r = tb // 8
    nb8 = t // 8
    return [pl.BlockSpec((None, tb, width), lambda i, j: (i, tmap(j), lane_blk)),
            pl.BlockSpec((None, 8, width), lambda i, j: (i, jnp.maximum(tmap(j) * r - 1, 0), lane_blk)),
            pl.BlockSpec((None, 8, width), lambda i, j: (i, jnp.minimum((tmap(j) + 1) * r, nb8 - 1), lane_blk))]


def _rglru_kernel(*refs, tb, nt, reverse):
    if reverse:
        (u_ref, up_ref, un_ref, cw_ref, cb_ref, wg_ref, gb_ref, cl_ref, h0_ref, gate_ref, hf_ref,
         y_ref, st_ref, carry) = refs
    else:
        (u_ref, up_ref, un_ref, cw_ref, cb_ref, wg_ref, gb_ref, cl_ref, h0_ref,
         y_ref, st_ref, carry) = refs
    j = pl.program_id(1)
    tblk = (nt - 1 - j) if reverse else j
    w = u_ref.shape[-1]

    @pl.when(j == 0)
    def _():
        carry[...] = h0_ref[...]

    u = u_ref[...]
    xc = _conv4(u, up_ref[...], un_ref[...], cw_ref, tblk == 0, tblk == nt - 1) + cb_ref[...]
    gates = _dot(xc.astype(BF16), wg_ref[...]) + gb_ref[...]
    r = jax.nn.sigmoid(gates[:, :w])
    ig = jax.nn.sigmoid(gates[:, w:])
    log_a = r * cl_ref[...]
    a = jnp.exp(log_a)
    th = jnp.tanh(log_a)
    bb = jnp.sqrt(-2.0 * th / (1.0 - th)) * (ig * xc)

    n_grp = tb // 8
    a = a.reshape(n_grp, 8, w)
    bb = bb.reshape(n_grp, 8, w)
    sub = lax.broadcasted_iota(jnp.int32, (n_grp, 8, w), 1)
    s = 1
    while s < 8:
        shift, ok = (8 - s, sub < 8 - s) if reverse else (s, sub >= s)
        a_sh = jnp.where(ok, pltpu.roll(a, shift, 1), 1.0)
        b_sh = jnp.where(ok, pltpu.roll(bb, shift, 1), 0.0)
        bb = a * b_sh + bb
        a = a * a_sh
        s *= 2
    h_prev = carry[...]
    hs = [None] * n_grp
    for g in (range(n_grp - 1, -1, -1) if reverse else range(n_grp)):
        hg = a[g] * h_prev + bb[g]
        h_prev = hg[0:1, :] if reverse else hg[7:8, :]
        hs[g] = hg
    h = jnp.concatenate(hs, axis=0)
    carry[...] = h_prev
    st_ref[...] = carry[...]
    if reverse:
        y_ref[...] = ((hf_ref[...] + h) * jax.nn.gelu(gate_ref[...])).astype(y_ref.dtype)
    else:
        y_ref[...] = h


def _rglru_pass(p_rg, conv_w, conv_b, wg, gate_b, c_lam, h0, hf, *, reverse, tb):
    b, t, w2 = p_rg.shape
    w = w2 // 2
    nt = t // tb
    tmap = (lambda j: nt - 1 - j) if reverse else (lambda j: j)
    const = lambda shape: pl.BlockSpec(shape, lambda i, j: (0,) * len(shape))
    in_specs = _halo_specs(tb, t, w, 0, tmap) + [
        const((4, w)), const((1, w)), const((w, 2 * w)), const((1, 2 * w)), const((1, w)),
        pl.BlockSpec((None, 1, w), lambda i, j: (i, 0, 0))]
    args = [p_rg, p_rg, p_rg, conv_w, conv_b, wg, gate_b, c_lam, h0]
    if reverse:
        in_specs += [pl.BlockSpec((None, tb, w), lambda i, j: (i, tmap(j), 1)),
                     pl.BlockSpec((None, tb, w), lambda i, j: (i, tmap(j), 0))]
        args += [p_rg, hf]
    y, st = pl.pallas_call(
        functools.partial(_rglru_kernel, tb=tb, nt=nt, reverse=reverse),
        grid=(b, nt),
        in_specs=in_specs,
        out_specs=[pl.BlockSpec((None, tb, w), lambda i, j: (i, tmap(j), 0)),
                   pl.BlockSpec((None, 1, w), lambda i, j: (i, 0, 0))],
        out_shape=[jax.ShapeDtypeStruct((b, t, w), F32),
                   jax.ShapeDtypeStruct((b, 1, w), F32)],
        scratch_shapes=[pltpu.VMEM((1, w), F32)],
        compiler_params=_cparams(("parallel", "arbitrary")),
    )(*args)
    return y, st


def _bmm(a, b):
    return lax.dot_general(a, b, (((2,), (1,)), ((0,), (0,))), preferred_element_type=F32)


def _bmm_nt(a, b):
    return lax.dot_general(a, b, (((2,), (2,)), ((0,), (0,))), preferred_element_type=F32)


def _unit_tri_inverse(a, diag_mask):
    c = a.shape[-1]
    eye = (lax.broadcasted_iota(jnp.int32, (1, c, c), 1) == lax.broadcasted_iota(jnp.int32, (1, c, c), 2)).astype(F32)
    mm = lambda p, q: _bmm(p.astype(BF16), q.astype(BF16))
    ad = a * diag_mask
    x = eye - ad
    pw = ad
    k = 2
    while k < GDN_SUB:
        pw = mm(pw, pw)
        x = x + mm(x, pw)
        k *= 2
    n = mm(x, a - ad)
    nblk = c // GDN_SUB
    y = eye - n
    pw = n
    k = 2
    while k < nblk:
        pw = mm(pw, pw)
        y = y + mm(y, pw)
        k *= 2
    return mm(y, x)


def _gdn_prep_kernel(qkv_ref, qp_ref, qn_ref, ab_ref, cw_ref, ea_ref, dtb_ref,
                     qg_ref, kd_ref, kc_ref, wv_ref, at_ref, eg_ref, *, tb, nt):
    c = GDN_CHUNK
    j = pl.program_id(1)
    hw = GDN_WIDTH
    x = _silu(_conv4(qkv_ref[...], qp_ref[...], qn_ref[...], cw_ref, j == 0, j == nt - 1))

    ab = ab_ref[...]
    col = lax.broadcasted_iota(jnp.int32, (tb, AB_PAD), 1)
    is_a = (col & 4) == 0
    gbv = jnp.where(is_a, -ea_ref[...] * _softplus(ab + dtb_ref[...]), jax.nn.sigmoid(ab))

    ri = lax.broadcasted_iota(jnp.int32, (tb, tb), 0)
    ci = lax.broadcasted_iota(jnp.int32, (tb, tb), 1)
    same = (ri // c) == (ci // c)
    l_f = (same & (ci <= ri)).astype(BF16)
    l_b = (same & (ci >= ri)).astype(BF16)
    l_t = same.astype(BF16)
    gcum = jnp.where(col < 8, _dot_sel(l_f, gbv), _dot_sel(l_b, gbv))
    gtot = _dot_sel(l_t, gbv)

    cols_a = [dr * 8 + hd for dr in range(2) for hd in range(GDN_HEADS)]
    bcast = lambda v, cols: jnp.concatenate([jnp.broadcast_to(v[:, cc:cc + 1], (tb, 128)) for cc in cols], axis=1)
    gc_all = bcast(gcum, cols_a)
    gt_all = bcast(gtot, cols_a)
    be_all = bcast(gbv, [cc + 4 for cc in cols_a])

    qs, ks, vs = [], [], []
    for hd in range(GDN_HEADS):
        q = x[:, hd * GDN_DK:(hd + 1) * GDN_DK]
        k = x[:, hw + hd * GDN_DK: hw + (hd + 1) * GDN_DK]
        qs.append(q * lax.rsqrt(jnp.sum(q * q, axis=-1, keepdims=True) + EPS) * (GDN_DK ** -0.5))
        ks.append(k * lax.rsqrt(jnp.sum(k * k, axis=-1, keepdims=True) + EPS))
        vs.append(x[:, 2 * hw + hd * GDN_DV: 2 * hw + (hd + 1) * GDN_DV])
    inst = [(ch, dr, hd) for ch in range(tb // c) for dr in range(2) for hd in range(GDN_HEADS)]
    n_inst = len(inst)
    rows = lambda ch: slice(ch * c, (ch + 1) * c)
    lanes = lambda dr, hd: slice((dr * GDN_HEADS + hd) * 128, (dr * GDN_HEADS + hd + 1) * 128)
    stack = lambda f: jnp.stack([f(ch, dr, hd) for ch, dr, hd in inst], axis=0)
    qq = stack(lambda ch, dr, hd: qs[hd][rows(ch)])
    kk = stack(lambda ch, dr, hd: ks[hd][rows(ch)])
    vv = stack(lambda ch, dr, hd: vs[hd][rows(ch)])
    gcb = stack(lambda ch, dr, hd: gc_all[rows(ch), lanes(dr, hd)])
    gtb = stack(lambda ch, dr, hd: gt_all[rows(ch), lanes(dr, hd)])
    beb = stack(lambda ch, dr, hd: be_all[rows(ch), lanes(dr, hd)])

    ii = lax.broadcasted_iota(jnp.int32, (c, 128), 0)
    jj = lax.broadcasted_iota(jnp.int32, (c, 128), 1)
    incl_d = ((jj <= ii).astype(F32), ((jj >= ii) & (jj < c)).astype(F32))
    i64 = lax.broadcasted_iota(jnp.int32, (c, c), 0)
    j64 = lax.broadcasted_iota(jnp.int32, (c, c), 1)
    strict_d = ((j64 < i64).astype(F32), (j64 > i64).astype(F32))
    incl = stack(lambda ch, dr, hd: incl_d[dr])
    strict = stack(lambda ch, dr, hd: strict_d[dr])
    diag_blk = ((i64 // GDN_SUB) == (j64 // GDN_SUB)).astype(F32)[None]

    eg = jnp.exp(gcb)
    kb = kk * beb
    diag = gcb * (ii == jj).astype(F32)[None]
    gcr = _dot_sel(jnp.ones((c, c), BF16), jnp.concatenate([diag[g] for g in range(n_inst)], axis=1))
    gcr = jnp.stack([gcr[:, g * 128:(g + 1) * 128] for g in range(n_inst)], axis=0)
    decay = jnp.exp((gcb - gcr) * incl) * incl
    kpad = jnp.concatenate([kk.astype(BF16), jnp.zeros((n_inst, c, GDN_DK), BF16)], axis=1)
    qk = _bmm_nt(jnp.concatenate([qq, kb], axis=1).astype(BF16), kpad)
    attn = (qk[:, :c] * decay).astype(BF16)
    a_mat = (qk[:, c:] * decay)[:, :, :c] * strict
    tinv = _unit_tri_inverse(a_mat, diag_blk)
    rhs = jnp.concatenate([vv * beb, kb * eg], axis=2)
    sol = _bmm(tinv.astype(BF16), rhs.astype(BF16))
    qg = (qq * eg).astype(BF16)
    kd = (kk * jnp.exp(gtb - gcb)).astype(BF16)
    egt = jnp.exp(gtb[:, 0:8, :])
    for g, (ch, dr, hd) in enumerate(inst):
        rs = rows(ch)
        ls = slice(hd * 128, (hd + 1) * 128)
        qg_ref[dr, rs, ls] = qg[g]
        kd_ref[dr, rs, ls] = kd[g]
        kc_ref[dr, rs, ls] = sol[g, :, GDN_DV:].astype(BF16)
        wv_ref[dr, rs, ls] = sol[g, :, :GDN_DV]
        at_ref[dr, rs, ls] = attn[g]
        eg_ref[dr, ch, :, ls] = egt[g]


def _gdn_prep(qkv, ab, conv_w, ea_row, dtb_row, *, tb):
    b, t, cw = qkv.shape
    nt = t // tb
    nch = t // GDN_CHUNK
    cpb = tb // GDN_CHUNK
    const = lambda shape: pl.BlockSpec(shape, lambda i, j: (0,) * len(shape))
    hw = GDN_WIDTH
    big = lambda: pl.BlockSpec((2, None, tb, hw), lambda i, j: (0, i, j, 0))
    outs = pl.pallas_call(
        functools.partial(_gdn_prep_kernel, tb=tb, nt=nt),
        grid=(b, nt),
        in_specs=_halo_specs(tb, t, cw, 0, lambda j: j) + [
            pl.BlockSpec((None, tb, AB_PAD), lambda i, j: (i, j, 0)),
            const((4, cw)), const((1, AB_PAD)), const((1, AB_PAD))],
        out_specs=[big(), big(), big(), big(), big(),
                   pl.BlockSpec((2, None, cpb, 8, hw), lambda i, j: (0, i, j, 0, 0))],
        out_shape=[jax.ShapeDtypeStruct((2, b, t, hw), BF16),
                   jax.ShapeDtypeStruct((2, b, t, hw), BF16),
                   jax.ShapeDtypeStruct((2, b, t, hw), BF16),
                   jax.ShapeDtypeStruct((2, b, t, hw), F32),
                   jax.ShapeDtypeStruct((2, b, t, hw), BF16),
                   jax.ShapeDtypeStruct((2, b, nch, 8, hw), F32)],
        compiler_params=_cparams(("parallel", "parallel")),
    )(qkv, qkv, qkv, ab, conv_w, ea_row, dtb_row)
    return outs


def _gdn_scan_kernel(qg_f, kd_f, kc_f, wv_f, at_f, eg_f, qg_b, kd_b, kc_b, wv_b, at_b, eg_b, s0_ref,
                     of_ref, ob_ref, sfin_ref, s_scr, *, cps):
    c = GDN_CHUNK
    j = pl.program_id(1)

    @pl.when(j == 0)
    def _():
        s_scr[...] = s0_ref[...]

    dirs = ((qg_f, kd_f, kc_f, wv_f, at_f, eg_f), (qg_b, kd_b, kc_b, wv_b, at_b, eg_b))
    chains = [(dr, hd) for dr in range(2) for hd in range(GDN_HEADS)]
    ls = lambda hd: slice(hd * 128, (hd + 1) * 128)
    s = s_scr[...].reshape(2 * GDN_HEADS, GDN_DK, GDN_DV)
    for q in range(cps):
        chunk = (q, cps - 1 - q)
        rs = lambda dr: slice(chunk[dr] * c, (chunk[dr] + 1) * c)
        stack = lambda f: jnp.stack([f(dirs[dr], rs(dr), ls(hd), chunk[dr]) for dr, hd in chains], axis=0)
        kq = stack(lambda r, t, l, n: jnp.concatenate([r[2][t, l], r[0][t, l]], axis=0))
        wv = stack(lambda r, t, l, n: r[3][t, l])
        at = stack(lambda r, t, l, n: r[4][t, l][:, :c])
        eg = stack(lambda r, t, l, n: r[5][n, 0:1, l])
        r = _bmm(kq, s.astype(BF16))
        vb = (wv - r[:, :c]).astype(BF16)
        o = r[:, c:] + _bmm(at, vb)
        upd = []
        for g, (dr, hd) in enumerate(chains):
            (of_ref, ob_ref)[dr][rs(dr), ls(hd)] = o[g]
            upd.append(_dot_tn(dirs[dr][1][rs(dr), ls(hd)], vb[g]))
        s = s * eg + jnp.stack(upd, axis=0)
    s_scr[...] = s.reshape(2, GDN_HEADS, GDN_DK, GDN_DV)
    sfin_ref[...] = s_scr[...]


def _gdn_scan(prep, s0, *, cps):
    qg, kd, kc, wv, at, eg = prep
    _, b, t, hw = qg.shape
    c = GDN_CHUNK
    nblk = t // (c * cps)
    fw = lambda: pl.BlockSpec((None, None, cps * c, hw), lambda i, j: (0, i, j, 0))
    bw = lambda: pl.BlockSpec((None, None, cps * c, hw), lambda i, j: (1, i, nblk - 1 - j, 0))
    egf = pl.BlockSpec((None, None, cps, 8, hw), lambda i, j: (0, i, j, 0, 0))
    egb = pl.BlockSpec((None, None, cps, 8, hw), lambda i, j: (1, i, nblk - 1 - j, 0, 0))
    st = pl.BlockSpec((None, 2, GDN_HEADS, GDN_DK, GDN_DV), lambda i, j: (i, 0, 0, 0, 0))
    o_f, o_b, s_fin = pl.pallas_call(
        functools.partial(_gdn_scan_kernel, cps=cps),
        grid=(b, nblk),
        in_specs=[fw(), fw(), fw(), fw(), fw(), egf, bw(), bw(), bw(), bw(), bw(), egb, st],
        out_specs=[pl.BlockSpec((None, cps * c, hw), lambda i, j: (i, j, 0)),
                   pl.BlockSpec((None, cps * c, hw), lambda i, j: (i, nblk - 1 - j, 0)),
                   st],
        out_shape=[jax.ShapeDtypeStruct((b, t, hw), F32), jax.ShapeDtypeStruct((b, t, hw), F32),
                   jax.ShapeDtypeStruct((b, 2, GDN_HEADS, GDN_DK, GDN_DV), F32)],
        scratch_shapes=[pltpu.VMEM((2, GDN_HEADS, GDN_DK, GDN_DV), F32)],
        compiler_params=_cparams(("parallel", "arbitrary")),
    )(qg, kd, kc, wv, at, eg, qg, kd, kc, wv, at, eg, s0)
    return o_f, o_b, s_fin


def _out_kernel(x_ref, yrg_ref, of_ref, ob_ref, z_ref, ng_ref, w_ref, gt_ref, o_ref, *, ncol):
    o = of_ref[...] + ob_ref[...]
    z = z_ref[...]
    parts = [jnp.concatenate([yrg_ref[:, j, :] for j in range(ncol)], axis=0)]
    for hd in range(GDN_HEADS):
        ls = slice(hd * GDN_DV, (hd + 1) * GDN_DV)
        parts.append(_rms(o[:, ls], ng_ref[...]) * _silu(z[:, ls]))
    y = jnp.concatenate(parts, axis=1).astype(BF16)
    t2 = gt_ref[...] * _dot(y, w_ref[...])
    rows = t2.shape[0] // ncol
    for j in range(ncol):
        o_ref[:, j, :] = x_ref[:, j, :] + t2[j * rows:(j + 1) * rows]


def _out_project(x3, y_rg, o_f, o_b, z, norm_g, w_out, gt, *, tile):
    b, t, d = x3.shape
    w = RG_WIDTH
    rows = t // GRID_W
    ncol = tile // rows
    hw = GDN_WIDTH
    grid_view = lambda wd: pl.BlockSpec((None, rows, ncol, wd), lambda i, j: (i, 0, j, 0))
    cm = lambda wd: pl.BlockSpec((None, tile, wd), lambda i, j: (i, j, 0))
    x1 = pl.pallas_call(
        functools.partial(_out_kernel, ncol=ncol),
        grid=(b, GRID_W // ncol),
        in_specs=[grid_view(d), grid_view(w), cm(hw), cm(hw), cm(hw),
                  pl.BlockSpec((1, GDN_DV), lambda i, j: (0, 0)),
                  pl.BlockSpec((w + hw, d), lambda i, j: (0, 0)),
                  pl.BlockSpec((None, 1, d), lambda i, j: (i, 0, 0))],
        out_specs=grid_view(d),
        out_shape=jax.ShapeDtypeStruct((b, rows, GRID_W, d), F32),
        compiler_params=_cparams(("parallel", "parallel")),
    )(x3.reshape(b, rows, GRID_W, d), y_rg.reshape(b, rows, GRID_W, w), o_f, o_b, z, norm_g, w_out, gt)
    return x1.reshape(b, t, d)


def _fold_kernel(wq_ref, k_ref, o_ref):
    o_ref[...] = lax.dot_general(wq_ref[...], k_ref[...], (((1,), (1,)), ((), ())),
                                 preferred_element_type=F32, precision=lax.Precision.HIGHEST)


def _fold_keys(wq, keys):
    d, n = wq.shape
    nblk = n // PEER_HALF
    return pl.pallas_call(
        _fold_kernel,
        grid=(nblk,),
        in_specs=[pl.BlockSpec((d, PEER_HALF), lambda j: (0, j)),
                  pl.BlockSpec((None, PEER_NKEYS, PEER_HALF), lambda j: (j % 2, 0, 0))],
        out_specs=pl.BlockSpec((d, PEER_NKEYS), lambda j: (0, j)),
        out_shape=jax.ShapeDtypeStruct((d, nblk * PEER_NKEYS), F32),
        compiler_params=_cparams(("parallel",)),
    )(wq, keys)


def _extract16(vals, ids, s_scr, i_scr, *, fast, n_masked=0):
    n, p = vals.shape
    for r in range(PEER_TOPK):
        m = jnp.max(vals, axis=0, keepdims=True)
        eq = vals == m
        if fast:
            idx = _dot(ids, jnp.where(eq, 1.0, 0.0).astype(BF16))[0:1, :].astype(jnp.int32)
            vals = jnp.where(eq, NEG_INF, vals)
        else:
            idx = jnp.min(jnp.where(eq, ids, 1 << 20), axis=0, keepdims=True)
            vals = jnp.where(ids == idx, NEG_INF, vals)
        s_scr[r:r + 1, :] = m
        i_scr[r:r + 1, :] = idx
    if not fast:
        return None
    dropped = jnp.sum(jnp.where(vals == NEG_INF, 1.0, 0.0), axis=0, keepdims=True)
    return jnp.where(dropped == float(PEER_TOPK + n_masked), 0.0, 1.0)


def _peer_sel_kernel(x_ref, g_ref, sh_ref, sc_ref, wt_ref, h_ref, a_ref, b_ref, gate_ref,
                     sc_scr, s1, i1, s2, i2, ts, tp, a_t, b_t, g_t):
    k = PEER_TOPK
    h = _rms(x_ref[...], g_ref[...]) * (1.0 + sc_ref[...]) + sh_ref[...]
    hb = h.astype(BF16)
    h_ref[...] = hb
    sc_scr[...] = _dot_nt(wt_ref[...], hb)
    p = hb.shape[0]

    n_cand = 56
    n_masked = 6

    def cand_pos(row):
        o = row % 8
        pos = jnp.where(row < 16, row, k + o)
        pos = jnp.where(row >= 24, jnp.where(o < 5, 2 * k + o, k * k - 1), pos)
        pos = jnp.where(row >= 32, jnp.where(o < 4, 3 * k + o, jnp.where(o < 7, 4 * k + o - 4, k * k - 1)), pos)
        pos = jnp.where(row >= 40, jnp.where(o < 6, (5 + o // 2) * k + o % 2, k * k - 1), pos)
        return jnp.where(row >= 48, (8 + o) * k, pos)

    def head(hd, fast):
        if fast:
            key_ids = lax.broadcasted_iota(jnp.int32, (8, PEER_NKEYS), 1).astype(F32).astype(BF16)
            pos_ids = cand_pos(lax.broadcasted_iota(jnp.int32, (8, n_cand), 1)).astype(F32).astype(BF16)
        else:
            key_ids = lax.broadcasted_iota(jnp.int32, (PEER_NKEYS, p), 0)
            pos_ids = cand_pos(lax.broadcasted_iota(jnp.int32, (n_cand, p), 0))
        sub = lax.broadcasted_iota(jnp.int32, (8, p), 0)
        base = pl.multiple_of(hd * 2 * PEER_NKEYS, 2 * PEER_NKEYS)
        bad1 = _extract16(sc_scr[pl.ds(base, PEER_NKEYS), :], key_ids, s1, i1, fast=fast)
        bad2 = _extract16(sc_scr[pl.ds(base + PEER_NKEYS, PEER_NKEYS), :], key_ids, s2, i2, fast=fast)
        s1v, s2v = s1[...], s2[...]
        s2lo = s2v[0:8, :]
        row1 = lambda r: s1v[r:r + 1, :]
        cands = [row1(0) + s2v,
                 row1(1) + s2lo,
                 jnp.where(sub < 5, row1(2) + s2lo, NEG_INF),
                 jnp.where(sub < 7, jnp.where(sub < 4, row1(3), row1(4))
                           + jnp.where(sub < 4, s2lo, pltpu.roll(s2lo, 4, 0)), NEG_INF),
                 jnp.where(sub < 6, jnp.where(sub < 2, row1(5), jnp.where(sub < 4, row1(6), row1(7)))
                           + jnp.where(sub % 2 == 0, s2v[0:1, :], s2v[1:2, :]), NEG_INF),
                 s1v[8:16, :] + s2v[0:1, :]]
        bad3 = _extract16(jnp.concatenate(cands, axis=0), pos_ids, ts, tp, fast=fast, n_masked=n_masked)
        top_s, top_p = ts[...], tp[...]
        rr = top_p >> 4
        cc = top_p & (k - 1)
        i1v, i2v = i1[...], i2[...]
        av = jnp.zeros_like(top_p)
        bv = jnp.zeros_like(top_p)
        for q in range(k):
            av = jnp.where(rr == q, i1v[q:q + 1, :], av)
            bv = jnp.where(cc == q, i2v[q:q + 1, :], bv)
        e = jnp.exp(top_s - top_s[0:1, :])
        gate = e / jnp.sum(e, axis=0, keepdims=True)
        o = pl.multiple_of(hd * k, k)
        a_t[pl.ds(o, k), :] = av
        b_t[pl.ds(o, k), :] = bv
        g_t[pl.ds(o, k), :] = gate
        return (bad1 + bad2 + bad3) if fast else None

    def head_step(hd, carry):
        bad = head(hd, True)

        @pl.when(jnp.max(bad) > 0.0)
        def _():
            head(hd, False)

        return carry

    lax.fori_loop(0, PEER_HEADS, head_step, 0)

    a_ref[...] = a_t[...].T
    b_ref[...] = b_t[...].T
    gate_ref[...] = g_t[...].T


def _peer_select(x1, g, sh, sc, wt, *, tile):
    b, t, d = x1.shape
    nt = t // tile
    nk = PEER_HEADS * PEER_TOPK
    nrow = wt.shape[0]
    vec = lambda: pl.BlockSpec((None, 1, d), lambda i, j: (i, 0, 0))
    tr = lambda: pl.BlockSpec((None, tile, nk), lambda i, j: (i, j, 0))
    k = PEER_TOPK
    return pl.pallas_call(
        _peer_sel_kernel,
        grid=(b, nt),
        in_specs=[pl.BlockSpec((None, tile, d), lambda i, j: (i, j, 0)),
                  pl.BlockSpec((1, d), lambda i, j: (0, 0)), vec(), vec(),
                  pl.BlockSpec((nrow, d), lambda i, j: (0, 0))],
        out_specs=[pl.BlockSpec((None, tile, d), lambda i, j: (i, j, 0)), tr(), tr(), tr()],
        out_shape=[jax.ShapeDtypeStruct((b, t, d), BF16),
                   jax.ShapeDtypeStruct((b, t, nk), jnp.int32),
                   jax.ShapeDtypeStruct((b, t, nk), jnp.int32),
                   jax.ShapeDtypeStruct((b, t, nk), F32)],
        scratch_shapes=[pltpu.VMEM((nrow, tile), F32),
                        pltpu.VMEM((k, tile), F32), pltpu.VMEM((k, tile), jnp.int32),
                        pltpu.VMEM((k, tile), F32), pltpu.VMEM((k, tile), jnp.int32),
                        pltpu.VMEM((k, tile), F32), pltpu.VMEM((k, tile), jnp.int32),
                        pltpu.VMEM((nk, tile), jnp.int32), pltpu.VMEM((nk, tile), jnp.int32),
                        pltpu.VMEM((nk, tile), F32)],
        compiler_params=_cparams(("parallel", "parallel")),
    )(x1, g.reshape(1, d), sh, sc, wt)


def _peer_mix_kernel(h_ref, a_ref, b_ref, gate_ref, ut_ref, v_ref, x_ref, gt_ref, fg_ref, o_ref,
                     m_scr, acc, *, tile, n_steps, pairs):
    nk = PEER_NKEYS
    half = nk // 2
    step = pl.program_id(2)
    hi_mask = jnp.uint32(0xFFFF0000)

    @pl.when((pl.program_id(0) == 0) & (pl.program_id(1) == 0) & (step == 0))
    def _():
        acc[...] = jnp.zeros(acc.shape, F32)

    @pl.when(step == 0)
    def _():
        sub = lax.broadcasted_iota(jnp.int32, (nk, a_ref.shape[1]), 0)
        a_of_row = jnp.where(sub < half, 2 * sub, 2 * (sub - half) + 1)

        def build(p, carry):
            arow = a_ref[pl.ds(p, 1), :]
            brow = b_ref[pl.ds(p, 1), :]
            grow = gate_ref[pl.ds(p, 1), :]
            xa = jnp.where(a_of_row == arow, 1.0, 0.0).astype(BF16)
            yb = jnp.where(sub == brow, 0.5 * grow, 0.0).astype(BF16)
            m = _dot_nt(xa, yb).astype(BF16).astype(F32)
            bits = lax.bitcast_convert_type(m, jnp.uint32)
            m_scr[pl.ds(pl.multiple_of(p * MASK_PITCH, 8), half), :] = (bits[half:] & hi_mask) | (bits[:half] >> 16)
            return carry

        lax.fori_loop(0, tile, build, 0, unroll=128)

    h = h_ref[...]
    parts = []
    for q in range(pairs):
        pr = step * pairs + q
        act = _gelu_times_2(_dot(h, ut_ref[:, q * 2 * nk:(q + 1) * 2 * nk]))
        w = m_scr[pl.ds(pr, tile, stride=MASK_PITCH), :]
        m_even = lax.bitcast_convert_type(w << 16, F32)
        m_odd = lax.bitcast_convert_type(w & hi_mask, F32)
        parts.append((act * jnp.concatenate([m_even, m_odd], axis=1)).astype(BF16))
    contrib = _dot(jnp.concatenate(parts, axis=1), v_ref[...])
    acc[...] = jnp.where(step == 0, contrib, acc[...] + contrib)

    @pl.when(step == n_steps - 1)
    def _():
        x2 = x_ref[...] + gt_ref[...] * acc[...]
        o_ref[...] = _rms(x2, fg_ref[...])


def _peer_mix(hb, aidx, bidx, gate, ut, v, x1, gt, final_g, *, tile, pairs):
    b, t, d = x1.shape
    nt = t // tile
    nk = PEER_NKEYS
    npk = aidx.shape[-1]
    n_steps = nk // (2 * pairs)
    eb = 2 * nk * pairs
    tok = lambda wd: pl.BlockSpec((None, tile, wd), lambda i, j, s: (i, j, 0))
    return pl.pallas_call(
        functools.partial(_peer_mix_kernel, tile=tile, n_steps=n_steps, pairs=pairs),
        grid=(b, nt, n_steps),
        in_specs=[tok(d), tok(npk), tok(npk), tok(npk),
                  pl.BlockSpec((d, eb), lambda i, j, s: (0, s)),
                  pl.BlockSpec((eb, d), lambda i, j, s: (s, 0)), tok(d),
                  pl.BlockSpec((None, 1, d), lambda i, j, s: (i, 0, 0)),
                  pl.BlockSpec((1, d), lambda i, j, s: (0, 0))],
        out_specs=tok(d),
        out_shape=jax.ShapeDtypeStruct((b, t, d), F32),
        scratch_shapes=[pltpu.VMEM((tile * MASK_PITCH, nk), jnp.uint32), pltpu.VMEM((tile, d), F32)],
        compiler_params=_cparams(("parallel", "parallel", "arbitrary")),
    )(hb, aidx, bidx, gate, ut, v, x1, gt, final_g.reshape(1, d))


def _block_diag(w):
    n, e, _ = w.shape
    eye = jnp.eye(n, dtype=w.dtype)
    return (eye[:, None, :, None] * w[:, :, None, :]).reshape(n * e, n * e)


def _mix_sequence(x3, ctx_mode, params, states, mods, *, rg_tb, gdn_tb, proj_tile):
    (norm1_g, w_rg, w_gdn, rg_conv_w, rg_conv_b, wg, gate_b, c_lam, gdn_conv_w, ea_row, dtb_row) = params
    sh1, sc1, mod_row = mods
    rg_h0_f, rg_h0_b, gdn_s0 = states
    p_rg, qkv, z, ab = _project(x3, norm1_g, sh1, sc1, mod_row, jnp.concatenate([w_rg, w_gdn], axis=1),
                                (2 * RG_WIDTH, 3 * GDN_WIDTH, GDN_WIDTH, AB_PAD), colmajor=not ctx_mode, tile=proj_tile)
    h_f, st_f = _rglru_pass(p_rg, rg_conv_w, rg_conv_b, wg[0], gate_b[0], c_lam[0], rg_h0_f, None,
                            reverse=False, tb=rg_tb)
    y_rg, st_b = _rglru_pass(p_rg, rg_conv_w, rg_conv_b, wg[1], gate_b[1], c_lam[1], rg_h0_b, h_f,
                             reverse=True, tb=rg_tb)
    prep = _gdn_prep(qkv, ab, gdn_conv_w, ea_row, dtb_row, tb=gdn_tb)
    o_f, o_b, s_fin = _gdn_scan(prep, gdn_s0, cps=min(GDN_SCAN_CHUNKS, x3.shape[1] // GDN_CHUNK))
    return y_rg, o_f, o_b, z, (st_f, st_b, s_fin)


def kernel(x, c, ctx, c_ctx, w_mod, b_mod, norm1_g, norm2_g, w_in, rg_conv_w, rg_conv_b, rg_gate_w, rg_gate_b,
           rg_lambda, gdn_conv_w, gdn_a_log, gdn_dt_bias, gdn_norm_g, w_out, peer_wq, peer_keys, peer_u, peer_v,
           final_g):
    b, t, d = x.shape
    depth = w_mod.shape[0]
    assert depth == 1, "context residual stream update is only needed for depth > 1"
    l = 0
    w = RG_WIDTH

    cc = jnp.zeros((16, d), F32).at[:b].set(c).at[b].set(c_ctx)
    w_rg = w_in[l][:, :2 * w].astype(BF16)
    n_ab = w_in.shape[2] - 2 * w - 4 * GDN_WIDTH
    w_gdn = jnp.concatenate([w_in[l][:, 2 * w:2 * w + 4 * GDN_WIDTH],
                             jnp.pad(w_in[l][:, 2 * w + 4 * GDN_WIDTH:], ((0, 0), (0, AB_PAD - n_ab)))],
                            axis=1).astype(BF16)
    wg = jnp.stack([jnp.concatenate([_block_diag(rg_gate_w[l, dr, 0]), _block_diag(rg_gate_w[l, dr, 1])], axis=1)
                    for dr in range(2)]).astype(BF16)
    gate_b = rg_gate_b[l].reshape(2, 1, 2 * w)
    c_lam = (-RG_C * jax.nn.softplus(-rg_lambda[l])).reshape(2, 1, w)
    ea = jnp.exp(gdn_a_log[l])
    pad4 = jnp.zeros((2, GDN_HEADS), F32)
    ea_row = jnp.pad(jnp.concatenate([ea, pad4], axis=1).reshape(1, -1), ((0, 0), (0, AB_PAD - n_ab)))
    dtb_row = jnp.pad(jnp.concatenate([gdn_dt_bias[l], pad4], axis=1).reshape(1, -1), ((0, 0), (0, AB_PAD - n_ab)))
    params = (norm1_g[l], w_rg, w_gdn, rg_conv_w[l], rg_conv_b[l].reshape(1, w), wg, gate_b, c_lam,
              gdn_conv_w[l], ea_row, dtb_row)

    mod = _modulation(cc, w_mod[l], b_mod[l])
    sh1, sc1, gt1, sh2, sc2, gt2 = [mod[:, i * d:(i + 1) * d].reshape(16, 1, d) for i in range(6)]

    tc = ctx.shape[1]
    zero_states = (jnp.zeros((b, 1, w), F32), jnp.zeros((b, 1, w), F32),
                   jnp.zeros((b, 2, GDN_HEADS, GDN_DK, GDN_DV), F32))
    _, _, _, _, ctx_states = _mix_sequence(ctx, True, params, zero_states, (sh1, sc1, lambda i: b),
                                           rg_tb=min(tc, RG_BLOCK), gdn_tb=min(tc, GDN_BLOCK),
                                           proj_tile=min(tc, TOKEN_TILE))

    tile = min(t, TOKEN_TILE)
    y_rg, o_f, o_b, z, _ = _mix_sequence(x, False, params, ctx_states, (sh1, sc1, lambda i: i),
                                         rg_tb=min(t, RG_BLOCK), gdn_tb=min(t, GDN_BLOCK), proj_tile=tile)
    x1 = _out_project(x, y_rg, o_f, o_b, z, gdn_norm_g[l].reshape(1, GDN_DV), w_out[l].astype(BF16), gt1, tile=tile)

    wfold_t = _fold_keys(peer_wq[l], peer_keys[l]).T.astype(BF16)
    hb, aidx, bidx, gate = _peer_select(x1, norm2_g[l], sh2, sc2, wfold_t, tile=tile)
    out = _peer_mix(hb, aidx, bidx, gate, peer_u[l].astype(BF16).T, peer_v[l].astype(BF16),
                    x1, gt2, final_g, tile=tile, pairs=MIX_PAIRS)
    return out
```

```python
import functools
import math

import jax
import jax.numpy as jnp
from jax import lax
from jax.experimental import pallas as pl
from jax.experimental.pallas import tpu as pltpu

F32 = jnp.float32
BF16 = jnp.bfloat16

GRID_W = 64
EPS = 1e-6
RG_WIDTH = 512
RG_C = 8.0
GDN_HEADS = 4
GDN_DK = 128
GDN_DV = 128
GDN_WIDTH = GDN_HEADS * GDN_DV
GDN_CHUNK = 64
GDN_SUB = 16
AB_PAD = 128
PEER_HEADS = 8
PEER_NKEYS = 128
PEER_HALF = 128
PEER_TOPK = 16
NEG_INF = float("-inf")
MASK_PITCH = 72

VMEM_LIMIT = 58 * 1024 * 1024

TOKEN_TILE = 512
RG_BLOCK = 512
GDN_BLOCK = 256
GDN_SCAN_CHUNKS = 16
MIX_PAIRS = 8


def _cparams(sem):
    return pltpu.CompilerParams(dimension_semantics=sem, vmem_limit_bytes=VMEM_LIMIT)


def _dot(a, b):
    return jnp.dot(a, b, preferred_element_type=F32)


def _dot_nt(a, b):
    return lax.dot_general(a, b, (((1,), (1,)), ((), ())), preferred_element_type=F32)


def _dot_tn(a, b):
    return lax.dot_general(a, b, (((0,), (0,)), ((), ())), preferred_element_type=F32)


def _split3(x):
    hi = x.astype(BF16)
    r = x - hi.astype(F32)
    mid = r.astype(BF16)
    lo = (r - mid.astype(F32)).astype(BF16)
    return hi, mid, lo


def _dot_sel(m, x):
    hi, mid, lo = _split3(x)
    return _dot(m, lo) + _dot(m, mid) + _dot(m, hi)


def _silu(x):
    return x * jax.nn.sigmoid(x)


def _softplus(x):
    return jnp.maximum(x, 0.0) + jnp.log1p(jnp.exp(-jnp.abs(x)))


def _gelu_times_2(x):
    c = math.sqrt(2.0 / math.pi)
    return x * (1.0 + jnp.tanh(x * (c + (c * 0.044715) * (x * x))))


def _rms(x, g):
    return x * lax.rsqrt(jnp.mean(x * x, axis=-1, keepdims=True) + EPS) * g


def _mod_kernel(c_ref, w_ref, b_ref, o_ref):
    s = _silu(c_ref[...])
    o_ref[...] = jnp.dot(s, w_ref[...], preferred_element_type=F32,
                         precision=lax.Precision.HIGHEST) + b_ref[...]


def _modulation(cc, w_mod, b_mod):
    m, d = cc.shape
    n = w_mod.shape[1]
    tn = 1536
    return pl.pallas_call(
        _mod_kernel,
        grid=(n // tn,),
        in_specs=[pl.BlockSpec((m, d), lambda j: (0, 0)),
                  pl.BlockSpec((d, tn), lambda j: (0, j)),
                  pl.BlockSpec((1, tn), lambda j: (0, j))],
        out_specs=pl.BlockSpec((m, tn), lambda j: (0, j)),
        out_shape=jax.ShapeDtypeStruct((m, n), F32),
        compiler_params=_cparams(("arbitrary",)),
    )(cc, w_mod, b_mod.reshape(1, n))


def _proj_kernel(x_ref, g_ref, sh_ref, sc_ref, w_ref, *o_refs, ncol, widths):
    if ncol > 1:
        x = jnp.concatenate([x_ref[:, j, :] for j in range(ncol)], axis=0)
    else:
        x = x_ref[...]
    h = _rms(x, g_ref[...]) * (1.0 + sc_ref[...]) + sh_ref[...]
    o = _dot(h.astype(BF16), w_ref[...])
    off = 0
    for n_out, (o_ref, wd) in enumerate(zip(o_refs, widths)):
        if ncol > 1 and n_out == 0:
            rows = o.shape[0] // ncol
            for j in range(ncol):
                o_ref[:, j, :] = o[j * rows:(j + 1) * rows, off:off + wd]
        else:
            o_ref[...] = o[:, off:off + wd]
        off += wd


def _project(x3, g, sh, sc, mod_row, w, widths, *, colmajor, tile):
    b, t, d = x3.shape
    n = w.shape[1]
    flat = lambda wd: pl.BlockSpec((None, tile, wd), lambda i, j: (i, j, 0))
    if colmajor:
        rows = t // GRID_W
        ncol = tile // rows
        grid_view = lambda wd: pl.BlockSpec((None, rows, ncol, wd), lambda i, j: (i, 0, j, 0))
        xv = x3.reshape(b, rows, GRID_W, d)
        x_spec = grid_view(d)
        nt = GRID_W // ncol
        out_specs = [grid_view(widths[0])] + [flat(wd) for wd in widths[1:]]
        out_shape = ([jax.ShapeDtypeStruct((b, rows, GRID_W, widths[0]), F32)]
                     + [jax.ShapeDtypeStruct((b, t, wd), F32) for wd in widths[1:]])
    else:
        ncol = 1
        xv = x3
        x_spec = flat(d)
        nt = t // tile
        out_specs = [flat(wd) for wd in widths]
        out_shape = [jax.ShapeDtypeStruct((b, t, wd), F32) for wd in widths]
    vec = lambda: pl.BlockSpec((None, 1, d), lambda i, j: (mod_row(i), 0, 0))
    outs = pl.pallas_call(
        functools.partial(_proj_kernel, ncol=ncol, widths=tuple(widths)),
        grid=(b, nt),
        in_specs=[x_spec, pl.BlockSpec((1, d), lambda i, j: (0, 0)), vec(), vec(),
                  pl.BlockSpec((d, n), lambda i, j: (0, 0))],
        out_specs=out_specs,
        out_shape=out_shape,
        compiler_params=_cparams(("parallel", "parallel")),
    )(xv, g.reshape(1, d), sh, sc, w)
    return [outs[0].reshape(b, t, widths[0])] + list(outs[1:])


def _conv4(cur, prev8, nxt8, w_ref, first, last):
    tb, c = cur.shape
    g = tb // 8
    prev8 = jnp.where(first, 0.0, prev8)
    nxt8 = jnp.where(last, 0.0, nxt8)
    ext = jnp.concatenate([prev8, cur, nxt8], axis=0).reshape(g + 2, 8, c)
    sub = lax.broadcasted_iota(jnp.int32, (g, 8, c), 1)

    def back(k):
        r = pltpu.roll(ext, k, 1)
        return jnp.where(sub >= k, r[1:g + 1], r[0:g])

    r = pltpu.roll(ext, 7, 1)
    xp1 = jnp.where(sub < 7, r[1:g + 1], r[2:g + 2])
    y = (w_ref[0:1, :] * back(2) + w_ref[1:2, :] * back(1) + w_ref[2:3, :] * ext[1:g + 1] + w_ref[3:4, :] * xp1)
    return y.reshape(tb, c)


def _halo_specs(tb, t, width, lane_blk, tmap):
    r = tb // 8
    nb8 = t // 8
    return [pl.BlockSpec((None, tb, width), lambda i, j: (i, tmap(j), lane_blk)),
            pl.BlockSpec((None, 8, width), lambda i, j: (i, jnp.maximum(tmap(j) * r - 1, 0), lane_blk)),
            pl.BlockSpec((None, 8, width), lambda i, j: (i, jnp.minimum((tmap(j) + 1) * r, nb8 - 1), lane_blk))]


def _rglru_kernel(*refs, tb, nt, reverse):
    if reverse:
        (u_ref, up_ref, un_ref, cw_ref, cb_ref, wg_ref, gb_ref, cl_ref, h0_ref, gate_ref, hf_ref,
         y_ref, st_ref, carry) = refs
    else:
        (u_ref, up_ref, un_ref, cw_ref, cb_ref, wg_ref, gb_ref, cl_ref, h0_ref,
         y_ref, st_ref, carry) = refs
    j = pl.program_id(1)
    tblk = (nt - 1 - j) if reverse else j
    w = u_ref.shape[-1]

    @pl.when(j == 0)
    def _():
        carry[...] = h0_ref[...]

    u = u_ref[...]
    xc = _conv4(u, up_ref[...], un_ref[...], cw_ref, tblk == 0, tblk == nt - 1) + cb_ref[...]
    gates = _dot(xc.astype(BF16), wg_ref[...]) + gb_ref[...]
    r = jax.nn.sigmoid(gates[:, :w])
    ig = jax.nn.sigmoid(gates[:, w:])
    log_a = r * cl_ref[...]
    a = jnp.exp(log_a)
    th = jnp.tanh(log_a)
    bb = jnp.sqrt(-2.0 * th / (1.0 - th)) * (ig * xc)

    n_grp = tb // 8
    a = a.reshape(n_grp, 8, w)
    bb = bb.reshape(n_grp, 8, w)
    sub = lax.broadcasted_iota(jnp.int32, (n_grp, 8, w), 1)
    s = 1
    while s < 8:
        shift, ok = (8 - s, sub < 8 - s) if reverse else (s, sub >= s)
        a_sh = jnp.where(ok, pltpu.roll(a, shift, 1), 1.0)
        b_sh = jnp.where(ok, pltpu.roll(bb, shift, 1), 0.0)
        bb = a * b_sh + bb
        a = a * a_sh
        s *= 2
    h_prev = carry[...]
    hs = [None] * n_grp
    for g in (range(n_grp - 1, -1, -1) if reverse else range(n_grp)):
        hg = a[g] * h_prev + bb[g]
        h_prev = hg[0:1, :] if reverse else hg[7:8, :]
        hs[g] = hg
    h = jnp.concatenate(hs, axis=0)
    carry[...] = h_prev
    st_ref[...] = carry[...]
    if reverse:
        y_ref[...] = ((hf_ref[...] + h) * jax.nn.gelu(gate_ref[...])).astype(y_ref.dtype)
    else:
        y_ref[...] = h


def _rglru_pass(p_rg, conv_w, conv_b, wg, gate_b, c_lam, h0, hf, *, reverse, tb):
    b, t, w2 = p_rg.shape
    w = w2 // 2
    nt = t // tb
    tmap = (lambda j: nt - 1 - j) if reverse else (lambda j: j)
    const = lambda shape: pl.BlockSpec(shape, lambda i, j: (0,) * len(shape))
    in_specs = _halo_specs(tb, t, w, 0, tmap) + [
        const((4, w)), const((1, w)), const((w, 2 * w)), const((1, 2 * w)), const((1, w)),
        pl.BlockSpec((None, 1, w), lambda i, j: (i, 0, 0))]
    args = [p_rg, p_rg, p_rg, conv_w, conv_b, wg, gate_b, c_lam, h0]
    if reverse:
        in_specs += [pl.BlockSpec((None, tb, w), lambda i, j: (i, tmap(j), 1)),
                     pl.BlockSpec((None, tb, w), lambda i, j: (i, tmap(j), 0))]
        args += [p_rg, hf]
    y, st = pl.pallas_call(
        functools.partial(_rglru_kernel, tb=tb, nt=nt, reverse=reverse),
        grid=(b, nt),
        in_specs=in_specs,
        out_specs=[pl.BlockSpec((None, tb, w), lambda i, j: (i, tmap(j), 0)),
                   pl.BlockSpec((None, 1, w), lambda i, j: (i, 0, 0))],
        out_shape=[jax.ShapeDtypeStruct((b, t, w), F32),
                   jax.ShapeDtypeStruct((b, 1, w), F32)],
        scratch_shapes=[pltpu.VMEM((1, w), F32)],
        compiler_params=_cparams(("parallel", "arbitrary")),
    )(*args)
    return y, st


def _bmm(a, b):
    return lax.dot_general(a, b, (((2,), (1,)), ((0,), (0,))), preferred_element_type=F32)


def _bmm_nt(a, b):
    return lax.dot_general(a, b, (((2,), (2,)), ((0,), (0,))), preferred_element_type=F32)


def _unit_tri_inverse(a, diag_mask):
    c = a.shape[-1]
    eye = (lax.broadcasted_iota(jnp.int32, (1, c, c), 1) == lax.broadcasted_iota(jnp.int32, (1, c, c), 2)).astype(F32)
    mm = lambda p, q: _bmm(p.astype(BF16), q.astype(BF16))
    ad = a * diag_mask
    x = eye - ad
    pw = ad
    k = 2
    while k < GDN_SUB:
        pw = mm(pw, pw)
        x = x + mm(x, pw)
        k *= 2
    n = mm(x, a - ad)
    nblk = c // GDN_SUB
    y = eye - n
    pw = n
    k = 2
    while k < nblk:
        pw = mm(pw, pw)
        y = y + mm(y, pw)
        k *= 2
    return mm(y, x)


def _gdn_prep_kernel(qkv_ref, qp_ref, qn_ref, ab_ref, cw_ref, ea_ref, dtb_ref,
                     qg_ref, kd_ref, kc_ref, wv_ref, at_ref, eg_ref, *, tb, nt):
    c = GDN_CHUNK
    j = pl.program_id(1)
    hw = GDN_WIDTH
    x = _silu(_conv4(qkv_ref[...], qp_ref[...], qn_ref[...], cw_ref, j == 0, j == nt - 1))

    ab = ab_ref[...]
    col = lax.broadcasted_iota(jnp.int32, (tb, AB_PAD), 1)
    is_a = (col & 4) == 0
    gbv = jnp.where(is_a, -ea_ref[...] * _softplus(ab + dtb_ref[...]), jax.nn.sigmoid(ab))

    ri = lax.broadcasted_iota(jnp.int32, (tb, tb), 0)
    ci = lax.broadcasted_iota(jnp.int32, (tb, tb), 1)
    same = (ri // c) == (ci // c)
    l_f = (same & (ci <= ri)).astype(BF16)
    l_b = (same & (ci >= ri)).astype(BF16)
    l_t = same.astype(BF16)
    gcum = jnp.where(col < 8, _dot_sel(l_f, gbv), _dot_sel(l_b, gbv))
    gtot = _dot_sel(l_t, gbv)

    cols_a = [dr * 8 + hd for dr in range(2) for hd in range(GDN_HEADS)]
    bcast = lambda v, cols: jnp.concatenate([jnp.broadcast_to(v[:, cc:cc + 1], (tb, 128)) for cc in cols], axis=1)
    gc_all = bcast(gcum, cols_a)
    gt_all = bcast(gtot, cols_a)
    be_all = bcast(gbv, [cc + 4 for cc in cols_a])

    qs, ks, vs = [], [], []
    for hd in range(GDN_HEADS):
        q = x[:, hd * GDN_DK:(hd + 1) * GDN_DK]
        k = x[:, hw + hd * GDN_DK: hw + (hd + 1) * GDN_DK]
        qs.append(q * lax.rsqrt(jnp.sum(q * q, axis=-1, keepdims=True) + EPS) * (GDN_DK ** -0.5))
        ks.append(k * lax.rsqrt(jnp.sum(k * k, axis=-1, keepdims=True) + EPS))
        vs.append(x[:, 2 * hw + hd * GDN_DV: 2 * hw + (hd + 1) * GDN_DV])
    inst = [(ch, dr, hd) for ch in range(tb // c) for dr in range(2) for hd in range(GDN_HEADS)]
    n_inst = len(inst)
    rows = lambda ch: slice(ch * c, (ch + 1) * c)
    lanes = lambda dr, hd: slice((dr * GDN_HEADS + hd) * 128, (dr * GDN_HEADS + hd + 1) * 128)
    stack = lambda f: jnp.stack([f(ch, dr, hd) for ch, dr, hd in inst], axis=0)
    qq = stack(lambda ch, dr, hd: qs[hd][rows(ch)])
    kk = stack(lambda ch, dr, hd: ks[hd][rows(ch)])
    vv = stack(lambda ch, dr, hd: vs[hd][rows(ch)])
    gcb = stack(lambda ch, dr, hd: gc_all[rows(ch), lanes(dr, hd)])
    gtb = stack(lambda ch, dr, hd: gt_all[rows(ch), lanes(dr, hd)])
    beb = stack(lambda ch, dr, hd: be_all[rows(ch), lanes(dr, hd)])

    ii = lax.broadcasted_iota(jnp.int32, (c, 128), 0)
    jj = lax.broadcasted_iota(jnp.int32, (c, 128), 1)
    incl_d = ((jj <= ii).astype(F32), ((jj >= ii) & (jj < c)).astype(F32))
    i64 = lax.broadcasted_iota(jnp.int32, (c, c), 0)
    j64 = lax.broadcasted_iota(jnp.int32, (c, c), 1)
    strict_d = ((j64 < i64).astype(F32), (j64 > i64).astype(F32))
    incl = stack(lambda ch, dr, hd: incl_d[dr])
    strict = stack(lambda ch, dr, hd: strict_d[dr])
    diag_blk = ((i64 // GDN_SUB) == (j64 // GDN_SUB)).astype(F32)[None]

    eg = jnp.exp(gcb)
    kb = kk * beb
    diag = gcb * (ii == jj).astype(F32)[None]
    gcr = _dot_sel(jnp.ones((c, c), BF16), jnp.concatenate([diag[g] for g in range(n_inst)], axis=1))
    gcr = jnp.stack([gcr[:, g * 128:(g + 1) * 128] for g in range(n_inst)], axis=0)
    decay = jnp.exp((gcb - gcr) * incl) * incl
    kpad = jnp.concatenate([kk.astype(BF16), jnp.zeros((n_inst, c, GDN_DK), BF16)], axis=1)
    qk = _bmm_nt(jnp.concatenate([qq, kb], axis=1).astype(BF16), kpad)
    attn = (qk[:, :c] * decay).astype(BF16)
    a_mat = (qk[:, c:] * decay)[:, :, :c] * strict
    tinv = _unit_tri_inverse(a_mat, diag_blk)
    rhs = jnp.concatenate([vv * beb, kb * eg], axis=2)
    sol = _bmm(tinv.astype(BF16), rhs.astype(BF16))
    qg = (qq * eg).astype(BF16)
    kd = (kk * jnp.exp(gtb - gcb)).astype(BF16)
    egt = jnp.exp(gtb[:, 0:8, :])
    for g, (ch, dr, hd) in enumerate(inst):
        rs = rows(ch)
        ls = slice(hd * 128, (hd + 1) * 128)
        qg_ref[dr, rs, ls] = qg[g]
        kd_ref[dr, rs, ls] = kd[g]
        kc_ref[dr, rs, ls] = sol[g, :, GDN_DV:].astype(BF16)
        wv_ref[dr, rs, ls] = sol[g, :, :GDN_DV]
        at_ref[dr, rs, ls] = attn[g]
        eg_ref[dr, ch, :, ls] = egt[g]


def _gdn_prep(qkv, ab, conv_w, ea_row, dtb_row, *, tb):
    b, t, cw = qkv.shape
    nt = t // tb
    nch = t // GDN_CHUNK
    cpb = tb // GDN_CHUNK
    const = lambda shape: pl.BlockSpec(shape, lambda i, j: (0,) * len(shape))
    hw = GDN_WIDTH
    big = lambda: pl.BlockSpec((2, None, tb, hw), lambda i, j: (0, i, j, 0))
    outs = pl.pallas_call(
        functools.partial(_gdn_prep_kernel, tb=tb, nt=nt),
        grid=(b, nt),
        in_specs=_halo_specs(tb, t, cw, 0, lambda j: j) + [
            pl.BlockSpec((None, tb, AB_PAD), lambda i, j: (i, j, 0)),
            const((4, cw)), const((1, AB_PAD)), const((1, AB_PAD))],
        out_specs=[big(), big(), big(), big(), big(),
                   pl.BlockSpec((2, None, cpb, 8, hw), lambda i, j: (0, i, j, 0, 0))],
        out_shape=[jax.ShapeDtypeStruct((2, b, t, hw), BF16),
                   jax.ShapeDtypeStruct((2, b, t, hw), BF16),
                   jax.ShapeDtypeStruct((2, b, t, hw), BF16),
                   jax.ShapeDtypeStruct((2, b, t, hw), F32),
                   jax.ShapeDtypeStruct((2, b, t, hw), BF16),
                   jax.ShapeDtypeStruct((2, b, nch, 8, hw), F32)],
        compiler_params=_cparams(("parallel", "parallel")),
    )(qkv, qkv, qkv, ab, conv_w, ea_row, dtb_row)
    return outs


def _gdn_scan_kernel(qg_f, kd_f, kc_f, wv_f, at_f, eg_f, qg_b, kd_b, kc_b, wv_b, at_b, eg_b, s0_ref,
                     of_ref, ob_ref, sfin_ref, s_scr, *, cps):
    c = GDN_CHUNK
    j = pl.program_id(1)

    @pl.when(j == 0)
    def _():
        s_scr[...] = s0_ref[...]

    dirs = ((qg_f, kd_f, kc_f, wv_f, at_f, eg_f), (qg_b, kd_b, kc_b, wv_b, at_b, eg_b))
    chains = [(dr, hd) for dr in range(2) for hd in range(GDN_HEADS)]
    ls = lambda hd: slice(hd * 128, (hd + 1) * 128)
    s = s_scr[...].reshape(2 * GDN_HEADS, GDN_DK, GDN_DV)
    for q in range(cps):
        chunk = (q, cps - 1 - q)
        rs = lambda dr: slice(chunk[dr] * c, (chunk[dr] + 1) * c)
        stack = lambda f: jnp.stack([f(dirs[dr], rs(dr), ls(hd), chunk[dr]) for dr, hd in chains], axis=0)
        kq = stack(lambda r, t, l, n: jnp.concatenate([r[2][t, l], r[0][t, l]], axis=0))
        wv = stack(lambda r, t, l, n: r[3][t, l])
        at = stack(lambda r, t, l, n: r[4][t, l][:, :c])
        eg = stack(lambda r, t, l, n: r[5][n, 0:1, l])
        r = _bmm(kq, s.astype(BF16))
        vb = (wv - r[:, :c]).astype(BF16)
        o = r[:, c:] + _bmm(at, vb)
        upd = []
        for g, (dr, hd) in enumerate(chains):
            (of_ref, ob_ref)[dr][rs(dr), ls(hd)] = o[g]
            upd.append(_dot_tn(dirs[dr][1][rs(dr), ls(hd)], vb[g]))
        s = s * eg + jnp.stack(upd, axis=0)
    s_scr[...] = s.reshape(2, GDN_HEADS, GDN_DK, GDN_DV)
    sfin_ref[...] = s_scr[...]


def _gdn_scan(prep, s0, *, cps):
    qg, kd, kc, wv, at, eg = prep
    _, b, t, hw = qg.shape
    c = GDN_CHUNK
    nblk = t // (c * cps)
    fw = lambda: pl.BlockSpec((None, None, cps * c, hw), lambda i, j: (0, i, j, 0))
    bw = lambda: pl.BlockSpec((None, None, cps * c, hw), lambda i, j: (1, i, nblk - 1 - j, 0))
    egf = pl.BlockSpec((None, None, cps, 8, hw), lambda i, j: (0, i, j, 0, 0))
    egb = pl.BlockSpec((None, None, cps, 8, hw), lambda i, j: (1, i, nblk - 1 - j, 0, 0))
    st = pl.BlockSpec((None, 2, GDN_HEADS, GDN_DK, GDN_DV), lambda i, j: (i, 0, 0, 0, 0))
    o_f, o_b, s_fin = pl.pallas_call(
        functools.partial(_gdn_scan_kernel, cps=cps),
        grid=(b, nblk),
        in_specs=[fw(), fw(), fw(), fw(), fw(), egf, bw(), bw(), bw(), bw(), bw(), egb, st],
        out_specs=[pl.BlockSpec((None, cps * c, hw), lambda i, j: (i, j, 0)),
                   pl.BlockSpec((None, cps * c, hw), lambda i, j: (i, nblk - 1 - j, 0)),
                   st],
        out_shape=[jax.ShapeDtypeStruct((b, t, hw), F32), jax.ShapeDtypeStruct((b, t, hw), F32),
                   jax.ShapeDtypeStruct((b, 2, GDN_HEADS, GDN_DK, GDN_DV), F32)],
        scratch_shapes=[pltpu.VMEM((2, GDN_HEADS, GDN_DK, GDN_DV), F32)],
        compiler_params=_cparams(("parallel", "arbitrary")),
    )(qg, kd, kc, wv, at, eg, qg, kd, kc, wv, at, eg, s0)
    return o_f, o_b, s_fin


def _out_kernel(x_ref, yrg_ref, of_ref, ob_ref, z_ref, ng_ref, w_ref, gt_ref, o_ref, *, ncol):
    o = of_ref[...] + ob_ref[...]
    z = z_ref[...]
    parts = [jnp.concatenate([yrg_ref[:, j, :] for j in range(ncol)], axis=0)]
    for hd in range(GDN_HEADS):
        ls = slice(hd * GDN_DV, (hd + 1) * GDN_DV)
        parts.append(_rms(o[:, ls], ng_ref[...]) * _silu(z[:, ls]))
    y = jnp.concatenate(parts, axis=1).astype(BF16)
    t2 = gt_ref[...] * _dot(y, w_ref[...])
    rows = t2.shape[0] // ncol
    for j in range(ncol):
        o_ref[:, j, :] = x_ref[:, j, :] + t2[j * rows:(j + 1) * rows]


def _out_project(x3, y_rg, o_f, o_b, z, norm_g, w_out, gt, *, tile):
    b, t, d = x3.shape
    w = RG_WIDTH
    rows = t // GRID_W
    ncol = tile // rows
    hw = GDN_WIDTH
    grid_view = lambda wd: pl.BlockSpec((None, rows, ncol, wd), lambda i, j: (i, 0, j, 0))
    cm = lambda wd: pl.BlockSpec((None, tile, wd), lambda i, j: (i, j, 0))
    x1 = pl.pallas_call(
        functools.partial(_out_kernel, ncol=ncol),
        grid=(b, GRID_W // ncol),
        in_specs=[grid_view(d), grid_view(w), cm(hw), cm(hw), cm(hw),
                  pl.BlockSpec((1, GDN_DV), lambda i, j: (0, 0)),
                  pl.BlockSpec((w + hw, d), lambda i, j: (0, 0)),
                  pl.BlockSpec((None, 1, d), lambda i, j: (i, 0, 0))],
        out_specs=grid_view(d),
        out_shape=jax.ShapeDtypeStruct((b, rows, GRID_W, d), F32),
        compiler_params=_cparams(("parallel", "parallel")),
    )(x3.reshape(b, rows, GRID_W, d), y_rg.reshape(b, rows, GRID_W, w), o_f, o_b, z, norm_g, w_out, gt)
    return x1.reshape(b, t, d)


def _fold_kernel(wq_ref, k_ref, o_ref):
    o_ref[...] = lax.dot_general(wq_ref[...], k_ref[...], (((1,), (1,)), ((), ())),
                                 preferred_element_type=F32, precision=lax.Precision.HIGHEST)


def _fold_keys(wq, keys):
    d, n = wq.shape
    nblk = n // PEER_HALF
    return pl.pallas_call(
        _fold_kernel,
        grid=(nblk,),
        in_specs=[pl.BlockSpec((d, PEER_HALF), lambda j: (0, j)),
                  pl.BlockSpec((None, PEER_NKEYS, PEER_HALF), lambda j: (j % 2, 0, 0))],
        out_specs=pl.BlockSpec((d, PEER_NKEYS), lambda j: (0, j)),
        out_shape=jax.ShapeDtypeStruct((d, nblk * PEER_NKEYS), F32),
        compiler_params=_cparams(("parallel",)),
    )(wq, keys)


def _extract16(vals, ids, s_scr, i_scr, *, fast, n_masked=0):
    n, p = vals.shape
    for r in range(PEER_TOPK):
        m = jnp.max(vals, axis=0, keepdims=True)
        eq = vals == m
        if fast:
            idx = _dot(ids, jnp.where(eq, 1.0, 0.0).astype(BF16))[0:1, :].astype(jnp.int32)
            vals = jnp.where(eq, NEG_INF, vals)
        else:
            idx = jnp.min(jnp.where(eq, ids, 1 << 20), axis=0, keepdims=True)
            vals = jnp.where(ids == idx, NEG_INF, vals)
        s_scr[r:r + 1, :] = m
        i_scr[r:r + 1, :] = idx
    if not fast:
        return None
    dropped = jnp.sum(jnp.where(vals == NEG_INF, 1.0, 0.0), axis=0, keepdims=True)
    return jnp.where(dropped == float(PEER_TOPK + n_masked), 0.0, 1.0)


def _peer_sel_kernel(x_ref, g_ref, sh_ref, sc_ref, wt_ref, h_ref, a_ref, b_ref, gate_ref,
                     sc_scr, s1, i1, s2, i2, ts, tp, a_t, b_t, g_t):
    k = PEER_TOPK
    h = _rms(x_ref[...], g_ref[...]) * (1.0 + sc_ref[...]) + sh_ref[...]
    hb = h.astype(BF16)
    h_ref[...] = hb
    sc_scr[...] = _dot_nt(wt_ref[...], hb)
    p = hb.shape[0]

    n_cand = 56
    n_masked = 6

    def cand_pos(row):
        o = row % 8
        pos = jnp.where(row < 16, row, k + o)
        pos = jnp.where(row >= 24, jnp.where(o < 5, 2 * k + o, k * k - 1), pos)
        pos = jnp.where(row >= 32, jnp.where(o < 4, 3 * k + o, jnp.where(o < 7, 4 * k + o - 4, k * k - 1)), pos)
        pos = jnp.where(row >= 40, jnp.where(o < 6, (5 + o // 2) * k + o % 2, k * k - 1), pos)
        return jnp.where(row >= 48, (8 + o) * k, pos)

    def head(hd, fast):
        if fast:
            key_ids = lax.broadcasted_iota(jnp.int32, (8, PEER_NKEYS), 1).astype(F32).astype(BF16)
            pos_ids = cand_pos(lax.broadcasted_iota(jnp.int32, (8, n_cand), 1)).astype(F32).astype(BF16)
        else:
            key_ids = lax.broadcasted_iota(jnp.int32, (PEER_NKEYS, p), 0)
            pos_ids = cand_pos(lax.broadcasted_iota(jnp.int32, (n_cand, p), 0))
        sub = lax.broadcasted_iota(jnp.int32, (8, p), 0)
        base = pl.multiple_of(hd * 2 * PEER_NKEYS, 2 * PEER_NKEYS)
        bad1 = _extract16(sc_scr[pl.ds(base, PEER_NKEYS), :], key_ids, s1, i1, fast=fast)
        bad2 = _extract16(sc_scr[pl.ds(base + PEER_NKEYS, PEER_NKEYS), :], key_ids, s2, i2, fast=fast)
        s1v, s2v = s1[...], s2[...]
        s2lo = s2v[0:8, :]
        row1 = lambda r: s1v[r:r + 1, :]
        cands = [row1(0) + s2v,
                 row1(1) + s2lo,
                 jnp.where(sub < 5, row1(2) + s2lo, NEG_INF),
                 jnp.where(sub < 7, jnp.where(sub < 4, row1(3), row1(4))
                           + jnp.where(sub < 4, s2lo, pltpu.roll(s2lo, 4, 0)), NEG_INF),
                 jnp.where(sub < 6, jnp.where(sub < 2, row1(5), jnp.where(sub < 4, row1(6), row1(7)))
                           + jnp.where(sub % 2 == 0, s2v[0:1, :], s2v[1:2, :]), NEG_INF),
                 s1v[8:16, :] + s2v[0:1, :]]
        bad3 = _extract16(jnp.concatenate(cands, axis=0), pos_ids, ts, tp, fast=fast, n_masked=n_masked)
        top_s, top_p = ts[...], tp[...]
        rr = top_p >> 4
        cc = top_p & (k - 1)
        i1v, i2v = i1[...], i2[...]
        av = jnp.zeros_like(top_p)
        bv = jnp.zeros_like(top_p)
        for q in range(k):
            av = jnp.where(rr == q, i1v[q:q + 1, :], av)
            bv = jnp.where(cc == q, i2v[q:q + 1, :], bv)
        e = jnp.exp(top_s - top_s[0:1, :])
        gate = e / jnp.sum(e, axis=0, keepdims=True)
        o = pl.multiple_of(hd * k, k)
        a_t[pl.ds(o, k), :] = av
        b_t[pl.ds(o, k), :] = bv
        g_t[pl.ds(o, k), :] = gate
        return (bad1 + bad2 + bad3) if fast else None

    def head_step(hd, carry):
        bad = head(hd, True)

        @pl.when(jnp.max(bad) > 0.0)
        def _():
            head(hd, False)

        return carry

    lax.fori_loop(0, PEER_HEADS, head_step, 0)

    a_ref[...] = a_t[...].T
    b_ref[...] = b_t[...].T
    gate_ref[...] = g_t[...].T


def _peer_select(x1, g, sh, sc, wt, *, tile):
    b, t, d = x1.shape
    nt = t // tile
    nk = PEER_HEADS * PEER_TOPK
    nrow = wt.shape[0]
    vec = lambda: pl.BlockSpec((None, 1, d), lambda i, j: (i, 0, 0))
    tr = lambda: pl.BlockSpec((None, tile, nk), lambda i, j: (i, j, 0))
    k = PEER_TOPK
    return pl.pallas_call(
        _peer_sel_kernel,
        grid=(b, nt),
        in_specs=[pl.BlockSpec((None, tile, d), lambda i, j: (i, j, 0)),
                  pl.BlockSpec((1, d), lambda i, j: (0, 0)), vec(), vec(),
                  pl.BlockSpec((nrow, d), lambda i, j: (0, 0))],
        out_specs=[pl.BlockSpec((None, tile, d), lambda i, j: (i, j, 0)), tr(), tr(), tr()],
        out_shape=[jax.ShapeDtypeStruct((b, t, d), BF16),
                   jax.ShapeDtypeStruct((b, t, nk), jnp.int32),
                   jax.ShapeDtypeStruct((b, t, nk), jnp.int32),
                   jax.ShapeDtypeStruct((b, t, nk), F32)],
        scratch_shapes=[pltpu.VMEM((nrow, tile), F32),
                        pltpu.VMEM((k, tile), F32), pltpu.VMEM((k, tile), jnp.int32),
                        pltpu.VMEM((k, tile), F32), pltpu.VMEM((k, tile), jnp.int32),
                        pltpu.VMEM((k, tile), F32), pltpu.VMEM((k, tile), jnp.int32),
                        pltpu.VMEM((nk, tile), jnp.int32), pltpu.VMEM((nk, tile), jnp.int32),
                        pltpu.VMEM((nk, tile), F32)],
        compiler_params=_cparams(("parallel", "parallel")),
    )(x1, g.reshape(1, d), sh, sc, wt)


def _peer_mix_kernel(h_ref, a_ref, b_ref, gate_ref, ut_ref, v_ref, x_ref, gt_ref, fg_ref, o_ref,
                     m_scr, acc, *, tile, n_steps, pairs):
    nk = PEER_NKEYS
    half = nk // 2
    step = pl.program_id(2)
    hi_mask = jnp.uint32(0xFFFF0000)

    @pl.when((pl.program_id(0) == 0) & (pl.program_id(1) == 0) & (step == 0))
    def _():
        acc[...] = jnp.zeros(acc.shape, F32)

    @pl.when(step == 0)
    def _():
        sub = lax.broadcasted_iota(jnp.int32, (nk, a_ref.shape[1]), 0)
        a_of_row = jnp.where(sub < half, 2 * sub, 2 * (sub - half) + 1)

        def build(p, carry):
            arow = a_ref[pl.ds(p, 1), :]
            brow = b_ref[pl.ds(p, 1), :]
            grow = gate_ref[pl.ds(p, 1), :]
            xa = jnp.where(a_of_row == arow, 1.0, 0.0).astype(BF16)
            yb = jnp.where(sub == brow, 0.5 * grow, 0.0).astype(BF16)
            m = _dot_nt(xa, yb).astype(BF16).astype(F32)
            bits = lax.bitcast_convert_type(m, jnp.uint32)
            m_scr[pl.ds(pl.multiple_of(p * MASK_PITCH, 8), half), :] = (bits[half:] & hi_mask) | (bits[:half] >> 16)
            return carry

        lax.fori_loop(0, tile, build, 0, unroll=128)

    h = h_ref[...]
    parts = []
    for q in range(pairs):
        pr = step * pairs + q
        act = _gelu_times_2(_dot(h, ut_ref[:, q * 2 * nk:(q + 1) * 2 * nk]))
        w = m_scr[pl.ds(pr, tile, stride=MASK_PITCH), :]
        m_even = lax.bitcast_convert_type(w << 16, F32)
        m_odd = lax.bitcast_convert_type(w & hi_mask, F32)
        parts.append((act * jnp.concatenate([m_even, m_odd], axis=1)).astype(BF16))
    contrib = _dot(jnp.concatenate(parts, axis=1), v_ref[...])
    acc[...] = jnp.where(step == 0, contrib, acc[...] + contrib)

    @pl.when(step == n_steps - 1)
    def _():
        x2 = x_ref[...] + gt_ref[...] * acc[...]
        o_ref[...] = _rms(x2, fg_ref[...])


def _peer_mix(hb, aidx, bidx, gate, ut, v, x1, gt, final_g, *, tile, pairs):
    b, t, d = x1.shape
    nt = t // tile
    nk = PEER_NKEYS
    npk = aidx.shape[-1]
    n_steps = nk // (2 * pairs)
    eb = 2 * nk * pairs
    tok = lambda wd: pl.BlockSpec((None, tile, wd), lambda i, j, s: (i, j, 0))
    return pl.pallas_call(
        functools.partial(_peer_mix_kernel, tile=tile, n_steps=n_steps, pairs=pairs),
        grid=(b, nt, n_steps),
        in_specs=[tok(d), tok(npk), tok(npk), tok(npk),
                  pl.BlockSpec((d, eb), lambda i, j, s: (0, s)),
                  pl.BlockSpec((eb, d), lambda i, j, s: (s, 0)), tok(d),
                  pl.BlockSpec((None, 1, d), lambda i, j, s: (i, 0, 0)),
                  pl.BlockSpec((1, d), lambda i, j, s: (0, 0))],
        out_specs=tok(d),
        out_shape=jax.ShapeDtypeStruct((b, t, d), F32),
        scratch_shapes=[pltpu.VMEM((tile * MASK_PITCH, nk), jnp.uint32), pltpu.VMEM((tile, d), F32)],
        compiler_params=_cparams(("parallel", "parallel", "arbitrary")),
    )(hb, aidx, bidx, gate, ut, v, x1, gt, final_g.reshape(1, d))


def _block_diag(w):
    n, e, _ = w.shape
    eye = jnp.eye(n, dtype=w.dtype)
    return (eye[:, None, :, None] * w[:, :, None, :]).reshape(n * e, n * e)


def _mix_sequence(x3, ctx_mode, params, states, mods, *, rg_tb, gdn_tb, proj_tile):
    (norm1_g, w_rg, w_gdn, rg_conv_w, rg_conv_b, wg, gate_b, c_lam, gdn_conv_w, ea_row, dtb_row) = params
    sh1, sc1, mod_row = mods
    rg_h0_f, rg_h0_b, gdn_s0 = states
    p_rg, qkv, z, ab = _project(x3, norm1_g, sh1, sc1, mod_row, jnp.concatenate([w_rg, w_gdn], axis=1),
                                (2 * RG_WIDTH, 3 * GDN_WIDTH, GDN_WIDTH, AB_PAD), colmajor=not ctx_mode, tile=proj_tile)
    h_f, st_f = _rglru_pass(p_rg, rg_conv_w, rg_conv_b, wg[0], gate_b[0], c_lam[0], rg_h0_f, None,
                            reverse=False, tb=rg_tb)
    y_rg, st_b = _rglru_pass(p_rg, rg_conv_w, rg_conv_b, wg[1], gate_b[1], c_lam[1], rg_h0_b, h_f,
                             reverse=True, tb=rg_tb)
    prep = _gdn_prep(qkv, ab, gdn_conv_w, ea_row, dtb_row, tb=gdn_tb)
    o_f, o_b, s_fin = _gdn_scan(prep, gdn_s0, cps=min(GDN_SCAN_CHUNKS, x3.shape[1] // GDN_CHUNK))
    return y_rg, o_f, o_b, z, (st_f, st_b, s_fin)


def kernel(x, c, ctx, c_ctx, w_mod, b_mod, norm1_g, norm2_g, w_in, rg_conv_w, rg_conv_b, rg_gate_w, rg_gate_b,
           rg_lambda, gdn_conv_w, gdn_a_log, gdn_dt_bias, gdn_norm_g, w_out, peer_wq, peer_keys, peer_u, peer_v,
           final_g):
    b, t, d = x.shape
    depth = w_mod.shape[0]
    assert depth == 1, "context residual stream update is only needed for depth > 1"
    l = 0
    w = RG_WIDTH

    cc = jnp.zeros((16, d), F32).at[:b].set(c).at[b].set(c_ctx)
    w_rg = w_in[l][:, :2 * w].astype(BF16)
    n_ab = w_in.shape[2] - 2 * w - 4 * GDN_WIDTH
    w_gdn = jnp.concatenate([w_in[l][:, 2 * w:2 * w + 4 * GDN_WIDTH],
                             jnp.pad(w_in[l][:, 2 * w + 4 * GDN_WIDTH:], ((0, 0), (0, AB_PAD - n_ab)))],
                            axis=1).astype(BF16)
    wg = jnp.stack([jnp.concatenate([_block_diag(rg_gate_w[l, dr, 0]), _block_diag(rg_gate_w[l, dr, 1])], axis=1)
                    for dr in range(2)]).astype(BF16)
    gate_b = rg_gate_b[l].reshape(2, 1, 2 * w)
    c_lam = (-RG_C * jax.nn.softplus(-rg_lambda[l])).reshape(2, 1, w)
    ea = jnp.exp(gdn_a_log[l])
    pad4 = jnp.zeros((2, GDN_HEADS), F32)
    ea_row = jnp.pad(jnp.concatenate([ea, pad4], axis=1).reshape(1, -1), ((0, 0), (0, AB_PAD - n_ab)))
    dtb_row = jnp.pad(jnp.concatenate([gdn_dt_bias[l], pad4], axis=1).reshape(1, -1), ((0, 0), (0, AB_PAD - n_ab)))
    params = (norm1_g[l], w_rg, w_gdn, rg_conv_w[l], rg_conv_b[l].reshape(1, w), wg, gate_b, c_lam,
              gdn_conv_w[l], ea_row, dtb_row)

    mod = _modulation(cc, w_mod[l], b_mod[l])
    sh1, sc1, gt1, sh2, sc2, gt2 = [mod[:, i * d:(i + 1) * d].reshape(16, 1, d) for i in range(6)]

    tc = ctx.shape[1]
    zero_states = (jnp.zeros((b, 1, w), F32), jnp.zeros((b, 1, w), F32),
                   jnp.zeros((b, 2, GDN_HEADS, GDN_DK, GDN_DV), F32))
    _, _, _, _, ctx_states = _mix_sequence(ctx, True, params, zero_states, (sh1, sc1, lambda i: b),
                                           rg_tb=min(tc, RG_BLOCK), gdn_tb=min(tc, GDN_BLOCK),
                                           proj_tile=min(tc, TOKEN_TILE))

    tile = min(t, TOKEN_TILE)
    y_rg, o_f, o_b, z, _ = _mix_sequence(x, False, params, ctx_states, (sh1, sc1, lambda i: i),
                                         rg_tb=min(t, RG_BLOCK), gdn_tb=min(t, GDN_BLOCK), proj_tile=tile)
    x1 = _out_project(x, y_rg, o_f, o_b, z, gdn_norm_g[l].reshape(1, GDN_DV), w_out[l].astype(BF16), gt1, tile=tile)

    wfold_t = _fold_keys(peer_wq[l], peer_keys[l]).T.astype(BF16)
    hb, aidx, bidx, gate = _peer_select(x1, norm2_g[l], sh2, sc2, wfold_t, tile=tile)
    out = _peer_mix(hb, aidx, bidx, gate, peer_u[l].astype(BF16).T, peer_v[l].astype(BF16),
                    x1, gt2, final_g, tile=tile, pairs=MIX_PAIRS)
    return out
```
